```python
import math
import jax, jax.numpy as jnp
from jax import lax
import numpy as np

D_MODEL = 1024
BATCH = 16
SEQ = 4096
DEPTH = 1

NSA_HEADS = 16
NSA_GROUPS = 4
NSA_HPG = NSA_HEADS // NSA_GROUPS
HEAD_DIM = 64
NSA_WIDTH = NSA_HEADS * HEAD_DIM
KV_WIDTH = NSA_GROUPS * HEAD_DIM
CMP_BLOCK = 32
CMP_STRIDE = 16
CMP_HIDDEN = 4 * HEAD_DIM
SEL_BLOCK = 64
SEL_TOPN = 8
WINDOW = 512
Q_BLOCK = 128
FORCE_SCORE = 1.0e4
REL_BUCKETS = 32
REL_MAX_DIST = 1024
SSM_WIDTH = 2 * D_MODEL
SSM_HEAD_DIM = 64
SSM_HEADS = SSM_WIDTH // SSM_HEAD_DIM
SSM_GROUPS = 4
SSM_HPG = SSM_HEADS // SSM_GROUPS
SSM_STATE = 128
CONV_WIDTH = 4
SSM_CHUNK = 128
CONV_DIM = SSM_WIDTH + 2 * SSM_GROUPS * SSM_STATE
NORM_EPS = 1e-6
IN_COLS = NSA_WIDTH + 6 * KV_WIDTH + 3 * NSA_HEADS + NSA_WIDTH + SSM_WIDTH + CONV_DIM + SSM_HEADS + 2 * D_MODEL

kernel_name = 'hybrid_nsa_ssd_gated_merge'


def column_offsets():
    sizes = (('q', NSA_WIDTH), ('k_cmp', KV_WIDTH), ('v_cmp', KV_WIDTH), ('k_slc', KV_WIDTH),
             ('v_slc', KV_WIDTH), ('k_swa', KV_WIDTH), ('v_swa', KV_WIDTH), ('nsa_gate', 3 * NSA_HEADS),
             ('z_nsa', NSA_WIDTH), ('z_ssm', SSM_WIDTH), ('xbc', CONV_DIM), ('dt', SSM_HEADS),
             ('merge_gate', 2 * D_MODEL))
    out, lo = {}, 0
    for name, n in sizes:
        out[name] = (lo, lo + n)
        lo += n
    return out


def rms_norm(x, w):
    xf = x.astype(jnp.float32)
    y = xf * lax.rsqrt(jnp.mean(xf * xf, axis=-1, keepdims=True) + NORM_EPS)
    return (y * w.astype(jnp.float32)).astype(x.dtype)


def masked_softmax(logits, mask):
    logits = jnp.where(mask, logits.astype(jnp.float32), -1e30)
    return jnp.where(mask, jax.nn.softmax(logits, axis=-1), 0.0)


def t5_bucket(dist):
    max_exact = REL_BUCKETS // 2
    d = jnp.maximum(dist, 0)
    df = jnp.maximum(d, 1).astype(jnp.float32)
    large = max_exact + (jnp.log(df / max_exact) / math.log(REL_MAX_DIST / max_exact)
                         * (REL_BUCKETS - max_exact)).astype(jnp.int32)
    return jnp.where(d < max_exact, d, jnp.minimum(large, REL_BUCKETS - 1))


def compress_blocks(kv, pos_emb, w1, b1, w2):
    b, s, g, dh = kv.shape
    ratio = CMP_BLOCK // CMP_STRIDE
    n_cmp = s // CMP_STRIDE - ratio + 1
    seg = kv.reshape(b, s // CMP_STRIDE, CMP_STRIDE, g, dh)
    blocks = jnp.concatenate([seg[:, r:r + n_cmp] for r in range(ratio)], axis=2)
    blocks = blocks + pos_emb[None, None, :, None, :]
    flat = jnp.moveaxis(blocks, 3, 2).reshape(b, n_cmp, g, CMP_BLOCK * dh)
    return jax.nn.silu(flat @ w1 + b1) @ w2


def nsa_mixer(q, k_cmp, v_cmp, k_slc, v_slc, k_swa, v_swa, gate_logits, rel_bias):
    b, s = q.shape[:2]
    G, R, dh = NSA_GROUPS, NSA_HPG, HEAD_DIM
    n_blk = s // Q_BLOCK
    n_cmp = k_cmp.shape[1]
    n_sel = s // SEL_BLOCK
    top_n = min(SEL_TOPN, n_sel)
    rel_bias = rel_bias.astype(jnp.float32)
    c_start = jnp.arange(n_cmp) * CMP_STRIDE
    cmp_end = c_start + CMP_BLOCK - 1
    s_start = jnp.arange(n_sel) * SEL_BLOCK
    overlap = ((c_start[:, None] < s_start[None, :] + SEL_BLOCK)
               & (c_start[:, None] + CMP_BLOCK > s_start[None, :])).astype(jnp.float32)
    ks_blocks = jnp.moveaxis(k_slc.reshape(b, n_sel, SEL_BLOCK, G, dh), 3, 1)
    vs_blocks = jnp.moveaxis(v_slc.reshape(b, n_sel, SEL_BLOCK, G, dh), 3, 1)
    k_swa_p = jnp.pad(k_swa, ((0, 0), (WINDOW, 0), (0, 0), (0, 0)))
    v_swa_p = jnp.pad(v_swa, ((0, 0), (WINDOW, 0), (0, 0), (0, 0)))
    bias_grp = rel_bias.reshape(REL_BUCKETS, G, R).transpose(1, 0, 2)
    g_ar = jnp.arange(G)
    j_ar = jnp.arange(n_sel)

    def head_bias(bucket):
        return jnp.transpose(rel_bias[bucket].reshape(*bucket.shape, G, R), (0, 2, 3, 1))

    def step(idx):
        bi, qb = idx
        q0 = qb * Q_BLOCK
        t = q0 + jnp.arange(Q_BLOCK)
        qt = lax.dynamic_slice_in_dim(q[bi], q0, Q_BLOCK, 0)
        kc, vc = k_cmp[bi], v_cmp[bi]
        lg = jnp.einsum('tgrd,ngd->tgrn', qt, kc).astype(jnp.float32) + head_bias(t5_bucket(t[:, None] - cmp_end[None, :]))
        p_cmp = masked_softmax(lg, (cmp_end[None, :] <= t[:, None])[:, None, None, :])
        o_cmp = jnp.einsum('tgrn,ngd->tgrd', p_cmp.astype(vc.dtype), vc)
        imp = jnp.einsum('tgrn,nj->tgj', p_cmp, overlap)
        cur = t // SEL_BLOCK
        valid = s_start[None, :] <= t[:, None]
        forced = (j_ar[None, :] == 0) | (j_ar[None, :] == cur[:, None]) | (j_ar[None, :] == cur[:, None] - 1)
        score = jnp.where(valid[:, None, :], imp + jnp.where(forced, FORCE_SCORE, 0.0)[:, None, :], -1.0)
        top_val, top_idx = lax.top_k(score, top_n)
        kg = ks_blocks[bi][g_ar[None, :, None], top_idx]
        vg = vs_blocks[bi][g_ar[None, :, None], top_idx]
        key_pos = top_idx[..., None] * SEL_BLOCK + jnp.arange(SEL_BLOCK)
        m_sel = (top_val >= 0.0)[..., None] & (key_pos <= t[:, None, None, None])
        bias = jnp.transpose(bias_grp[g_ar[None, :, None, None], t5_bucket(t[:, None, None, None] - key_pos)], (0, 1, 4, 2, 3))
        lg = jnp.einsum('tgrd,tgjpd->tgrjp', qt, kg).astype(jnp.float32) + bias
        n_keys = top_n * SEL_BLOCK
        p_sel = masked_softmax(lg.reshape(Q_BLOCK, G, R, n_keys), m_sel.reshape(Q_BLOCK, G, 1, n_keys))
        o_slc = jnp.einsum('tgrk,tgkd->tgrd', p_sel.astype(vg.dtype), vg.reshape(Q_BLOCK, G, n_keys, dh))
        kw = lax.dynamic_slice_in_dim(k_swa_p[bi], q0, WINDOW + Q_BLOCK, 0)
        vw = lax.dynamic_slice_in_dim(v_swa_p[bi], q0, WINDOW + Q_BLOCK, 0)
        spos = q0 - WINDOW + jnp.arange(WINDOW + Q_BLOCK)
        dist = t[:, None] - spos[None, :]
        m_win = (spos[None, :] >= 0) & (dist >= 0) & (dist < WINDOW)
        lg = jnp.einsum('tgrd,kgd->tgrk', qt, kw).astype(jnp.float32) + head_bias(t5_bucket(dist))
        p_win = masked_softmax(lg, m_win[:, None, None, :])
        o_swa = jnp.einsum('tgrk,kgd->tgrd', p_win.astype(vw.dtype), vw)
        gts = jax.nn.sigmoid(lax.dynamic_slice_in_dim(gate_logits[bi], q0, Q_BLOCK, 0).astype(jnp.float32))
        o = gts[..., 0:1] * o_cmp + gts[..., 1:2] * o_slc + gts[..., 2:3] * o_swa
        return o.reshape(Q_BLOCK, NSA_WIDTH).astype(q.dtype)

    b_idx = jnp.repeat(jnp.arange(b), n_blk)
    qb_idx = jnp.tile(jnp.arange(n_blk), b)
    out = lax.map(step, (b_idx, qb_idx))
    return out.reshape(b, s, NSA_WIDTH)


def causal_depthwise_conv(x, w, bias):
    y = lax.conv_general_dilated(x, w[:, None, :], window_strides=(1,), padding=[(CONV_WIDTH - 1, 0)],
                                 dimension_numbers=('NWC', 'WIO', 'NWC'), feature_group_count=x.shape[-1])
    return y + bias


def ssd_chunked(x, dt, a, bm, cm):
    b, s = x.shape[:2]
    L = SSM_CHUNK
    nc = s // L

    def to_chunks(t):
        return jnp.moveaxis(t.reshape(b, nc, L, *t.shape[2:]), 1, 0)

    tril = jnp.tril(jnp.ones((L, L), dtype=bool))

    def step(state, inp):
        xc, dtc, bc, cc = inp
        cs = jnp.cumsum(dtc * a, axis=1)
        csT = jnp.moveaxis(cs, 1, -1)
        diff = csT[..., :, None] - csT[..., None, :]
        decay = jnp.where(tril, jnp.exp(jnp.where(tril, diff, 0.0)), 0.0)
        cb = jnp.einsum('bign,bjgn->bgij', cc, bc)
        w = cb[:, :, None] * decay * jnp.moveaxis(dtc, 1, -1)[..., None, :]
        y_diag = jnp.einsum('bgrij,bjgrp->bigrp', w, xc)
        y_off = jnp.einsum('bign,bgrpn->bigrp', cc, state) * jnp.exp(cs)[..., None]
        total = cs[:, -1]
        to_end = jnp.exp(total[:, None] - cs) * dtc
        new_state = state * jnp.exp(total)[..., None, None] + jnp.einsum('bjgn,bjgr,bjgrp->bgrpn', bc, to_end, xc)
        return new_state, y_diag + y_off

    state0 = jnp.zeros((b, SSM_GROUPS, SSM_HPG, SSM_HEAD_DIM, SSM_STATE), jnp.float32)
    _, y = lax.scan(step, state0, (to_chunks(x), to_chunks(dt), to_chunks(bm), to_chunks(cm)))
    return jnp.moveaxis(y, 0, 1).reshape(x.shape)


def gated_group_rmsnorm(y, z, w):
    h = (y * jax.nn.silu(z.astype(jnp.float32))).reshape(*y.shape[:-1], SSM_GROUPS, -1)
    h = h * lax.rsqrt(jnp.mean(h * h, axis=-1, keepdims=True) + NORM_EPS)
    return (h.reshape(y.shape) * w.astype(jnp.float32)).astype(z.dtype)


def hybrid_layer(x, norm_w, w_in, cmp_pos_k, cmp_pos_v, cmp_k_w1, cmp_k_b1, cmp_k_w2, cmp_v_w1, cmp_v_b1,
                 cmp_v_w2, conv_w, conv_b, dt_bias, a_log, d_skip, ssm_norm_w, w_out_nsa, w_out_ssm, w_out, rel_bias):
    b, s, _ = x.shape
    G, R, dh = NSA_GROUPS, NSA_HPG, HEAD_DIM
    xn = rms_norm(x, norm_w)
    cols = column_offsets()

    def proj(name):
        lo, hi = cols[name]
        return xn @ w_in[:, lo:hi]

    def kv(name):
        return proj(name).reshape(b, s, G, dh)

    q = proj('q').reshape(b, s, G, R, dh) * (HEAD_DIM ** -0.5)
    k_cmp = compress_blocks(kv('k_cmp'), cmp_pos_k, cmp_k_w1, cmp_k_b1, cmp_k_w2)
    v_cmp = compress_blocks(kv('v_cmp'), cmp_pos_v, cmp_v_w1, cmp_v_b1, cmp_v_w2)
    o_nsa = nsa_mixer(q, k_cmp, v_cmp, kv('k_slc'), kv('v_slc'), kv('k_swa'), kv('v_swa'),
                      proj('nsa_gate').reshape(b, s, G, R, 3), rel_bias)
    h_nsa = (o_nsa * jax.nn.silu(proj('z_nsa'))) @ w_out_nsa
    xbc = jax.nn.silu(causal_depthwise_conv(proj('xbc'), conv_w, conv_b))
    xs, bm, cm = jnp.split(xbc, [SSM_WIDTH, SSM_WIDTH + SSM_GROUPS * SSM_STATE], axis=-1)
    dt = jax.nn.softplus(proj('dt').astype(jnp.float32) + dt_bias.astype(jnp.float32)).reshape(b, s, SSM_GROUPS, SSM_HPG)
    a = -jnp.exp(a_log.astype(jnp.float32)).reshape(SSM_GROUPS, SSM_HPG)
    xh = xs.astype(jnp.float32).reshape(b, s, SSM_GROUPS, SSM_HPG, SSM_HEAD_DIM)
    y = ssd_chunked(xh, dt, a, bm.astype(jnp.float32).reshape(b, s, SSM_GROUPS, SSM_STATE),
                    cm.astype(jnp.float32).reshape(b, s, SSM_GROUPS, SSM_STATE))
    y = y + d_skip.astype(jnp.float32).reshape(SSM_GROUPS, SSM_HPG)[..., None] * xh
    y = gated_group_rmsnorm(y.reshape(b, s, SSM_WIDTH), proj('z_ssm'), ssm_norm_w)
    h_ssm = y @ w_out_ssm
    g_nsa, g_ssm = jnp.split(jax.nn.sigmoid(proj('merge_gate')), 2, axis=-1)
    return x + (g_nsa * h_nsa + g_ssm * h_ssm) @ w_out


def setup_inputs(seed: int = 0) -> dict:
    key = jax.random.key(seed)
    ks = jax.random.split(key, 24)
    nrm = lambda k, shape, scale: jax.random.normal(k, shape, jnp.float32) * scale
    dt0 = jnp.exp(jax.random.uniform(ks[13], (DEPTH, SSM_HEADS), jnp.float32) * (math.log(0.1) - math.log(0.001)) + math.log(0.001))
    return {
        'x': nrm(ks[0], (BATCH, SEQ, D_MODEL), 1.0),
        'norm_w': 1.0 + nrm(ks[1], (DEPTH, D_MODEL), 0.02),
        'w_in': nrm(ks[2], (DEPTH, D_MODEL, IN_COLS), D_MODEL ** -0.5),
        'cmp_pos_k': nrm(ks[3], (DEPTH, CMP_BLOCK, HEAD_DIM), 0.1),
        'cmp_pos_v': nrm(ks[4], (DEPTH, CMP_BLOCK, HEAD_DIM), 0.1),
        'cmp_k_w1': nrm(ks[5], (DEPTH, CMP_BLOCK * HEAD_DIM, CMP_HIDDEN), (CMP_BLOCK * HEAD_DIM) ** -0.5),
        'cmp_k_b1': nrm(ks[6], (DEPTH, CMP_HIDDEN), 0.02),
        'cmp_k_w2': nrm(ks[7], (DEPTH, CMP_HIDDEN, HEAD_DIM), CMP_HIDDEN ** -0.5),
        'cmp_v_w1': nrm(ks[8], (DEPTH, CMP_BLOCK * HEAD_DIM, CMP_HIDDEN), (CMP_BLOCK * HEAD_DIM) ** -0.5),
        'cmp_v_b1': nrm(ks[9], (DEPTH, CMP_HIDDEN), 0.02),
        'cmp_v_w2': nrm(ks[10], (DEPTH, CMP_HIDDEN, HEAD_DIM), CMP_HIDDEN ** -0.5),
        'conv_w': nrm(ks[11], (DEPTH, CONV_WIDTH, CONV_DIM), CONV_WIDTH ** -0.5),
        'conv_b': nrm(ks[12], (DEPTH, CONV_DIM), 0.02),
        'dt_bias': dt0 + jnp.log(-jnp.expm1(-dt0)),
        'a_log': jnp.log(jax.random.uniform(ks[14], (DEPTH, SSM_HEADS), jnp.float32, 1.0, 16.0)),
        'd_skip': 1.0 + nrm(ks[15], (DEPTH, SSM_HEADS), 0.02),
        'ssm_norm_w': 1.0 + nrm(ks[16], (DEPTH, SSM_WIDTH), 0.02),
        'w_out_nsa': nrm(ks[17], (DEPTH, NSA_WIDTH, D_MODEL), NSA_WIDTH ** -0.5),
        'w_out_ssm': nrm(ks[18], (DEPTH, SSM_WIDTH, D_MODEL), SSM_WIDTH ** -0.5),
        'w_out': nrm(ks[19], (DEPTH, D_MODEL, D_MODEL), D_MODEL ** -0.5),
        'rel_bias': nrm(ks[20], (REL_BUCKETS, NSA_HEADS), 0.2),
        'final_norm_w': 1.0 + nrm(ks[21], (D_MODEL,), 0.02),
    }


def reference(x, norm_w, w_in, cmp_pos_k, cmp_pos_v, cmp_k_w1, cmp_k_b1, cmp_k_w2, cmp_v_w1, cmp_v_b1, cmp_v_w2,
              conv_w, conv_b, dt_bias, a_log, d_skip, ssm_norm_w, w_out_nsa, w_out_ssm, w_out, rel_bias, final_norm_w):
    for layer in range(DEPTH):
        x = hybrid_layer(x, norm_w[layer], w_in[layer], cmp_pos_k[layer], cmp_pos_v[layer], cmp_k_w1[layer],
                         cmp_k_b1[layer], cmp_k_w2[layer], cmp_v_w1[layer], cmp_v_b1[layer], cmp_v_w2[layer],
                         conv_w[layer], conv_b[layer], dt_bias[layer], a_log[layer], d_skip[layer],
                         ssm_norm_w[layer], w_out_nsa[layer], w_out_ssm[layer], w_out[layer], rel_bias)
    return rms_norm(x, final_norm_w)
```

```python
import functools

import numpy as np
import jax
import jax.numpy as jnp
from jax import lax
from jax.experimental import pallas as pl
from jax.experimental.pallas import tpu as pltpu

F32 = jnp.float32
BF16 = jnp.bfloat16

D_MODEL = 1024
NSA_HEADS = 16
NSA_GROUPS = 4
NSA_HPG = NSA_HEADS // NSA_GROUPS
HEAD_DIM = 64
NSA_WIDTH = NSA_HEADS * HEAD_DIM
KV_WIDTH = NSA_GROUPS * HEAD_DIM
CMP_BLOCK = 32
CMP_STRIDE = 16
CMP_HIDDEN = 4 * HEAD_DIM
SEL_BLOCK = 64
SEL_TOPN = 8
WINDOW = 512
Q_BLOCK = 128
FORCE_SCORE = 1.0e4
REL_BUCKETS = 32
SSM_WIDTH = 2 * D_MODEL
SSM_HEAD_DIM = 64
SSM_HEADS = SSM_WIDTH // SSM_HEAD_DIM
SSM_GROUPS = 4
SSM_HPG = SSM_HEADS // SSM_GROUPS
SSM_STATE = 128
CONV_WIDTH = 4
SSM_CHUNK = 128
CONV_DIM = SSM_WIDTH + 2 * SSM_GROUPS * SSM_STATE
NORM_EPS = 1e-6

NEG = -1e30
LANES = 128
QCOLS = NSA_HEADS * Q_BLOCK
GCOLS = NSA_HPG * Q_BLOCK
N_NEAR = 8
SUM_ROWS = 16
VMEM_LIMIT = 56 * 1024 * 1024


def _column_offsets():
    sizes = (('q', NSA_WIDTH), ('k_cmp', KV_WIDTH), ('v_cmp', KV_WIDTH), ('k_slc', KV_WIDTH),
             ('v_slc', KV_WIDTH), ('k_swa', KV_WIDTH), ('v_swa', KV_WIDTH), ('nsa_gate', 3 * NSA_HEADS),
             ('z_nsa', NSA_WIDTH), ('z_ssm', SSM_WIDTH), ('xbc', CONV_DIM), ('dt', SSM_HEADS),
             ('merge_gate', 2 * D_MODEL))
    out, lo = {}, 0
    for name, n in sizes:
        out[name] = (lo, lo + n)
        lo += n
    return out


def _t5_bucket_np(dist):
    d = np.maximum(np.asarray(dist, dtype=np.int64), 0)
    max_exact = REL_BUCKETS // 2
    large = np.full(d.shape, max_exact, dtype=np.int64)
    d8 = d.astype(object) ** 8
    for k in range(1, REL_BUCKETS - max_exact):
        large = large + (d8 >= 2 ** (32 + 3 * k)).astype(np.int64)
    return np.where(d < max_exact, d, np.minimum(large, REL_BUCKETS - 1)).astype(np.int32)


def _sigmoid(x):
    return 1.0 / (1.0 + jnp.exp(-x))


def _silu(x):
    return x * _sigmoid(x)


def _softplus(x):
    return jnp.maximum(x, 0.0) + jnp.log1p(jnp.exp(-jnp.abs(x)))


def _dot(a, b):
    return jnp.dot(a, b, preferred_element_type=F32)


def _dot_nt(a, b):
    return lax.dot_general(a, b, (((1,), (1,)), ((), ())), preferred_element_type=F32)


def _split3(x):
    hi = x.astype(BF16)
    r1 = x - hi.astype(F32)
    mid = r1.astype(BF16)
    lo = (r1 - mid.astype(F32)).astype(BF16)
    return hi, mid, lo


def _dot_exact_lhs(x, w01):
    hi, mid, lo = _split3(x)
    return _dot(hi, w01) + _dot(mid, w01) + _dot(lo, w01)


def _dot_exact_rhs(w01, x):
    hi, mid, lo = _split3(x)
    return _dot(w01, hi) + _dot(w01, mid) + _dot(w01, lo)


def _norm_tile(x_ref, nw_ref, xn_ref):
    x = x_ref[...]
    ms = jnp.mean(x * x, axis=-1, keepdims=True)
    xn_ref[...] = (x * lax.rsqrt(ms + NORM_EPS) * nw_ref[...]).astype(BF16)


def _proj_nat_kernel(x_ref, nw_ref, w_ref, o_ref, xn_ref):
    @pl.when(pl.program_id(1) == 0)
    def _():
        _norm_tile(x_ref, nw_ref, xn_ref)

    o_ref[...] = _dot(xn_ref[...], w_ref[...]).astype(o_ref.dtype)


def _proj_tr_kernel(x_ref, nw_ref, wt_ref, o_ref, xn_ref, *, n_sub):
    @pl.when(pl.program_id(1) == 0)
    def _():
        _norm_tile(x_ref, nw_ref, xn_ref)

    res = _dot_nt(wt_ref[...], xn_ref[...])
    for s in range(n_sub):
        o_ref[s] = res[:, s * LANES:(s + 1) * LANES].astype(o_ref.dtype)


def _proj_nat(x2, nw, w, out_dtype, tm, tn):
    t, d = x2.shape
    n = w.shape[1]
    assert t % tm == 0 and n % tn == 0
    return pl.pallas_call(
        _proj_nat_kernel,
        grid=(t // tm, n // tn),
        in_specs=[pl.BlockSpec((tm, d), lambda i, j: (i, 0)),
                  pl.BlockSpec((1, d), lambda i, j: (0, 0)),
                  pl.BlockSpec((d, tn), lambda i, j: (0, j))],
        out_specs=pl.BlockSpec((tm, tn), lambda i, j: (i, j)),
        out_shape=jax.ShapeDtypeStruct((t, n), out_dtype),
        scratch_shapes=[pltpu.VMEM((tm, d), BF16)],
        compiler_params=pltpu.CompilerParams(dimension_semantics=("parallel", "arbitrary"),
                                             vmem_limit_bytes=VMEM_LIMIT),
        name="proj_nat",
    )(x2, nw, w)


def _proj_tr(x2, nw, wt, out_dtype, tm, tr):
    t, d = x2.shape
    rows = wt.shape[0]
    assert t % tm == 0 and rows % tr == 0 and tm % LANES == 0
    n_sub = tm // LANES
    return pl.pallas_call(
        functools.partial(_proj_tr_kernel, n_sub=n_sub),
        grid=(t // tm, rows // tr),
        in_specs=[pl.BlockSpec((tm, d), lambda i, j: (i, 0)),
                  pl.BlockSpec((1, d), lambda i, j: (0, 0)),
                  pl.BlockSpec((tr, d), lambda i, j: (j, 0))],
        out_specs=pl.BlockSpec((n_sub, tr, LANES), lambda i, j: (i, j, 0)),
        out_shape=jax.ShapeDtypeStruct((t // LANES, rows, LANES), out_dtype),
        scratch_shapes=[pltpu.VMEM((tm, d), BF16)],
        compiler_params=pltpu.CompilerParams(dimension_semantics=("parallel", "arbitrary"),
                                             vmem_limit_bytes=VMEM_LIMIT),
        name="proj_tr",
    )(x2, nw, wt)


def _compress_kernel(ak_ref, av_ref, posk_ref, posv_ref, w1k_ref, b1k_ref, w2k_ref,
                     w1v_ref, b1v_ref, w2vt_ref, kc_ref, vct_ref):
    half = CMP_STRIDE * HEAD_DIM
    nseg = ak_ref.shape[1]

    def hidden(a, pos_ref, w1_ref, b1_ref):
        p0 = _dot((a + pos_ref[0:1, :]).astype(BF16), w1_ref[0:half, :])
        p1 = _dot((a + pos_ref[1:2, :]).astype(BF16), w1_ref[half:2 * half, :])
        p1 = pltpu.roll(p1, nseg - 1, 0)
        return _silu(p0 + p1 + b1_ref[...]).astype(BF16)

    for g in range(NSA_GROUPS):
        hk = hidden(ak_ref[g], posk_ref, w1k_ref, b1k_ref)
        kc_ref[:, g * HEAD_DIM:(g + 1) * HEAD_DIM] = _dot(hk, w2k_ref[...]).astype(kc_ref.dtype)
        hv = hidden(av_ref[g], posv_ref, w1v_ref, b1v_ref)
        vct_ref[g * HEAD_DIM:(g + 1) * HEAD_DIM, :] = _dot_nt(w2vt_ref[...], hv).astype(vct_ref.dtype)


def _compress(ak, av, posk, posv, w1k, b1k, w2k, w1v, b1v, w2vt):
    b, g, nseg, width = ak.shape
    const2 = lambda shape: pl.BlockSpec(shape, lambda i: (0, 0))
    return pl.pallas_call(
        _compress_kernel,
        grid=(b,),
        in_specs=[pl.BlockSpec((None, g, nseg, width), lambda i: (i, 0, 0, 0)),
                  pl.BlockSpec((None, g, nseg, width), lambda i: (i, 0, 0, 0)),
                  const2(posk.shape), const2(posv.shape),
                  const2(w1k.shape), const2(b1k.shape), const2(w2k.shape),
                  const2(w1v.shape), const2(b1v.shape), const2(w2vt.shape)],
        out_specs=[pl.BlockSpec((None, nseg, KV_WIDTH), lambda i: (i, 0, 0)),
                   pl.BlockSpec((None, KV_WIDTH, nseg), lambda i: (i, 0, 0))],
        out_shape=[jax.ShapeDtypeStruct((b, nseg, KV_WIDTH), BF16),
                   jax.ShapeDtypeStruct((b, KV_WIDTH, nseg), BF16)],
        compiler_params=pltpu.CompilerParams(dimension_semantics=("parallel",),
                                             vmem_limit_bytes=VMEM_LIMIT),
        name="compress",
    )(ak, av, posk, posv, w1k, b1k, w2k, w1v, b1v, w2vt)


def _nsa_kernel(qt_ref, gt_ref, kc_ref, vct_ref, kslc_ref, vslct_ref, kswa_ref, vswat_ref,
                tsel_ref, tcmp_ref, ovt_ref, out_ref,
                qbd_ref, m_ref, acc_ref, msel_ref, ot_ref, *, n_qb, n_cmp_rows, n_sel, top_n):
    qb = pl.program_id(1)
    q0 = qb * Q_BLOCK

    qbd_ref[...] = jnp.zeros_like(qbd_ref)
    for g in range(NSA_GROUPS):
        for r in range(NSA_HPG):
            h = g * NSA_HPG + r
            qbd_ref[g * HEAD_DIM:(g + 1) * HEAD_DIM, h * Q_BLOCK:(h + 1) * Q_BLOCK] = \
                qt_ref[h * HEAD_DIM:(h + 1) * HEAD_DIM, :]
    qbd = qbd_ref[...]

    ones_rows = jnp.ones((SUM_ROWS, LANES), BF16)

    u0 = pl.multiple_of(8 * (n_qb - 1 - qb), 8)
    s = _dot(kc_ref[...], qbd) + tcmp_ref[pl.ds(u0, n_cmp_rows), :]
    m = jnp.maximum(jnp.max(s, axis=0, keepdims=True), 0.1 * NEG)
    e = jnp.exp(s - m)
    l = jnp.sum(e, axis=0, keepdims=True)
    p = e * (1.0 / jnp.where(l == 0.0, 1.0, l))

    j_iota = lax.broadcasted_iota(jnp.int32, (n_sel, Q_BLOCK), 0)
    j_f32 = j_iota.astype(F32)
    t_pos = q0 + lax.broadcasted_iota(jnp.int32, (n_sel, Q_BLOCK), 1)
    cur = t_pos // SEL_BLOCK
    valid = j_iota <= cur
    forced = (j_iota == 0) | (j_iota == cur) | (j_iota == cur - 1)

    o_cmp = []
    for g in range(NSA_GROUPS):
        pg = p[:, g * GCOLS:(g + 1) * GCOLS]
        o_cmp.append(_dot(vct_ref[g * HEAD_DIM:(g + 1) * HEAD_DIM, :], pg.astype(BF16)))
        psum = pg[:, 0:Q_BLOCK]
        for r in range(1, NSA_HPG):
            psum = psum + pg[:, r * Q_BLOCK:(r + 1) * Q_BLOCK]
        imp = _dot(ovt_ref[...], psum.astype(BF16))
        score = jnp.where(valid, imp + jnp.where(forced, FORCE_SCORE, 0.0), -1.0)
        work = score
        chosen = jnp.zeros(score.shape, dtype=jnp.bool_)
        for _ in range(top_n):
            best = jnp.max(work, axis=0, keepdims=True)
            first = jnp.min(jnp.where(work == best, j_f32, float(n_sel)), axis=0, keepdims=True)
            hit = j_f32 == first
            chosen = chosen | hit
            work = jnp.where(hit, -2.0, work)
        addm = jnp.where(chosen & (score >= 0.0), 0.0, NEG)
        for j in range(n_sel):
            msel_ref[g, j] = jnp.broadcast_to(addm[j:j + 1, :], (8, Q_BLOCK))

    def attend(k_tile, vt_tile, bias_tile, mask_rows):
        s = _dot(k_tile, qbd) + bias_tile
        if mask_rows is not None:
            slabs = []
            for g in range(NSA_GROUPS):
                halves = []
                for hb in range(LANES // SEL_BLOCK):
                    row = mask_rows[g][hb]
                    add = jnp.concatenate([row] * (SEL_BLOCK // 8), axis=0)
                    add = jnp.concatenate([add] * NSA_HPG, axis=1)
                    halves.append(s[hb * SEL_BLOCK:(hb + 1) * SEL_BLOCK, g * GCOLS:(g + 1) * GCOLS] + add)
                slabs.append(jnp.concatenate(halves, axis=0))
            s = jnp.concatenate(slabs, axis=1)
        m_old = m_ref[...]
        m_new = jnp.maximum(m_old, jnp.max(s, axis=0, keepdims=True))
        alpha = jnp.exp(m_old - m_new)
        pt = jnp.exp(s - m_new).astype(BF16)
        m_ref[...] = m_new
        for g in range(NSA_GROUPS):
            v1 = jnp.concatenate([vt_tile[g * HEAD_DIM:(g + 1) * HEAD_DIM, :], ones_rows], axis=0)
            acc_ref[g] = acc_ref[g] * alpha[:, g * GCOLS:(g + 1) * GCOLS] + \
                _dot(v1, pt[:, g * GCOLS:(g + 1) * GCOLS])

    def finish():
        outs = []
        for g in range(NSA_GROUPS):
            a = acc_ref[g]
            outs.append(a[0:HEAD_DIM, :] * (1.0 / a[HEAD_DIM:HEAD_DIM + 1, :]))
        return outs

    def reset():
        m_ref[...] = jnp.full(m_ref.shape, NEG, F32)
        acc_ref[...] = jnp.zeros_like(acc_ref)

    reset()

    def sel_body(i, carry):
        jj = qb - i
        k_tile = kslc_ref[pl.ds(pl.multiple_of(jj * LANES, LANES), LANES), :]
        vt_tile = vslct_ref[jj]
        bias_tile = tsel_ref[jnp.minimum(i, N_NEAR)]
        rows = [[msel_ref[g, 2 * jj + hb] for hb in range(LANES // SEL_BLOCK)]
                for g in range(NSA_GROUPS)]
        attend(k_tile, vt_tile, bias_tile, rows)
        return carry

    lax.fori_loop(0, qb + 1, sel_body, 0)
    o_slc = finish()

    reset()
    n_win = WINDOW // LANES

    def win_body(i, carry):
        jj = qb - i
        k_tile = kswa_ref[pl.ds(pl.multiple_of(jj * LANES, LANES), LANES), :]
        vt_tile = vswat_ref[jj]
        bias_tile = tsel_ref[jnp.where(i == n_win, N_NEAR + 1, i)]
        attend(k_tile, vt_tile, bias_tile, None)
        return carry

    lax.fori_loop(0, jnp.minimum(qb, n_win) + 1, win_body, 0)
    o_swa = finish()

    gates = _sigmoid(gt_ref[...])
    for g in range(NSA_GROUPS):
        def gate_row(c):
            return jnp.concatenate(
                [gates[c * NSA_HEADS + g * NSA_HPG + r:c * NSA_HEADS + g * NSA_HPG + r + 1, :]
                 for r in range(NSA_HPG)], axis=1)
        og = gate_row(0) * o_cmp[g] + gate_row(1) * o_slc[g] + gate_row(2) * o_swa[g]
        for r in range(NSA_HPG):
            h = g * NSA_HPG + r
            ot_ref[h * HEAD_DIM:(h + 1) * HEAD_DIM, :] = og[:, r * Q_BLOCK:(r + 1) * Q_BLOCK]
    out_ref[...] = ot_ref[...].T


def _nsa(qt, gt, kc, vct, kslc, vslct, kswa, vswat, tsel, tcmp, ovt, seq):
    b = kc.shape[0]
    n_qb = seq // Q_BLOCK
    n_cmp_rows = kc.shape[1]
    n_sel = seq // SEL_BLOCK
    top_n = min(SEL_TOPN, n_sel)
    n_gate = gt.shape[2]
    const = lambda arr: pl.BlockSpec(arr.shape, lambda i, j: (0,) * arr.ndim,
                                     pipeline_mode=pl.Buffered(1))
    kern = functools.partial(_nsa_kernel, n_qb=n_qb, n_cmp_rows=n_cmp_rows, n_sel=n_sel, top_n=top_n)
    return pl.pallas_call(
        kern,
        grid=(b, n_qb),
        in_specs=[pl.BlockSpec((None, None, NSA_WIDTH, Q_BLOCK), lambda i, j: (i, j, 0, 0)),
                  pl.BlockSpec((None, None, n_gate, Q_BLOCK), lambda i, j: (i, j, 0, 0)),
                  pl.BlockSpec((None, n_cmp_rows, KV_WIDTH), lambda i, j: (i, 0, 0)),
                  pl.BlockSpec((None, KV_WIDTH, n_cmp_rows), lambda i, j: (i, 0, 0)),
                  pl.BlockSpec((None, seq, KV_WIDTH), lambda i, j: (i, 0, 0)),
                  pl.BlockSpec((None, n_qb, KV_WIDTH, Q_BLOCK), lambda i, j: (i, 0, 0, 0)),
                  pl.BlockSpec((None, seq, KV_WIDTH), lambda i, j: (i, 0, 0)),
                  pl.BlockSpec((None, n_qb, KV_WIDTH, Q_BLOCK), lambda i, j: (i, 0, 0, 0)),
                  const(tsel), const(tcmp), const(ovt)],
        out_specs=pl.BlockSpec((None, Q_BLOCK, NSA_WIDTH), lambda i, j: (i, j, 0)),
        out_shape=jax.ShapeDtypeStruct((b, seq, NSA_WIDTH), F32),
        scratch_shapes=[pltpu.VMEM((KV_WIDTH, QCOLS), BF16),
                        pltpu.VMEM((1, QCOLS), F32),
                        pltpu.VMEM((NSA_GROUPS, HEAD_DIM + SUM_ROWS, GCOLS), F32),
                        pltpu.VMEM((NSA_GROUPS, n_sel, 8, Q_BLOCK), F32),
                        pltpu.VMEM((NSA_WIDTH, Q_BLOCK), F32)],
        compiler_params=pltpu.CompilerParams(dimension_semantics=("parallel", "arbitrary"),
                                             vmem_limit_bytes=VMEM_LIMIT),
        name="nsa",
    )(qt, gt, kc, vct, kslc, vslct, kswa, vswat, tsel, tcmp, ovt)


def _ssd_kernel(xbc_ref, dt_ref, dtt_ref, convw_ref, convb_ref, dtb_ref, dtbt_ref, a_ref, at_ref,
                dskip_ref, expand_ref, y_ref, xpad_ref, state_ref):
    L = SSM_CHUNK
    c = pl.program_id(1)

    @pl.when(c == 0)
    def _():
        xpad_ref[0:8, :] = jnp.zeros((8, CONV_DIM), F32)
        state_ref[...] = jnp.zeros_like(state_ref)

    xpad_ref[8:8 + L, :] = xbc_ref[...]
    conv = convb_ref[...]
    for k in range(CONV_WIDTH):
        conv = conv + convw_ref[k:k + 1, :] * xpad_ref[pl.ds(8 - (CONV_WIDTH - 1) + k, L), :]
    xpad_ref[0:8, :] = xpad_ref[L:L + 8, :]
    xbc = _silu(conv)
    xs = xbc[:, 0:SSM_WIDTH]
    bm = xbc[:, SSM_WIDTH:SSM_WIDTH + SSM_GROUPS * SSM_STATE]
    cm = xbc[:, SSM_WIDTH + SSM_GROUPS * SSM_STATE:]

    dt = _softplus(dt_ref[...] + dtb_ref[...])
    dtt = _softplus(dtt_ref[...] + dtbt_ref[...])
    a_row = -jnp.exp(a_ref[...])
    a_col = -jnp.exp(at_ref[...])
    row_i = lax.broadcasted_iota(jnp.int32, (L, L), 0)
    col_j = lax.broadcasted_iota(jnp.int32, (L, L), 1)
    tril = row_i >= col_j
    tril01 = jnp.where(tril, 1.0, 0.0).astype(BF16)
    triu01 = jnp.where(row_i <= col_j, 1.0, 0.0).astype(BF16)
    cs = _dot_exact_rhs(tril01, dt * a_row)
    cst = _dot_exact_lhs(dtt * a_col, triu01)

    expand = expand_ref[...]
    cs_x = _dot_exact_lhs(cs, expand)
    dt_x = _dot_exact_lhs(dt, expand)
    total_x = cs_x[L - 1:L, :]
    xdt = (xs * dt_x).astype(BF16)
    xw = (xs * (jnp.exp(total_x - cs_x) * dt_x)).astype(BF16)
    decay_out = jnp.exp(cs_x)
    decay_state = jnp.exp(total_x)

    lane = lax.broadcasted_iota(jnp.int32, (L, LANES), 1)
    gw = SSM_HPG * SSM_HEAD_DIM
    for g in range(SSM_GROUPS):
        bg = bm[:, g * SSM_STATE:(g + 1) * SSM_STATE]
        cg = cm[:, g * SSM_STATE:(g + 1) * SSM_STATE].astype(BF16)
        cb = _dot_nt(cg, bg.astype(BF16))
        st = state_ref[g]
        y_g = _dot(cg, st.astype(BF16)) * decay_out[:, g * gw:(g + 1) * gw]
        state_ref[g] = st * decay_state[:, g * gw:(g + 1) * gw] + \
            _dot(bg.T.astype(BF16), xw[:, g * gw:(g + 1) * gw])
        pieces = []
        for pair in range(SSM_HPG // 2):
            ws = []
            for hh in range(2):
                h = g * SSM_HPG + 2 * pair + hh
                diff = cs[:, h:h + 1] - cst[h:h + 1, :]
                decay = jnp.where(tril, jnp.exp(jnp.where(tril, diff, 0.0)), 0.0)
                ws.append((cb * decay).astype(BF16))
            w_pair = jnp.concatenate(ws, axis=1)
            lo = g * gw + pair * LANES
            slab = xdt[:, lo:lo + LANES]
            zero = jnp.zeros_like(slab)
            x_bd = jnp.concatenate([jnp.where(lane < SSM_HEAD_DIM, slab, zero),
                                    jnp.where(lane >= SSM_HEAD_DIM, slab, zero)], axis=0)
            pieces.append(_dot(w_pair, x_bd))
        y_ref[:, g * gw:(g + 1) * gw] = y_g + jnp.concatenate(pieces, axis=1) + \
            dskip_ref[:, g * gw:(g + 1) * gw] * xs[:, g * gw:(g + 1) * gw]


def _ssd(xbc, dt, dtt, convw, convb, dtb, dtbt, a_log, a_logt, dskip_x, expand, seq):
    b = xbc.shape[0]
    nc = seq // SSM_CHUNK
    const = lambda arr: pl.BlockSpec(arr.shape, lambda i, j: (0,) * arr.ndim)
    return pl.pallas_call(
        _ssd_kernel,
        grid=(b, nc),
        in_specs=[pl.BlockSpec((None, SSM_CHUNK, CONV_DIM), lambda i, j: (i, j, 0)),
                  pl.BlockSpec((None, SSM_CHUNK, SSM_HEADS), lambda i, j: (i, j, 0)),
                  pl.BlockSpec((None, None, SSM_HEADS, SSM_CHUNK), lambda i, j: (i, j, 0, 0)),
                  const(convw), const(convb), const(dtb), const(dtbt), const(a_log), const(a_logt),
                  const(dskip_x), const(expand)],
        out_specs=pl.BlockSpec((None, SSM_CHUNK, SSM_WIDTH), lambda i, j: (i, j, 0)),
        out_shape=jax.ShapeDtypeStruct((b, seq, SSM_WIDTH), F32),
        scratch_shapes=[pltpu.VMEM((SSM_CHUNK + 8, CONV_DIM), F32),
                        pltpu.VMEM((SSM_GROUPS, SSM_STATE, SSM_HPG * SSM_HEAD_DIM), F32)],
        compiler_params=pltpu.CompilerParams(dimension_semantics=("parallel", "arbitrary"),
                                             vmem_limit_bytes=VMEM_LIMIT),
        name="ssd",
    )(xbc, dt, dtt, convw, convb, dtb, dtbt, a_log, a_logt, dskip_x, expand)


def _epilogue_kernel(x_ref, o_ref, y_ref, nw_ref, wz1_ref, wz2_ref, wmg_ref, won_ref, wos_ref, wo_ref,
                     snw_ref, fnw_ref, out_ref):
    x = x_ref[...]
    ms = jnp.mean(x * x, axis=-1, keepdims=True)
    xn = (x * lax.rsqrt(ms + NORM_EPS) * nw_ref[...]).astype(BF16)
    u = (o_ref[...] * _silu(_dot(xn, wz1_ref[...]))).astype(BF16)
    h_nsa = _dot(u, won_ref[...])
    hh = y_ref[...] * _silu(_dot(xn, wz2_ref[...]))
    gw = SSM_WIDTH // SSM_GROUPS
    parts = []
    for g in range(SSM_GROUPS):
        hg = hh[:, g * gw:(g + 1) * gw]
        hg = hg * lax.rsqrt(jnp.mean(hg * hg, axis=-1, keepdims=True) + NORM_EPS)
        parts.append((hg * snw_ref[:, g * gw:(g + 1) * gw]).astype(BF16))
    h_ssm = _dot(jnp.concatenate(parts, axis=1), wos_ref[...])
    gate = _sigmoid(_dot(xn, wmg_ref[...]))
    mix = (gate[:, 0:D_MODEL] * h_nsa + gate[:, D_MODEL:] * h_ssm).astype(BF16)
    r = x + _dot(mix, wo_ref[...])
    ms2 = jnp.mean(r * r, axis=-1, keepdims=True)
    out_ref[...] = r * lax.rsqrt(ms2 + NORM_EPS) * fnw_ref[...]


def _epilogue(x2, o2, y2, nw, wz1, wz2, wmg, won, wos, wo, snw, fnw, tm):
    t, d = x2.shape
    assert t % tm == 0
    const = lambda arr: pl.BlockSpec(arr.shape, lambda i: (0,) * arr.ndim, pipeline_mode=pl.Buffered(1))
    return pl.pallas_call(
        _epilogue_kernel,
        grid=(t // tm,),
        in_specs=[pl.BlockSpec((tm, d), lambda i: (i, 0)),
                  pl.BlockSpec((tm, NSA_WIDTH), lambda i: (i, 0)),
                  pl.BlockSpec((tm, SSM_WIDTH), lambda i: (i, 0)),
                  const(nw), const(wz1), const(wz2), const(wmg), const(won), const(wos), const(wo),
                  const(snw), const(fnw)],
        out_specs=pl.BlockSpec((tm, d), lambda i: (i, 0)),
        out_shape=jax.ShapeDtypeStruct((t, d), F32),
        compiler_params=pltpu.CompilerParams(dimension_semantics=("parallel",),
                                             vmem_limit_bytes=VMEM_LIMIT),
        name="epilogue",
    )(x2, o2, y2, nw, wz1, wz2, wmg, won, wos, wo, snw, fnw)


def _bias_tables(rel_bias, seq):
    n_qb = seq // Q_BLOCK
    fvec = rel_bias.astype(F32)[_t5_bucket_np(np.arange(max(seq, 2 * WINDOW)))]

    def lookup(dist):
        vals = fvec[np.clip(dist, 0, fvec.shape[0] - 1)]
        vals = jnp.where(jnp.asarray(dist < 0)[..., None], NEG, vals)
        return jnp.transpose(vals, (0, 2, 1)).reshape(dist.shape[0], -1)

    key = np.arange(LANES)[:, None]
    tok = np.arange(Q_BLOCK)[None, :]
    tiles = []
    for o in range(N_NEAR):
        tiles.append(lookup(LANES * o + tok - key))
    tiles.append(lookup(np.full((LANES, Q_BLOCK), fvec.shape[0] - 1)))
    dwin = WINDOW + tok - key
    tiles.append(lookup(np.where(dwin < WINDOW, dwin, -1)))
    tsel = jnp.stack(tiles)

    n_cmp_rows = seq // CMP_STRIDE
    u = np.arange(8 * (n_qb - 1) + n_cmp_rows)[:, None]
    tcmp = lookup(tok - CMP_STRIDE * u + Q_BLOCK * (n_qb - 1) - (CMP_BLOCK - 1))
    return tsel, tcmp


def _overlap_t(seq):
    n_cmp_rows = seq // CMP_STRIDE
    n_sel = seq // SEL_BLOCK
    c_start = np.arange(n_cmp_rows)[None, :] * CMP_STRIDE
    s_start = np.arange(n_sel)[:, None] * SEL_BLOCK
    ov = (c_start < s_start + SEL_BLOCK) & (c_start + CMP_BLOCK > s_start)
    ov[:, n_cmp_rows - 1] = False
    return jnp.asarray(ov, dtype=BF16)


def _layer(x, norm_w, w_in, cmp_pos_k, cmp_pos_v, cmp_k_w1, cmp_k_b1, cmp_k_w2, cmp_v_w1, cmp_v_b1,
           cmp_v_w2, conv_w, conv_b, dt_bias, a_log, d_skip, ssm_norm_w, w_out_nsa, w_out_ssm, w_out,
           rel_bias, out_norm_w):
    b, s, d = x.shape
    assert d == D_MODEL and s % Q_BLOCK == 0 and s >= 2 * WINDOW
    t = b * s
    n_qb = s // Q_BLOCK
    cols = _column_offsets()
    wcol = lambda name: w_in[:, cols[name][0]:cols[name][1]]
    x2 = x.reshape(t, d)
    nw = norm_w.reshape(1, d).astype(F32)
    tm = 1024 if t % 1024 == 0 else Q_BLOCK

    w_k = jnp.concatenate([wcol('k_slc'), wcol('k_swa')], axis=1).astype(BF16)
    kk = _proj_nat(x2, nw, w_k, BF16, tm, 2 * KV_WIDTH)
    kslc = kk[:, :KV_WIDTH].reshape(b, s, KV_WIDTH)
    kswa = kk[:, KV_WIDTH:].reshape(b, s, KV_WIDTH)
    w_c = jnp.concatenate([wcol('k_cmp'), wcol('v_cmp')], axis=1).astype(BF16)
    cc = _proj_nat(x2, nw, w_c, F32, tm, 2 * KV_WIDTH)
    xbc = _proj_nat(x2, nw, wcol('xbc').astype(BF16), F32, tm, CONV_DIM // 2).reshape(b, s, CONV_DIM)
    dt = _proj_nat(x2, nw, wcol('dt').astype(BF16), F32, tm, SSM_HEADS).reshape(b, s, SSM_HEADS)

    gate_w = wcol('nsa_gate').reshape(d, NSA_HEADS, 3).transpose(0, 2, 1).reshape(d, 3 * NSA_HEADS)
    qt = _proj_tr(x2, nw, (wcol('q') * (HEAD_DIM ** -0.5)).T.astype(BF16), BF16, tm, NSA_WIDTH)
    w_v = jnp.concatenate([wcol('v_slc'), wcol('v_swa')], axis=1).T.astype(BF16)
    vv = _proj_tr(x2, nw, w_v, BF16, tm, 2 * KV_WIDTH)
    w_s = jnp.concatenate([gate_w, wcol('dt')], axis=1).T.astype(BF16)
    sm = _proj_tr(x2, nw, w_s, F32, tm, w_s.shape[0])
    qt = qt.reshape(b, n_qb, NSA_WIDTH, Q_BLOCK)
    vslct = vv[:, :KV_WIDTH].reshape(b, n_qb, KV_WIDTH, Q_BLOCK)
    vswat = vv[:, KV_WIDTH:].reshape(b, n_qb, KV_WIDTH, Q_BLOCK)
    gt = sm[:, :3 * NSA_HEADS].reshape(b, n_qb, 3 * NSA_HEADS, Q_BLOCK)
    dtt = sm[:, 3 * NSA_HEADS:].reshape(b, n_qb, SSM_HEADS, Q_BLOCK)

    nseg = s // CMP_STRIDE

    def seg_major(a):
        a = a.reshape(b, nseg, CMP_STRIDE, NSA_GROUPS, HEAD_DIM)
        return a.transpose(0, 3, 1, 2, 4).reshape(b, NSA_GROUPS, nseg, CMP_STRIDE * HEAD_DIM)

    pos2 = lambda p: p.astype(F32).reshape(2, CMP_STRIDE * HEAD_DIM)
    kc, vct = _compress(seg_major(cc[:, :KV_WIDTH]), seg_major(cc[:, KV_WIDTH:]),
                        pos2(cmp_pos_k), pos2(cmp_pos_v),
                        cmp_k_w1.astype(BF16), cmp_k_b1.reshape(1, -1).astype(F32), cmp_k_w2.astype(BF16),
                        cmp_v_w1.astype(BF16), cmp_v_b1.reshape(1, -1).astype(F32), cmp_v_w2.T.astype(BF16))

    tsel, tcmp = _bias_tables(rel_bias, s)
    o_nsa = _nsa(qt, gt, kc, vct, kslc, vslct, kswa, vswat, tsel, tcmp, _overlap_t(s), s)

    expand = jnp.asarray(np.kron(np.eye(SSM_HEADS), np.ones((1, SSM_HEAD_DIM))), dtype=BF16)
    y = _ssd(xbc, dt, dtt, conv_w.astype(F32), conv_b.reshape(1, -1).astype(F32),
             dt_bias.reshape(1, -1).astype(F32), dt_bias.reshape(-1, 1).astype(F32),
             a_log.reshape(1, -1).astype(F32), a_log.reshape(-1, 1).astype(F32),
             jnp.repeat(d_skip.astype(F32), SSM_HEAD_DIM).reshape(1, -1), expand, s)

    out = _epilogue(x2, o_nsa.reshape(t, NSA_WIDTH), y.reshape(t, SSM_WIDTH), nw,
                    wcol('z_nsa').astype(BF16), wcol('z_ssm').astype(BF16), wcol('merge_gate').astype(BF16),
                    w_out_nsa.astype(BF16), w_out_ssm.astype(BF16), w_out.astype(BF16),
                    ssm_norm_w.reshape(1, -1).astype(F32), out_norm_w.reshape(1, -1).astype(F32),
                    256 if t % 256 == 0 else Q_BLOCK)
    return out.reshape(b, s, d)


def kernel(x, norm_w, w_in, cmp_pos_k, cmp_pos_v, cmp_k_w1, cmp_k_b1, cmp_k_w2, cmp_v_w1, cmp_v_b1, cmp_v_w2,
           conv_w, conv_b, dt_bias, a_log, d_skip, ssm_norm_w, w_out_nsa, w_out_ssm, w_out, rel_bias,
           final_norm_w):
    depth = norm_w.shape[0]
    assert depth == 1, "the epilogue fuses the final norm into the single layer"
    return _layer(x, norm_w[0], w_in[0], cmp_pos_k[0], cmp_pos_v[0], cmp_k_w1[0], cmp_k_b1[0], cmp_k_w2[0],
                  cmp_v_w1[0], cmp_v_b1[0], cmp_v_w2[0], conv_w[0], conv_b[0], dt_bias[0], a_log[0],
                  d_skip[0], ssm_norm_w[0], w_out_nsa[0], w_out_ssm[0], w_out[0], rel_bias, final_norm_w)
```

```python
import functools
import math

import numpy as np
import jax
import jax.numpy as jnp
from jax import lax
from jax.experimental import pallas as pl
from jax.experimental.pallas import tpu as pltpu

F32 = jnp.float32
BF16 = jnp.bfloat16

D_MODEL = 1024
NSA_HEADS = 16
NSA_GROUPS = 4
NSA_HPG = NSA_HEADS // NSA_GROUPS
HEAD_DIM = 64
NSA_WIDTH = NSA_HEADS * HEAD_DIM
KV_WIDTH = NSA_GROUPS * HEAD_DIM
CMP_BLOCK = 32
CMP_STRIDE = 16
CMP_HIDDEN = 4 * HEAD_DIM
SEL_BLOCK = 64
SEL_TOPN = 8
WINDOW = 512
Q_BLOCK = 128
FORCE_SCORE = 1.0e4
REL_BUCKETS = 32
SSM_WIDTH = 2 * D_MODEL
SSM_HEAD_DIM = 64
SSM_HEADS = SSM_WIDTH // SSM_HEAD_DIM
SSM_GROUPS = 4
SSM_HPG = SSM_HEADS // SSM_GROUPS
SSM_STATE = 128
CONV_WIDTH = 4
SSM_CHUNK = 128
CONV_DIM = SSM_WIDTH + 2 * SSM_GROUPS * SSM_STATE
NORM_EPS = 1e-6

NEG = -1e30
LOG2E = math.log2(math.e)
LANES = 128
QCOLS = NSA_HEADS * Q_BLOCK
GCOLS = NSA_HPG * Q_BLOCK
N_NEAR = 8
TILE_FAR = N_NEAR
TILE_WIN_OLD = N_NEAR + 1
TILE_MASKED = N_NEAR + 2
N_TILES = N_NEAR + 3
MASKED_BUCKET = REL_BUCKETS
SUM_ROWS = 16
MAX_EXP2_EXCESS = 100.0
STEP_TILES = 4
STEP = STEP_TILES * LANES
WIN_TILES = WINDOW // LANES + 1
VMEM_LIMIT = 56 * 1024 * 1024


def _column_offsets():
    sizes = (('q', NSA_WIDTH), ('k_cmp', KV_WIDTH), ('v_cmp', KV_WIDTH), ('k_slc', KV_WIDTH),
             ('v_slc', KV_WIDTH), ('k_swa', KV_WIDTH), ('v_swa', KV_WIDTH), ('nsa_gate', 3 * NSA_HEADS),
             ('z_nsa', NSA_WIDTH), ('z_ssm', SSM_WIDTH), ('xbc', CONV_DIM), ('dt', SSM_HEADS),
             ('merge_gate', 2 * D_MODEL))
    out, lo = {}, 0
    for name, n in sizes:
        out[name] = (lo, lo + n)
        lo += n
    return out


def _t5_bucket_np(dist):
    dist = np.asarray(dist, dtype=np.int64)
    d = np.maximum(dist, 0)
    max_exact = REL_BUCKETS // 2
    large = np.full(d.shape, max_exact, dtype=np.int64)
    d8 = d.astype(object) ** 8
    for k in range(1, REL_BUCKETS - max_exact):
        large = large + (d8 >= 2 ** (32 + 3 * k)).astype(np.int64)
    bucket = np.where(d < max_exact, d, np.minimum(large, REL_BUCKETS - 1))
    return np.where(dist < 0, MASKED_BUCKET, bucket).astype(np.int32)


def _sigmoid(x):
    return 1.0 / (1.0 + jnp.exp(-x))


def _silu(x):
    return x * _sigmoid(x)


def _softplus(x):
    return jnp.maximum(x, 0.0) + jnp.log1p(jnp.exp(-jnp.abs(x)))


def _dot(a, b):
    return jnp.dot(a, b, preferred_element_type=F32)


def _dot_nt(a, b):
    return lax.dot_general(a, b, (((1,), (1,)), ((), ())), preferred_element_type=F32)


def _split3(x):
    hi = x.astype(BF16)
    r1 = x - hi.astype(F32)
    mid = r1.astype(BF16)
    lo = (r1 - mid.astype(F32)).astype(BF16)
    return hi, mid, lo


def _dot_exact_lhs(x, w01):
    hi, mid, lo = _split3(x)
    return _dot(hi, w01) + _dot(mid, w01) + _dot(lo, w01)


def _dot_exact_rhs(w01, x):
    hi, mid, lo = _split3(x)
    return _dot(w01, hi) + _dot(w01, mid) + _dot(w01, lo)


_NAT_OUTS = (('k_slc', KV_WIDTH, BF16), ('k_swa', KV_WIDTH, BF16), ('kv_cmp', 2 * KV_WIDTH, F32),
             ('xbc', CONV_DIM, F32), ('dt', SSM_HEADS, F32))
_TR_OUTS = (('q', NSA_WIDTH, BF16), ('v_slc', KV_WIDTH, BF16), ('v_swa', KV_WIDTH, BF16),
            ('gate', 3 * NSA_HEADS, F32), ('dt', SSM_HEADS, F32))


def _proj_kernel(x_ref, nw_ref, wn_ref, wt_ref, *out_refs, n_sub):
    x = x_ref[...]
    ms = jnp.mean(x * x, axis=-1, keepdims=True)
    xn = (x * lax.rsqrt(ms + NORM_EPS) * nw_ref[...]).astype(BF16)
    lo = 0
    for (_, n, _), o_ref in zip(_NAT_OUTS, out_refs[:len(_NAT_OUTS)]):
        o_ref[...] = _dot(xn, wn_ref[:, lo:lo + n]).astype(o_ref.dtype)
        lo += n
    lo = 0
    for (_, n, _), o_ref in zip(_TR_OUTS, out_refs[len(_NAT_OUTS):]):
        res = _dot_nt(wt_ref[lo:lo + n, :], xn)
        for s in range(n_sub):
            o_ref[s] = res[:, s * LANES:(s + 1) * LANES].astype(o_ref.dtype)
        lo += n


def _proj(x2, nw, wn, wt, tm):
    t, d = x2.shape
    assert t % tm == 0 and tm % LANES == 0
    n_sub = tm // LANES
    const = lambda arr: pl.BlockSpec(arr.shape, lambda i: (0,) * arr.ndim, pipeline_mode=pl.Buffered(1))
    out_specs = [pl.BlockSpec((tm, n), lambda i: (i, 0)) for _, n, _ in _NAT_OUTS] + \
                [pl.BlockSpec((n_sub, n, LANES), lambda i: (i, 0, 0)) for _, n, _ in _TR_OUTS]
    out_shape = [jax.ShapeDtypeStruct((t, n), dt) for _, n, dt in _NAT_OUTS] + \
                [jax.ShapeDtypeStruct((t // LANES, n, LANES), dt) for _, n, dt in _TR_OUTS]
    return pl.pallas_call(
        functools.partial(_proj_kernel, n_sub=n_sub),
        grid=(t // tm,),
        in_specs=[pl.BlockSpec((tm, d), lambda i: (i, 0)), const(nw), const(wn), const(wt)],
        out_specs=out_specs,
        out_shape=out_shape,
        compiler_params=pltpu.CompilerParams(dimension_semantics=("parallel",),
                                             vmem_limit_bytes=VMEM_LIMIT),
        name="proj",
    )(x2, nw, wn, wt)


def _compress_kernel(ak_ref, av_ref, posk_ref, posv_ref, w1k_ref, b1k_ref, w2k_ref,
                     w1v_ref, b1v_ref, w2vt_ref, kc_ref, vct_ref):
    half = CMP_STRIDE * HEAD_DIM
    nseg = ak_ref.shape[1]

    def hidden(a, pos_ref, w1_ref, b1_ref):
        p0 = _dot((a + pos_ref[0:1, :]).astype(BF16), w1_ref[0:half, :])
        p1 = _dot((a + pos_ref[1:2, :]).astype(BF16), w1_ref[half:2 * half, :])
        p1 = pltpu.roll(p1, nseg - 1, 0)
        return _silu(p0 + p1 + b1_ref[...]).astype(BF16)

    for g in range(NSA_GROUPS):
        hk = hidden(ak_ref[g], posk_ref, w1k_ref, b1k_ref)
        kc_ref[:, g * HEAD_DIM:(g + 1) * HEAD_DIM] = _dot(hk, w2k_ref[...]).astype(kc_ref.dtype)
        hv = hidden(av_ref[g], posv_ref, w1v_ref, b1v_ref)
        vct_ref[g * HEAD_DIM:(g + 1) * HEAD_DIM, :] = _dot_nt(w2vt_ref[...], hv).astype(vct_ref.dtype)


def _compress(ak, av, posk, posv, w1k, b1k, w2k, w1v, b1v, w2vt):
    b, g, nseg, width = ak.shape
    const2 = lambda shape: pl.BlockSpec(shape, lambda i: (0, 0))
    return pl.pallas_call(
        _compress_kernel,
        grid=(b,),
        in_specs=[pl.BlockSpec((None, g, nseg, width), lambda i: (i, 0, 0, 0)),
                  pl.BlockSpec((None, g, nseg, width), lambda i: (i, 0, 0, 0)),
                  const2(posk.shape), const2(posv.shape),
                  const2(w1k.shape), const2(b1k.shape), const2(w2k.shape),
                  const2(w1v.shape), const2(b1v.shape), const2(w2vt.shape)],
        out_specs=[pl.BlockSpec((None, nseg, KV_WIDTH), lambda i: (i, 0, 0)),
                   pl.BlockSpec((None, KV_WIDTH, nseg), lambda i: (i, 0, 0))],
        out_shape=[jax.ShapeDtypeStruct((b, nseg, KV_WIDTH), BF16),
                   jax.ShapeDtypeStruct((b, KV_WIDTH, nseg), BF16)],
        compiler_params=pltpu.CompilerParams(dimension_semantics=("parallel",),
                                             vmem_limit_bytes=VMEM_LIMIT),
        name="compress",
    )(ak, av, posk, posv, w1k, b1k, w2k, w1v, b1v, w2vt)


def _nsa_kernel(qt_ref, gt_ref, kc_ref, vct_ref, kslc_ref, vslct_ref, kswa_ref, vswat_ref,
                tsel_ref, tcmp_ref, ovt_ref, out_ref,
                qbd_ref, m_ref, acc_ref, msel_ref, ot_ref, s_ref, p_ref, viol_ref,
                *, n_qb, n_cmp_rows, n_sel, top_n):
    qb = pl.program_id(1)
    q0 = qb * Q_BLOCK

    @pl.when(qb == 0)
    def _():
        qbd_ref[...] = jnp.zeros_like(qbd_ref)

    for g in range(NSA_GROUPS):
        for r in range(NSA_HPG):
            h = g * NSA_HPG + r
            qbd_ref[g * HEAD_DIM:(g + 1) * HEAD_DIM, h * Q_BLOCK:(h + 1) * Q_BLOCK] = \
                qt_ref[h * HEAD_DIM:(h + 1) * HEAD_DIM, :]

    def gslice(g):
        return slice(g * GCOLS, (g + 1) * GCOLS)

    def values_with_ones(vt_tiles, g):
        vt = jnp.concatenate([t[g * HEAD_DIM:(g + 1) * HEAD_DIM, :] for t in vt_tiles], axis=1)
        return jnp.concatenate([vt, jnp.ones((SUM_ROWS, vt.shape[1]), BF16)], axis=0)

    def normalized(a):
        return a[0:HEAD_DIM, :] * (1.0 / a[HEAD_DIM:HEAD_DIM + 1, :])

    u0 = pl.multiple_of(8 * (n_qb - 1 - qb), 8)
    j_iota = lax.broadcasted_iota(jnp.int32, (n_sel, Q_BLOCK), 0)
    j_f32 = j_iota.astype(F32)
    t_pos = q0 + lax.broadcasted_iota(jnp.int32, (n_sel, Q_BLOCK), 1)
    cur = t_pos // SEL_BLOCK
    valid = j_iota <= cur
    forced = (j_iota == 0) | (j_iota == cur) | (j_iota == cur - 1)

    o_cmp = []
    for g in range(NSA_GROUPS):
        s = _dot(kc_ref[...], qbd_ref[:, gslice(g)]) + tcmp_ref[pl.ds(u0, n_cmp_rows), gslice(g)]
        m = jnp.maximum(jnp.max(s, axis=0, keepdims=True), 0.1 * NEG)
        e = jnp.exp2(s - m)
        l = jnp.sum(e, axis=0, keepdims=True)
        p = e * (1.0 / jnp.where(l == 0.0, 1.0, l))
        o_cmp.append(_dot(vct_ref[g * HEAD_DIM:(g + 1) * HEAD_DIM, :], p.astype(BF16)))
        psum = p[:, 0:Q_BLOCK]
        for r in range(1, NSA_HPG):
            psum = psum + p[:, r * Q_BLOCK:(r + 1) * Q_BLOCK]
        imp = _dot(ovt_ref[...], psum.astype(BF16))
        score = jnp.where(valid, imp + jnp.where(forced, FORCE_SCORE, 0.0), -1.0)
        work = score
        chosen = jnp.zeros(score.shape, dtype=jnp.bool_)
        for _ in range(top_n):
            best = jnp.max(work, axis=0, keepdims=True)
            first = jnp.min(jnp.where(work == best, j_f32, float(n_sel)), axis=0, keepdims=True)
            hit = j_f32 == first
            chosen = chosen | hit
            work = jnp.where(hit, -2.0, work)
        addm = jnp.where(chosen & (score >= 0.0), 0.0, NEG)
        for j in range(n_sel):
            msel_ref[g, j] = jnp.broadcast_to(addm[j:j + 1, :], (8, Q_BLOCK))

    blocks_per_step = STEP // SEL_BLOCK
    blocks_per_tile = LANES // SEL_BLOCK

    def sel_step(r, near, exact):
        ks = kslc_ref[pl.ds(pl.multiple_of(r * STEP, STEP), STEP), :]
        vt_tiles = [vslct_ref[STEP_TILES * r + t] for t in range(STEP_TILES)]
        if near:
            tile_ids = []
            for t in range(STEP_TILES):
                i = qb - (STEP_TILES * r + t)
                tile_ids.append(jnp.where(i < 0, TILE_MASKED, jnp.minimum(i, TILE_FAR)))
        logits = [_dot(ks, qbd_ref[:, gslice(g)]) for g in range(NSA_GROUPS)]
        alphas = []
        for g in range(NSA_GROUPS):
            m_old = m_ref[:, gslice(g)]
            if not near:
                far_row = tsel_ref[TILE_FAR, 0:8, gslice(g)]
            m8 = None
            for blk in range(blocks_per_step):
                rows = slice(blk * SEL_BLOCK, (blk + 1) * SEL_BLOCK)
                row = msel_ref[g, blocks_per_step * r + blk]
                row = jnp.concatenate([row] * NSA_HPG, axis=1)
                v = logits[g][rows, :]
                if near:
                    t, off = divmod(blk, blocks_per_tile)
                    v = v + tsel_ref[tile_ids[t], off * SEL_BLOCK:(off + 1) * SEL_BLOCK, gslice(g)]
                else:
                    row = row + far_row
                v = v + jnp.concatenate([row] * (SEL_BLOCK // 8), axis=0)
                vm = jnp.max(v.reshape(SEL_BLOCK // 8, 8, GCOLS), axis=0)
                m8 = vm if m8 is None else jnp.maximum(m8, vm)
                if exact:
                    s_ref[rows, gslice(g)] = v
                else:
                    p_ref[rows, gslice(g)] = jnp.exp2(v - m_old).astype(BF16)
            step_max = jnp.max(m8, axis=0, keepdims=True)
            m_new = jnp.maximum(m_old, step_max)
            alphas.append(jnp.exp2(m_old - m_new))
            m_ref[:, gslice(g)] = m_new
            if exact:
                for blk in range(blocks_per_step):
                    rows = slice(blk * SEL_BLOCK, (blk + 1) * SEL_BLOCK)
                    p_ref[rows, gslice(g)] = jnp.exp2(s_ref[rows, gslice(g)] - m_new).astype(BF16)
            else:
                viol_ref[:, gslice(g)] = jnp.maximum(viol_ref[:, gslice(g)], step_max - m_old)
        for g in range(NSA_GROUPS):
            pv = _dot(values_with_ones(vt_tiles, g), p_ref[0:STEP, gslice(g)])
            if exact:
                acc_ref[g] = acc_ref[g] * alphas[g] + pv
            else:
                acc_ref[g] = (acc_ref[g] + pv) * alphas[g]

    r_diag = qb // STEP_TILES
    n_far = jnp.maximum((qb - (N_NEAR - 1)) // STEP_TILES, 0)

    def selected(exact):
        m_ref[...] = jnp.full(m_ref.shape, NEG, F32)
        acc_ref[...] = jnp.zeros_like(acc_ref)
        sel_step(r_diag, True, True)

        def near_body(i, carry):
            sel_step(r_diag - i, True, exact)
            return carry

        def far_body(r, carry):
            sel_step(r, False, exact)
            return carry

        lax.fori_loop(1, r_diag - n_far + 1, near_body, 0)
        lax.fori_loop(0, n_far, far_body, 0)

    viol_ref[...] = jnp.zeros_like(viol_ref)
    selected(False)

    @pl.when(jnp.max(viol_ref[...]) > MAX_EXP2_EXCESS)
    def _():
        selected(True)

    o_slc = [normalized(acc_ref[g]) for g in range(NSA_GROUPS)]

    first_tile = jnp.maximum(qb - (WIN_TILES - 1), 0)
    kw = kswa_ref[pl.ds(pl.multiple_of(first_tile * LANES, LANES), WIN_TILES * LANES), :]
    vt_tiles = [vswat_ref[first_tile + t] for t in range(WIN_TILES)]
    win_ids = []
    for t in range(WIN_TILES):
        i = qb - (first_tile + t)
        win_ids.append(jnp.where(i < 0, TILE_MASKED, jnp.where(i == WIN_TILES - 1, TILE_WIN_OLD, i)))
    for g in range(NSA_GROUPS):
        s_ref[:, gslice(g)] = _dot(kw, qbd_ref[:, gslice(g)])
    for g in range(NSA_GROUPS):
        s = jnp.concatenate([s_ref[t * LANES:(t + 1) * LANES, gslice(g)] + tsel_ref[win_ids[t], :, gslice(g)]
                             for t in range(WIN_TILES)], axis=0)
        m = jnp.max(s, axis=0, keepdims=True)
        p_ref[:, gslice(g)] = jnp.exp2(s - m).astype(BF16)
    o_swa = [normalized(_dot(values_with_ones(vt_tiles, g), p_ref[:, gslice(g)]))
             for g in range(NSA_GROUPS)]

    gates = _sigmoid(gt_ref[...])
    for g in range(NSA_GROUPS):
        def gate_row(c):
            return jnp.concatenate(
                [gates[c * NSA_HEADS + g * NSA_HPG + r:c * NSA_HEADS + g * NSA_HPG + r + 1, :]
                 for r in range(NSA_HPG)], axis=1)
        og = gate_row(0) * o_cmp[g] + gate_row(1) * o_slc[g] + gate_row(2) * o_swa[g]
        for r in range(NSA_HPG):
            h = g * NSA_HPG + r
            ot_ref[h * HEAD_DIM:(h + 1) * HEAD_DIM, :] = og[:, r * Q_BLOCK:(r + 1) * Q_BLOCK]
    out_ref[...] = ot_ref[...].T


def _nsa(qt, gt, kc, vct, kslc, vslct, kswa, vswat, tsel, tcmp, ovt, seq):
    b = kc.shape[0]
    n_qb = seq // Q_BLOCK
    assert n_qb % STEP_TILES == 0 and n_qb >= WIN_TILES and WIN_TILES >= STEP_TILES
    n_cmp_rows = kc.shape[1]
    n_sel = seq // SEL_BLOCK
    top_n = min(SEL_TOPN, n_sel)
    n_gate = gt.shape[2]
    const = lambda arr: pl.BlockSpec(arr.shape, lambda i, j: (0,) * arr.ndim,
                                     pipeline_mode=pl.Buffered(1))
    kern = functools.partial(_nsa_kernel, n_qb=n_qb, n_cmp_rows=n_cmp_rows, n_sel=n_sel, top_n=top_n)
    return pl.pallas_call(
        kern,
        grid=(b, n_qb),
        in_specs=[pl.BlockSpec((None, None, NSA_WIDTH, Q_BLOCK), lambda i, j: (i, j, 0, 0)),
                  pl.BlockSpec((None, None, n_gate, Q_BLOCK), lambda i, j: (i, j, 0, 0)),
                  pl.BlockSpec((None, n_cmp_rows, KV_WIDTH), lambda i, j: (i, 0, 0)),
                  pl.BlockSpec((None, KV_WIDTH, n_cmp_rows), lambda i, j: (i, 0, 0)),
                  pl.BlockSpec((None, seq, KV_WIDTH), lambda i, j: (i, 0, 0)),
                  pl.BlockSpec((None, n_qb, KV_WIDTH, Q_BLOCK), lambda i, j: (i, 0, 0, 0)),
                  pl.BlockSpec((None, seq, KV_WIDTH), lambda i, j: (i, 0, 0)),
                  pl.BlockSpec((None, n_qb, KV_WIDTH, Q_BLOCK), lambda i, j: (i, 0, 0, 0)),
                  const(tsel), const(tcmp), const(ovt)],
        out_specs=pl.BlockSpec((None, Q_BLOCK, NSA_WIDTH), lambda i, j: (i, j, 0)),
        out_shape=jax.ShapeDtypeStruct((b, seq, NSA_WIDTH), F32),
        scratch_shapes=[pltpu.VMEM((KV_WIDTH, QCOLS), BF16),
                        pltpu.VMEM((1, QCOLS), F32),
                        pltpu.VMEM((NSA_GROUPS, HEAD_DIM + SUM_ROWS, GCOLS), F32),
                        pltpu.VMEM((NSA_GROUPS, n_sel, 8, Q_BLOCK), F32),
                        pltpu.VMEM((NSA_WIDTH, Q_BLOCK), F32),
                        pltpu.VMEM((WIN_TILES * LANES, QCOLS), F32),
                        pltpu.VMEM((WIN_TILES * LANES, QCOLS), BF16),
                        pltpu.VMEM((1, QCOLS), F32)],
        compiler_params=pltpu.CompilerParams(dimension_semantics=("parallel", "arbitrary"),
                                             vmem_limit_bytes=VMEM_LIMIT),
        name="nsa",
    )(qt, gt, kc, vct, kslc, vslct, kswa, vswat, tsel, tcmp, ovt)


def _ssd_kernel(xbc_ref, dt_ref, dtt_ref, convw_ref, convb_ref, dtb_ref, dtbt_ref, a_ref, at_ref,
                dskip_ref, expand_ref, y_ref, xpad_ref, state_ref):
    L = SSM_CHUNK
    c = pl.program_id(1)

    @pl.when(c == 0)
    def _():
        xpad_ref[0:8, :] = jnp.zeros((8, CONV_DIM), F32)
        state_ref[...] = jnp.zeros_like(state_ref)

    xpad_ref[8:8 + L, :] = xbc_ref[...]
    conv = convb_ref[...]
    for k in range(CONV_WIDTH):
        conv = conv + convw_ref[k:k + 1, :] * xpad_ref[pl.ds(8 - (CONV_WIDTH - 1) + k, L), :]
    xpad_ref[0:8, :] = xpad_ref[L:L + 8, :]
    xbc = _silu(conv)
    xs = xbc[:, 0:SSM_WIDTH]
    bm = xbc[:, SSM_WIDTH:SSM_WIDTH + SSM_GROUPS * SSM_STATE]
    cm = xbc[:, SSM_WIDTH + SSM_GROUPS * SSM_STATE:]

    dt = _softplus(dt_ref[...] + dtb_ref[...])
    dtt = _softplus(dtt_ref[...] + dtbt_ref[...])
    a_row = -jnp.exp(a_ref[...])
    a_col = -jnp.exp(at_ref[...])
    row_i = lax.broadcasted_iota(jnp.int32, (L, L), 0)
    col_j = lax.broadcasted_iota(jnp.int32, (L, L), 1)
    tril = row_i >= col_j
    tril01 = jnp.where(tril, 1.0, 0.0).astype(BF16)
    triu01 = jnp.where(row_i <= col_j, 1.0, 0.0).astype(BF16)
    cs = _dot_exact_rhs(tril01, dt * a_row)
    cst = _dot_exact_lhs(dtt * a_col, triu01)

    expand = expand_ref[...]
    cs_x = _dot_exact_lhs(cs, expand)
    dt_x = _dot_exact_lhs(dt, expand)
    total_x = cs_x[L - 1:L, :]
    xdt = (xs * dt_x).astype(BF16)
    xw = (xs * (jnp.exp(total_x - cs_x) * dt_x)).astype(BF16)
    decay_out = jnp.exp(cs_x)
    decay_state = jnp.exp(total_x)

    lane = lax.broadcasted_iota(jnp.int32, (L, LANES), 1)
    gw = SSM_HPG * SSM_HEAD_DIM
    for g in range(SSM_GROUPS):
        bg = bm[:, g * SSM_STATE:(g + 1) * SSM_STATE]
        cg = cm[:, g * SSM_STATE:(g + 1) * SSM_STATE].astype(BF16)
        cb = _dot_nt(cg, bg.astype(BF16))
        st = state_ref[g]
        y_g = _dot(cg, st.astype(BF16)) * decay_out[:, g * gw:(g + 1) * gw]
        state_ref[g] = st * decay_state[:, g * gw:(g + 1) * gw] + \
            _dot(bg.T.astype(BF16), xw[:, g * gw:(g + 1) * gw])
        pieces = []
        for pair in range(SSM_HPG // 2):
            ws = []
            for hh in range(2):
                h = g * SSM_HPG + 2 * pair + hh
                diff = cs[:, h:h + 1] - cst[h:h + 1, :]
                decay = jnp.where(tril, jnp.exp(jnp.where(tril, diff, 0.0)), 0.0)
                ws.append((cb * decay).astype(BF16))
            w_pair = jnp.concatenate(ws, axis=1)
            lo = g * gw + pair * LANES
            slab = xdt[:, lo:lo + LANES]
            zero = jnp.zeros_like(slab)
            x_bd = jnp.concatenate([jnp.where(lane < SSM_HEAD_DIM, slab, zero),
                                    jnp.where(lane >= SSM_HEAD_DIM, slab, zero)], axis=0)
            pieces.append(_dot(w_pair, x_bd))
        y_ref[:, g * gw:(g + 1) * gw] = y_g + jnp.concatenate(pieces, axis=1) + \
            dskip_ref[:, g * gw:(g + 1) * gw] * xs[:, g * gw:(g + 1) * gw]


def _ssd(xbc, dt, dtt, convw, convb, dtb, dtbt, a_log, a_logt, dskip_x, expand, seq):
    b = xbc.shape[0]
    nc = seq // SSM_CHUNK
    const = lambda arr: pl.BlockSpec(arr.shape, lambda i, j: (0,) * arr.ndim)
    return pl.pallas_call(
        _ssd_kernel,
        grid=(b, nc),
        in_specs=[pl.BlockSpec((None, SSM_CHUNK, CONV_DIM), lambda i, j: (i, j, 0)),
                  pl.BlockSpec((None, SSM_CHUNK, SSM_HEADS), lambda i, j: (i, j, 0)),
                  pl.BlockSpec((None, None, SSM_HEADS, SSM_CHUNK), lambda i, j: (i, j, 0, 0)),
                  const(convw), const(convb), const(dtb), const(dtbt), const(a_log), const(a_logt),
                  const(dskip_x), const(expand)],
        out_specs=pl.BlockSpec((None, SSM_CHUNK, SSM_WIDTH), lambda i, j: (i, j, 0)),
        out_shape=jax.ShapeDtypeStruct((b, seq, SSM_WIDTH), F32),
        scratch_shapes=[pltpu.VMEM((SSM_CHUNK + 8, CONV_DIM), F32),
                        pltpu.VMEM((SSM_GROUPS, SSM_STATE, SSM_HPG * SSM_HEAD_DIM), F32)],
        compiler_params=pltpu.CompilerParams(dimension_semantics=("parallel", "arbitrary"),
                                             vmem_limit_bytes=VMEM_LIMIT),
        name="ssd",
    )(xbc, dt, dtt, convw, convb, dtb, dtbt, a_log, a_logt, dskip_x, expand)


def _epilogue_kernel(x_ref, o_ref, y_ref, nw_ref, wz1_ref, wz2_ref, wmg_ref, won_ref, wos_ref, wo_ref,
                     snw_ref, fnw_ref, out_ref):
    x = x_ref[...]
    ms = jnp.mean(x * x, axis=-1, keepdims=True)
    xn = (x * lax.rsqrt(ms + NORM_EPS) * nw_ref[...]).astype(BF16)
    u = (o_ref[...] * _silu(_dot(xn, wz1_ref[...]))).astype(BF16)
    h_nsa = _dot(u, won_ref[...])
    hh = y_ref[...] * _silu(_dot(xn, wz2_ref[...]))
    gw = SSM_WIDTH // SSM_GROUPS
    parts = []
    for g in range(SSM_GROUPS):
        hg = hh[:, g * gw:(g + 1) * gw]
        hg = hg * lax.rsqrt(jnp.mean(hg * hg, axis=-1, keepdims=True) + NORM_EPS)
        parts.append((hg * snw_ref[:, g * gw:(g + 1) * gw]).astype(BF16))
    h_ssm = _dot(jnp.concatenate(parts, axis=1), wos_ref[...])
    gate = _sigmoid(_dot(xn, wmg_ref[...]))
    mix = (gate[:, 0:D_MODEL] * h_nsa + gate[:, D_MODEL:] * h_ssm).astype(BF16)
    r = x + _dot(mix, wo_ref[...])
    ms2 = jnp.mean(r * r, axis=-1, keepdims=True)
    out_ref[...] = r * lax.rsqrt(ms2 + NORM_EPS) * fnw_ref[...]


def _epilogue(x2, o2, y2, nw, wz1, wz2, wmg, won, wos, wo, snw, fnw, tm):
    t, d = x2.shape
    assert t % tm == 0
    const = lambda arr: pl.BlockSpec(arr.shape, lambda i: (0,) * arr.ndim, pipeline_mode=pl.Buffered(1))
    return pl.pallas_call(
        _epilogue_kernel,
        grid=(t // tm,),
        in_specs=[pl.BlockSpec((tm, d), lambda i: (i, 0)),
                  pl.BlockSpec((tm, NSA_WIDTH), lambda i: (i, 0)),
                  pl.BlockSpec((tm, SSM_WIDTH), lambda i: (i, 0)),
                  const(nw), const(wz1), const(wz2), const(wmg), const(won), const(wos), const(wo),
                  const(snw), const(fnw)],
        out_specs=pl.BlockSpec((tm, d), lambda i: (i, 0)),
        out_shape=jax.ShapeDtypeStruct((t, d), F32),
        compiler_params=pltpu.CompilerParams(dimension_semantics=("parallel",),
                                             vmem_limit_bytes=VMEM_LIMIT),
        name="epilogue",
    )(x2, o2, y2, nw, wz1, wz2, wmg, won, wos, wo, snw, fnw)


def _table_kernel(rb_ref, bucket_ref, out_ref):
    bk = bucket_ref[...]
    for h in range(NSA_HEADS):
        acc = jnp.full(bk.shape, NEG, F32)
        for b in range(REL_BUCKETS):
            acc = jnp.where(bk == b, rb_ref[b, h], acc)
        out_ref[:, h * Q_BLOCK:(h + 1) * Q_BLOCK] = acc


def _bias_table(rb, buckets, tr):
    rows = buckets.shape[0]
    assert rows % tr == 0
    return pl.pallas_call(
        _table_kernel,
        grid=(rows // tr,),
        in_specs=[pl.BlockSpec(memory_space=pltpu.SMEM),
                  pl.BlockSpec((tr, Q_BLOCK), lambda i: (i, 0))],
        out_specs=pl.BlockSpec((tr, QCOLS), lambda i: (i, 0)),
        out_shape=jax.ShapeDtypeStruct((rows, QCOLS), F32),
        compiler_params=pltpu.CompilerParams(dimension_semantics=("parallel",)),
        name="bias_table",
    )(rb, buckets)


def _bucket_tables(seq):
    n_qb = seq // Q_BLOCK
    key = np.arange(LANES)[:, None]
    tok = np.arange(Q_BLOCK)[None, :]
    tiles = [_t5_bucket_np(LANES * o + tok - key) for o in range(N_NEAR)]
    tiles.append(np.full((LANES, Q_BLOCK), REL_BUCKETS - 1, np.int32))
    dwin = WINDOW + tok - key
    tiles.append(_t5_bucket_np(np.where(dwin < WINDOW, dwin, -1)))
    tiles.append(np.full((LANES, Q_BLOCK), MASKED_BUCKET, np.int32))
    assert len(tiles) == N_TILES and LANES * N_NEAR - (LANES - 1) >= 790
    n_cmp_rows = seq // CMP_STRIDE
    rows = 8 * (n_qb - 1) + n_cmp_rows
    rows_pad = -(-rows // LANES) * LANES
    u = np.arange(rows_pad)[:, None]
    cmp_tbl = _t5_bucket_np(tok - CMP_STRIDE * u + Q_BLOCK * (n_qb - 1) - (CMP_BLOCK - 1))
    return np.concatenate(tiles, axis=0), cmp_tbl


def _overlap_t(seq):
    n_cmp_rows = seq // CMP_STRIDE
    n_sel = seq // SEL_BLOCK
    c_start = np.arange(n_cmp_rows)[None, :] * CMP_STRIDE
    s_start = np.arange(n_sel)[:, None] * SEL_BLOCK
    ov = (c_start < s_start + SEL_BLOCK) & (c_start + CMP_BLOCK > s_start)
    ov[:, n_cmp_rows - 1] = False
    return jnp.asarray(ov, dtype=BF16)


def _layer(x, norm_w, w_in, cmp_pos_k, cmp_pos_v, cmp_k_w1, cmp_k_b1, cmp_k_w2, cmp_v_w1, cmp_v_b1,
           cmp_v_w2, conv_w, conv_b, dt_bias, a_log, d_skip, ssm_norm_w, w_out_nsa, w_out_ssm, w_out,
           rel_bias, out_norm_w):
    b, s, d = x.shape
    assert d == D_MODEL and s % STEP == 0 and s >= 2 * WINDOW
    t = b * s
    n_qb = s // Q_BLOCK
    cols = _column_offsets()
    wcol = lambda name: w_in[:, cols[name][0]:cols[name][1]]
    x2 = x.reshape(t, d)
    nw = norm_w.reshape(1, d).astype(F32)

    gate_w = wcol('nsa_gate').reshape(d, NSA_HEADS, 3).transpose(0, 2, 1).reshape(d, 3 * NSA_HEADS)
    wn = jnp.concatenate([wcol('k_slc'), wcol('k_swa'), wcol('k_cmp'), wcol('v_cmp'), wcol('xbc'),
                          wcol('dt')], axis=1).astype(BF16)
    wt = jnp.concatenate([wcol('q') * (HEAD_DIM ** -0.5 * LOG2E), wcol('v_slc'), wcol('v_swa'), gate_w,
                          wcol('dt')], axis=1).T.astype(BF16)
    kslc, kswa, cc, xbc, dt, qt, vslct, vswat, gt, dtt = _proj(x2, nw, wn, wt, STEP)
    kslc = kslc.reshape(b, s, KV_WIDTH)
    kswa = kswa.reshape(b, s, KV_WIDTH)
    xbc = xbc.reshape(b, s, CONV_DIM)
    dt = dt.reshape(b, s, SSM_HEADS)
    qt = qt.reshape(b, n_qb, NSA_WIDTH, Q_BLOCK)
    vslct = vslct.reshape(b, n_qb, KV_WIDTH, Q_BLOCK)
    vswat = vswat.reshape(b, n_qb, KV_WIDTH, Q_BLOCK)
    gt = gt.reshape(b, n_qb, 3 * NSA_HEADS, Q_BLOCK)
    dtt = dtt.reshape(b, n_qb, SSM_HEADS, Q_BLOCK)

    nseg = s // CMP_STRIDE

    def seg_major(a):
        a = a.reshape(b, nseg, CMP_STRIDE, NSA_GROUPS, HEAD_DIM)
        return a.transpose(0, 3, 1, 2, 4).reshape(b, NSA_GROUPS, nseg, CMP_STRIDE * HEAD_DIM)

    pos2 = lambda p: p.astype(F32).reshape(2, CMP_STRIDE * HEAD_DIM)
    kc, vct = _compress(seg_major(cc[:, :KV_WIDTH]), seg_major(cc[:, KV_WIDTH:]),
                        pos2(cmp_pos_k), pos2(cmp_pos_v),
                        cmp_k_w1.astype(BF16), cmp_k_b1.reshape(1, -1).astype(F32), cmp_k_w2.astype(BF16),
                        cmp_v_w1.astype(BF16), cmp_v_b1.reshape(1, -1).astype(F32), cmp_v_w2.T.astype(BF16))

    sel_buckets, cmp_buckets = _bucket_tables(s)
    rb = rel_bias.astype(F32) * LOG2E
    tsel = _bias_table(rb, jnp.asarray(sel_buckets), LANES).reshape(N_TILES, LANES, QCOLS)
    tcmp = _bias_table(rb, jnp.asarray(cmp_buckets), LANES)
    o_nsa = _nsa(qt, gt, kc, vct, kslc, vslct, kswa, vswat, tsel, tcmp, _overlap_t(s), s)

    expand = jnp.asarray(np.kron(np.eye(SSM_HEADS), np.ones((1, SSM_HEAD_DIM))), dtype=BF16)
    y = _ssd(xbc, dt, dtt, conv_w.astype(F32), conv_b.reshape(1, -1).astype(F32),
             dt_bias.reshape(1, -1).astype(F32), dt_bias.reshape(-1, 1).astype(F32),
             a_log.reshape(1, -1).astype(F32), a_log.reshape(-1, 1).astype(F32),
             jnp.repeat(d_skip.astype(F32), SSM_HEAD_DIM).reshape(1, -1), expand, s)

    out = _epilogue(x2, o_nsa.reshape(t, NSA_WIDTH), y.reshape(t, SSM_WIDTH), nw,
                    wcol('z_nsa').astype(BF16), wcol('z_ssm').astype(BF16), wcol('merge_gate').astype(BF16),
                    w_out_nsa.astype(BF16), w_out_ssm.astype(BF16), w_out.astype(BF16),
                    ssm_norm_w.reshape(1, -1).astype(F32), out_norm_w.reshape(1, -1).astype(F32),
                    256 if t % 256 == 0 else Q_BLOCK)
    return out.reshape(b, s, d)


def kernel(x, norm_w, w_in, cmp_pos_k, cmp_pos_v, cmp_k_w1, cmp_k_b1, cmp_k_w2, cmp_v_w1, cmp_v_b1, cmp_v_w2,
           conv_w, conv_b, dt_bias, a_log, d_skip, ssm_norm_w, w_out_nsa, w_out_ssm, w_out, rel_bias,
           final_norm_w):
    depth = norm_w.shape[0]
    assert depth == 1, "the epilogue fuses the final norm into the single layer"
    return _layer(x, norm_w[0], w_in[0], cmp_pos_k[0], cmp_pos_v[0], cmp_k_w1[0], cmp_k_b1[0], cmp_k_w2[0],
                  cmp_v_w1[0], cmp_v_b1[0], cmp_v_w2[0], conv_w[0], conv_b[0], dt_bias[0], a_log[0],
                  d_skip[0], ssm_norm_w[0], w_out_nsa[0], w_out_ssm[0], w_out[0], rel_bias, final_norm_w)
```

```python
import functools
import math

import numpy as np
import jax
import jax.numpy as jnp
from jax import lax
from jax.experimental import pallas as pl
from jax.experimental.pallas import tpu as pltpu

F32 = jnp.float32
BF16 = jnp.bfloat16

D_MODEL = 1024
NSA_HEADS = 16
NSA_GROUPS = 4
NSA_HPG = NSA_HEADS // NSA_GROUPS
HEAD_DIM = 64
NSA_WIDTH = NSA_HEADS * HEAD_DIM
KV_WIDTH = NSA_GROUPS * HEAD_DIM
CMP_BLOCK = 32
CMP_STRIDE = 16
CMP_HIDDEN = 4 * HEAD_DIM
SEL_BLOCK = 64
SEL_TOPN = 8
WINDOW = 512
Q_BLOCK = 128
FORCE_SCORE = 1.0e4
REL_BUCKETS = 32
SSM_WIDTH = 2 * D_MODEL
SSM_HEAD_DIM = 64
SSM_HEADS = SSM_WIDTH // SSM_HEAD_DIM
SSM_GROUPS = 4
SSM_HPG = SSM_HEADS // SSM_GROUPS
SSM_STATE = 128
CONV_WIDTH = 4
SSM_CHUNK = 128
CONV_DIM = SSM_WIDTH + 2 * SSM_GROUPS * SSM_STATE
NORM_EPS = 1e-6

NEG = -1e30
LOG2E = math.log2(math.e)
LANES = 128
QCOLS = NSA_HEADS * Q_BLOCK
GCOLS = NSA_HPG * Q_BLOCK
N_NEAR = 8
TILE_FAR = N_NEAR
TILE_WIN_OLD = N_NEAR + 1
TILE_MASKED = N_NEAR + 2
N_TILES = N_NEAR + 3
MASKED_BUCKET = REL_BUCKETS
SUM_ROWS = 16
MAX_EXP2_EXCESS = 100.0
STEP_TILES = 4
STEP = STEP_TILES * LANES
WIN_TILES = WINDOW // LANES + 1
VMEM_LIMIT = 56 * 1024 * 1024


def _column_offsets():
    sizes = (('q', NSA_WIDTH), ('k_cmp', KV_WIDTH), ('v_cmp', KV_WIDTH), ('k_slc', KV_WIDTH),
             ('v_slc', KV_WIDTH), ('k_swa', KV_WIDTH), ('v_swa', KV_WIDTH), ('nsa_gate', 3 * NSA_HEADS),
             ('z_nsa', NSA_WIDTH), ('z_ssm', SSM_WIDTH), ('xbc', CONV_DIM), ('dt', SSM_HEADS),
             ('merge_gate', 2 * D_MODEL))
    out, lo = {}, 0
    for name, n in sizes:
        out[name] = (lo, lo + n)
        lo += n
    return out


def _t5_bucket_np(dist):
    dist = np.asarray(dist, dtype=np.int64)
    d = np.maximum(dist, 0)
    max_exact = REL_BUCKETS // 2
    large = np.full(d.shape, max_exact, dtype=np.int64)
    d8 = d.astype(object) ** 8
    for k in range(1, REL_BUCKETS - max_exact):
        large = large + (d8 >= 2 ** (32 + 3 * k)).astype(np.int64)
    bucket = np.where(d < max_exact, d, np.minimum(large, REL_BUCKETS - 1))
    return np.where(dist < 0, MASKED_BUCKET, bucket).astype(np.int32)


def _sigmoid(x):
    return 1.0 / (1.0 + jnp.exp(-x))


def _silu(x):
    return x * _sigmoid(x)


def _softplus(x):
    return jnp.maximum(x, 0.0) + jnp.log1p(jnp.exp(-jnp.abs(x)))


def _dot(a, b):
    return jnp.dot(a, b, preferred_element_type=F32)


def _dot_nt(a, b):
    return lax.dot_general(a, b, (((1,), (1,)), ((), ())), preferred_element_type=F32)


def _split3(x):
    hi = x.astype(BF16)
    r1 = x - hi.astype(F32)
    mid = r1.astype(BF16)
    lo = (r1 - mid.astype(F32)).astype(BF16)
    return hi, mid, lo


def _dot_exact_lhs(x, w01):
    hi, mid, lo = _split3(x)
    return _dot(hi, w01) + _dot(mid, w01) + _dot(lo, w01)


def _dot_exact_rhs(w01, x):
    hi, mid, lo = _split3(x)
    return _dot(w01, hi) + _dot(w01, mid) + _dot(w01, lo)


_NAT_OUTS = (('k_slc', KV_WIDTH, BF16), ('k_swa', KV_WIDTH, BF16), ('kv_cmp', 2 * KV_WIDTH, F32),
             ('xbc', CONV_DIM, F32), ('dt', SSM_HEADS, F32))
_TR_OUTS = (('q', NSA_WIDTH, BF16), ('v_slc', KV_WIDTH, BF16), ('v_swa', KV_WIDTH, BF16),
            ('gate', 3 * NSA_HEADS, F32), ('dt', SSM_HEADS, F32))


def _proj_kernel(x_ref, nw_ref, wn_ref, wt_ref, *out_refs, n_sub):
    x = x_ref[...]
    ms = jnp.mean(x * x, axis=-1, keepdims=True)
    xn = (x * lax.rsqrt(ms + NORM_EPS) * nw_ref[...]).astype(BF16)
    lo = 0
    for (_, n, _), o_ref in zip(_NAT_OUTS, out_refs[:len(_NAT_OUTS)]):
        o_ref[...] = _dot(xn, wn_ref[:, lo:lo + n]).astype(o_ref.dtype)
        lo += n
    lo = 0
    for (_, n, _), o_ref in zip(_TR_OUTS, out_refs[len(_NAT_OUTS):]):
        res = _dot_nt(wt_ref[lo:lo + n, :], xn)
        for s in range(n_sub):
            o_ref[s] = res[:, s * LANES:(s + 1) * LANES].astype(o_ref.dtype)
        lo += n


def _proj(x2, nw, wn, wt, tm):
    t, d = x2.shape
    assert t % tm == 0 and tm % LANES == 0
    n_sub = tm // LANES
    const = lambda arr: pl.BlockSpec(arr.shape, lambda i: (0,) * arr.ndim, pipeline_mode=pl.Buffered(1))
    out_specs = [pl.BlockSpec((tm, n), lambda i: (i, 0)) for _, n, _ in _NAT_OUTS] + \
                [pl.BlockSpec((n_sub, n, LANES), lambda i: (i, 0, 0)) for _, n, _ in _TR_OUTS]
    out_shape = [jax.ShapeDtypeStruct((t, n), dt) for _, n, dt in _NAT_OUTS] + \
                [jax.ShapeDtypeStruct((t // LANES, n, LANES), dt) for _, n, dt in _TR_OUTS]
    return pl.pallas_call(
        functools.partial(_proj_kernel, n_sub=n_sub),
        grid=(t // tm,),
        in_specs=[pl.BlockSpec((tm, d), lambda i: (i, 0)), const(nw), const(wn), const(wt)],
        out_specs=out_specs,
        out_shape=out_shape,
        compiler_params=pltpu.CompilerParams(dimension_semantics=("parallel",),
                                             vmem_limit_bytes=VMEM_LIMIT),
        name="proj",
    )(x2, nw, wn, wt)


def _compress_kernel(ak_ref, av_ref, posk_ref, posv_ref, w1k_ref, b1k_ref, w2k_ref,
                     w1v_ref, b1v_ref, w2vt_ref, kc_ref, vct_ref):
    half = CMP_STRIDE * HEAD_DIM
    nseg = ak_ref.shape[1]

    def hidden(a, pos_ref, w1_ref, b1_ref):
        p0 = _dot((a + pos_ref[0:1, :]).astype(BF16), w1_ref[0:half, :])
        p1 = _dot((a + pos_ref[1:2, :]).astype(BF16), w1_ref[half:2 * half, :])
        p1 = pltpu.roll(p1, nseg - 1, 0)
        return _silu(p0 + p1 + b1_ref[...]).astype(BF16)

    for g in range(NSA_GROUPS):
        hk = hidden(ak_ref[g], posk_ref, w1k_ref, b1k_ref)
        kc_ref[:, g * HEAD_DIM:(g + 1) * HEAD_DIM] = _dot(hk, w2k_ref[...]).astype(kc_ref.dtype)
        hv = hidden(av_ref[g], posv_ref, w1v_ref, b1v_ref)
        vct_ref[g * HEAD_DIM:(g + 1) * HEAD_DIM, :] = _dot_nt(w2vt_ref[...], hv).astype(vct_ref.dtype)


def _compress(ak, av, posk, posv, w1k, b1k, w2k, w1v, b1v, w2vt):
    b, g, nseg, width = ak.shape
    const2 = lambda shape: pl.BlockSpec(shape, lambda i: (0, 0))
    return pl.pallas_call(
        _compress_kernel,
        grid=(b,),
        in_specs=[pl.BlockSpec((None, g, nseg, width), lambda i: (i, 0, 0, 0)),
                  pl.BlockSpec((None, g, nseg, width), lambda i: (i, 0, 0, 0)),
                  const2(posk.shape), const2(posv.shape),
                  const2(w1k.shape), const2(b1k.shape), const2(w2k.shape),
                  const2(w1v.shape), const2(b1v.shape), const2(w2vt.shape)],
        out_specs=[pl.BlockSpec((None, nseg, KV_WIDTH), lambda i: (i, 0, 0)),
                   pl.BlockSpec((None, KV_WIDTH, nseg), lambda i: (i, 0, 0))],
        out_shape=[jax.ShapeDtypeStruct((b, nseg, KV_WIDTH), BF16),
                   jax.ShapeDtypeStruct((b, KV_WIDTH, nseg), BF16)],
        compiler_params=pltpu.CompilerParams(dimension_semantics=("parallel",),
                                             vmem_limit_bytes=VMEM_LIMIT),
        name="compress",
    )(ak, av, posk, posv, w1k, b1k, w2k, w1v, b1v, w2vt)


def _nsa_kernel(qt_ref, gt_ref, kc_ref, vct_ref, kslc_ref, vslct_ref, kswa_ref, vswat_ref,
                tsel_ref, tcmp_ref, ovt_ref, rep_ref, out_ref,
                qbd_ref, m_ref, acc_ref, msel_ref, ot_ref, s_ref, p_ref, viol_ref, work_ref, chosen_ref,
                *, n_qb, n_cmp_rows, n_sel, top_n):
    qb = pl.program_id(1)
    q0 = qb * Q_BLOCK

    @pl.when(qb == 0)
    def _():
        qbd_ref[...] = jnp.zeros_like(qbd_ref)

    for g in range(NSA_GROUPS):
        for r in range(NSA_HPG):
            h = g * NSA_HPG + r
            qbd_ref[g * HEAD_DIM:(g + 1) * HEAD_DIM, h * Q_BLOCK:(h + 1) * Q_BLOCK] = \
                qt_ref[h * HEAD_DIM:(h + 1) * HEAD_DIM, :]

    def gslice(g):
        return slice(g * GCOLS, (g + 1) * GCOLS)

    def values_with_ones(vt_tiles, g):
        vt = jnp.concatenate([t[g * HEAD_DIM:(g + 1) * HEAD_DIM, :] for t in vt_tiles], axis=1)
        return jnp.concatenate([vt, jnp.ones((SUM_ROWS, vt.shape[1]), BF16)], axis=0)

    def normalized(a):
        return a[0:HEAD_DIM, :] * (1.0 / a[HEAD_DIM:HEAD_DIM + 1, :])

    def self_logit(k_ref):
        kt = k_ref[pl.ds(pl.multiple_of(q0, Q_BLOCK), Q_BLOCK), :].astype(F32).T
        cols = []
        for h in range(NSA_HEADS):
            g = h // NSA_HPG
            prod = qt_ref[h * HEAD_DIM:(h + 1) * HEAD_DIM, :].astype(F32) * kt[g * HEAD_DIM:(g + 1) * HEAD_DIM, :]
            cols.append(jnp.sum(prod, axis=0, keepdims=True))
        return jnp.concatenate(cols, axis=1)

    u0 = pl.multiple_of(8 * (n_qb - 1 - qb), 8)
    j_iota = lax.broadcasted_iota(jnp.int32, (n_sel, Q_BLOCK), 0)
    j_f32 = j_iota.astype(F32)
    t_pos = q0 + lax.broadcasted_iota(jnp.int32, (n_sel, Q_BLOCK), 1)
    cur = t_pos // SEL_BLOCK
    valid = j_iota <= cur
    forced = (j_iota == 0) | (j_iota == cur) | (j_iota == cur - 1)

    logits = [_dot(kc_ref[...], qbd_ref[:, gslice(g)]) for g in range(NSA_GROUPS)]
    probs, psums = [], []
    for g in range(NSA_GROUPS):
        s = logits[g] + tcmp_ref[pl.ds(u0, n_cmp_rows), gslice(g)]
        m = jnp.maximum(jnp.max(s, axis=0, keepdims=True), 0.1 * NEG)
        e = jnp.exp2(s - m)
        l = jnp.sum(e, axis=0, keepdims=True)
        p = e * (1.0 / jnp.where(l == 0.0, 1.0, l))
        probs.append(p.astype(BF16))
        psum = p[:, 0:Q_BLOCK]
        for r in range(1, NSA_HPG):
            psum = psum + p[:, r * Q_BLOCK:(r + 1) * Q_BLOCK]
        psums.append(psum.astype(BF16))
    o_cmp = [_dot(vct_ref[g * HEAD_DIM:(g + 1) * HEAD_DIM, :], probs[g]) for g in range(NSA_GROUPS)]
    imps = [_dot(ovt_ref[...], psums[g]) for g in range(NSA_GROUPS)]

    quota = top_n - (1 + (cur >= 1).astype(jnp.int32) + (cur >= 2).astype(jnp.int32))
    for g in range(NSA_GROUPS):
        work_ref[g] = jnp.where(forced, -2.0, jnp.where(valid, imps[g], -1.0))
        chosen_ref[g] = jnp.where(forced, 1.0, 0.0)

    def pick_round(index, limited):
        for g in range(NSA_GROUPS):
            work = work_ref[g]
            best = jnp.max(work, axis=0, keepdims=True)
            first = jnp.min(jnp.where(work == best, j_f32, float(n_sel)), axis=0, keepdims=True)
            hit = j_f32 == first
            if limited:
                hit = hit & (quota > index)
            chosen_ref[g] = jnp.where(hit, 1.0, chosen_ref[g])
            work_ref[g] = jnp.where(hit, -2.0, work)

    max_forced = 3
    for index in range(top_n - max_forced):
        pick_round(index, False)

    @pl.when(q0 < (max_forced - 1) * SEL_BLOCK)
    def _():
        for index in range(top_n - max_forced, top_n - 1):
            pick_round(index, True)

    for g in range(NSA_GROUPS):
        addm = jnp.where((chosen_ref[g] > 0.5) & valid, 0.0, NEG).astype(BF16)
        msel_ref[g] = _dot(rep_ref[...], addm).reshape(n_sel, 8, Q_BLOCK)

    blocks_per_step = STEP // SEL_BLOCK
    blocks_per_tile = LANES // SEL_BLOCK

    def sel_step(r, near, exact):
        ks = kslc_ref[pl.ds(pl.multiple_of(r * STEP, STEP), STEP), :]
        vt_tiles = [vslct_ref[STEP_TILES * r + t] for t in range(STEP_TILES)]
        if near:
            tile_ids = []
            for t in range(STEP_TILES):
                i = qb - (STEP_TILES * r + t)
                tile_ids.append(jnp.where(i < 0, TILE_MASKED, jnp.minimum(i, TILE_FAR)))
        logits = [_dot(ks, qbd_ref[:, gslice(g)]) for g in range(NSA_GROUPS)]
        alphas = []
        for g in range(NSA_GROUPS):
            m_old = m_ref[:, gslice(g)]
            if not near:
                far_row = tsel_ref[TILE_FAR, 0:8, gslice(g)]
            m8 = None
            for blk in range(blocks_per_step):
                rows = slice(blk * SEL_BLOCK, (blk + 1) * SEL_BLOCK)
                row = msel_ref[g, blocks_per_step * r + blk]
                row = jnp.concatenate([row] * NSA_HPG, axis=1)
                v = logits[g][rows, :]
                if near:
                    t, off = divmod(blk, blocks_per_tile)
                    v = v + tsel_ref[tile_ids[t], off * SEL_BLOCK:(off + 1) * SEL_BLOCK, gslice(g)]
                else:
                    row = row + far_row
                v = v + jnp.concatenate([row] * (SEL_BLOCK // 8), axis=0)
                vm = jnp.max(v.reshape(SEL_BLOCK // 8, 8, GCOLS), axis=0)
                m8 = vm if m8 is None else jnp.maximum(m8, vm)
                if exact:
                    s_ref[rows, gslice(g)] = v
                else:
                    p_ref[rows, gslice(g)] = jnp.exp2(v - m_old).astype(BF16)
            step_max = jnp.max(m8, axis=0, keepdims=True)
            if exact:
                m_new = jnp.maximum(m_old, step_max)
                alphas.append(jnp.exp2(m_old - m_new))
                m_ref[:, gslice(g)] = m_new
                for blk in range(blocks_per_step):
                    rows = slice(blk * SEL_BLOCK, (blk + 1) * SEL_BLOCK)
                    p_ref[rows, gslice(g)] = jnp.exp2(s_ref[rows, gslice(g)] - m_new).astype(BF16)
            else:
                viol_ref[:, gslice(g)] = jnp.maximum(viol_ref[:, gslice(g)], step_max - m_old)
        for g in range(NSA_GROUPS):
            pv = _dot(values_with_ones(vt_tiles, g), p_ref[0:STEP, gslice(g)])
            if exact:
                acc_ref[g] = acc_ref[g] * alphas[g] + pv
            else:
                acc_ref[g] = acc_ref[g] + pv

    r_diag = qb // STEP_TILES
    n_far = jnp.maximum((qb - (N_NEAR - 1)) // STEP_TILES, 0)

    def selected(exact):
        if exact:
            m_ref[...] = jnp.full(m_ref.shape, NEG, F32)
        else:
            m_ref[...] = self_logit(kslc_ref)
        acc_ref[...] = jnp.zeros_like(acc_ref)

        def near_body(i, carry):
            sel_step(r_diag - i, True, exact)
            return carry

        def far_body(r, carry):
            sel_step(r, False, exact)
            return carry

        lax.fori_loop(0, r_diag - n_far + 1, near_body, 0)
        lax.fori_loop(0, n_far, far_body, 0)

    viol_ref[...] = jnp.zeros_like(viol_ref)
    selected(False)

    @pl.when(jnp.max(viol_ref[...]) > MAX_EXP2_EXCESS)
    def _():
        selected(True)

    o_slc = [normalized(acc_ref[g]) for g in range(NSA_GROUPS)]

    first_tile = jnp.maximum(qb - (WIN_TILES - 1), 0)
    kw = kswa_ref[pl.ds(pl.multiple_of(first_tile * LANES, LANES), WIN_TILES * LANES), :]
    vt_tiles = [vswat_ref[first_tile + t] for t in range(WIN_TILES)]
    win_ids = []
    for t in range(WIN_TILES):
        i = qb - (first_tile + t)
        win_ids.append(jnp.where(i < 0, TILE_MASKED, jnp.where(i == WIN_TILES - 1, TILE_WIN_OLD, i)))
    def window(exact):
        logits = [_dot(kw, qbd_ref[:, gslice(g)]) for g in range(NSA_GROUPS)]
        if not exact:
            m_fix = self_logit(kswa_ref)
        for g in range(NSA_GROUPS):
            m8 = None
            for t in range(WIN_TILES):
                rows = slice(t * LANES, (t + 1) * LANES)
                v = logits[g][rows, :] + tsel_ref[win_ids[t], :, gslice(g)]
                vm = jnp.max(v.reshape(LANES // 8, 8, GCOLS), axis=0)
                m8 = vm if m8 is None else jnp.maximum(m8, vm)
                if exact:
                    s_ref[rows, gslice(g)] = v
                else:
                    p_ref[rows, gslice(g)] = jnp.exp2(v - m_fix[:, gslice(g)]).astype(BF16)
            m = jnp.max(m8, axis=0, keepdims=True)
            if exact:
                p_ref[:, gslice(g)] = jnp.exp2(s_ref[:, gslice(g)] - m).astype(BF16)
            else:
                viol_ref[:, gslice(g)] = m - m_fix[:, gslice(g)]
        for g in range(NSA_GROUPS):
            acc_ref[g] = _dot(values_with_ones(vt_tiles, g), p_ref[:, gslice(g)])

    window(False)

    @pl.when(jnp.max(viol_ref[...]) > MAX_EXP2_EXCESS)
    def _():
        window(True)

    o_swa = [normalized(acc_ref[g]) for g in range(NSA_GROUPS)]

    gates = _sigmoid(gt_ref[...])
    for g in range(NSA_GROUPS):
        def gate_row(c):
            return jnp.concatenate(
                [gates[c * NSA_HEADS + g * NSA_HPG + r:c * NSA_HEADS + g * NSA_HPG + r + 1, :]
                 for r in range(NSA_HPG)], axis=1)
        og = gate_row(0) * o_cmp[g] + gate_row(1) * o_slc[g] + gate_row(2) * o_swa[g]
        for r in range(NSA_HPG):
            h = g * NSA_HPG + r
            ot_ref[h * HEAD_DIM:(h + 1) * HEAD_DIM, :] = og[:, r * Q_BLOCK:(r + 1) * Q_BLOCK]
    out_ref[...] = ot_ref[...].T


def _nsa(qt, gt, kc, vct, kslc, vslct, kswa, vswat, tsel, tcmp, ovt, rep, seq):
    b = kc.shape[0]
    n_qb = seq // Q_BLOCK
    assert n_qb % STEP_TILES == 0 and n_qb >= WIN_TILES and WIN_TILES >= STEP_TILES
    n_cmp_rows = kc.shape[1]
    n_sel = seq // SEL_BLOCK
    top_n = min(SEL_TOPN, n_sel)
    assert top_n > 3
    n_gate = gt.shape[2]
    const = lambda arr: pl.BlockSpec(arr.shape, lambda i, j: (0,) * arr.ndim,
                                     pipeline_mode=pl.Buffered(1))
    kern = functools.partial(_nsa_kernel, n_qb=n_qb, n_cmp_rows=n_cmp_rows, n_sel=n_sel, top_n=top_n)
    return pl.pallas_call(
        kern,
        grid=(b, n_qb),
        in_specs=[pl.BlockSpec((None, None, NSA_WIDTH, Q_BLOCK), lambda i, j: (i, j, 0, 0)),
                  pl.BlockSpec((None, None, n_gate, Q_BLOCK), lambda i, j: (i, j, 0, 0)),
                  pl.BlockSpec((None, n_cmp_rows, KV_WIDTH), lambda i, j: (i, 0, 0)),
                  pl.BlockSpec((None, KV_WIDTH, n_cmp_rows), lambda i, j: (i, 0, 0)),
                  pl.BlockSpec((None, seq, KV_WIDTH), lambda i, j: (i, 0, 0)),
                  pl.BlockSpec((None, n_qb, KV_WIDTH, Q_BLOCK), lambda i, j: (i, 0, 0, 0)),
                  pl.BlockSpec((None, seq, KV_WIDTH), lambda i, j: (i, 0, 0)),
                  pl.BlockSpec((None, n_qb, KV_WIDTH, Q_BLOCK), lambda i, j: (i, 0, 0, 0)),
                  const(tsel), const(tcmp), const(ovt), const(rep)],
        out_specs=pl.BlockSpec((None, Q_BLOCK, NSA_WIDTH), lambda i, j: (i, j, 0)),
        out_shape=jax.ShapeDtypeStruct((b, seq, NSA_WIDTH), F32),
        scratch_shapes=[pltpu.VMEM((KV_WIDTH, QCOLS), BF16),
                        pltpu.VMEM((1, QCOLS), F32),
                        pltpu.VMEM((NSA_GROUPS, HEAD_DIM + SUM_ROWS, GCOLS), F32),
                        pltpu.VMEM((NSA_GROUPS, n_sel, 8, Q_BLOCK), F32),
                        pltpu.VMEM((NSA_WIDTH, Q_BLOCK), F32),
                        pltpu.VMEM((WIN_TILES * LANES, QCOLS), F32),
                        pltpu.VMEM((WIN_TILES * LANES, QCOLS), BF16),
                        pltpu.VMEM((1, QCOLS), F32),
                        pltpu.VMEM((NSA_GROUPS, n_sel, Q_BLOCK), F32),
                        pltpu.VMEM((NSA_GROUPS, n_sel, Q_BLOCK), F32)],
        compiler_params=pltpu.CompilerParams(dimension_semantics=("parallel", "arbitrary"),
                                             vmem_limit_bytes=VMEM_LIMIT),
        name="nsa",
    )(qt, gt, kc, vct, kslc, vslct, kswa, vswat, tsel, tcmp, ovt, rep)


def _ssd_kernel(xbc_ref, dt_ref, dtt_ref, convw_ref, convb_ref, dtb_ref, dtbt_ref, a_ref, at_ref,
                dskip_ref, expand_ref, y_ref, xpad_ref, state_ref):
    L = SSM_CHUNK
    c = pl.program_id(1)

    @pl.when(c == 0)
    def _():
        xpad_ref[0:8, :] = jnp.zeros((8, CONV_DIM), F32)
        state_ref[...] = jnp.zeros_like(state_ref)

    xpad_ref[8:8 + L, :] = xbc_ref[...]
    conv = convb_ref[...]
    for k in range(CONV_WIDTH):
        conv = conv + convw_ref[k:k + 1, :] * xpad_ref[pl.ds(8 - (CONV_WIDTH - 1) + k, L), :]
    xpad_ref[0:8, :] = xpad_ref[L:L + 8, :]
    xbc = _silu(conv)
    xs = xbc[:, 0:SSM_WIDTH]
    bm = xbc[:, SSM_WIDTH:SSM_WIDTH + SSM_GROUPS * SSM_STATE]
    cm = xbc[:, SSM_WIDTH + SSM_GROUPS * SSM_STATE:]

    dt = _softplus(dt_ref[...] + dtb_ref[...])
    dtt = _softplus(dtt_ref[...] + dtbt_ref[...])
    a_row = -jnp.exp(a_ref[...])
    a_col = -jnp.exp(at_ref[...])
    row_i = lax.broadcasted_iota(jnp.int32, (L, L), 0)
    col_j = lax.broadcasted_iota(jnp.int32, (L, L), 1)
    tril = row_i >= col_j
    tril01 = jnp.where(tril, 1.0, 0.0).astype(BF16)
    triu01 = jnp.where(row_i <= col_j, 1.0, 0.0).astype(BF16)
    cs = _dot_exact_rhs(tril01, dt * a_row)
    cst = _dot_exact_lhs(dtt * a_col, triu01)

    expand = expand_ref[...]
    cs_x = _dot_exact_lhs(cs, expand)
    dt_x = _dot_exact_lhs(dt, expand)
    total_x = cs_x[L - 1:L, :]
    xdt = (xs * dt_x).astype(BF16)
    xw = (xs * (jnp.exp(total_x - cs_x) * dt_x)).astype(BF16)
    decay_out = jnp.exp(cs_x)
    decay_state = jnp.exp(total_x)

    lane = lax.broadcasted_iota(jnp.int32, (L, LANES), 1)
    gw = SSM_HPG * SSM_HEAD_DIM
    for g in range(SSM_GROUPS):
        bg = bm[:, g * SSM_STATE:(g + 1) * SSM_STATE]
        cg = cm[:, g * SSM_STATE:(g + 1) * SSM_STATE].astype(BF16)
        cb = _dot_nt(cg, bg.astype(BF16))
        st = state_ref[g]
        y_g = _dot(cg, st.astype(BF16)) * decay_out[:, g * gw:(g + 1) * gw]
        state_ref[g] = st * decay_state[:, g * gw:(g + 1) * gw] + \
            _dot(bg.T.astype(BF16), xw[:, g * gw:(g + 1) * gw])
        pieces = []
        for pair in range(SSM_HPG // 2):
            ws = []
            for hh in range(2):
                h = g * SSM_HPG + 2 * pair + hh
                diff = cs[:, h:h + 1] - cst[h:h + 1, :]
                decay = jnp.where(tril, jnp.exp(jnp.where(tril, diff, 0.0)), 0.0)
                ws.append((cb * decay).astype(BF16))
            w_pair = jnp.concatenate(ws, axis=1)
            lo = g * gw + pair * LANES
            slab = xdt[:, lo:lo + LANES]
            zero = jnp.zeros_like(slab)
            x_bd = jnp.concatenate([jnp.where(lane < SSM_HEAD_DIM, slab, zero),
                                    jnp.where(lane >= SSM_HEAD_DIM, slab, zero)], axis=0)
            pieces.append(_dot(w_pair, x_bd))
        y_ref[:, g * gw:(g + 1) * gw] = y_g + jnp.concatenate(pieces, axis=1) + \
            dskip_ref[:, g * gw:(g + 1) * gw] * xs[:, g * gw:(g + 1) * gw]


def _ssd(xbc, dt, dtt, convw, convb, dtb, dtbt, a_log, a_logt, dskip_x, expand, seq):
    b = xbc.shape[0]
    nc = seq // SSM_CHUNK
    const = lambda arr: pl.BlockSpec(arr.shape, lambda i, j: (0,) * arr.ndim)
    return pl.pallas_call(
        _ssd_kernel,
        grid=(b, nc),
        in_specs=[pl.BlockSpec((None, SSM_CHUNK, CONV_DIM), lambda i, j: (i, j, 0)),
                  pl.BlockSpec((None, SSM_CHUNK, SSM_HEADS), lambda i, j: (i, j, 0)),
                  pl.BlockSpec((None, None, SSM_HEADS, SSM_CHUNK), lambda i, j: (i, j, 0, 0)),
                  const(convw), const(convb), const(dtb), const(dtbt), const(a_log), const(a_logt),
                  const(dskip_x), const(expand)],
        out_specs=pl.BlockSpec((None, SSM_CHUNK, SSM_WIDTH), lambda i, j: (i, j, 0)),
        out_shape=jax.ShapeDtypeStruct((b, seq, SSM_WIDTH), F32),
        scratch_shapes=[pltpu.VMEM((SSM_CHUNK + 8, CONV_DIM), F32),
                        pltpu.VMEM((SSM_GROUPS, SSM_STATE, SSM_HPG * SSM_HEAD_DIM), F32)],
        compiler_params=pltpu.CompilerParams(dimension_semantics=("parallel", "arbitrary"),
                                             vmem_limit_bytes=VMEM_LIMIT),
        name="ssd",
    )(xbc, dt, dtt, convw, convb, dtb, dtbt, a_log, a_logt, dskip_x, expand)


def _epilogue_kernel(x_ref, o_ref, y_ref, nw_ref, wz1_ref, wz2_ref, wmg_ref, won_ref, wos_ref, wo_ref,
                     snw_ref, fnw_ref, out_ref):
    x = x_ref[...]
    ms = jnp.mean(x * x, axis=-1, keepdims=True)
    xn = (x * lax.rsqrt(ms + NORM_EPS) * nw_ref[...]).astype(BF16)
    u = (o_ref[...] * _silu(_dot(xn, wz1_ref[...]))).astype(BF16)
    h_nsa = _dot(u, won_ref[...])
    hh = y_ref[...] * _silu(_dot(xn, wz2_ref[...]))
    gw = SSM_WIDTH // SSM_GROUPS
    parts = []
    for g in range(SSM_GROUPS):
        hg = hh[:, g * gw:(g + 1) * gw]
        hg = hg * lax.rsqrt(jnp.mean(hg * hg, axis=-1, keepdims=True) + NORM_EPS)
        parts.append((hg * snw_ref[:, g * gw:(g + 1) * gw]).astype(BF16))
    h_ssm = _dot(jnp.concatenate(parts, axis=1), wos_ref[...])
    gate = _sigmoid(_dot(xn, wmg_ref[...]))
    mix = (gate[:, 0:D_MODEL] * h_nsa + gate[:, D_MODEL:] * h_ssm).astype(BF16)
    r = x + _dot(mix, wo_ref[...])
    ms2 = jnp.mean(r * r, axis=-1, keepdims=True)
    out_ref[...] = r * lax.rsqrt(ms2 + NORM_EPS) * fnw_ref[...]


def _epilogue(x2, o2, y2, nw, wz1, wz2, wmg, won, wos, wo, snw, fnw, tm):
    t, d = x2.shape
    assert t % tm == 0
    const = lambda arr: pl.BlockSpec(arr.shape, lambda i: (0,) * arr.ndim, pipeline_mode=pl.Buffered(1))
    return pl.pallas_call(
        _epilogue_kernel,
        grid=(t // tm,),
        in_specs=[pl.BlockSpec((tm, d), lambda i: (i, 0)),
                  pl.BlockSpec((tm, NSA_WIDTH), lambda i: (i, 0)),
                  pl.BlockSpec((tm, SSM_WIDTH), lambda i: (i, 0)),
                  const(nw), const(wz1), const(wz2), const(wmg), const(won), const(wos), const(wo),
                  const(snw), const(fnw)],
        out_specs=pl.BlockSpec((tm, d), lambda i: (i, 0)),
        out_shape=jax.ShapeDtypeStruct((t, d), F32),
        compiler_params=pltpu.CompilerParams(dimension_semantics=("parallel",),
                                             vmem_limit_bytes=VMEM_LIMIT),
        name="epilogue",
    )(x2, o2, y2, nw, wz1, wz2, wmg, won, wos, wo, snw, fnw)


def _table_kernel(rb_ref, bucket_ref, out_ref):
    bk = bucket_ref[...]
    for h in range(NSA_HEADS):
        acc = jnp.full(bk.shape, NEG, F32)
        for b in range(REL_BUCKETS):
            acc = jnp.where(bk == b, rb_ref[b, h], acc)
        out_ref[:, h * Q_BLOCK:(h + 1) * Q_BLOCK] = acc


def _bias_table(rb, buckets, tr):
    rows = buckets.shape[0]
    assert rows % tr == 0
    return pl.pallas_call(
        _table_kernel,
        grid=(rows // tr,),
        in_specs=[pl.BlockSpec(memory_space=pltpu.SMEM),
                  pl.BlockSpec((tr, Q_BLOCK), lambda i: (i, 0))],
        out_specs=pl.BlockSpec((tr, QCOLS), lambda i: (i, 0)),
        out_shape=jax.ShapeDtypeStruct((rows, QCOLS), F32),
        compiler_params=pltpu.CompilerParams(dimension_semantics=("parallel",)),
        name="bias_table",
    )(rb, buckets)


def _bucket_tables(seq):
    n_qb = seq // Q_BLOCK
    key = np.arange(LANES)[:, None]
    tok = np.arange(Q_BLOCK)[None, :]
    tiles = [_t5_bucket_np(LANES * o + tok - key) for o in range(N_NEAR)]
    tiles.append(np.full((LANES, Q_BLOCK), REL_BUCKETS - 1, np.int32))
    dwin = WINDOW + tok - key
    tiles.append(_t5_bucket_np(np.where(dwin < WINDOW, dwin, -1)))
    tiles.append(np.full((LANES, Q_BLOCK), MASKED_BUCKET, np.int32))
    assert len(tiles) == N_TILES and LANES * N_NEAR - (LANES - 1) >= 790
    n_cmp_rows = seq // CMP_STRIDE
    rows = 8 * (n_qb - 1) + n_cmp_rows
    rows_pad = -(-rows // LANES) * LANES
    u = np.arange(rows_pad)[:, None]
    cmp_tbl = _t5_bucket_np(tok - CMP_STRIDE * u + Q_BLOCK * (n_qb - 1) - (CMP_BLOCK - 1))
    return np.concatenate(tiles, axis=0), cmp_tbl


def _overlap_t(seq):
    n_cmp_rows = seq // CMP_STRIDE
    n_sel = seq // SEL_BLOCK
    c_start = np.arange(n_cmp_rows)[None, :] * CMP_STRIDE
    s_start = np.arange(n_sel)[:, None] * SEL_BLOCK
    ov = (c_start < s_start + SEL_BLOCK) & (c_start + CMP_BLOCK > s_start)
    ov[:, n_cmp_rows - 1] = False
    return jnp.asarray(ov, dtype=BF16)


def _layer(x, norm_w, w_in, cmp_pos_k, cmp_pos_v, cmp_k_w1, cmp_k_b1, cmp_k_w2, cmp_v_w1, cmp_v_b1,
           cmp_v_w2, conv_w, conv_b, dt_bias, a_log, d_skip, ssm_norm_w, w_out_nsa, w_out_ssm, w_out,
           rel_bias, out_norm_w):
    b, s, d = x.shape
    assert d == D_MODEL and s % STEP == 0 and s >= 2 * WINDOW
    t = b * s
    n_qb = s // Q_BLOCK
    cols = _column_offsets()
    wcol = lambda name: w_in[:, cols[name][0]:cols[name][1]]
    x2 = x.reshape(t, d)
    nw = norm_w.reshape(1, d).astype(F32)

    gate_w = wcol('nsa_gate').reshape(d, NSA_HEADS, 3).transpose(0, 2, 1).reshape(d, 3 * NSA_HEADS)
    wn = jnp.concatenate([wcol('k_slc'), wcol('k_swa'), wcol('k_cmp'), wcol('v_cmp'), wcol('xbc'),
                          wcol('dt')], axis=1).astype(BF16)
    wt = jnp.concatenate([wcol('q') * (HEAD_DIM ** -0.5 * LOG2E), wcol('v_slc'), wcol('v_swa'), gate_w,
                          wcol('dt')], axis=1).T.astype(BF16)
    kslc, kswa, cc, xbc, dt, qt, vslct, vswat, gt, dtt = _proj(x2, nw, wn, wt, STEP)
    kslc = kslc.reshape(b, s, KV_WIDTH)
    kswa = kswa.reshape(b, s, KV_WIDTH)
    xbc = xbc.reshape(b, s, CONV_DIM)
    dt = dt.reshape(b, s, SSM_HEADS)
    qt = qt.reshape(b, n_qb, NSA_WIDTH, Q_BLOCK)
    vslct = vslct.reshape(b, n_qb, KV_WIDTH, Q_BLOCK)
    vswat = vswat.reshape(b, n_qb, KV_WIDTH, Q_BLOCK)
    gt = gt.reshape(b, n_qb, 3 * NSA_HEADS, Q_BLOCK)
    dtt = dtt.reshape(b, n_qb, SSM_HEADS, Q_BLOCK)

    nseg = s // CMP_STRIDE

    def seg_major(a):
        a = a.reshape(b, nseg, CMP_STRIDE, NSA_GROUPS, HEAD_DIM)
        return a.transpose(0, 3, 1, 2, 4).reshape(b, NSA_GROUPS, nseg, CMP_STRIDE * HEAD_DIM)

    pos2 = lambda p: p.astype(F32).reshape(2, CMP_STRIDE * HEAD_DIM)
    kc, vct = _compress(seg_major(cc[:, :KV_WIDTH]), seg_major(cc[:, KV_WIDTH:]),
                        pos2(cmp_pos_k), pos2(cmp_pos_v),
                        cmp_k_w1.astype(BF16), cmp_k_b1.reshape(1, -1).astype(F32), cmp_k_w2.astype(BF16),
                        cmp_v_w1.astype(BF16), cmp_v_b1.reshape(1, -1).astype(F32), cmp_v_w2.T.astype(BF16))

    sel_buckets, cmp_buckets = _bucket_tables(s)
    rb = rel_bias.astype(F32) * LOG2E
    tsel = _bias_table(rb, jnp.asarray(sel_buckets), LANES).reshape(N_TILES, LANES, QCOLS)
    tcmp = _bias_table(rb, jnp.asarray(cmp_buckets), LANES)
    rep = jnp.asarray(np.kron(np.eye(s // SEL_BLOCK), np.ones((8, 1))), dtype=BF16)
    o_nsa = _nsa(qt, gt, kc, vct, kslc, vslct, kswa, vswat, tsel, tcmp, _overlap_t(s), rep, s)

    expand = jnp.asarray(np.kron(np.eye(SSM_HEADS), np.ones((1, SSM_HEAD_DIM))), dtype=BF16)
    y = _ssd(xbc, dt, dtt, conv_w.astype(F32), conv_b.reshape(1, -1).astype(F32),
             dt_bias.reshape(1, -1).astype(F32), dt_bias.reshape(-1, 1).astype(F32),
             a_log.reshape(1, -1).astype(F32), a_log.reshape(-1, 1).astype(F32),
             jnp.repeat(d_skip.astype(F32), SSM_HEAD_DIM).reshape(1, -1), expand, s)

    out = _epilogue(x2, o_nsa.reshape(t, NSA_WIDTH), y.reshape(t, SSM_WIDTH), nw,
                    wcol('z_nsa').astype(BF16), wcol('z_ssm').astype(BF16), wcol('merge_gate').astype(BF16),
                    w_out_nsa.astype(BF16), w_out_ssm.astype(BF16), w_out.astype(BF16),
                    ssm_norm_w.reshape(1, -1).astype(F32), out_norm_w.reshape(1, -1).astype(F32),
                    256 if t % 256 == 0 else Q_BLOCK)
    return out.reshape(b, s, d)


def kernel(x, norm_w, w_in, cmp_pos_k, cmp_pos_v, cmp_k_w1, cmp_k_b1, cmp_k_w2, cmp_v_w1, cmp_v_b1, cmp_v_w2,
           conv_w, conv_b, dt_bias, a_log, d_skip, ssm_norm_w, w_out_nsa, w_out_ssm, w_out, rel_bias,
           final_norm_w):
    depth = norm_w.shape[0]
    assert depth == 1, "the epilogue fuses the final norm into the single layer"
    return _layer(x, norm_w[0], w_in[0], cmp_pos_k[0], cmp_pos_v[0], cmp_k_w1[0], cmp_k_b1[0], cmp_k_w2[0],
                  cmp_v_w1[0], cmp_v_b1[0], cmp_v_w2[0], conv_w[0], conv_b[0], dt_bias[0], a_log[0],
                  d_skip[0], ssm_norm_w[0], w_out_nsa[0], w_out_ssm[0], w_out[0], rel_bias, final_norm_w)
```

```python
import functools
import math

import numpy as np
import jax
import jax.numpy as jnp
from jax import lax
from jax.experimental import pallas as pl
from jax.experimental.pallas import tpu as pltpu

F32 = jnp.float32
BF16 = jnp.bfloat16

D_MODEL = 1024
NSA_HEADS = 16
NSA_GROUPS = 4
NSA_HPG = NSA_HEADS // NSA_GROUPS
HEAD_DIM = 64
NSA_WIDTH = NSA_HEADS * HEAD_DIM
KV_WIDTH = NSA_GROUPS * HEAD_DIM
CMP_BLOCK = 32
CMP_STRIDE = 16
CMP_HIDDEN = 4 * HEAD_DIM
SEL_BLOCK = 64
SEL_TOPN = 8
WINDOW = 512
Q_BLOCK = 128
FORCE_SCORE = 1.0e4
REL_BUCKETS = 32
SSM_WIDTH = 2 * D_MODEL
SSM_HEAD_DIM = 64
SSM_HEADS = SSM_WIDTH // SSM_HEAD_DIM
SSM_GROUPS = 4
SSM_HPG = SSM_HEADS // SSM_GROUPS
SSM_STATE = 128
CONV_WIDTH = 4
SSM_CHUNK = 128
CONV_DIM = SSM_WIDTH + 2 * SSM_GROUPS * SSM_STATE
NORM_EPS = 1e-6

NEG = -1e30
LOG2E = math.log2(math.e)
LANES = 128
QCOLS = NSA_HEADS * Q_BLOCK
GCOLS = NSA_HPG * Q_BLOCK
AUG_DIM = 2 * HEAD_DIM
N_NEAR = 8
TILE_FAR = N_NEAR
TILE_WIN_OLD = N_NEAR + 1
TILE_MASKED = N_NEAR + 2
N_TILES = N_NEAR + 3
MASKED_BUCKET = REL_BUCKETS
SUM_ROWS = 16
MAX_EXP2_EXCESS = 100.0
STEP_TILES = 4
STEP = STEP_TILES * LANES
WIN_TILES = WINDOW // LANES + 1
VMEM_LIMIT = 56 * 1024 * 1024


def _column_offsets():
    sizes = (('q', NSA_WIDTH), ('k_cmp', KV_WIDTH), ('v_cmp', KV_WIDTH), ('k_slc', KV_WIDTH),
             ('v_slc', KV_WIDTH), ('k_swa', KV_WIDTH), ('v_swa', KV_WIDTH), ('nsa_gate', 3 * NSA_HEADS),
             ('z_nsa', NSA_WIDTH), ('z_ssm', SSM_WIDTH), ('xbc', CONV_DIM), ('dt', SSM_HEADS),
             ('merge_gate', 2 * D_MODEL))
    out, lo = {}, 0
    for name, n in sizes:
        out[name] = (lo, lo + n)
        lo += n
    return out


def _t5_bucket_np(dist):
    dist = np.asarray(dist, dtype=np.int64)
    d = np.maximum(dist, 0)
    max_exact = REL_BUCKETS // 2
    large = np.full(d.shape, max_exact, dtype=np.int64)
    d8 = d.astype(object) ** 8
    for k in range(1, REL_BUCKETS - max_exact):
        large = large + (d8 >= 2 ** (32 + 3 * k)).astype(np.int64)
    bucket = np.where(d < max_exact, d, np.minimum(large, REL_BUCKETS - 1))
    return np.where(dist < 0, MASKED_BUCKET, bucket).astype(np.int32)


def _sigmoid(x):
    return 1.0 / (1.0 + jnp.exp(-x))


def _silu(x):
    return x * _sigmoid(x)


def _softplus(x):
    return jnp.maximum(x, 0.0) + jnp.log1p(jnp.exp(-jnp.abs(x)))


def _dot(a, b):
    return jnp.dot(a, b, preferred_element_type=F32)


def _dot_nt(a, b):
    return lax.dot_general(a, b, (((1,), (1,)), ((), ())), preferred_element_type=F32)


def _split3(x):
    hi = x.astype(BF16)
    r1 = x - hi.astype(F32)
    mid = r1.astype(BF16)
    lo = (r1 - mid.astype(F32)).astype(BF16)
    return hi, mid, lo


def _dot_exact_lhs(x, w01):
    hi, mid, lo = _split3(x)
    return _dot(hi, w01) + _dot(mid, w01) + _dot(lo, w01)


def _dot_exact_rhs(w01, x):
    hi, mid, lo = _split3(x)
    return _dot(w01, hi) + _dot(w01, mid) + _dot(w01, lo)


_NAT_OUTS = (('k_slc', NSA_GROUPS * AUG_DIM, BF16), ('k_swa', KV_WIDTH, BF16), ('kv_cmp', 2 * KV_WIDTH, F32),
             ('xbc', CONV_DIM, F32), ('dt', SSM_HEADS, F32))
_TR_OUTS = (('q', NSA_WIDTH, BF16), ('v_slc', KV_WIDTH, BF16), ('v_swa', KV_WIDTH, BF16),
            ('gate', 3 * NSA_HEADS, F32), ('dt', SSM_HEADS, F32))


def _proj_kernel(x_ref, nw_ref, wn_ref, wt_ref, onehot_ref, *out_refs, n_sub):
    x = x_ref[...]
    ms = jnp.mean(x * x, axis=-1, keepdims=True)
    xn = (x * lax.rsqrt(ms + NORM_EPS) * nw_ref[...]).astype(BF16)
    lo = 0
    for idx, ((_, n, _), o_ref) in enumerate(zip(_NAT_OUTS, out_refs[:len(_NAT_OUTS)])):
        res = _dot(xn, wn_ref[:, lo:lo + n])
        if idx == 0:
            res = res + onehot_ref[...]
        o_ref[...] = res.astype(o_ref.dtype)
        lo += n
    lo = 0
    for (_, n, _), o_ref in zip(_TR_OUTS, out_refs[len(_NAT_OUTS):]):
        res = _dot_nt(wt_ref[lo:lo + n, :], xn)
        for s in range(n_sub):
            o_ref[s] = res[:, s * LANES:(s + 1) * LANES].astype(o_ref.dtype)
        lo += n


def _proj(x2, nw, wn, wt, onehot, tm):
    t, d = x2.shape
    assert t % tm == 0 and tm % LANES == 0 and onehot.shape[1] == tm
    n_sub = tm // LANES
    n_pat = onehot.shape[0]
    const = lambda arr: pl.BlockSpec(arr.shape, lambda i: (0,) * arr.ndim, pipeline_mode=pl.Buffered(1))
    out_specs = [pl.BlockSpec((tm, n), lambda i: (i, 0)) for _, n, _ in _NAT_OUTS] + \
                [pl.BlockSpec((n_sub, n, LANES), lambda i: (i, 0, 0)) for _, n, _ in _TR_OUTS]
    out_shape = [jax.ShapeDtypeStruct((t, n), dt) for _, n, dt in _NAT_OUTS] + \
                [jax.ShapeDtypeStruct((t // LANES, n, LANES), dt) for _, n, dt in _TR_OUTS]
    return pl.pallas_call(
        functools.partial(_proj_kernel, n_sub=n_sub),
        grid=(t // tm,),
        in_specs=[pl.BlockSpec((tm, d), lambda i: (i, 0)), const(nw), const(wn), const(wt),
                  pl.BlockSpec((None, tm, onehot.shape[2]), lambda i: (i % n_pat, 0, 0))],
        out_specs=out_specs,
        out_shape=out_shape,
        compiler_params=pltpu.CompilerParams(dimension_semantics=("parallel",),
                                             vmem_limit_bytes=VMEM_LIMIT),
        name="proj",
    )(x2, nw, wn, wt, onehot)


def _compress_kernel(ak_ref, av_ref, posk_ref, posv_ref, w1k_ref, b1k_ref, w2k_ref,
                     w1v_ref, b1v_ref, w2vt_ref, kc_ref, vct_ref):
    half = CMP_STRIDE * HEAD_DIM
    nseg = ak_ref.shape[1]

    def hidden(a, pos_ref, w1_ref, b1_ref):
        p0 = _dot((a + pos_ref[0:1, :]).astype(BF16), w1_ref[0:half, :])
        p1 = _dot((a + pos_ref[1:2, :]).astype(BF16), w1_ref[half:2 * half, :])
        p1 = pltpu.roll(p1, nseg - 1, 0)
        return _silu(p0 + p1 + b1_ref[...]).astype(BF16)

    for g in range(NSA_GROUPS):
        hk = hidden(ak_ref[g], posk_ref, w1k_ref, b1k_ref)
        kc_ref[:, g * HEAD_DIM:(g + 1) * HEAD_DIM] = _dot(hk, w2k_ref[...]).astype(kc_ref.dtype)
        hv = hidden(av_ref[g], posv_ref, w1v_ref, b1v_ref)
        vct_ref[g * HEAD_DIM:(g + 1) * HEAD_DIM, :] = _dot_nt(w2vt_ref[...], hv).astype(vct_ref.dtype)


def _compress(ak, av, posk, posv, w1k, b1k, w2k, w1v, b1v, w2vt):
    b, g, nseg, width = ak.shape
    const2 = lambda shape: pl.BlockSpec(shape, lambda i: (0, 0))
    return pl.pallas_call(
        _compress_kernel,
        grid=(b,),
        in_specs=[pl.BlockSpec((None, g, nseg, width), lambda i: (i, 0, 0, 0)),
                  pl.BlockSpec((None, g, nseg, width), lambda i: (i, 0, 0, 0)),
                  const2(posk.shape), const2(posv.shape),
                  const2(w1k.shape), const2(b1k.shape), const2(w2k.shape),
                  const2(w1v.shape), const2(b1v.shape), const2(w2vt.shape)],
        out_specs=[pl.BlockSpec((None, nseg, KV_WIDTH), lambda i: (i, 0, 0)),
                   pl.BlockSpec((None, KV_WIDTH, nseg), lambda i: (i, 0, 0))],
        out_shape=[jax.ShapeDtypeStruct((b, nseg, KV_WIDTH), BF16),
                   jax.ShapeDtypeStruct((b, KV_WIDTH, nseg), BF16)],
        compiler_params=pltpu.CompilerParams(dimension_semantics=("parallel",),
                                             vmem_limit_bytes=VMEM_LIMIT),
        name="compress",
    )(ak, av, posk, posv, w1k, b1k, w2k, w1v, b1v, w2vt)


def _nsa_kernel(qt_ref, gt_ref, kc_ref, vct_ref, kslc_ref, vslct_ref, kswa_ref, vswat_ref,
                tsel_ref, tcmp_ref, ovt_ref, far_ref, out_ref,
                qbd_ref, qaug_ref, m_ref, acc_ref, ot_ref, s_ref, p_ref, viol_ref, work_ref, chosen_ref,
                *, n_qb, n_cmp_rows, n_sel, top_n):
    qb = pl.program_id(1)
    q0 = qb * Q_BLOCK

    @pl.when(qb == 0)
    def _():
        qbd_ref[...] = jnp.zeros_like(qbd_ref)
        qaug_ref[...] = jnp.zeros_like(qaug_ref)

    for g in range(NSA_GROUPS):
        for r in range(NSA_HPG):
            h = g * NSA_HPG + r
            q_head = qt_ref[h * HEAD_DIM:(h + 1) * HEAD_DIM, :]
            qbd_ref[g * HEAD_DIM:(g + 1) * HEAD_DIM, h * Q_BLOCK:(h + 1) * Q_BLOCK] = q_head
            qaug_ref[g, 0:HEAD_DIM, r * Q_BLOCK:(r + 1) * Q_BLOCK] = q_head

    def gslice(g):
        return slice(g * GCOLS, (g + 1) * GCOLS)

    def values_with_ones(vt_tiles, g):
        vt = jnp.concatenate([t[g * HEAD_DIM:(g + 1) * HEAD_DIM, :] for t in vt_tiles], axis=1)
        return jnp.concatenate([vt, jnp.ones((SUM_ROWS, vt.shape[1]), BF16)], axis=0)

    def normalized(a):
        return a[0:HEAD_DIM, :] * (1.0 / a[HEAD_DIM:HEAD_DIM + 1, :])

    def self_logit(kt):
        cols = []
        for h in range(NSA_HEADS):
            g = h // NSA_HPG
            prod = qt_ref[h * HEAD_DIM:(h + 1) * HEAD_DIM, :].astype(F32) * kt[g * HEAD_DIM:(g + 1) * HEAD_DIM, :]
            cols.append(jnp.sum(prod, axis=0, keepdims=True))
        return jnp.concatenate(cols, axis=1) - far_ref[...]

    diag_rows = pl.ds(pl.multiple_of(q0, Q_BLOCK), Q_BLOCK)

    u0 = pl.multiple_of(8 * (n_qb - 1 - qb), 8)
    j_iota = lax.broadcasted_iota(jnp.int32, (n_sel, Q_BLOCK), 0)
    j_f32 = j_iota.astype(F32)
    t_pos = q0 + lax.broadcasted_iota(jnp.int32, (n_sel, Q_BLOCK), 1)
    cur = t_pos // SEL_BLOCK
    valid = j_iota <= cur
    forced = (j_iota == 0) | (j_iota == cur) | (j_iota == cur - 1)

    logits = [_dot(kc_ref[...], qbd_ref[:, gslice(g)]) for g in range(NSA_GROUPS)]
    for g in range(NSA_GROUPS):
        s = logits[g] + tcmp_ref[pl.ds(u0, n_cmp_rows), gslice(g)]
        m = jnp.maximum(jnp.max(s, axis=0, keepdims=True), 0.1 * NEG)
        p_ref[0:n_cmp_rows, gslice(g)] = jnp.exp2(s - m).astype(BF16)
    o_cmp, imps = [], []
    for g in range(NSA_GROUPS):
        e = p_ref[0:n_cmp_rows, gslice(g)]
        raw = _dot(values_with_ones([vct_ref[...]], g), e)
        total = raw[HEAD_DIM:HEAD_DIM + 1, :]
        inv = 1.0 / jnp.where(total == 0.0, 1.0, total)
        o_cmp.append(raw[0:HEAD_DIM, :] * inv)
        imp_heads = _dot(ovt_ref[...], e) * inv
        imp = imp_heads[:, 0:Q_BLOCK]
        for r in range(1, NSA_HPG):
            imp = imp + imp_heads[:, r * Q_BLOCK:(r + 1) * Q_BLOCK]
        imps.append(imp)

    quota = top_n - (1 + (cur >= 1).astype(jnp.int32) + (cur >= 2).astype(jnp.int32))
    for g in range(NSA_GROUPS):
        work_ref[g] = jnp.where(forced, -2.0, jnp.where(valid, imps[g], -1.0))
        chosen_ref[g] = jnp.where(forced, 1.0, 0.0)

    def pick_round(index, limited):
        for g in range(NSA_GROUPS):
            work = work_ref[g]
            best = jnp.max(work, axis=0, keepdims=True)
            first = jnp.min(jnp.where(work == best, j_f32, float(n_sel)), axis=0, keepdims=True)
            hit = j_f32 == first
            if limited:
                hit = hit & (quota > index)
            chosen_ref[g] = jnp.where(hit, 1.0, chosen_ref[g])
            work_ref[g] = jnp.where(hit, -2.0, work)

    max_forced = 3
    for index in range(top_n - max_forced):
        pick_round(index, False)

    @pl.when(q0 < (max_forced - 1) * SEL_BLOCK)
    def _():
        for index in range(top_n - max_forced, top_n - 1):
            pick_round(index, True)

    for g in range(NSA_GROUPS):
        addm = jnp.where((chosen_ref[g] > 0.5) & valid, 0.0, NEG).astype(BF16)
        qaug_ref[g, HEAD_DIM:HEAD_DIM + n_sel, :] = jnp.concatenate([addm] * NSA_HPG, axis=1)

    def sel_step(r, near, exact):
        key_rows = pl.ds(pl.multiple_of(r * STEP, STEP), STEP)
        vt_tiles = [vslct_ref[STEP_TILES * r + t] for t in range(STEP_TILES)]
        if near:
            tile_ids = []
            for t in range(STEP_TILES):
                i = qb - (STEP_TILES * r + t)
                tile_ids.append(jnp.where(i < 0, TILE_MASKED, jnp.minimum(i, TILE_FAR)))
        logits = [_dot(kslc_ref[key_rows, g * AUG_DIM:(g + 1) * AUG_DIM], qaug_ref[g])
                  for g in range(NSA_GROUPS)]
        alphas = []
        for g in range(NSA_GROUPS):
            m_old = m_ref[:, gslice(g)]
            m8 = None
            for t in range(STEP_TILES):
                rows = slice(t * LANES, (t + 1) * LANES)
                v = logits[g][rows, :]
                if near:
                    v = v + tsel_ref[tile_ids[t], :, gslice(g)]
                if exact:
                    s_ref[rows, gslice(g)] = v
                    vm = jnp.max(v.reshape(LANES // 8, 8, GCOLS), axis=0)
                else:
                    pt = jnp.exp2(v - m_old).astype(BF16)
                    p_ref[rows, gslice(g)] = pt
                    vm = jnp.max(pt.reshape(LANES // 16, 16, GCOLS), axis=0)
                m8 = vm if m8 is None else jnp.maximum(m8, vm)
            if exact:
                m_new = jnp.maximum(m_old, jnp.max(m8, axis=0, keepdims=True))
                alphas.append(jnp.exp2(m_old - m_new))
                m_ref[:, gslice(g)] = m_new
                for t in range(STEP_TILES):
                    rows = slice(t * LANES, (t + 1) * LANES)
                    p_ref[rows, gslice(g)] = jnp.exp2(s_ref[rows, gslice(g)] - m_new).astype(BF16)
            else:
                viol_ref[:, gslice(g)] = jnp.maximum(viol_ref[:, gslice(g)],
                                                     jnp.max(m8.astype(F32), axis=0, keepdims=True))
        for g in range(NSA_GROUPS):
            pv = _dot(values_with_ones(vt_tiles, g), p_ref[0:STEP, gslice(g)])
            if exact:
                acc_ref[g] = acc_ref[g] * alphas[g] + pv
            else:
                acc_ref[g] = acc_ref[g] + pv

    r_diag = qb // STEP_TILES
    n_far = jnp.maximum((qb - (N_NEAR - 1)) // STEP_TILES, 0)

    def selected(exact):
        if exact:
            m_ref[...] = jnp.full(m_ref.shape, NEG, F32)
        else:
            kt = kslc_ref[diag_rows, :].astype(F32).T
            m_ref[...] = self_logit(jnp.concatenate(
                [kt[g * AUG_DIM:g * AUG_DIM + HEAD_DIM, :] for g in range(NSA_GROUPS)], axis=0))
        acc_ref[...] = jnp.zeros_like(acc_ref)

        def near_body(i, carry):
            sel_step(r_diag - i, True, exact)
            return carry

        def far_body(r, carry):
            sel_step(r, False, exact)
            return carry

        lax.fori_loop(0, r_diag - n_far + 1, near_body, 0)
        lax.fori_loop(0, n_far, far_body, 0)

    viol_ref[...] = jnp.zeros_like(viol_ref)
    selected(False)

    @pl.when(jnp.max(viol_ref[...]) > 2.0 ** MAX_EXP2_EXCESS)
    def _():
        selected(True)

    o_slc = [normalized(acc_ref[g]) for g in range(NSA_GROUPS)]

    first_tile = jnp.maximum(qb - (WIN_TILES - 1), 0)
    kw = kswa_ref[pl.ds(pl.multiple_of(first_tile * LANES, LANES), WIN_TILES * LANES), :]
    vt_tiles = [vswat_ref[first_tile + t] for t in range(WIN_TILES)]
    win_ids = []
    for t in range(WIN_TILES):
        i = qb - (first_tile + t)
        win_ids.append(jnp.where(i < 0, TILE_MASKED, jnp.where(i == WIN_TILES - 1, TILE_WIN_OLD, i)))
    def window(exact):
        logits = [_dot(kw, qbd_ref[:, gslice(g)]) for g in range(NSA_GROUPS)]
        if not exact:
            m_fix = self_logit(kswa_ref[diag_rows, :].astype(F32).T)
        for g in range(NSA_GROUPS):
            m8 = None
            for t in range(WIN_TILES):
                rows = slice(t * LANES, (t + 1) * LANES)
                v = logits[g][rows, :] + tsel_ref[win_ids[t], :, gslice(g)]
                if exact:
                    s_ref[rows, gslice(g)] = v
                    vm = jnp.max(v.reshape(LANES // 8, 8, GCOLS), axis=0)
                else:
                    pt = jnp.exp2(v - m_fix[:, gslice(g)]).astype(BF16)
                    p_ref[rows, gslice(g)] = pt
                    vm = jnp.max(pt.reshape(LANES // 16, 16, GCOLS), axis=0)
                m8 = vm if m8 is None else jnp.maximum(m8, vm)
            if exact:
                m = jnp.max(m8, axis=0, keepdims=True)
                p_ref[:, gslice(g)] = jnp.exp2(s_ref[:, gslice(g)] - m).astype(BF16)
            else:
                viol_ref[:, gslice(g)] = jnp.max(m8.astype(F32), axis=0, keepdims=True)
        for g in range(NSA_GROUPS):
            acc_ref[g] = _dot(values_with_ones(vt_tiles, g), p_ref[:, gslice(g)])

    window(False)

    @pl.when(jnp.max(viol_ref[...]) > 2.0 ** MAX_EXP2_EXCESS)
    def _():
        window(True)

    o_swa = [normalized(acc_ref[g]) for g in range(NSA_GROUPS)]

    gates = _sigmoid(gt_ref[...])
    for g in range(NSA_GROUPS):
        def gate_row(c):
            return jnp.concatenate(
                [gates[c * NSA_HEADS + g * NSA_HPG + r:c * NSA_HEADS + g * NSA_HPG + r + 1, :]
                 for r in range(NSA_HPG)], axis=1)
        og = gate_row(0) * o_cmp[g] + gate_row(1) * o_slc[g] + gate_row(2) * o_swa[g]
        for r in range(NSA_HPG):
            h = g * NSA_HPG + r
            ot_ref[h * HEAD_DIM:(h + 1) * HEAD_DIM, :] = og[:, r * Q_BLOCK:(r + 1) * Q_BLOCK]
    out_ref[...] = ot_ref[...].T


def _nsa(qt, gt, kc, vct, kslc, vslct, kswa, vswat, tsel, tcmp, ovt, far, seq):
    b = kc.shape[0]
    n_qb = seq // Q_BLOCK
    assert n_qb % STEP_TILES == 0 and n_qb >= WIN_TILES and WIN_TILES >= STEP_TILES
    n_cmp_rows = kc.shape[1]
    n_sel = seq // SEL_BLOCK
    assert n_sel <= AUG_DIM - HEAD_DIM
    top_n = min(SEL_TOPN, n_sel)
    assert top_n > 3
    n_gate = gt.shape[2]
    const = lambda arr: pl.BlockSpec(arr.shape, lambda i, j: (0,) * arr.ndim,
                                     pipeline_mode=pl.Buffered(1))
    kern = functools.partial(_nsa_kernel, n_qb=n_qb, n_cmp_rows=n_cmp_rows, n_sel=n_sel, top_n=top_n)
    return pl.pallas_call(
        kern,
        grid=(b, n_qb),
        in_specs=[pl.BlockSpec((None, None, NSA_WIDTH, Q_BLOCK), lambda i, j: (i, j, 0, 0)),
                  pl.BlockSpec((None, None, n_gate, Q_BLOCK), lambda i, j: (i, j, 0, 0)),
                  pl.BlockSpec((None, n_cmp_rows, KV_WIDTH), lambda i, j: (i, 0, 0)),
                  pl.BlockSpec((None, KV_WIDTH, n_cmp_rows), lambda i, j: (i, 0, 0)),
                  pl.BlockSpec((None, seq, NSA_GROUPS * AUG_DIM), lambda i, j: (i, 0, 0)),
                  pl.BlockSpec((None, n_qb, KV_WIDTH, Q_BLOCK), lambda i, j: (i, 0, 0, 0)),
                  pl.BlockSpec((None, seq, KV_WIDTH), lambda i, j: (i, 0, 0)),
                  pl.BlockSpec((None, n_qb, KV_WIDTH, Q_BLOCK), lambda i, j: (i, 0, 0, 0)),
                  const(tsel), const(tcmp), const(ovt), const(far)],
        out_specs=pl.BlockSpec((None, Q_BLOCK, NSA_WIDTH), lambda i, j: (i, j, 0)),
        out_shape=jax.ShapeDtypeStruct((b, seq, NSA_WIDTH), F32),
        scratch_shapes=[pltpu.VMEM((KV_WIDTH, QCOLS), BF16),
                        pltpu.VMEM((NSA_GROUPS, AUG_DIM, GCOLS), BF16),
                        pltpu.VMEM((1, QCOLS), F32),
                        pltpu.VMEM((NSA_GROUPS, HEAD_DIM + SUM_ROWS, GCOLS), F32),
                        pltpu.VMEM((NSA_WIDTH, Q_BLOCK), F32),
                        pltpu.VMEM((WIN_TILES * LANES, QCOLS), F32),
                        pltpu.VMEM((WIN_TILES * LANES, QCOLS), BF16),
                        pltpu.VMEM((1, QCOLS), F32),
                        pltpu.VMEM((NSA_GROUPS, n_sel, Q_BLOCK), F32),
                        pltpu.VMEM((NSA_GROUPS, n_sel, Q_BLOCK), F32)],
        compiler_params=pltpu.CompilerParams(dimension_semantics=("parallel", "arbitrary"),
                                             vmem_limit_bytes=VMEM_LIMIT),
        name="nsa",
    )(qt, gt, kc, vct, kslc, vslct, kswa, vswat, tsel, tcmp, ovt, far)


def _ssd_kernel(xbc_ref, dt_ref, dtt_ref, convw_ref, convb_ref, dtb_ref, dtbt_ref, a_ref, at_ref,
                dskip_ref, expand_ref, y_ref, xpad_ref, state_ref):
    L = SSM_CHUNK
    c = pl.program_id(1)

    @pl.when(c == 0)
    def _():
        xpad_ref[0:8, :] = jnp.zeros((8, CONV_DIM), F32)
        state_ref[...] = jnp.zeros_like(state_ref)

    xpad_ref[8:8 + L, :] = xbc_ref[...]
    conv = convb_ref[...]
    for k in range(CONV_WIDTH):
        conv = conv + convw_ref[k:k + 1, :] * xpad_ref[pl.ds(8 - (CONV_WIDTH - 1) + k, L), :]
    xpad_ref[0:8, :] = xpad_ref[L:L + 8, :]
    xbc = _silu(conv)
    xs = xbc[:, 0:SSM_WIDTH]
    bm = xbc[:, SSM_WIDTH:SSM_WIDTH + SSM_GROUPS * SSM_STATE]
    cm = xbc[:, SSM_WIDTH + SSM_GROUPS * SSM_STATE:]

    dt = _softplus(dt_ref[...] + dtb_ref[...])
    dtt = _softplus(dtt_ref[...] + dtbt_ref[...])
    a_row = -jnp.exp(a_ref[...])
    a_col = -jnp.exp(at_ref[...])
    row_i = lax.broadcasted_iota(jnp.int32, (L, L), 0)
    col_j = lax.broadcasted_iota(jnp.int32, (L, L), 1)
    tril = row_i >= col_j
    tril01 = jnp.where(tril, 1.0, 0.0).astype(BF16)
    triu01 = jnp.where(row_i <= col_j, 1.0, 0.0).astype(BF16)
    cs = _dot_exact_rhs(tril01, dt * a_row)
    cst = _dot_exact_lhs(dtt * a_col, triu01)

    expand = expand_ref[...]
    cs_x = _dot_exact_lhs(cs, expand)
    dt_x = _dot_exact_lhs(dt, expand)
    total_x = cs_x[L - 1:L, :]
    xdt = (xs * dt_x).astype(BF16)
    xw = (xs * (jnp.exp(total_x - cs_x) * dt_x)).astype(BF16)
    decay_out = jnp.exp(cs_x)
    decay_state = jnp.exp(total_x)

    lane = lax.broadcasted_iota(jnp.int32, (L, LANES), 1)
    gw = SSM_HPG * SSM_HEAD_DIM
    for g in range(SSM_GROUPS):
        bg = bm[:, g * SSM_STATE:(g + 1) * SSM_STATE]
        cg = cm[:, g * SSM_STATE:(g + 1) * SSM_STATE].astype(BF16)
        cb = _dot_nt(cg, bg.astype(BF16))
        st = state_ref[g]
        y_g = _dot(cg, st.astype(BF16)) * decay_out[:, g * gw:(g + 1) * gw]
        state_ref[g] = st * decay_state[:, g * gw:(g + 1) * gw] + \
            _dot(bg.T.astype(BF16), xw[:, g * gw:(g + 1) * gw])
        pieces = []
        for pair in range(SSM_HPG // 2):
            ws = []
            for hh in range(2):
                h = g * SSM_HPG + 2 * pair + hh
                diff = cs[:, h:h + 1] - cst[h:h + 1, :]
                decay = jnp.where(tril, jnp.exp(jnp.where(tril, diff, 0.0)), 0.0)
                ws.append((cb * decay).astype(BF16))
            w_pair = jnp.concatenate(ws, axis=1)
            lo = g * gw + pair * LANES
            slab = xdt[:, lo:lo + LANES]
            zero = jnp.zeros_like(slab)
            x_bd = jnp.concatenate([jnp.where(lane < SSM_HEAD_DIM, slab, zero),
                                    jnp.where(lane >= SSM_HEAD_DIM, slab, zero)], axis=0)
            pieces.append(_dot(w_pair, x_bd))
        y_ref[:, g * gw:(g + 1) * gw] = y_g + jnp.concatenate(pieces, axis=1) + \
            dskip_ref[:, g * gw:(g + 1) * gw] * xs[:, g * gw:(g + 1) * gw]


def _ssd(xbc, dt, dtt, convw, convb, dtb, dtbt, a_log, a_logt, dskip_x, expand, seq):
    b = xbc.shape[0]
    nc = seq // SSM_CHUNK
    const = lambda arr: pl.BlockSpec(arr.shape, lambda i, j: (0,) * arr.ndim)
    return pl.pallas_call(
        _ssd_kernel,
        grid=(b, nc),
        in_specs=[pl.BlockSpec((None, SSM_CHUNK, CONV_DIM), lambda i, j: (i, j, 0)),
                  pl.BlockSpec((None, SSM_CHUNK, SSM_HEADS), lambda i, j: (i, j, 0)),
                  pl.BlockSpec((None, None, SSM_HEADS, SSM_CHUNK), lambda i, j: (i, j, 0, 0)),
                  const(convw), const(convb), const(dtb), const(dtbt), const(a_log), const(a_logt),
                  const(dskip_x), const(expand)],
        out_specs=pl.BlockSpec((None, SSM_CHUNK, SSM_WIDTH), lambda i, j: (i, j, 0)),
        out_shape=jax.ShapeDtypeStruct((b, seq, SSM_WIDTH), F32),
        scratch_shapes=[pltpu.VMEM((SSM_CHUNK + 8, CONV_DIM), F32),
                        pltpu.VMEM((SSM_GROUPS, SSM_STATE, SSM_HPG * SSM_HEAD_DIM), F32)],
        compiler_params=pltpu.CompilerParams(dimension_semantics=("parallel", "arbitrary"),
                                             vmem_limit_bytes=VMEM_LIMIT),
        name="ssd",
    )(xbc, dt, dtt, convw, convb, dtb, dtbt, a_log, a_logt, dskip_x, expand)


def _epilogue_kernel(x_ref, o_ref, y_ref, nw_ref, wz1_ref, wz2_ref, wmg_ref, won_ref, wos_ref, wo_ref,
                     snw_ref, fnw_ref, out_ref):
    x = x_ref[...]
    ms = jnp.mean(x * x, axis=-1, keepdims=True)
    xn = (x * lax.rsqrt(ms + NORM_EPS) * nw_ref[...]).astype(BF16)
    u = (o_ref[...] * _silu(_dot(xn, wz1_ref[...]))).astype(BF16)
    h_nsa = _dot(u, won_ref[...])
    hh = y_ref[...] * _silu(_dot(xn, wz2_ref[...]))
    gw = SSM_WIDTH // SSM_GROUPS
    parts = []
    for g in range(SSM_GROUPS):
        hg = hh[:, g * gw:(g + 1) * gw]
        hg = hg * lax.rsqrt(jnp.mean(hg * hg, axis=-1, keepdims=True) + NORM_EPS)
        parts.append((hg * snw_ref[:, g * gw:(g + 1) * gw]).astype(BF16))
    h_ssm = _dot(jnp.concatenate(parts, axis=1), wos_ref[...])
    gate = _sigmoid(_dot(xn, wmg_ref[...]))
    mix = (gate[:, 0:D_MODEL] * h_nsa + gate[:, D_MODEL:] * h_ssm).astype(BF16)
    r = x + _dot(mix, wo_ref[...])
    ms2 = jnp.mean(r * r, axis=-1, keepdims=True)
    out_ref[...] = r * lax.rsqrt(ms2 + NORM_EPS) * fnw_ref[...]


def _epilogue(x2, o2, y2, nw, wz1, wz2, wmg, won, wos, wo, snw, fnw, tm):
    t, d = x2.shape
    assert t % tm == 0
    const = lambda arr: pl.BlockSpec(arr.shape, lambda i: (0,) * arr.ndim, pipeline_mode=pl.Buffered(1))
    return pl.pallas_call(
        _epilogue_kernel,
        grid=(t // tm,),
        in_specs=[pl.BlockSpec((tm, d), lambda i: (i, 0)),
                  pl.BlockSpec((tm, NSA_WIDTH), lambda i: (i, 0)),
                  pl.BlockSpec((tm, SSM_WIDTH), lambda i: (i, 0)),
                  const(nw), const(wz1), const(wz2), const(wmg), const(won), const(wos), const(wo),
                  const(snw), const(fnw)],
        out_specs=pl.BlockSpec((tm, d), lambda i: (i, 0)),
        out_shape=jax.ShapeDtypeStruct((t, d), F32),
        compiler_params=pltpu.CompilerParams(dimension_semantics=("parallel",),
                                             vmem_limit_bytes=VMEM_LIMIT),
        name="epilogue",
    )(x2, o2, y2, nw, wz1, wz2, wmg, won, wos, wo, snw, fnw)


def _table_kernel(rb_ref, bucket_ref, out_ref):
    bk = bucket_ref[...]
    for h in range(NSA_HEADS):
        acc = jnp.full(bk.shape, NEG, F32)
        for b in range(REL_BUCKETS):
            acc = jnp.where(bk == b, rb_ref[b, h], acc)
        out_ref[:, h * Q_BLOCK:(h + 1) * Q_BLOCK] = acc


def _bias_table(rb, buckets, tr):
    rows = buckets.shape[0]
    assert rows % tr == 0
    return pl.pallas_call(
        _table_kernel,
        grid=(rows // tr,),
        in_specs=[pl.BlockSpec(memory_space=pltpu.SMEM),
                  pl.BlockSpec((tr, Q_BLOCK), lambda i: (i, 0))],
        out_specs=pl.BlockSpec((tr, QCOLS), lambda i: (i, 0)),
        out_shape=jax.ShapeDtypeStruct((rows, QCOLS), F32),
        compiler_params=pltpu.CompilerParams(dimension_semantics=("parallel",)),
        name="bias_table",
    )(rb, buckets)


def _bucket_tables(seq):
    n_qb = seq // Q_BLOCK
    key = np.arange(LANES)[:, None]
    tok = np.arange(Q_BLOCK)[None, :]
    tiles = [_t5_bucket_np(LANES * o + tok - key) for o in range(N_NEAR)]
    tiles.append(np.full((LANES, Q_BLOCK), REL_BUCKETS - 1, np.int32))
    dwin = WINDOW + tok - key
    tiles.append(_t5_bucket_np(np.where(dwin < WINDOW, dwin, -1)))
    tiles.append(np.full((LANES, Q_BLOCK), MASKED_BUCKET, np.int32))
    assert len(tiles) == N_TILES and LANES * N_NEAR - (LANES - 1) >= 790
    n_cmp_rows = seq // CMP_STRIDE
    rows = 8 * (n_qb - 1) + n_cmp_rows
    rows_pad = -(-rows // LANES) * LANES
    u = np.arange(rows_pad)[:, None]
    cmp_tbl = _t5_bucket_np(tok - CMP_STRIDE * u + Q_BLOCK * (n_qb - 1) - (CMP_BLOCK - 1))
    return np.concatenate(tiles, axis=0), cmp_tbl


def _block_onehot(seq, tm):
    pos = np.arange(seq)
    pat = np.zeros((seq, NSA_GROUPS, AUG_DIM), np.float32)
    pat[pos, :, HEAD_DIM + pos // SEL_BLOCK] = 1.0
    return jnp.asarray(pat.reshape(seq // tm, tm, NSA_GROUPS * AUG_DIM), dtype=BF16)


def _overlap_t(seq):
    n_cmp_rows = seq // CMP_STRIDE
    n_sel = seq // SEL_BLOCK
    c_start = np.arange(n_cmp_rows)[None, :] * CMP_STRIDE
    s_start = np.arange(n_sel)[:, None] * SEL_BLOCK
    ov = (c_start < s_start + SEL_BLOCK) & (c_start + CMP_BLOCK > s_start)
    ov[:, n_cmp_rows - 1] = False
    return jnp.asarray(ov, dtype=BF16)


def _layer(x, norm_w, w_in, cmp_pos_k, cmp_pos_v, cmp_k_w1, cmp_k_b1, cmp_k_w2, cmp_v_w1, cmp_v_b1,
           cmp_v_w2, conv_w, conv_b, dt_bias, a_log, d_skip, ssm_norm_w, w_out_nsa, w_out_ssm, w_out,
           rel_bias, out_norm_w):
    b, s, d = x.shape
    assert d == D_MODEL and s % STEP == 0 and s >= 2 * WINDOW
    t = b * s
    n_qb = s // Q_BLOCK
    cols = _column_offsets()
    wcol = lambda name: w_in[:, cols[name][0]:cols[name][1]]
    x2 = x.reshape(t, d)
    nw = norm_w.reshape(1, d).astype(F32)

    gate_w = wcol('nsa_gate').reshape(d, NSA_HEADS, 3).transpose(0, 2, 1).reshape(d, 3 * NSA_HEADS)
    w_kslc = jnp.pad(wcol('k_slc').reshape(d, NSA_GROUPS, HEAD_DIM),
                     ((0, 0), (0, 0), (0, AUG_DIM - HEAD_DIM))).reshape(d, NSA_GROUPS * AUG_DIM)
    wn = jnp.concatenate([w_kslc, wcol('k_swa'), wcol('k_cmp'), wcol('v_cmp'), wcol('xbc'),
                          wcol('dt')], axis=1).astype(BF16)
    wt = jnp.concatenate([wcol('q') * (HEAD_DIM ** -0.5 * LOG2E), wcol('v_slc'), wcol('v_swa'), gate_w,
                          wcol('dt')], axis=1).T.astype(BF16)
    kslc, kswa, cc, xbc, dt, qt, vslct, vswat, gt, dtt = _proj(x2, nw, wn, wt, _block_onehot(s, STEP), STEP)
    kslc = kslc.reshape(b, s, NSA_GROUPS * AUG_DIM)
    kswa = kswa.reshape(b, s, KV_WIDTH)
    xbc = xbc.reshape(b, s, CONV_DIM)
    dt = dt.reshape(b, s, SSM_HEADS)
    qt = qt.reshape(b, n_qb, NSA_WIDTH, Q_BLOCK)
    vslct = vslct.reshape(b, n_qb, KV_WIDTH, Q_BLOCK)
    vswat = vswat.reshape(b, n_qb, KV_WIDTH, Q_BLOCK)
    gt = gt.reshape(b, n_qb, 3 * NSA_HEADS, Q_BLOCK)
    dtt = dtt.reshape(b, n_qb, SSM_HEADS, Q_BLOCK)

    nseg = s // CMP_STRIDE

    def seg_major(a):
        a = a.reshape(b, nseg, CMP_STRIDE, NSA_GROUPS, HEAD_DIM)
        return a.transpose(0, 3, 1, 2, 4).reshape(b, NSA_GROUPS, nseg, CMP_STRIDE * HEAD_DIM)

    pos2 = lambda p: p.astype(F32).reshape(2, CMP_STRIDE * HEAD_DIM)
    kc, vct = _compress(seg_major(cc[:, :KV_WIDTH]), seg_major(cc[:, KV_WIDTH:]),
                        pos2(cmp_pos_k), pos2(cmp_pos_v),
                        cmp_k_w1.astype(BF16), cmp_k_b1.reshape(1, -1).astype(F32), cmp_k_w2.astype(BF16),
                        cmp_v_w1.astype(BF16), cmp_v_b1.reshape(1, -1).astype(F32), cmp_v_w2.T.astype(BF16))

    sel_buckets, cmp_buckets = _bucket_tables(s)
    rb = rel_bias.astype(F32) * LOG2E
    rb_far = rb[REL_BUCKETS - 1:REL_BUCKETS, :]
    tsel = _bias_table(rb - rb_far, jnp.asarray(sel_buckets), LANES).reshape(N_TILES, LANES, QCOLS)
    tcmp = _bias_table(rb, jnp.asarray(cmp_buckets), LANES)
    far = jnp.repeat(rb_far.reshape(-1), Q_BLOCK).reshape(1, QCOLS)
    o_nsa = _nsa(qt, gt, kc, vct, kslc, vslct, kswa, vswat, tsel, tcmp, _overlap_t(s), far, s)

    expand = jnp.asarray(np.kron(np.eye(SSM_HEADS), np.ones((1, SSM_HEAD_DIM))), dtype=BF16)
    y = _ssd(xbc, dt, dtt, conv_w.astype(F32), conv_b.reshape(1, -1).astype(F32),
             dt_bias.reshape(1, -1).astype(F32), dt_bias.reshape(-1, 1).astype(F32),
             a_log.reshape(1, -1).astype(F32), a_log.reshape(-1, 1).astype(F32),
             jnp.repeat(d_skip.astype(F32), SSM_HEAD_DIM).reshape(1, -1), expand, s)

    out = _epilogue(x2, o_nsa.reshape(t, NSA_WIDTH), y.reshape(t, SSM_WIDTH), nw,
                    wcol('z_nsa').astype(BF16), wcol('z_ssm').astype(BF16), wcol('merge_gate').astype(BF16),
                    w_out_nsa.astype(BF16), w_out_ssm.astype(BF16), w_out.astype(BF16),
                    ssm_norm_w.reshape(1, -1).astype(F32), out_norm_w.reshape(1, -1).astype(F32),
                    256 if t % 256 == 0 else Q_BLOCK)
    return out.reshape(b, s, d)


def kernel(x, norm_w, w_in, cmp_pos_k, cmp_pos_v, cmp_k_w1, cmp_k_b1, cmp_k_w2, cmp_v_w1, cmp_v_b1, cmp_v_w2,
           conv_w, conv_b, dt_bias, a_log, d_skip, ssm_norm_w, w_out_nsa, w_out_ssm, w_out, rel_bias,
           final_norm_w):
    depth = norm_w.shape[0]
    assert depth == 1, "the epilogue fuses the final norm into the single layer"
    return _layer(x, norm_w[0], w_in[0], cmp_pos_k[0], cmp_pos_v[0], cmp_k_w1[0], cmp_k_b1[0], cmp_k_w2[0],
                  cmp_v_w1[0], cmp_v_b1[0], cmp_v_w2[0], conv_w[0], conv_b[0], dt_bias[0], a_log[0],
                  d_skip[0], ssm_norm_w[0], w_out_nsa[0], w_out_ssm[0], w_out[0], rel_bias, final_norm_w)
```

```python
import functools
import math

import numpy as np
import jax
import jax.numpy as jnp
from jax import lax
from jax.experimental import pallas as pl
from jax.experimental.pallas import tpu as pltpu

F32 = jnp.float32
BF16 = jnp.bfloat16

D_MODEL = 1024
NSA_HEADS = 16
NSA_GROUPS = 4
NSA_HPG = NSA_HEADS // NSA_GROUPS
HEAD_DIM = 64
NSA_WIDTH = NSA_HEADS * HEAD_DIM
KV_WIDTH = NSA_GROUPS * HEAD_DIM
CMP_BLOCK = 32
CMP_STRIDE = 16
CMP_HIDDEN = 4 * HEAD_DIM
SEL_BLOCK = 64
SEL_TOPN = 8
WINDOW = 512
Q_BLOCK = 128
FORCE_SCORE = 1.0e4
REL_BUCKETS = 32
SSM_WIDTH = 2 * D_MODEL
SSM_HEAD_DIM = 64
SSM_HEADS = SSM_WIDTH // SSM_HEAD_DIM
SSM_GROUPS = 4
SSM_HPG = SSM_HEADS // SSM_GROUPS
SSM_STATE = 128
CONV_WIDTH = 4
SSM_CHUNK = 128
CONV_DIM = SSM_WIDTH + 2 * SSM_GROUPS * SSM_STATE
NORM_EPS = 1e-6

NEG = -1e30
LOG2E = math.log2(math.e)
LANES = 128
QCOLS = NSA_HEADS * Q_BLOCK
GCOLS = NSA_HPG * Q_BLOCK
AUG_DIM = 2 * HEAD_DIM
N_NEAR = 8
TILE_FAR = N_NEAR
TILE_WIN_OLD = N_NEAR + 1
TILE_MASKED = N_NEAR + 2
N_TILES = N_NEAR + 3
MASKED_BUCKET = REL_BUCKETS
SUM_ROWS = 16
MAX_EXP2_EXCESS = 100.0
STEP_TILES = 4
STEP = STEP_TILES * LANES
WIN_TILES = WINDOW // LANES + 1
VMEM_LIMIT = 56 * 1024 * 1024


def _column_offsets():
    sizes = (('q', NSA_WIDTH), ('k_cmp', KV_WIDTH), ('v_cmp', KV_WIDTH), ('k_slc', KV_WIDTH),
             ('v_slc', KV_WIDTH), ('k_swa', KV_WIDTH), ('v_swa', KV_WIDTH), ('nsa_gate', 3 * NSA_HEADS),
             ('z_nsa', NSA_WIDTH), ('z_ssm', SSM_WIDTH), ('xbc', CONV_DIM), ('dt', SSM_HEADS),
             ('merge_gate', 2 * D_MODEL))
    out, lo = {}, 0
    for name, n in sizes:
        out[name] = (lo, lo + n)
        lo += n
    return out


def _t5_bucket_np(dist):
    dist = np.asarray(dist, dtype=np.int64)
    d = np.maximum(dist, 0)
    max_exact = REL_BUCKETS // 2
    large = np.full(d.shape, max_exact, dtype=np.int64)
    d8 = d.astype(object) ** 8
    for k in range(1, REL_BUCKETS - max_exact):
        large = large + (d8 >= 2 ** (32 + 3 * k)).astype(np.int64)
    bucket = np.where(d < max_exact, d, np.minimum(large, REL_BUCKETS - 1))
    return np.where(dist < 0, MASKED_BUCKET, bucket).astype(np.int32)


def _sigmoid(x):
    return 1.0 / (1.0 + jnp.exp(-x))


def _silu(x):
    return x * _sigmoid(x)


def _softplus(x):
    return jnp.maximum(x, 0.0) + jnp.log1p(jnp.exp(-jnp.abs(x)))


def _dot(a, b):
    return jnp.dot(a, b, preferred_element_type=F32)


def _dot_nt(a, b):
    return lax.dot_general(a, b, (((1,), (1,)), ((), ())), preferred_element_type=F32)


def _split3(x):
    hi = x.astype(BF16)
    r1 = x - hi.astype(F32)
    mid = r1.astype(BF16)
    lo = (r1 - mid.astype(F32)).astype(BF16)
    return hi, mid, lo


def _dot_exact_lhs(x, w01):
    hi, mid, lo = _split3(x)
    return _dot(hi, w01) + _dot(mid, w01) + _dot(lo, w01)


def _dot_exact_rhs(w01, x):
    hi, mid, lo = _split3(x)
    return _dot(w01, hi) + _dot(w01, mid) + _dot(w01, lo)


_NAT_OUTS = (('k_slc', NSA_GROUPS * AUG_DIM, BF16), ('k_swa', KV_WIDTH, BF16), ('kv_cmp', 2 * KV_WIDTH, F32),
             ('xbc', CONV_DIM, F32), ('dt', SSM_HEADS, F32))
_TR_OUTS = (('q', NSA_WIDTH, BF16), ('v_slc', KV_WIDTH, BF16), ('v_swa', KV_WIDTH, BF16),
            ('gate', 3 * NSA_HEADS, F32))


def _proj_kernel(x_ref, nw_ref, wn_ref, wt_ref, onehot_ref, *out_refs, n_sub):
    x = x_ref[...]
    ms = jnp.mean(x * x, axis=-1, keepdims=True)
    xn = (x * lax.rsqrt(ms + NORM_EPS) * nw_ref[...]).astype(BF16)
    lo = 0
    for idx, ((_, n, _), o_ref) in enumerate(zip(_NAT_OUTS, out_refs[:len(_NAT_OUTS)])):
        res = _dot(xn, wn_ref[:, lo:lo + n])
        if idx == 0:
            res = res + onehot_ref[...]
        o_ref[...] = res.astype(o_ref.dtype)
        lo += n
    lo = 0
    for (_, n, _), o_ref in zip(_TR_OUTS, out_refs[len(_NAT_OUTS):]):
        res = _dot_nt(wt_ref[lo:lo + n, :], xn)
        for s in range(n_sub):
            o_ref[s] = res[:, s * LANES:(s + 1) * LANES].astype(o_ref.dtype)
        lo += n


def _proj(x2, nw, wn, wt, onehot, tm):
    t, d = x2.shape
    assert t % tm == 0 and tm % LANES == 0 and onehot.shape[1] == tm
    n_sub = tm // LANES
    n_pat = onehot.shape[0]
    const = lambda arr: pl.BlockSpec(arr.shape, lambda i: (0,) * arr.ndim, pipeline_mode=pl.Buffered(1))
    out_specs = [pl.BlockSpec((tm, n), lambda i: (i, 0)) for _, n, _ in _NAT_OUTS] + \
                [pl.BlockSpec((n_sub, n, LANES), lambda i: (i, 0, 0)) for _, n, _ in _TR_OUTS]
    out_shape = [jax.ShapeDtypeStruct((t, n), dt) for _, n, dt in _NAT_OUTS] + \
                [jax.ShapeDtypeStruct((t // LANES, n, LANES), dt) for _, n, dt in _TR_OUTS]
    return pl.pallas_call(
        functools.partial(_proj_kernel, n_sub=n_sub),
        grid=(t // tm,),
        in_specs=[pl.BlockSpec((tm, d), lambda i: (i, 0)), const(nw), const(wn), const(wt),
                  pl.BlockSpec((None, tm, onehot.shape[2]), lambda i: (i % n_pat, 0, 0))],
        out_specs=out_specs,
        out_shape=out_shape,
        compiler_params=pltpu.CompilerParams(dimension_semantics=("parallel",),
                                             vmem_limit_bytes=VMEM_LIMIT),
        name="proj",
    )(x2, nw, wn, wt, onehot)


def _compress_kernel(ak_ref, av_ref, posk_ref, posv_ref, w1k_ref, b1k_ref, w2k_ref,
                     w1v_ref, b1v_ref, w2vt_ref, kc_ref, vct_ref):
    half = CMP_STRIDE * HEAD_DIM
    nseg = ak_ref.shape[1]

    def hidden(a, pos_ref, w1_ref, b1_ref):
        p0 = _dot((a + pos_ref[0:1, :]).astype(BF16), w1_ref[0:half, :])
        p1 = _dot((a + pos_ref[1:2, :]).astype(BF16), w1_ref[half:2 * half, :])
        p1 = pltpu.roll(p1, nseg - 1, 0)
        return _silu(p0 + p1 + b1_ref[...]).astype(BF16)

    for g in range(NSA_GROUPS):
        hk = hidden(ak_ref[g], posk_ref, w1k_ref, b1k_ref)
        kc_ref[:, g * HEAD_DIM:(g + 1) * HEAD_DIM] = _dot(hk, w2k_ref[...]).astype(kc_ref.dtype)
        hv = hidden(av_ref[g], posv_ref, w1v_ref, b1v_ref)
        vct_ref[g * HEAD_DIM:(g + 1) * HEAD_DIM, :] = _dot_nt(w2vt_ref[...], hv).astype(vct_ref.dtype)


def _compress(ak, av, posk, posv, w1k, b1k, w2k, w1v, b1v, w2vt):
    b, g, nseg, width = ak.shape
    const2 = lambda shape: pl.BlockSpec(shape, lambda i: (0, 0))
    return pl.pallas_call(
        _compress_kernel,
        grid=(b,),
        in_specs=[pl.BlockSpec((None, g, nseg, width), lambda i: (i, 0, 0, 0)),
                  pl.BlockSpec((None, g, nseg, width), lambda i: (i, 0, 0, 0)),
                  const2(posk.shape), const2(posv.shape),
                  const2(w1k.shape), const2(b1k.shape), const2(w2k.shape),
                  const2(w1v.shape), const2(b1v.shape), const2(w2vt.shape)],
        out_specs=[pl.BlockSpec((None, nseg, KV_WIDTH), lambda i: (i, 0, 0)),
                   pl.BlockSpec((None, KV_WIDTH, nseg), lambda i: (i, 0, 0))],
        out_shape=[jax.ShapeDtypeStruct((b, nseg, KV_WIDTH), BF16),
                   jax.ShapeDtypeStruct((b, KV_WIDTH, nseg), BF16)],
        compiler_params=pltpu.CompilerParams(dimension_semantics=("parallel",),
                                             vmem_limit_bytes=VMEM_LIMIT),
        name="compress",
    )(ak, av, posk, posv, w1k, b1k, w2k, w1v, b1v, w2vt)


def _nsa_kernel(qt_ref, gt_ref, kc_ref, vct_ref, kslc_ref, vslct_ref, kswa_ref, vswat_ref,
                tsel_ref, tcmp_ref, ovt_ref, far_ref, out_ref,
                qbd_ref, qaug_ref, m_ref, acc_ref, ot_ref, s_ref, p_ref, viol_ref, work_ref, chosen_ref,
                pw_ref, accw_ref, violw_ref,
                *, n_qb, n_cmp_rows, n_sel, top_n):
    qb = pl.program_id(1)
    q0 = qb * Q_BLOCK

    @pl.when(qb == 0)
    def _():
        qbd_ref[...] = jnp.zeros_like(qbd_ref)
        qaug_ref[...] = jnp.zeros_like(qaug_ref)

    for g in range(NSA_GROUPS):
        for r in range(NSA_HPG):
            h = g * NSA_HPG + r
            q_head = qt_ref[h * HEAD_DIM:(h + 1) * HEAD_DIM, :]
            qbd_ref[g * HEAD_DIM:(g + 1) * HEAD_DIM, h * Q_BLOCK:(h + 1) * Q_BLOCK] = q_head
            qaug_ref[g, 0:HEAD_DIM, r * Q_BLOCK:(r + 1) * Q_BLOCK] = q_head

    def gslice(g):
        return slice(g * GCOLS, (g + 1) * GCOLS)

    def values_with_ones(vt_tiles, g):
        vt = jnp.concatenate([t[g * HEAD_DIM:(g + 1) * HEAD_DIM, :] for t in vt_tiles], axis=1)
        return jnp.concatenate([vt, jnp.ones((SUM_ROWS, vt.shape[1]), BF16)], axis=0)

    def normalized(a):
        return a[0:HEAD_DIM, :] * (1.0 / a[HEAD_DIM:HEAD_DIM + 1, :])

    def self_logit(kt):
        cols = []
        for h in range(NSA_HEADS):
            g = h // NSA_HPG
            prod = qt_ref[h * HEAD_DIM:(h + 1) * HEAD_DIM, :].astype(F32) * kt[g * HEAD_DIM:(g + 1) * HEAD_DIM, :]
            cols.append(jnp.sum(prod, axis=0, keepdims=True))
        return jnp.concatenate(cols, axis=1) - far_ref[...]

    diag_rows = pl.ds(pl.multiple_of(q0, Q_BLOCK), Q_BLOCK)

    u0 = pl.multiple_of(8 * (n_qb - 1 - qb), 8)
    j_iota = lax.broadcasted_iota(jnp.int32, (n_sel, Q_BLOCK), 0)
    j_f32 = j_iota.astype(F32)
    t_pos = q0 + lax.broadcasted_iota(jnp.int32, (n_sel, Q_BLOCK), 1)
    cur = t_pos // SEL_BLOCK
    valid = j_iota <= cur
    forced = (j_iota == 0) | (j_iota == cur) | (j_iota == cur - 1)

    logits = [_dot(kc_ref[...], qbd_ref[:, gslice(g)]) for g in range(NSA_GROUPS)]
    for g in range(NSA_GROUPS):
        s = logits[g] + tcmp_ref[pl.ds(u0, n_cmp_rows), gslice(g)]
        m = jnp.maximum(jnp.max(s, axis=0, keepdims=True), 0.1 * NEG)
        p_ref[0:n_cmp_rows, gslice(g)] = jnp.exp2(s - m).astype(BF16)
    o_cmp, imps = [], []
    for g in range(NSA_GROUPS):
        e = p_ref[0:n_cmp_rows, gslice(g)]
        raw = _dot(values_with_ones([vct_ref[...]], g), e)
        total = raw[HEAD_DIM:HEAD_DIM + 1, :]
        inv = 1.0 / jnp.where(total == 0.0, 1.0, total)
        o_cmp.append(raw[0:HEAD_DIM, :] * inv)
        imp_heads = _dot(ovt_ref[...], e) * inv
        imp = imp_heads[:, 0:Q_BLOCK]
        for r in range(1, NSA_HPG):
            imp = imp + imp_heads[:, r * Q_BLOCK:(r + 1) * Q_BLOCK]
        imps.append(imp)

    first_tile = jnp.maximum(qb - (WIN_TILES - 1), 0)
    kw = kswa_ref[pl.ds(pl.multiple_of(first_tile * LANES, LANES), WIN_TILES * LANES), :]
    win_vt_tiles = [vswat_ref[first_tile + t] for t in range(WIN_TILES)]
    win_ids = []
    for t in range(WIN_TILES):
        i = qb - (first_tile + t)
        win_ids.append(jnp.where(i < 0, TILE_MASKED, jnp.where(i == WIN_TILES - 1, TILE_WIN_OLD, i)))

    def window(exact):
        logits = [_dot(kw, qbd_ref[:, gslice(g)]) for g in range(NSA_GROUPS)]
        if not exact:
            m_fix = self_logit(kswa_ref[diag_rows, :].astype(F32).T)
        for g in range(NSA_GROUPS):
            m8 = None
            for t in range(WIN_TILES):
                rows = slice(t * LANES, (t + 1) * LANES)
                v = logits[g][rows, :] + tsel_ref[win_ids[t], :, gslice(g)]
                if exact:
                    s_ref[rows, gslice(g)] = v
                    vm = jnp.max(v.reshape(LANES // 8, 8, GCOLS), axis=0)
                else:
                    pt = jnp.exp2(v - m_fix[:, gslice(g)]).astype(BF16)
                    pw_ref[rows, gslice(g)] = pt
                    vm = jnp.max(pt.reshape(LANES // 16, 16, GCOLS), axis=0)
                m8 = vm if m8 is None else jnp.maximum(m8, vm)
            if exact:
                m = jnp.max(m8, axis=0, keepdims=True)
                pw_ref[:, gslice(g)] = jnp.exp2(s_ref[:, gslice(g)] - m).astype(BF16)
            else:
                violw_ref[:, gslice(g)] = jnp.max(m8.astype(F32), axis=0, keepdims=True)
        for g in range(NSA_GROUPS):
            accw_ref[g] = _dot(values_with_ones(win_vt_tiles, g), pw_ref[:, gslice(g)])

    window(False)

    quota = top_n - (1 + (cur >= 1).astype(jnp.int32) + (cur >= 2).astype(jnp.int32))
    for g in range(NSA_GROUPS):
        work_ref[g] = jnp.where(forced, -2.0, jnp.where(valid, imps[g], -1.0))
        chosen_ref[g] = jnp.where(forced, 1.0, 0.0)

    def pick_round(index, limited):
        for g in range(NSA_GROUPS):
            work = work_ref[g]
            best = jnp.max(work, axis=0, keepdims=True)
            first = jnp.min(jnp.where(work == best, j_f32, float(n_sel)), axis=0, keepdims=True)
            hit = j_f32 == first
            if limited:
                hit = hit & (quota > index)
            chosen_ref[g] = jnp.where(hit, 1.0, chosen_ref[g])
            work_ref[g] = jnp.where(hit, -2.0, work)

    max_forced = 3
    for index in range(top_n - max_forced):
        pick_round(index, False)

    @pl.when(q0 < (max_forced - 1) * SEL_BLOCK)
    def _():
        for index in range(top_n - max_forced, top_n - 1):
            pick_round(index, True)

    for g in range(NSA_GROUPS):
        addm = jnp.where((chosen_ref[g] > 0.5) & valid, 0.0, NEG).astype(BF16)
        qaug_ref[g, HEAD_DIM:HEAD_DIM + n_sel, :] = jnp.concatenate([addm] * NSA_HPG, axis=1)

    def sel_step(r, near, exact):
        key_rows = pl.ds(pl.multiple_of(r * STEP, STEP), STEP)
        vt_tiles = [vslct_ref[STEP_TILES * r + t] for t in range(STEP_TILES)]
        if near:
            tile_ids = []
            for t in range(STEP_TILES):
                i = qb - (STEP_TILES * r + t)
                tile_ids.append(jnp.where(i < 0, TILE_MASKED, jnp.minimum(i, TILE_FAR)))
        logits = [_dot(kslc_ref[key_rows, g * AUG_DIM:(g + 1) * AUG_DIM], qaug_ref[g])
                  for g in range(NSA_GROUPS)]
        alphas = []
        for g in range(NSA_GROUPS):
            m_old = m_ref[:, gslice(g)]
            m8 = None
            for t in range(STEP_TILES):
                rows = slice(t * LANES, (t + 1) * LANES)
                v = logits[g][rows, :]
                if near:
                    v = v + tsel_ref[tile_ids[t], :, gslice(g)]
                if exact:
                    s_ref[rows, gslice(g)] = v
                    vm = jnp.max(v.reshape(LANES // 8, 8, GCOLS), axis=0)
                else:
                    pt = jnp.exp2(v - m_old).astype(BF16)
                    p_ref[rows, gslice(g)] = pt
                    vm = jnp.max(pt.reshape(LANES // 16, 16, GCOLS), axis=0)
                m8 = vm if m8 is None else jnp.maximum(m8, vm)
            if exact:
                m_new = jnp.maximum(m_old, jnp.max(m8, axis=0, keepdims=True))
                alphas.append(jnp.exp2(m_old - m_new))
                m_ref[:, gslice(g)] = m_new
                for t in range(STEP_TILES):
                    rows = slice(t * LANES, (t + 1) * LANES)
                    p_ref[rows, gslice(g)] = jnp.exp2(s_ref[rows, gslice(g)] - m_new).astype(BF16)
            else:
                viol_ref[:, gslice(g)] = jnp.maximum(viol_ref[:, gslice(g)],
                                                     jnp.max(m8.astype(F32), axis=0, keepdims=True))
        for g in range(NSA_GROUPS):
            pv = _dot(values_with_ones(vt_tiles, g), p_ref[0:STEP, gslice(g)])
            if exact:
                acc_ref[g] = acc_ref[g] * alphas[g] + pv
            else:
                acc_ref[g] = acc_ref[g] + pv

    r_diag = qb // STEP_TILES
    n_far = jnp.maximum((qb - (N_NEAR - 1)) // STEP_TILES, 0)

    def selected(exact):
        if exact:
            m_ref[...] = jnp.full(m_ref.shape, NEG, F32)
        else:
            kt = kslc_ref[diag_rows, :].astype(F32).T
            m_ref[...] = self_logit(jnp.concatenate(
                [kt[g * AUG_DIM:g * AUG_DIM + HEAD_DIM, :] for g in range(NSA_GROUPS)], axis=0))
        acc_ref[...] = jnp.zeros_like(acc_ref)

        def near_body(i, carry):
            sel_step(r_diag - i, True, exact)
            return carry

        def far_body(r, carry):
            sel_step(r, False, exact)
            return carry

        lax.fori_loop(0, r_diag - n_far + 1, near_body, 0)
        lax.fori_loop(0, n_far, far_body, 0)

    viol_ref[...] = jnp.zeros_like(viol_ref)
    selected(False)

    @pl.when(jnp.max(viol_ref[...]) > 2.0 ** MAX_EXP2_EXCESS)
    def _():
        selected(True)

    o_slc = [normalized(acc_ref[g]) for g in range(NSA_GROUPS)]

    @pl.when(jnp.max(violw_ref[...]) > 2.0 ** MAX_EXP2_EXCESS)
    def _():
        window(True)

    o_swa = [normalized(accw_ref[g]) for g in range(NSA_GROUPS)]

    gates = _sigmoid(gt_ref[...])
    for g in range(NSA_GROUPS):
        def gate_row(c):
            return jnp.concatenate(
                [gates[c * NSA_HEADS + g * NSA_HPG + r:c * NSA_HEADS + g * NSA_HPG + r + 1, :]
                 for r in range(NSA_HPG)], axis=1)
        og = gate_row(0) * o_cmp[g] + gate_row(1) * o_slc[g] + gate_row(2) * o_swa[g]
        for r in range(NSA_HPG):
            h = g * NSA_HPG + r
            ot_ref[h * HEAD_DIM:(h + 1) * HEAD_DIM, :] = og[:, r * Q_BLOCK:(r + 1) * Q_BLOCK]
    out_ref[...] = ot_ref[...].T


def _nsa(qt, gt, kc, vct, kslc, vslct, kswa, vswat, tsel, tcmp, ovt, far, seq):
    b = kc.shape[0]
    n_qb = seq // Q_BLOCK
    assert n_qb % STEP_TILES == 0 and n_qb >= WIN_TILES and WIN_TILES >= STEP_TILES
    n_cmp_rows = kc.shape[1]
    n_sel = seq // SEL_BLOCK
    assert n_sel <= AUG_DIM - HEAD_DIM
    top_n = min(SEL_TOPN, n_sel)
    assert top_n > 3
    n_gate = gt.shape[2]
    const = lambda arr: pl.BlockSpec(arr.shape, lambda i, j: (0,) * arr.ndim,
                                     pipeline_mode=pl.Buffered(1))
    kern = functools.partial(_nsa_kernel, n_qb=n_qb, n_cmp_rows=n_cmp_rows, n_sel=n_sel, top_n=top_n)
    return pl.pallas_call(
        kern,
        grid=(b, n_qb),
        in_specs=[pl.BlockSpec((None, None, NSA_WIDTH, Q_BLOCK), lambda i, j: (i, j, 0, 0)),
                  pl.BlockSpec((None, None, n_gate, Q_BLOCK), lambda i, j: (i, j, 0, 0)),
                  pl.BlockSpec((None, n_cmp_rows, KV_WIDTH), lambda i, j: (i, 0, 0)),
                  pl.BlockSpec((None, KV_WIDTH, n_cmp_rows), lambda i, j: (i, 0, 0)),
                  pl.BlockSpec((None, seq, NSA_GROUPS * AUG_DIM), lambda i, j: (i, 0, 0)),
                  pl.BlockSpec((None, n_qb, KV_WIDTH, Q_BLOCK), lambda i, j: (i, 0, 0, 0)),
                  pl.BlockSpec((None, seq, KV_WIDTH), lambda i, j: (i, 0, 0)),
                  pl.BlockSpec((None, n_qb, KV_WIDTH, Q_BLOCK), lambda i, j: (i, 0, 0, 0)),
                  const(tsel), const(tcmp), const(ovt), const(far)],
        out_specs=pl.BlockSpec((None, Q_BLOCK, NSA_WIDTH), lambda i, j: (i, j, 0)),
        out_shape=jax.ShapeDtypeStruct((b, seq, NSA_WIDTH), F32),
        scratch_shapes=[pltpu.VMEM((KV_WIDTH, QCOLS), BF16),
                        pltpu.VMEM((NSA_GROUPS, AUG_DIM, GCOLS), BF16),
                        pltpu.VMEM((1, QCOLS), F32),
                        pltpu.VMEM((NSA_GROUPS, HEAD_DIM + SUM_ROWS, GCOLS), F32),
                        pltpu.VMEM((NSA_WIDTH, Q_BLOCK), F32),
                        pltpu.VMEM((WIN_TILES * LANES, QCOLS), F32),
                        pltpu.VMEM((WIN_TILES * LANES, QCOLS), BF16),
                        pltpu.VMEM((1, QCOLS), F32),
                        pltpu.VMEM((NSA_GROUPS, n_sel, Q_BLOCK), F32),
                        pltpu.VMEM((NSA_GROUPS, n_sel, Q_BLOCK), F32),
                        pltpu.VMEM((WIN_TILES * LANES, QCOLS), BF16),
                        pltpu.VMEM((NSA_GROUPS, HEAD_DIM + SUM_ROWS, GCOLS), F32),
                        pltpu.VMEM((1, QCOLS), F32)],
        compiler_params=pltpu.CompilerParams(dimension_semantics=("parallel", "arbitrary"),
                                             vmem_limit_bytes=VMEM_LIMIT),
        name="nsa",
    )(qt, gt, kc, vct, kslc, vslct, kswa, vswat, tsel, tcmp, ovt, far)


def _ssd_kernel(xbc_ref, dt_ref, convw_ref, convb_ref, dtb_ref, a_ref, dskip_ref, expand_ref, y_ref,
                xp_ref, tail_ref, state_ref, slab_ref):
    L = SSM_CHUNK
    NV = L // 8
    halo = CONV_WIDTH - 1
    c = pl.program_id(1)

    @pl.when(c == 0)
    def _():
        tail_ref[...] = jnp.zeros_like(tail_ref)
        state_ref[...] = jnp.zeros_like(state_ref)

    n_slab = CONV_DIM // LANES
    for s in range(n_slab):
        slab_ref[s] = xbc_ref[:, s * LANES:(s + 1) * LANES]
    slab_ref[n_slab] = jnp.concatenate([dt_ref[...], jnp.zeros((L, LANES - SSM_HEADS), F32)], axis=1)
    for v in range(NV):
        for s in range(n_slab):
            xp_ref[(halo + v) * 8:(halo + v + 1) * 8, s * LANES:(s + 1) * LANES] = \
                slab_ref[s, pl.ds(v, 8, stride=NV), :]
    sub = lax.broadcasted_iota(jnp.int32, (8, CONV_DIM), 0)
    for w in range(halo):
        cur = xp_ref[(NV + w) * 8:(NV + w + 1) * 8, :]
        prev = tail_ref[w * 8:(w + 1) * 8, :]
        xp_ref[w * 8:(w + 1) * 8, :] = pltpu.roll(jnp.where(sub == 7, prev, cur), 1, 0)
    tail_ref[...] = xp_ref[NV * 8:(NV + halo) * 8, :]
    conv = convb_ref[...]
    for k in range(CONV_WIDTH):
        conv = conv + convw_ref[k:k + 1, :] * xp_ref[8 * k:8 * k + L, :]
    xbc = _silu(conv)
    xs = xbc[:, 0:SSM_WIDTH]
    bm = xbc[:, SSM_WIDTH:SSM_WIDTH + SSM_GROUPS * SSM_STATE]
    cm = xbc[:, SSM_WIDTH + SSM_GROUPS * SSM_STATE:]

    dt_raw = jnp.concatenate([slab_ref[n_slab, pl.ds(v, 8, stride=NV), :] for v in range(NV)], axis=0)
    dt = _softplus(dt_raw[:, 0:SSM_HEADS] + dtb_ref[...])
    a_row = -jnp.exp(a_ref[...])
    row_i = lax.broadcasted_iota(jnp.int32, (L, L), 0)
    col_j = lax.broadcasted_iota(jnp.int32, (L, L), 1)
    token = lambda r: (r % 8) * NV + r // 8
    tril = token(row_i) >= token(col_j)
    tril01 = jnp.where(tril, 1.0, 0.0).astype(BF16)
    cs = _dot_exact_rhs(tril01, dt * a_row)
    cst = jnp.concatenate([cs, jnp.zeros((L, L - SSM_HEADS), F32)], axis=1).T

    expand = expand_ref[...]
    cs_x = _dot_exact_lhs(cs, expand)
    dt_x = _dot_exact_lhs(dt, expand)
    total_x = cs_x[L - 1:L, :]
    xdt = (xs * dt_x).astype(BF16)
    xw = (xs * (jnp.exp(total_x - cs_x) * dt_x)).astype(BF16)
    decay_out = jnp.exp(cs_x)
    decay_state = jnp.exp(total_x)

    lane = lax.broadcasted_iota(jnp.int32, (L, LANES), 1)
    gw = SSM_HPG * SSM_HEAD_DIM
    for g in range(SSM_GROUPS):
        bg = bm[:, g * SSM_STATE:(g + 1) * SSM_STATE]
        cg = cm[:, g * SSM_STATE:(g + 1) * SSM_STATE].astype(BF16)
        cb = _dot_nt(cg, bg.astype(BF16))
        st = state_ref[g]
        y_g = _dot(cg, st.astype(BF16)) * decay_out[:, g * gw:(g + 1) * gw]
        state_ref[g] = st * decay_state[:, g * gw:(g + 1) * gw] + \
            _dot(bg.T.astype(BF16), xw[:, g * gw:(g + 1) * gw])
        pieces = []
        for pair in range(SSM_HPG // 2):
            ws = []
            for hh in range(2):
                h = g * SSM_HPG + 2 * pair + hh
                diff = cs[:, h:h + 1] - cst[h:h + 1, :]
                decay = jnp.where(tril, jnp.exp(jnp.where(tril, diff, 0.0)), 0.0)
                ws.append((cb * decay).astype(BF16))
            w_pair = jnp.concatenate(ws, axis=1)
            lo = g * gw + pair * LANES
            slab = xdt[:, lo:lo + LANES]
            zero = jnp.zeros_like(slab)
            x_bd = jnp.concatenate([jnp.where(lane < SSM_HEAD_DIM, slab, zero),
                                    jnp.where(lane >= SSM_HEAD_DIM, slab, zero)], axis=0)
            pieces.append(_dot(w_pair, x_bd))
        y_g = y_g + jnp.concatenate(pieces, axis=1) + \
            dskip_ref[:, g * gw:(g + 1) * gw] * xs[:, g * gw:(g + 1) * gw]
        for s in range(gw // LANES):
            for v in range(NV):
                slab_ref[s, pl.ds(v, 8, stride=NV), :] = y_g[v * 8:(v + 1) * 8, s * LANES:(s + 1) * LANES]
        for s in range(gw // LANES):
            y_ref[:, g * gw + s * LANES:g * gw + (s + 1) * LANES] = slab_ref[s]


def _ssd(xbc, dt, convw, convb, dtb, a_log, dskip_x, expand, seq):
    b = xbc.shape[0]
    nc = seq // SSM_CHUNK
    halo_rows = (CONV_WIDTH - 1) * 8
    const = lambda arr: pl.BlockSpec(arr.shape, lambda i, j: (0,) * arr.ndim)
    return pl.pallas_call(
        _ssd_kernel,
        grid=(b, nc),
        in_specs=[pl.BlockSpec((None, SSM_CHUNK, CONV_DIM), lambda i, j: (i, j, 0)),
                  pl.BlockSpec((None, SSM_CHUNK, SSM_HEADS), lambda i, j: (i, j, 0)),
                  const(convw), const(convb), const(dtb), const(a_log), const(dskip_x), const(expand)],
        out_specs=pl.BlockSpec((None, SSM_CHUNK, SSM_WIDTH), lambda i, j: (i, j, 0)),
        out_shape=jax.ShapeDtypeStruct((b, seq, SSM_WIDTH), F32),
        scratch_shapes=[pltpu.VMEM((SSM_CHUNK + halo_rows, CONV_DIM), F32),
                        pltpu.VMEM((halo_rows, CONV_DIM), F32),
                        pltpu.VMEM((SSM_GROUPS, SSM_STATE, SSM_HPG * SSM_HEAD_DIM), F32),
                        pltpu.VMEM((CONV_DIM // LANES + 1, SSM_CHUNK, LANES), F32)],
        compiler_params=pltpu.CompilerParams(dimension_semantics=("parallel", "arbitrary"),
                                             vmem_limit_bytes=VMEM_LIMIT),
        name="ssd",
    )(xbc, dt, convw, convb, dtb, a_log, dskip_x, expand)


def _epilogue_kernel(x_ref, o_ref, y_ref, nw_ref, wz1_ref, wz2_ref, wmg_ref, won_ref, wos_ref, wo_ref,
                     snw_ref, fnw_ref, out_ref):
    x = x_ref[...]
    ms = jnp.mean(x * x, axis=-1, keepdims=True)
    xn = (x * lax.rsqrt(ms + NORM_EPS) * nw_ref[...]).astype(BF16)
    u = (o_ref[...] * _silu(_dot(xn, wz1_ref[...]))).astype(BF16)
    h_nsa = _dot(u, won_ref[...])
    hh = y_ref[...] * _silu(_dot(xn, wz2_ref[...]))
    gw = SSM_WIDTH // SSM_GROUPS
    parts = []
    for g in range(SSM_GROUPS):
        hg = hh[:, g * gw:(g + 1) * gw]
        hg = hg * lax.rsqrt(jnp.mean(hg * hg, axis=-1, keepdims=True) + NORM_EPS)
        parts.append((hg * snw_ref[:, g * gw:(g + 1) * gw]).astype(BF16))
    h_ssm = _dot(jnp.concatenate(parts, axis=1), wos_ref[...])
    gate = _sigmoid(_dot(xn, wmg_ref[...]))
    mix = (gate[:, 0:D_MODEL] * h_nsa + gate[:, D_MODEL:] * h_ssm).astype(BF16)
    r = x + _dot(mix, wo_ref[...])
    ms2 = jnp.mean(r * r, axis=-1, keepdims=True)
    out_ref[...] = r * lax.rsqrt(ms2 + NORM_EPS) * fnw_ref[...]


def _epilogue(x2, o2, y2, nw, wz1, wz2, wmg, won, wos, wo, snw, fnw, tm):
    t, d = x2.shape
    assert t % tm == 0
    const = lambda arr: pl.BlockSpec(arr.shape, lambda i: (0,) * arr.ndim, pipeline_mode=pl.Buffered(1))
    return pl.pallas_call(
        _epilogue_kernel,
        grid=(t // tm,),
        in_specs=[pl.BlockSpec((tm, d), lambda i: (i, 0)),
                  pl.BlockSpec((tm, NSA_WIDTH), lambda i: (i, 0)),
                  pl.BlockSpec((tm, SSM_WIDTH), lambda i: (i, 0)),
                  const(nw), const(wz1), const(wz2), const(wmg), const(won), const(wos), const(wo),
                  const(snw), const(fnw)],
        out_specs=pl.BlockSpec((tm, d), lambda i: (i, 0)),
        out_shape=jax.ShapeDtypeStruct((t, d), F32),
        compiler_params=pltpu.CompilerParams(dimension_semantics=("parallel",),
                                             vmem_limit_bytes=VMEM_LIMIT),
        name="epilogue",
    )(x2, o2, y2, nw, wz1, wz2, wmg, won, wos, wo, snw, fnw)


def _table_kernel(rb_ref, bucket_ref, out_ref):
    bk = bucket_ref[...]
    for h in range(NSA_HEADS):
        acc = jnp.full(bk.shape, NEG, F32)
        for b in range(REL_BUCKETS):
            acc = jnp.where(bk == b, rb_ref[b, h], acc)
        out_ref[:, h * Q_BLOCK:(h + 1) * Q_BLOCK] = acc


def _bias_table(rb, buckets, tr):
    rows = buckets.shape[0]
    assert rows % tr == 0
    return pl.pallas_call(
        _table_kernel,
        grid=(rows // tr,),
        in_specs=[pl.BlockSpec(memory_space=pltpu.SMEM),
                  pl.BlockSpec((tr, Q_BLOCK), lambda i: (i, 0))],
        out_specs=pl.BlockSpec((tr, QCOLS), lambda i: (i, 0)),
        out_shape=jax.ShapeDtypeStruct((rows, QCOLS), F32),
        compiler_params=pltpu.CompilerParams(dimension_semantics=("parallel",)),
        name="bias_table",
    )(rb, buckets)


def _bucket_tables(seq):
    n_qb = seq // Q_BLOCK
    key = np.arange(LANES)[:, None]
    tok = np.arange(Q_BLOCK)[None, :]
    tiles = [_t5_bucket_np(LANES * o + tok - key) for o in range(N_NEAR)]
    tiles.append(np.full((LANES, Q_BLOCK), REL_BUCKETS - 1, np.int32))
    dwin = WINDOW + tok - key
    tiles.append(_t5_bucket_np(np.where(dwin < WINDOW, dwin, -1)))
    tiles.append(np.full((LANES, Q_BLOCK), MASKED_BUCKET, np.int32))
    assert len(tiles) == N_TILES and LANES * N_NEAR - (LANES - 1) >= 790
    n_cmp_rows = seq // CMP_STRIDE
    rows = 8 * (n_qb - 1) + n_cmp_rows
    rows_pad = -(-rows // LANES) * LANES
    u = np.arange(rows_pad)[:, None]
    cmp_tbl = _t5_bucket_np(tok - CMP_STRIDE * u + Q_BLOCK * (n_qb - 1) - (CMP_BLOCK - 1))
    return np.concatenate(tiles, axis=0), cmp_tbl


def _block_onehot(seq, tm):
    pos = np.arange(seq)
    pat = np.zeros((seq, NSA_GROUPS, AUG_DIM), np.float32)
    pat[pos, :, HEAD_DIM + pos // SEL_BLOCK] = 1.0
    return jnp.asarray(pat.reshape(seq // tm, tm, NSA_GROUPS * AUG_DIM), dtype=BF16)


def _overlap_t(seq):
    n_cmp_rows = seq // CMP_STRIDE
    n_sel = seq // SEL_BLOCK
    c_start = np.arange(n_cmp_rows)[None, :] * CMP_STRIDE
    s_start = np.arange(n_sel)[:, None] * SEL_BLOCK
    ov = (c_start < s_start + SEL_BLOCK) & (c_start + CMP_BLOCK > s_start)
    ov[:, n_cmp_rows - 1] = False
    return jnp.asarray(ov, dtype=BF16)


def _layer(x, norm_w, w_in, cmp_pos_k, cmp_pos_v, cmp_k_w1, cmp_k_b1, cmp_k_w2, cmp_v_w1, cmp_v_b1,
           cmp_v_w2, conv_w, conv_b, dt_bias, a_log, d_skip, ssm_norm_w, w_out_nsa, w_out_ssm, w_out,
           rel_bias, out_norm_w):
    b, s, d = x.shape
    assert d == D_MODEL and s % STEP == 0 and s >= 2 * WINDOW
    t = b * s
    n_qb = s // Q_BLOCK
    cols = _column_offsets()
    wcol = lambda name: w_in[:, cols[name][0]:cols[name][1]]
    x2 = x.reshape(t, d)
    nw = norm_w.reshape(1, d).astype(F32)

    gate_w = wcol('nsa_gate').reshape(d, NSA_HEADS, 3).transpose(0, 2, 1).reshape(d, 3 * NSA_HEADS)
    w_kslc = jnp.pad(wcol('k_slc').reshape(d, NSA_GROUPS, HEAD_DIM),
                     ((0, 0), (0, 0), (0, AUG_DIM - HEAD_DIM))).reshape(d, NSA_GROUPS * AUG_DIM)
    wn = jnp.concatenate([w_kslc, wcol('k_swa'), wcol('k_cmp'), wcol('v_cmp'), wcol('xbc'),
                          wcol('dt')], axis=1).astype(BF16)
    wt = jnp.concatenate([wcol('q') * (HEAD_DIM ** -0.5 * LOG2E), wcol('v_slc'), wcol('v_swa'), gate_w],
                         axis=1).T.astype(BF16)
    kslc, kswa, cc, xbc, dt, qt, vslct, vswat, gt = _proj(x2, nw, wn, wt, _block_onehot(s, STEP), STEP)
    kslc = kslc.reshape(b, s, NSA_GROUPS * AUG_DIM)
    kswa = kswa.reshape(b, s, KV_WIDTH)
    xbc = xbc.reshape(b, s, CONV_DIM)
    dt = dt.reshape(b, s, SSM_HEADS)
    qt = qt.reshape(b, n_qb, NSA_WIDTH, Q_BLOCK)
    vslct = vslct.reshape(b, n_qb, KV_WIDTH, Q_BLOCK)
    vswat = vswat.reshape(b, n_qb, KV_WIDTH, Q_BLOCK)
    gt = gt.reshape(b, n_qb, 3 * NSA_HEADS, Q_BLOCK)

    nseg = s // CMP_STRIDE

    def seg_major(a):
        a = a.reshape(b, nseg, CMP_STRIDE, NSA_GROUPS, HEAD_DIM)
        return a.transpose(0, 3, 1, 2, 4).reshape(b, NSA_GROUPS, nseg, CMP_STRIDE * HEAD_DIM)

    pos2 = lambda p: p.astype(F32).reshape(2, CMP_STRIDE * HEAD_DIM)
    kc, vct = _compress(seg_major(cc[:, :KV_WIDTH]), seg_major(cc[:, KV_WIDTH:]),
                        pos2(cmp_pos_k), pos2(cmp_pos_v),
                        cmp_k_w1.astype(BF16), cmp_k_b1.reshape(1, -1).astype(F32), cmp_k_w2.astype(BF16),
                        cmp_v_w1.astype(BF16), cmp_v_b1.reshape(1, -1).astype(F32), cmp_v_w2.T.astype(BF16))

    sel_buckets, cmp_buckets = _bucket_tables(s)
    rb = rel_bias.astype(F32) * LOG2E
    rb_far = rb[REL_BUCKETS - 1:REL_BUCKETS, :]
    tsel = _bias_table(rb - rb_far, jnp.asarray(sel_buckets), LANES).reshape(N_TILES, LANES, QCOLS)
    tcmp = _bias_table(rb, jnp.asarray(cmp_buckets), LANES)
    far = jnp.repeat(rb_far.reshape(-1), Q_BLOCK).reshape(1, QCOLS)
    o_nsa = _nsa(qt, gt, kc, vct, kslc, vslct, kswa, vswat, tsel, tcmp, _overlap_t(s), far, s)

    expand = jnp.asarray(np.kron(np.eye(SSM_HEADS), np.ones((1, SSM_HEAD_DIM))), dtype=BF16)
    y = _ssd(xbc, dt, conv_w.astype(F32), conv_b.reshape(1, -1).astype(F32),
             dt_bias.reshape(1, -1).astype(F32), a_log.reshape(1, -1).astype(F32),
             jnp.repeat(d_skip.astype(F32), SSM_HEAD_DIM).reshape(1, -1), expand, s)

    out = _epilogue(x2, o_nsa.reshape(t, NSA_WIDTH), y.reshape(t, SSM_WIDTH), nw,
                    wcol('z_nsa').astype(BF16), wcol('z_ssm').astype(BF16), wcol('merge_gate').astype(BF16),
                    w_out_nsa.astype(BF16), w_out_ssm.astype(BF16), w_out.astype(BF16),
                    ssm_norm_w.reshape(1, -1).astype(F32), out_norm_w.reshape(1, -1).astype(F32),
                    256 if t % 256 == 0 else Q_BLOCK)
    return out.reshape(b, s, d)


def kernel(x, norm_w, w_in, cmp_pos_k, cmp_pos_v, cmp_k_w1, cmp_k_b1, cmp_k_w2, cmp_v_w1, cmp_v_b1, cmp_v_w2,
           conv_w, conv_b, dt_bias, a_log, d_skip, ssm_norm_w, w_out_nsa, w_out_ssm, w_out, rel_bias,
           final_norm_w):
    depth = norm_w.shape[0]
    assert depth == 1, "the epilogue fuses the final norm into the single layer"
    return _layer(x, norm_w[0], w_in[0], cmp_pos_k[0], cmp_pos_v[0], cmp_k_w1[0], cmp_k_b1[0], cmp_k_w2[0],
                  cmp_v_w1[0], cmp_v_b1[0], cmp_v_w2[0], conv_w[0], conv_b[0], dt_bias[0], a_log[0],
                  d_skip[0], ssm_norm_w[0], w_out_nsa[0], w_out_ssm[0], w_out[0], rel_bias, final_norm_w)
```

```python
import functools
import math

import numpy as np
import jax
import jax.numpy as jnp
from jax import lax
from jax.experimental import pallas as pl
from jax.experimental.pallas import tpu as pltpu

F32 = jnp.float32
BF16 = jnp.bfloat16

D_MODEL = 1024
NSA_HEADS = 16
NSA_GROUPS = 4
NSA_HPG = NSA_HEADS // NSA_GROUPS
HEAD_DIM = 64
NSA_WIDTH = NSA_HEADS * HEAD_DIM
KV_WIDTH = NSA_GROUPS * HEAD_DIM
CMP_BLOCK = 32
CMP_STRIDE = 16
CMP_HIDDEN = 4 * HEAD_DIM
SEL_BLOCK = 64
SEL_TOPN = 8
WINDOW = 512
Q_BLOCK = 128
FORCE_SCORE = 1.0e4
REL_BUCKETS = 32
SSM_WIDTH = 2 * D_MODEL
SSM_HEAD_DIM = 64
SSM_HEADS = SSM_WIDTH // SSM_HEAD_DIM
SSM_GROUPS = 4
SSM_HPG = SSM_HEADS // SSM_GROUPS
SSM_STATE = 128
CONV_WIDTH = 4
SSM_CHUNK = 128
CONV_DIM = SSM_WIDTH + 2 * SSM_GROUPS * SSM_STATE
NORM_EPS = 1e-6

NEG = -1e30
LOG2E = math.log2(math.e)
LANES = 128
QCOLS = NSA_HEADS * Q_BLOCK
GCOLS = NSA_HPG * Q_BLOCK
AUG_DIM = 2 * HEAD_DIM
N_NEAR = 8
TILE_FAR = N_NEAR
TILE_WIN_OLD = N_NEAR + 1
TILE_MASKED = N_NEAR + 2
N_TILES = N_NEAR + 3
MASKED_BUCKET = REL_BUCKETS
SUM_ROWS = 16
MAX_EXP2_EXCESS = 100.0
STEP_TILES = 4
STEP = STEP_TILES * LANES
WIN_TILES = WINDOW // LANES + 1
VMEM_LIMIT = 56 * 1024 * 1024


def _column_offsets():
    sizes = (('q', NSA_WIDTH), ('k_cmp', KV_WIDTH), ('v_cmp', KV_WIDTH), ('k_slc', KV_WIDTH),
             ('v_slc', KV_WIDTH), ('k_swa', KV_WIDTH), ('v_swa', KV_WIDTH), ('nsa_gate', 3 * NSA_HEADS),
             ('z_nsa', NSA_WIDTH), ('z_ssm', SSM_WIDTH), ('xbc', CONV_DIM), ('dt', SSM_HEADS),
             ('merge_gate', 2 * D_MODEL))
    out, lo = {}, 0
    for name, n in sizes:
        out[name] = (lo, lo + n)
        lo += n
    return out


def _t5_bucket_np(dist):
    dist = np.asarray(dist, dtype=np.int64)
    d = np.maximum(dist, 0)
    max_exact = REL_BUCKETS // 2
    large = np.full(d.shape, max_exact, dtype=np.int64)
    d8 = d.astype(object) ** 8
    for k in range(1, REL_BUCKETS - max_exact):
        large = large + (d8 >= 2 ** (32 + 3 * k)).astype(np.int64)
    bucket = np.where(d < max_exact, d, np.minimum(large, REL_BUCKETS - 1))
    return np.where(dist < 0, MASKED_BUCKET, bucket).astype(np.int32)


def _sigmoid(x):
    return 1.0 / (1.0 + jnp.exp(-x))


def _silu(x):
    return x * _sigmoid(x)


def _softplus(x):
    return jnp.maximum(x, 0.0) + jnp.log1p(jnp.exp(-jnp.abs(x)))


def _dot(a, b):
    return jnp.dot(a, b, preferred_element_type=F32)


def _dot_nt(a, b):
    return lax.dot_general(a, b, (((1,), (1,)), ((), ())), preferred_element_type=F32)


def _split3(x):
    hi = x.astype(BF16)
    r1 = x - hi.astype(F32)
    mid = r1.astype(BF16)
    lo = (r1 - mid.astype(F32)).astype(BF16)
    return hi, mid, lo


def _dot_exact_lhs(x, w01):
    hi, mid, lo = _split3(x)
    return _dot(hi, w01) + _dot(mid, w01) + _dot(lo, w01)


def _dot_exact_rhs(w01, x):
    hi, mid, lo = _split3(x)
    return _dot(w01, hi) + _dot(w01, mid) + _dot(w01, lo)


_NAT_OUTS = (('k_slc', NSA_GROUPS * AUG_DIM, BF16), ('k_swa', KV_WIDTH, BF16), ('kv_cmp', 2 * KV_WIDTH, F32),
             ('xbc', CONV_DIM, F32), ('dt', SSM_HEADS, F32))
_TR_OUTS = (('q', NSA_WIDTH, BF16), ('v_slc', KV_WIDTH, BF16), ('v_swa', KV_WIDTH, BF16),
            ('gate', 3 * NSA_HEADS, F32))


def _proj_kernel(x_ref, nw_ref, wn_ref, wt_ref, onehot_ref, *out_refs, n_sub):
    x = x_ref[...]
    ms = jnp.mean(x * x, axis=-1, keepdims=True)
    xn = (x * lax.rsqrt(ms + NORM_EPS) * nw_ref[...]).astype(BF16)
    lo = 0
    for (name, n, _), o_ref in zip(_NAT_OUTS, out_refs[:len(_NAT_OUTS)]):
        res = _dot(xn, wn_ref[:, lo:lo + n])
        if name == 'k_slc':
            res = res + onehot_ref[...]
        if name == 'kv_cmp':
            for j in range(n // LANES):
                o_ref[j] = res[:, j * LANES:(j + 1) * LANES].astype(o_ref.dtype)
        else:
            o_ref[...] = res.astype(o_ref.dtype)
        lo += n
    lo = 0
    for (_, n, _), o_ref in zip(_TR_OUTS, out_refs[len(_NAT_OUTS):]):
        res = _dot_nt(wt_ref[lo:lo + n, :], xn)
        for s in range(n_sub):
            o_ref[s] = res[:, s * LANES:(s + 1) * LANES].astype(o_ref.dtype)
        lo += n


def _proj(x2, nw, wn, wt, onehot, tm):
    t, d = x2.shape
    assert t % tm == 0 and tm % LANES == 0 and onehot.shape[1] == tm
    n_sub = tm // LANES
    n_pat = onehot.shape[0]
    const = lambda arr: pl.BlockSpec(arr.shape, lambda i: (0,) * arr.ndim, pipeline_mode=pl.Buffered(1))
    slabbed = lambda name: name == 'kv_cmp'
    out_specs = [pl.BlockSpec((n // LANES, tm, LANES), lambda i: (0, i, 0)) if slabbed(name)
                 else pl.BlockSpec((tm, n), lambda i: (i, 0)) for name, n, _ in _NAT_OUTS] + \
                [pl.BlockSpec((n_sub, n, LANES), lambda i: (i, 0, 0)) for _, n, _ in _TR_OUTS]
    out_shape = [jax.ShapeDtypeStruct((n // LANES, t, LANES) if slabbed(name) else (t, n), dt)
                 for name, n, dt in _NAT_OUTS] + \
                [jax.ShapeDtypeStruct((t // LANES, n, LANES), dt) for _, n, dt in _TR_OUTS]
    return pl.pallas_call(
        functools.partial(_proj_kernel, n_sub=n_sub),
        grid=(t // tm,),
        in_specs=[pl.BlockSpec((tm, d), lambda i: (i, 0)), const(nw), const(wn), const(wt),
                  pl.BlockSpec((None, tm, onehot.shape[2]), lambda i: (i % n_pat, 0, 0))],
        out_specs=out_specs,
        out_shape=out_shape,
        compiler_params=pltpu.CompilerParams(dimension_semantics=("parallel",),
                                             vmem_limit_bytes=VMEM_LIMIT),
        name="proj",
    )(x2, nw, wn, wt, onehot)


def _compress_kernel(raw_ref, posk_ref, posv_ref, w1k_ref, b1k_ref, w2k_ref,
                     w1v_ref, b1v_ref, w2vt_ref, kc_ref, vct_ref):
    nseg = kc_ref.shape[0]
    n_slab = raw_ref.shape[0]
    per_kv = n_slab // 2
    for j in range(n_slab):
        is_k = j < per_kv
        pos_ref, w1_ref, b1_ref = (posk_ref, w1k_ref, b1k_ref) if is_k else (posv_ref, w1v_ref, b1v_ref)
        first = jnp.zeros((nseg, 2 * CMP_HIDDEN), F32)
        second = jnp.zeros((nseg, 2 * CMP_HIDDEN), F32)
        for p in range(CMP_STRIDE):
            rows = raw_ref[j, pl.ds(p, nseg, stride=CMP_STRIDE), :]
            first = first + _dot((rows + pos_ref[p:p + 1, :]).astype(BF16), w1_ref[p])
            q = CMP_STRIDE + p
            second = second + _dot((rows + pos_ref[q:q + 1, :]).astype(BF16), w1_ref[q])
        hid = _silu(first + pltpu.roll(second, nseg - 1, 0) + b1_ref[...]).astype(BF16)
        for gi in range(2):
            g = 2 * (j % per_kv) + gi
            hg = hid[:, gi * CMP_HIDDEN:(gi + 1) * CMP_HIDDEN]
            if is_k:
                kc_ref[:, g * HEAD_DIM:(g + 1) * HEAD_DIM] = _dot(hg, w2k_ref[...]).astype(kc_ref.dtype)
            else:
                vct_ref[g * HEAD_DIM:(g + 1) * HEAD_DIM, :] = _dot_nt(w2vt_ref[...], hg).astype(vct_ref.dtype)


def _compress(raw, batch, posk, posv, w1k, b1k, w2k, w1v, b1v, w2vt):
    n_slab, t, lanes = raw.shape
    seq = t // batch
    nseg = seq // CMP_STRIDE
    assert n_slab * lanes == 2 * KV_WIDTH and lanes == 2 * HEAD_DIM
    const = lambda arr: pl.BlockSpec(arr.shape, lambda i: (0,) * arr.ndim, pipeline_mode=pl.Buffered(1))
    return pl.pallas_call(
        _compress_kernel,
        grid=(batch,),
        in_specs=[pl.BlockSpec((n_slab, seq, lanes), lambda i: (0, i, 0)),
                  const(posk), const(posv), const(w1k), const(b1k), const(w2k),
                  const(w1v), const(b1v), const(w2vt)],
        out_specs=[pl.BlockSpec((None, nseg, KV_WIDTH), lambda i: (i, 0, 0)),
                   pl.BlockSpec((None, KV_WIDTH, nseg), lambda i: (i, 0, 0))],
        out_shape=[jax.ShapeDtypeStruct((batch, nseg, KV_WIDTH), BF16),
                   jax.ShapeDtypeStruct((batch, KV_WIDTH, nseg), BF16)],
        compiler_params=pltpu.CompilerParams(dimension_semantics=("parallel",),
                                             vmem_limit_bytes=VMEM_LIMIT),
        name="compress",
    )(raw, posk, posv, w1k, b1k, w2k, w1v, b1v, w2vt)


def _nsa_kernel(qt_ref, gt_ref, kc_ref, vct_ref, kslc_ref, vslct_ref, kswa_ref, vswat_ref,
                tsel_ref, tcmp_ref, ovt_ref, far_ref, out_ref,
                qbd_ref, qaug_ref, m_ref, acc_ref, ot_ref, s_ref, p_ref, viol_ref, work_ref, chosen_ref,
                *, n_qb, n_cmp_rows, n_sel, top_n):
    qb = pl.program_id(1)
    q0 = qb * Q_BLOCK

    @pl.when(qb == 0)
    def _():
        qbd_ref[...] = jnp.zeros_like(qbd_ref)
        qaug_ref[...] = jnp.zeros_like(qaug_ref)

    for g in range(NSA_GROUPS):
        for r in range(NSA_HPG):
            h = g * NSA_HPG + r
            q_head = qt_ref[h * HEAD_DIM:(h + 1) * HEAD_DIM, :]
            qbd_ref[g * HEAD_DIM:(g + 1) * HEAD_DIM, h * Q_BLOCK:(h + 1) * Q_BLOCK] = q_head
            qaug_ref[g, 0:HEAD_DIM, r * Q_BLOCK:(r + 1) * Q_BLOCK] = q_head

    def gslice(g):
        return slice(g * GCOLS, (g + 1) * GCOLS)

    def values_with_ones(vt_tiles, g):
        vt = jnp.concatenate([t[g * HEAD_DIM:(g + 1) * HEAD_DIM, :] for t in vt_tiles], axis=1)
        return jnp.concatenate([vt, jnp.ones((SUM_ROWS, vt.shape[1]), BF16)], axis=0)

    def normalized(a):
        return a[0:HEAD_DIM, :] * (1.0 / a[HEAD_DIM:HEAD_DIM + 1, :])

    def self_logit(kt):
        cols = []
        for h in range(NSA_HEADS):
            g = h // NSA_HPG
            prod = qt_ref[h * HEAD_DIM:(h + 1) * HEAD_DIM, :].astype(F32) * kt[g * HEAD_DIM:(g + 1) * HEAD_DIM, :]
            cols.append(jnp.sum(prod, axis=0, keepdims=True))
        return jnp.concatenate(cols, axis=1) - far_ref[...]

    diag_rows = pl.ds(pl.multiple_of(q0, Q_BLOCK), Q_BLOCK)

    u0 = pl.multiple_of(8 * (n_qb - 1 - qb), 8)
    j_iota = lax.broadcasted_iota(jnp.int32, (n_sel, Q_BLOCK), 0)
    j_f32 = j_iota.astype(F32)
    t_pos = q0 + lax.broadcasted_iota(jnp.int32, (n_sel, Q_BLOCK), 1)
    cur = t_pos // SEL_BLOCK
    valid = j_iota <= cur
    forced = (j_iota == 0) | (j_iota == cur) | (j_iota == cur - 1)

    logits = [_dot(kc_ref[...], qbd_ref[:, gslice(g)]) for g in range(NSA_GROUPS)]
    for g in range(NSA_GROUPS):
        s = logits[g] + tcmp_ref[pl.ds(u0, n_cmp_rows), gslice(g)]
        m = jnp.maximum(jnp.max(s, axis=0, keepdims=True), 0.1 * NEG)
        p_ref[0:n_cmp_rows, gslice(g)] = jnp.exp2(s - m).astype(BF16)
    o_cmp, imps = [], []
    for g in range(NSA_GROUPS):
        e = p_ref[0:n_cmp_rows, gslice(g)]
        raw = _dot(values_with_ones([vct_ref[...]], g), e)
        total = raw[HEAD_DIM:HEAD_DIM + 1, :]
        inv = 1.0 / jnp.where(total == 0.0, 1.0, total)
        o_cmp.append(raw[0:HEAD_DIM, :] * inv)
        imp_heads = _dot(ovt_ref[...], e) * inv
        imp = imp_heads[:, 0:Q_BLOCK]
        for r in range(1, NSA_HPG):
            imp = imp + imp_heads[:, r * Q_BLOCK:(r + 1) * Q_BLOCK]
        imps.append(imp)

    def window(exact):
        first_tile = jnp.maximum(qb - (WIN_TILES - 1), 0)
        kw = kswa_ref[pl.ds(pl.multiple_of(first_tile * LANES, LANES), WIN_TILES * LANES), :]
        win_vt_tiles = [vswat_ref[first_tile + t] for t in range(WIN_TILES)]
        win_ids = []
        for t in range(WIN_TILES):
            i = qb - (first_tile + t)
            win_ids.append(jnp.where(i < 0, TILE_MASKED, jnp.where(i == WIN_TILES - 1, TILE_WIN_OLD, i)))
        logits = [_dot(kw, qbd_ref[:, gslice(g)]) for g in range(NSA_GROUPS)]
        if not exact:
            m_fix = self_logit(kswa_ref[diag_rows, :].astype(F32).T)
        for g in range(NSA_GROUPS):
            m8 = None
            for t in range(WIN_TILES):
                rows = slice(t * LANES, (t + 1) * LANES)
                v = logits[g][rows, :] + tsel_ref[win_ids[t], :, gslice(g)]
                if exact:
                    s_ref[rows, gslice(g)] = v
                    vm = jnp.max(v.reshape(LANES // 8, 8, GCOLS), axis=0)
                else:
                    pt = jnp.exp2(v - m_fix[:, gslice(g)]).astype(BF16)
                    p_ref[rows, gslice(g)] = pt
                    vm = jnp.max(pt.reshape(LANES // 16, 16, GCOLS), axis=0)
                m8 = vm if m8 is None else jnp.maximum(m8, vm)
            if exact:
                m = jnp.max(m8, axis=0, keepdims=True)
                p_ref[:, gslice(g)] = jnp.exp2(s_ref[:, gslice(g)] - m).astype(BF16)
            else:
                viol_ref[:, gslice(g)] = jnp.max(m8.astype(F32), axis=0, keepdims=True)
        for g in range(NSA_GROUPS):
            acc_ref[g] = _dot(values_with_ones(win_vt_tiles, g), p_ref[:, gslice(g)])

    quota = top_n - (1 + (cur >= 1).astype(jnp.int32) + (cur >= 2).astype(jnp.int32))
    for g in range(NSA_GROUPS):
        work_ref[g] = jnp.where(forced, -2.0, jnp.where(valid, imps[g], -1.0))
        chosen_ref[g] = jnp.where(forced, 1.0, 0.0)

    def pick_round(index, limited):
        for g in range(NSA_GROUPS):
            work = work_ref[g]
            best = jnp.max(work, axis=0, keepdims=True)
            first = jnp.min(jnp.where(work == best, j_f32, float(n_sel)), axis=0, keepdims=True)
            hit = j_f32 == first
            if limited:
                hit = hit & (quota > index)
            chosen_ref[g] = jnp.where(hit, 1.0, chosen_ref[g])
            work_ref[g] = jnp.where(hit, -2.0, work)

    max_forced = 3
    for index in range(top_n - max_forced):
        pick_round(index, False)

    @pl.when(q0 < (max_forced - 1) * SEL_BLOCK)
    def _():
        for index in range(top_n - max_forced, top_n - 1):
            pick_round(index, True)

    for g in range(NSA_GROUPS):
        addm = jnp.where((chosen_ref[g] > 0.5) & valid, 0.0, NEG).astype(BF16)
        qaug_ref[g, HEAD_DIM:HEAD_DIM + n_sel, :] = jnp.concatenate([addm] * NSA_HPG, axis=1)

    def sel_step(r, near, exact):
        key_rows = pl.ds(pl.multiple_of(r * STEP, STEP), STEP)
        vt_tiles = [vslct_ref[STEP_TILES * r + t] for t in range(STEP_TILES)]
        if near:
            tile_ids = []
            for t in range(STEP_TILES):
                i = qb - (STEP_TILES * r + t)
                tile_ids.append(jnp.where(i < 0, TILE_MASKED, jnp.minimum(i, TILE_FAR)))
        logits = [_dot(kslc_ref[key_rows, g * AUG_DIM:(g + 1) * AUG_DIM], qaug_ref[g])
                  for g in range(NSA_GROUPS)]
        alphas = []
        for g in range(NSA_GROUPS):
            m_old = m_ref[:, gslice(g)]
            m8 = None
            for t in range(STEP_TILES):
                rows = slice(t * LANES, (t + 1) * LANES)
                v = logits[g][rows, :]
                if near:
                    v = v + tsel_ref[tile_ids[t], :, gslice(g)]
                if exact:
                    s_ref[rows, gslice(g)] = v
                    vm = jnp.max(v.reshape(LANES // 8, 8, GCOLS), axis=0)
                else:
                    pt = jnp.exp2(v - m_old).astype(BF16)
                    p_ref[rows, gslice(g)] = pt
                    vm = jnp.max(pt.reshape(LANES // 16, 16, GCOLS), axis=0)
                m8 = vm if m8 is None else jnp.maximum(m8, vm)
            if exact:
                m_new = jnp.maximum(m_old, jnp.max(m8, axis=0, keepdims=True))
                alphas.append(jnp.exp2(m_old - m_new))
                m_ref[:, gslice(g)] = m_new
                for t in range(STEP_TILES):
                    rows = slice(t * LANES, (t + 1) * LANES)
                    p_ref[rows, gslice(g)] = jnp.exp2(s_ref[rows, gslice(g)] - m_new).astype(BF16)
            else:
                viol_ref[:, gslice(g)] = jnp.maximum(viol_ref[:, gslice(g)],
                                                     jnp.max(m8.astype(F32), axis=0, keepdims=True))
        for g in range(NSA_GROUPS):
            pv = _dot(values_with_ones(vt_tiles, g), p_ref[0:STEP, gslice(g)])
            if exact:
                acc_ref[g] = acc_ref[g] * alphas[g] + pv
            else:
                acc_ref[g] = acc_ref[g] + pv

    r_diag = qb // STEP_TILES
    n_far = jnp.maximum((qb - (N_NEAR - 1)) // STEP_TILES, 0)

    def selected(exact):
        if exact:
            m_ref[...] = jnp.full(m_ref.shape, NEG, F32)
        else:
            kt = kslc_ref[diag_rows, :].astype(F32).T
            m_ref[...] = self_logit(jnp.concatenate(
                [kt[g * AUG_DIM:g * AUG_DIM + HEAD_DIM, :] for g in range(NSA_GROUPS)], axis=0))
        acc_ref[...] = jnp.zeros_like(acc_ref)

        def near_body(i, carry):
            sel_step(r_diag - i, True, exact)
            return carry

        def far_body(r, carry):
            sel_step(r, False, exact)
            return carry

        lax.fori_loop(0, r_diag - n_far + 1, near_body, 0)
        lax.fori_loop(0, n_far, far_body, 0)

    viol_ref[...] = jnp.zeros_like(viol_ref)
    selected(False)

    @pl.when(jnp.max(viol_ref[...]) > 2.0 ** MAX_EXP2_EXCESS)
    def _():
        selected(True)

    o_slc = [normalized(acc_ref[g]) for g in range(NSA_GROUPS)]

    window(False)

    @pl.when(jnp.max(viol_ref[...]) > 2.0 ** MAX_EXP2_EXCESS)
    def _():
        window(True)

    o_swa = [normalized(acc_ref[g]) for g in range(NSA_GROUPS)]

    gates = _sigmoid(gt_ref[...])
    for g in range(NSA_GROUPS):
        def gate_row(c):
            return jnp.concatenate(
                [gates[c * NSA_HEADS + g * NSA_HPG + r:c * NSA_HEADS + g * NSA_HPG + r + 1, :]
                 for r in range(NSA_HPG)], axis=1)
        og = gate_row(0) * o_cmp[g] + gate_row(1) * o_slc[g] + gate_row(2) * o_swa[g]
        for r in range(NSA_HPG):
            h = g * NSA_HPG + r
            ot_ref[h * HEAD_DIM:(h + 1) * HEAD_DIM, :] = og[:, r * Q_BLOCK:(r + 1) * Q_BLOCK]
    out_ref[...] = ot_ref[...].T


def _nsa(qt, gt, kc, vct, kslc, vslct, kswa, vswat, tsel, tcmp, ovt, far, seq):
    b = kc.shape[0]
    n_qb = seq // Q_BLOCK
    assert n_qb % STEP_TILES == 0 and n_qb >= WIN_TILES and WIN_TILES >= STEP_TILES
    n_cmp_rows = kc.shape[1]
    n_sel = seq // SEL_BLOCK
    assert n_sel <= AUG_DIM - HEAD_DIM
    top_n = min(SEL_TOPN, n_sel)
    assert top_n > 3
    n_gate = gt.shape[2]
    const = lambda arr: pl.BlockSpec(arr.shape, lambda i, j: (0,) * arr.ndim,
                                     pipeline_mode=pl.Buffered(1))
    kern = functools.partial(_nsa_kernel, n_qb=n_qb, n_cmp_rows=n_cmp_rows, n_sel=n_sel, top_n=top_n)
    return pl.pallas_call(
        kern,
        grid=(b, n_qb),
        in_specs=[pl.BlockSpec((None, None, NSA_WIDTH, Q_BLOCK), lambda i, j: (i, j, 0, 0)),
                  pl.BlockSpec((None, None, n_gate, Q_BLOCK), lambda i, j: (i, j, 0, 0)),
                  pl.BlockSpec((None, n_cmp_rows, KV_WIDTH), lambda i, j: (i, 0, 0)),
                  pl.BlockSpec((None, KV_WIDTH, n_cmp_rows), lambda i, j: (i, 0, 0)),
                  pl.BlockSpec((None, seq, NSA_GROUPS * AUG_DIM), lambda i, j: (i, 0, 0)),
                  pl.BlockSpec((None, n_qb, KV_WIDTH, Q_BLOCK), lambda i, j: (i, 0, 0, 0)),
                  pl.BlockSpec((None, seq, KV_WIDTH), lambda i, j: (i, 0, 0)),
                  pl.BlockSpec((None, n_qb, KV_WIDTH, Q_BLOCK), lambda i, j: (i, 0, 0, 0)),
                  const(tsel), const(tcmp), const(ovt), const(far)],
        out_specs=pl.BlockSpec((None, Q_BLOCK, NSA_WIDTH), lambda i, j: (i, j, 0)),
        out_shape=jax.ShapeDtypeStruct((b, seq, NSA_WIDTH), F32),
        scratch_shapes=[pltpu.VMEM((KV_WIDTH, QCOLS), BF16),
                        pltpu.VMEM((NSA_GROUPS, AUG_DIM, GCOLS), BF16),
                        pltpu.VMEM((1, QCOLS), F32),
                        pltpu.VMEM((NSA_GROUPS, HEAD_DIM + SUM_ROWS, GCOLS), F32),
                        pltpu.VMEM((NSA_WIDTH, Q_BLOCK), F32),
                        pltpu.VMEM((WIN_TILES * LANES, QCOLS), F32),
                        pltpu.VMEM((WIN_TILES * LANES, QCOLS), BF16),
                        pltpu.VMEM((1, QCOLS), F32),
                        pltpu.VMEM((NSA_GROUPS, n_sel, Q_BLOCK), F32),
                        pltpu.VMEM((NSA_GROUPS, n_sel, Q_BLOCK), F32)],
        compiler_params=pltpu.CompilerParams(dimension_semantics=("parallel", "arbitrary"),
                                             vmem_limit_bytes=VMEM_LIMIT),
        name="nsa",
    )(qt, gt, kc, vct, kslc, vslct, kswa, vswat, tsel, tcmp, ovt, far)


def _ssd_kernel(xbc_ref, dt_ref, convw_ref, convb_ref, dtb_ref, a_ref, dskip_ref, expand_ref, y_ref,
                xp_ref, tail_ref, state_ref, slab_ref):
    L = SSM_CHUNK
    NV = L // 8
    halo = CONV_WIDTH - 1
    c = pl.program_id(1)

    @pl.when(c == 0)
    def _():
        tail_ref[...] = jnp.zeros_like(tail_ref)
        state_ref[...] = jnp.zeros_like(state_ref)

    n_slab = CONV_DIM // LANES
    for s in range(n_slab):
        slab_ref[s] = xbc_ref[:, s * LANES:(s + 1) * LANES]
    slab_ref[n_slab] = jnp.concatenate([dt_ref[...], jnp.zeros((L, LANES - SSM_HEADS), F32)], axis=1)
    for v in range(NV):
        for s in range(n_slab):
            xp_ref[(halo + v) * 8:(halo + v + 1) * 8, s * LANES:(s + 1) * LANES] = \
                slab_ref[s, pl.ds(v, 8, stride=NV), :]
    sub = lax.broadcasted_iota(jnp.int32, (8, CONV_DIM), 0)
    for w in range(halo):
        cur = xp_ref[(NV + w) * 8:(NV + w + 1) * 8, :]
        prev = tail_ref[w * 8:(w + 1) * 8, :]
        xp_ref[w * 8:(w + 1) * 8, :] = pltpu.roll(jnp.where(sub == 7, prev, cur), 1, 0)
    tail_ref[...] = xp_ref[NV * 8:(NV + halo) * 8, :]
    conv = convb_ref[...]
    for k in range(CONV_WIDTH):
        conv = conv + convw_ref[k:k + 1, :] * xp_ref[8 * k:8 * k + L, :]
    xbc = _silu(conv)
    xs = xbc[:, 0:SSM_WIDTH]
    bm = xbc[:, SSM_WIDTH:SSM_WIDTH + SSM_GROUPS * SSM_STATE]
    cm = xbc[:, SSM_WIDTH + SSM_GROUPS * SSM_STATE:]

    dt_raw = jnp.concatenate([slab_ref[n_slab, pl.ds(v, 8, stride=NV), :] for v in range(NV)], axis=0)
    dt = _softplus(dt_raw[:, 0:SSM_HEADS] + dtb_ref[...])
    a_row = -jnp.exp(a_ref[...])
    row_i = lax.broadcasted_iota(jnp.int32, (L, L), 0)
    col_j = lax.broadcasted_iota(jnp.int32, (L, L), 1)
    token = lambda r: (r % 8) * NV + r // 8
    tril = token(row_i) >= token(col_j)
    tril01 = jnp.where(tril, 1.0, 0.0).astype(BF16)
    cs = _dot_exact_rhs(tril01, dt * a_row)
    cst = jnp.concatenate([cs, jnp.zeros((L, L - SSM_HEADS), F32)], axis=1).T

    expand = expand_ref[...]
    cs_x = _dot_exact_lhs(cs, expand)
    dt_x = _dot_exact_lhs(dt, expand)
    total_x = cs_x[L - 1:L, :]
    xdt = (xs * dt_x).astype(BF16)
    xw = (xs * (jnp.exp(total_x - cs_x) * dt_x)).astype(BF16)
    decay_out = jnp.exp(cs_x)
    decay_state = jnp.exp(total_x)

    lane = lax.broadcasted_iota(jnp.int32, (L, LANES), 1)
    gw = SSM_HPG * SSM_HEAD_DIM
    for g in range(SSM_GROUPS):
        bg = bm[:, g * SSM_STATE:(g + 1) * SSM_STATE]
        cg = cm[:, g * SSM_STATE:(g + 1) * SSM_STATE].astype(BF16)
        cb = _dot_nt(cg, bg.astype(BF16))
        st = state_ref[g]
        y_g = _dot(cg, st.astype(BF16)) * decay_out[:, g * gw:(g + 1) * gw]
        state_ref[g] = st * decay_state[:, g * gw:(g + 1) * gw] + \
            _dot(bg.T.astype(BF16), xw[:, g * gw:(g + 1) * gw])
        pieces = []
        for pair in range(SSM_HPG // 2):
            ws = []
            for hh in range(2):
                h = g * SSM_HPG + 2 * pair + hh
                diff = cs[:, h:h + 1] - cst[h:h + 1, :]
                decay = jnp.where(tril, jnp.exp(jnp.where(tril, diff, 0.0)), 0.0)
                ws.append((cb * decay).astype(BF16))
            w_pair = jnp.concatenate(ws, axis=1)
            lo = g * gw + pair * LANES
            slab = xdt[:, lo:lo + LANES]
            zero = jnp.zeros_like(slab)
            x_bd = jnp.concatenate([jnp.where(lane < SSM_HEAD_DIM, slab, zero),
                                    jnp.where(lane >= SSM_HEAD_DIM, slab, zero)], axis=0)
            pieces.append(_dot(w_pair, x_bd))
        y_g = y_g + jnp.concatenate(pieces, axis=1) + \
            dskip_ref[:, g * gw:(g + 1) * gw] * xs[:, g * gw:(g + 1) * gw]
        for s in range(gw // LANES):
            for v in range(NV):
                slab_ref[s, pl.ds(v, 8, stride=NV), :] = y_g[v * 8:(v + 1) * 8, s * LANES:(s + 1) * LANES]
        for s in range(gw // LANES):
            y_ref[:, g * gw + s * LANES:g * gw + (s + 1) * LANES] = slab_ref[s]


def _ssd(xbc, dt, convw, convb, dtb, a_log, dskip_x, expand, seq):
    b = xbc.shape[0]
    nc = seq // SSM_CHUNK
    halo_rows = (CONV_WIDTH - 1) * 8
    const = lambda arr: pl.BlockSpec(arr.shape, lambda i, j: (0,) * arr.ndim)
    return pl.pallas_call(
        _ssd_kernel,
        grid=(b, nc),
        in_specs=[pl.BlockSpec((None, SSM_CHUNK, CONV_DIM), lambda i, j: (i, j, 0)),
                  pl.BlockSpec((None, SSM_CHUNK, SSM_HEADS), lambda i, j: (i, j, 0)),
                  const(convw), const(convb), const(dtb), const(a_log), const(dskip_x), const(expand)],
        out_specs=pl.BlockSpec((None, SSM_CHUNK, SSM_WIDTH), lambda i, j: (i, j, 0)),
        out_shape=jax.ShapeDtypeStruct((b, seq, SSM_WIDTH), F32),
        scratch_shapes=[pltpu.VMEM((SSM_CHUNK + halo_rows, CONV_DIM), F32),
                        pltpu.VMEM((halo_rows, CONV_DIM), F32),
                        pltpu.VMEM((SSM_GROUPS, SSM_STATE, SSM_HPG * SSM_HEAD_DIM), F32),
                        pltpu.VMEM((CONV_DIM // LANES + 1, SSM_CHUNK, LANES), F32)],
        compiler_params=pltpu.CompilerParams(dimension_semantics=("parallel", "arbitrary"),
                                             vmem_limit_bytes=VMEM_LIMIT),
        name="ssd",
    )(xbc, dt, convw, convb, dtb, a_log, dskip_x, expand)


def _epilogue_kernel(x_ref, o_ref, y_ref, nw_ref, wz1_ref, wz2_ref, wmg_ref, won_ref, wos_ref, wo_ref,
                     snw_ref, fnw_ref, out_ref):
    x = x_ref[...]
    ms = jnp.mean(x * x, axis=-1, keepdims=True)
    xn = (x * lax.rsqrt(ms + NORM_EPS) * nw_ref[...]).astype(BF16)
    u = (o_ref[...] * _silu(_dot(xn, wz1_ref[...]))).astype(BF16)
    h_nsa = _dot(u, won_ref[...])
    hh = y_ref[...] * _silu(_dot(xn, wz2_ref[...]))
    gw = SSM_WIDTH // SSM_GROUPS
    parts = []
    for g in range(SSM_GROUPS):
        hg = hh[:, g * gw:(g + 1) * gw]
        hg = hg * lax.rsqrt(jnp.mean(hg * hg, axis=-1, keepdims=True) + NORM_EPS)
        parts.append((hg * snw_ref[:, g * gw:(g + 1) * gw]).astype(BF16))
    h_ssm = _dot(jnp.concatenate(parts, axis=1), wos_ref[...])
    gate = _sigmoid(_dot(xn, wmg_ref[...]))
    mix = (gate[:, 0:D_MODEL] * h_nsa + gate[:, D_MODEL:] * h_ssm).astype(BF16)
    r = x + _dot(mix, wo_ref[...])
    ms2 = jnp.mean(r * r, axis=-1, keepdims=True)
    out_ref[...] = r * lax.rsqrt(ms2 + NORM_EPS) * fnw_ref[...]


def _epilogue(x2, o2, y2, nw, wz1, wz2, wmg, won, wos, wo, snw, fnw, tm):
    t, d = x2.shape
    assert t % tm == 0
    const = lambda arr: pl.BlockSpec(arr.shape, lambda i: (0,) * arr.ndim, pipeline_mode=pl.Buffered(1))
    return pl.pallas_call(
        _epilogue_kernel,
        grid=(t // tm,),
        in_specs=[pl.BlockSpec((tm, d), lambda i: (i, 0)),
                  pl.BlockSpec((tm, NSA_WIDTH), lambda i: (i, 0)),
                  pl.BlockSpec((tm, SSM_WIDTH), lambda i: (i, 0)),
                  const(nw), const(wz1), const(wz2), const(wmg), const(won), const(wos), const(wo),
                  const(snw), const(fnw)],
        out_specs=pl.BlockSpec((tm, d), lambda i: (i, 0)),
        out_shape=jax.ShapeDtypeStruct((t, d), F32),
        compiler_params=pltpu.CompilerParams(dimension_semantics=("parallel",),
                                             vmem_limit_bytes=VMEM_LIMIT),
        name="epilogue",
    )(x2, o2, y2, nw, wz1, wz2, wmg, won, wos, wo, snw, fnw)


def _table_kernel(rb_ref, bucket_ref, out_ref):
    bk = bucket_ref[...]
    for h in range(NSA_HEADS):
        acc = jnp.full(bk.shape, NEG, F32)
        for b in range(REL_BUCKETS):
            acc = jnp.where(bk == b, rb_ref[b, h], acc)
        out_ref[:, h * Q_BLOCK:(h + 1) * Q_BLOCK] = acc


def _bias_table(rb, buckets, tr):
    rows = buckets.shape[0]
    assert rows % tr == 0
    return pl.pallas_call(
        _table_kernel,
        grid=(rows // tr,),
        in_specs=[pl.BlockSpec(memory_space=pltpu.SMEM),
                  pl.BlockSpec((tr, Q_BLOCK), lambda i: (i, 0))],
        out_specs=pl.BlockSpec((tr, QCOLS), lambda i: (i, 0)),
        out_shape=jax.ShapeDtypeStruct((rows, QCOLS), F32),
        compiler_params=pltpu.CompilerParams(dimension_semantics=("parallel",)),
        name="bias_table",
    )(rb, buckets)


def _bucket_tables(seq):
    n_qb = seq // Q_BLOCK
    key = np.arange(LANES)[:, None]
    tok = np.arange(Q_BLOCK)[None, :]
    tiles = [_t5_bucket_np(LANES * o + tok - key) for o in range(N_NEAR)]
    tiles.append(np.full((LANES, Q_BLOCK), REL_BUCKETS - 1, np.int32))
    dwin = WINDOW + tok - key
    tiles.append(_t5_bucket_np(np.where(dwin < WINDOW, dwin, -1)))
    tiles.append(np.full((LANES, Q_BLOCK), MASKED_BUCKET, np.int32))
    assert len(tiles) == N_TILES and LANES * N_NEAR - (LANES - 1) >= 790
    n_cmp_rows = seq // CMP_STRIDE
    rows = 8 * (n_qb - 1) + n_cmp_rows
    rows_pad = -(-rows // LANES) * LANES
    u = np.arange(rows_pad)[:, None]
    cmp_tbl = _t5_bucket_np(tok - CMP_STRIDE * u + Q_BLOCK * (n_qb - 1) - (CMP_BLOCK - 1))
    return np.concatenate(tiles, axis=0), cmp_tbl


def _block_onehot(seq, tm):
    pos = np.arange(seq)
    pat = np.zeros((seq, NSA_GROUPS, AUG_DIM), np.float32)
    pat[pos, :, HEAD_DIM + pos // SEL_BLOCK] = 1.0
    return jnp.asarray(pat.reshape(seq // tm, tm, NSA_GROUPS * AUG_DIM), dtype=BF16)


def _overlap_t(seq):
    n_cmp_rows = seq // CMP_STRIDE
    n_sel = seq // SEL_BLOCK
    c_start = np.arange(n_cmp_rows)[None, :] * CMP_STRIDE
    s_start = np.arange(n_sel)[:, None] * SEL_BLOCK
    ov = (c_start < s_start + SEL_BLOCK) & (c_start + CMP_BLOCK > s_start)
    ov[:, n_cmp_rows - 1] = False
    return jnp.asarray(ov, dtype=BF16)


def _layer(x, norm_w, w_in, cmp_pos_k, cmp_pos_v, cmp_k_w1, cmp_k_b1, cmp_k_w2, cmp_v_w1, cmp_v_b1,
           cmp_v_w2, conv_w, conv_b, dt_bias, a_log, d_skip, ssm_norm_w, w_out_nsa, w_out_ssm, w_out,
           rel_bias, out_norm_w):
    b, s, d = x.shape
    assert d == D_MODEL and s % STEP == 0 and s >= 2 * WINDOW
    t = b * s
    n_qb = s // Q_BLOCK
    cols = _column_offsets()
    wcol = lambda name: w_in[:, cols[name][0]:cols[name][1]]
    x2 = x.reshape(t, d)
    nw = norm_w.reshape(1, d).astype(F32)

    gate_w = wcol('nsa_gate').reshape(d, NSA_HEADS, 3).transpose(0, 2, 1).reshape(d, 3 * NSA_HEADS)
    w_kslc = jnp.pad(wcol('k_slc').reshape(d, NSA_GROUPS, HEAD_DIM),
                     ((0, 0), (0, 0), (0, AUG_DIM - HEAD_DIM))).reshape(d, NSA_GROUPS * AUG_DIM)
    wn = jnp.concatenate([w_kslc, wcol('k_swa'), wcol('k_cmp'), wcol('v_cmp'), wcol('xbc'),
                          wcol('dt')], axis=1).astype(BF16)
    wt = jnp.concatenate([wcol('q') * (HEAD_DIM ** -0.5 * LOG2E), wcol('v_slc'), wcol('v_swa'), gate_w],
                         axis=1).T.astype(BF16)
    kslc, kswa, cc, xbc, dt, qt, vslct, vswat, gt = _proj(x2, nw, wn, wt, _block_onehot(s, STEP), STEP)
    kslc = kslc.reshape(b, s, NSA_GROUPS * AUG_DIM)
    kswa = kswa.reshape(b, s, KV_WIDTH)
    xbc = xbc.reshape(b, s, CONV_DIM)
    dt = dt.reshape(b, s, SSM_HEADS)
    qt = qt.reshape(b, n_qb, NSA_WIDTH, Q_BLOCK)
    vslct = vslct.reshape(b, n_qb, KV_WIDTH, Q_BLOCK)
    vswat = vswat.reshape(b, n_qb, KV_WIDTH, Q_BLOCK)
    gt = gt.reshape(b, n_qb, 3 * NSA_HEADS, Q_BLOCK)

    def pair_diag(w1):
        w = w1.reshape(CMP_BLOCK, HEAD_DIM, CMP_HIDDEN)
        z = jnp.zeros_like(w)
        return jnp.concatenate([jnp.concatenate([w, z], axis=2),
                                jnp.concatenate([z, w], axis=2)], axis=1).astype(BF16)

    twice = lambda a: jnp.tile(a.astype(F32), (1, 2))
    kc, vct = _compress(cc, b, twice(cmp_pos_k), twice(cmp_pos_v),
                        pair_diag(cmp_k_w1), twice(cmp_k_b1.reshape(1, -1)), cmp_k_w2.astype(BF16),
                        pair_diag(cmp_v_w1), twice(cmp_v_b1.reshape(1, -1)), cmp_v_w2.T.astype(BF16))

    sel_buckets, cmp_buckets = _bucket_tables(s)
    rb = rel_bias.astype(F32) * LOG2E
    rb_far = rb[REL_BUCKETS - 1:REL_BUCKETS, :]
    tsel = _bias_table(rb - rb_far, jnp.asarray(sel_buckets), LANES).reshape(N_TILES, LANES, QCOLS)
    tcmp = _bias_table(rb, jnp.asarray(cmp_buckets), LANES)
    far = jnp.repeat(rb_far.reshape(-1), Q_BLOCK).reshape(1, QCOLS)
    o_nsa = _nsa(qt, gt, kc, vct, kslc, vslct, kswa, vswat, tsel, tcmp, _overlap_t(s), far, s)

    expand = jnp.asarray(np.kron(np.eye(SSM_HEADS), np.ones((1, SSM_HEAD_DIM))), dtype=BF16)
    y = _ssd(xbc, dt, conv_w.astype(F32), conv_b.reshape(1, -1).astype(F32),
             dt_bias.reshape(1, -1).astype(F32), a_log.reshape(1, -1).astype(F32),
             jnp.repeat(d_skip.astype(F32), SSM_HEAD_DIM).reshape(1, -1), expand, s)

    out = _epilogue(x2, o_nsa.reshape(t, NSA_WIDTH), y.reshape(t, SSM_WIDTH), nw,
                    wcol('z_nsa').astype(BF16), wcol('z_ssm').astype(BF16), wcol('merge_gate').astype(BF16),
                    w_out_nsa.astype(BF16), w_out_ssm.astype(BF16), w_out.astype(BF16),
                    ssm_norm_w.reshape(1, -1).astype(F32), out_norm_w.reshape(1, -1).astype(F32),
                    256 if t % 256 == 0 else Q_BLOCK)
    return out.reshape(b, s, d)


def kernel(x, norm_w, w_in, cmp_pos_k, cmp_pos_v, cmp_k_w1, cmp_k_b1, cmp_k_w2, cmp_v_w1, cmp_v_b1, cmp_v_w2,
           conv_w, conv_b, dt_bias, a_log, d_skip, ssm_norm_w, w_out_nsa, w_out_ssm, w_out, rel_bias,
           final_norm_w):
    depth = norm_w.shape[0]
    assert depth == 1, "the epilogue fuses the final norm into the single layer"
    return _layer(x, norm_w[0], w_in[0], cmp_pos_k[0], cmp_pos_v[0], cmp_k_w1[0], cmp_k_b1[0], cmp_k_w2[0],
                  cmp_v_w1[0], cmp_v_b1[0], cmp_v_w2[0], conv_w[0], conv_b[0], dt_bias[0], a_log[0],
                  d_skip[0], ssm_norm_w[0], w_out_nsa[0], w_out_ssm[0], w_out[0], rel_bias, final_norm_w)
```

```python
import functools
import math

import numpy as np
import jax
import jax.numpy as jnp
from jax import lax
from jax.experimental import pallas as pl
from jax.experimental.pallas import tpu as pltpu

F32 = jnp.float32
BF16 = jnp.bfloat16

D_MODEL = 1024
NSA_HEADS = 16
NSA_GROUPS = 4
NSA_HPG = NSA_HEADS // NSA_GROUPS
HEAD_DIM = 64
NSA_WIDTH = NSA_HEADS * HEAD_DIM
KV_WIDTH = NSA_GROUPS * HEAD_DIM
CMP_BLOCK = 32
CMP_STRIDE = 16
CMP_HIDDEN = 4 * HEAD_DIM
SEL_BLOCK = 64
SEL_TOPN = 8
WINDOW = 512
Q_BLOCK = 128
FORCE_SCORE = 1.0e4
REL_BUCKETS = 32
SSM_WIDTH = 2 * D_MODEL
SSM_HEAD_DIM = 64
SSM_HEADS = SSM_WIDTH // SSM_HEAD_DIM
SSM_GROUPS = 4
SSM_HPG = SSM_HEADS // SSM_GROUPS
SSM_STATE = 128
CONV_WIDTH = 4
SSM_CHUNK = 128
CONV_DIM = SSM_WIDTH + 2 * SSM_GROUPS * SSM_STATE
NORM_EPS = 1e-6

NEG = -1e30
LOG2E = math.log2(math.e)
LANES = 128
QCOLS = NSA_HEADS * Q_BLOCK
GCOLS = NSA_HPG * Q_BLOCK
AUG_DIM = 2 * HEAD_DIM
N_NEAR = 8
TILE_FAR = N_NEAR
TILE_WIN_OLD = N_NEAR + 1
TILE_MASKED = N_NEAR + 2
N_TILES = N_NEAR + 3
MASKED_BUCKET = REL_BUCKETS
SUM_ROWS = 16
MAX_EXP2_EXCESS = 100.0
STEP_TILES = 4
STEP = STEP_TILES * LANES
WIN_TILES = WINDOW // LANES + 1
VMEM_LIMIT = 56 * 1024 * 1024


def _column_offsets():
    sizes = (('q', NSA_WIDTH), ('k_cmp', KV_WIDTH), ('v_cmp', KV_WIDTH), ('k_slc', KV_WIDTH),
             ('v_slc', KV_WIDTH), ('k_swa', KV_WIDTH), ('v_swa', KV_WIDTH), ('nsa_gate', 3 * NSA_HEADS),
             ('z_nsa', NSA_WIDTH), ('z_ssm', SSM_WIDTH), ('xbc', CONV_DIM), ('dt', SSM_HEADS),
             ('merge_gate', 2 * D_MODEL))
    out, lo = {}, 0
    for name, n in sizes:
        out[name] = (lo, lo + n)
        lo += n
    return out


def _t5_bucket_np(dist):
    dist = np.asarray(dist, dtype=np.int64)
    d = np.maximum(dist, 0)
    max_exact = REL_BUCKETS // 2
    large = np.full(d.shape, max_exact, dtype=np.int64)
    d8 = d.astype(object) ** 8
    for k in range(1, REL_BUCKETS - max_exact):
        large = large + (d8 >= 2 ** (32 + 3 * k)).astype(np.int64)
    bucket = np.where(d < max_exact, d, np.minimum(large, REL_BUCKETS - 1))
    return np.where(dist < 0, MASKED_BUCKET, bucket).astype(np.int32)


def _sigmoid(x):
    return 1.0 / (1.0 + jnp.exp(-x))


def _silu(x):
    return x * _sigmoid(x)


def _softplus(x):
    return jnp.maximum(x, 0.0) + jnp.log1p(jnp.exp(-jnp.abs(x)))


def _dot(a, b):
    return jnp.dot(a, b, preferred_element_type=F32)


def _dot_nt(a, b):
    return lax.dot_general(a, b, (((1,), (1,)), ((), ())), preferred_element_type=F32)


def _split3(x):
    hi = x.astype(BF16)
    r1 = x - hi.astype(F32)
    mid = r1.astype(BF16)
    lo = (r1 - mid.astype(F32)).astype(BF16)
    return hi, mid, lo


def _dot_exact_lhs(x, w01):
    hi, mid, lo = _split3(x)
    return _dot(hi, w01) + _dot(mid, w01) + _dot(lo, w01)


def _dot_exact_rhs(w01, x):
    hi, mid, lo = _split3(x)
    return _dot(w01, hi) + _dot(w01, mid) + _dot(w01, lo)


_NAT_OUTS = (('k_slc', NSA_GROUPS * AUG_DIM, BF16), ('k_swa', KV_WIDTH, BF16), ('kv_cmp', 2 * KV_WIDTH, F32),
             ('xbc', CONV_DIM, F32), ('dt', SSM_HEADS, F32))
_TR_OUTS = (('q', NSA_WIDTH, BF16), ('v_slc', KV_WIDTH, BF16), ('v_swa', KV_WIDTH, BF16),
            ('gate', 3 * NSA_HEADS, F32))


def _proj_kernel(x_ref, nw_ref, wn_ref, wt_ref, onehot_ref, *out_refs, n_sub):
    x = x_ref[...]
    ms = jnp.mean(x * x, axis=-1, keepdims=True)
    xn = (x * lax.rsqrt(ms + NORM_EPS) * nw_ref[...]).astype(BF16)
    lo = 0
    for (name, n, _), o_ref in zip(_NAT_OUTS, out_refs[:len(_NAT_OUTS)]):
        res = _dot(xn, wn_ref[:, lo:lo + n])
        if name == 'k_slc':
            res = res + onehot_ref[...]
        if name == 'kv_cmp':
            for j in range(n // LANES):
                o_ref[j] = res[:, j * LANES:(j + 1) * LANES].astype(o_ref.dtype)
        else:
            o_ref[...] = res.astype(o_ref.dtype)
        lo += n
    lo = 0
    for (_, n, _), o_ref in zip(_TR_OUTS, out_refs[len(_NAT_OUTS):]):
        res = _dot_nt(wt_ref[lo:lo + n, :], xn)
        for s in range(n_sub):
            o_ref[s] = res[:, s * LANES:(s + 1) * LANES].astype(o_ref.dtype)
        lo += n


def _proj(x2, nw, wn, wt, onehot, tm):
    t, d = x2.shape
    assert t % tm == 0 and tm % LANES == 0 and onehot.shape[1] == tm
    n_sub = tm // LANES
    n_pat = onehot.shape[0]
    const = lambda arr: pl.BlockSpec(arr.shape, lambda i: (0,) * arr.ndim, pipeline_mode=pl.Buffered(1))
    slabbed = lambda name: name == 'kv_cmp'
    out_specs = [pl.BlockSpec((n // LANES, tm, LANES), lambda i: (0, i, 0)) if slabbed(name)
                 else pl.BlockSpec((tm, n), lambda i: (i, 0)) for name, n, _ in _NAT_OUTS] + \
                [pl.BlockSpec((n_sub, n, LANES), lambda i: (i, 0, 0)) for _, n, _ in _TR_OUTS]
    out_shape = [jax.ShapeDtypeStruct((n // LANES, t, LANES) if slabbed(name) else (t, n), dt)
                 for name, n, dt in _NAT_OUTS] + \
                [jax.ShapeDtypeStruct((t // LANES, n, LANES), dt) for _, n, dt in _TR_OUTS]
    return pl.pallas_call(
        functools.partial(_proj_kernel, n_sub=n_sub),
        grid=(t // tm,),
        in_specs=[pl.BlockSpec((tm, d), lambda i: (i, 0)), const(nw), const(wn), const(wt),
                  pl.BlockSpec((None, tm, onehot.shape[2]), lambda i: (i % n_pat, 0, 0))],
        out_specs=out_specs,
        out_shape=out_shape,
        compiler_params=pltpu.CompilerParams(dimension_semantics=("parallel",),
                                             vmem_limit_bytes=VMEM_LIMIT),
        name="proj",
    )(x2, nw, wn, wt, onehot)


def _compress_kernel(raw_ref, posk_ref, posv_ref, w1k_ref, b1k_ref, w2k_ref,
                     w1v_ref, b1v_ref, w2vt_ref, kc_ref, vct_ref):
    nseg = kc_ref.shape[0]
    n_slab = raw_ref.shape[0]
    per_kv = n_slab // 2
    for j in range(n_slab):
        is_k = j < per_kv
        pos_ref, w1_ref, b1_ref = (posk_ref, w1k_ref, b1k_ref) if is_k else (posv_ref, w1v_ref, b1v_ref)
        first = jnp.zeros((nseg, 2 * CMP_HIDDEN), F32)
        second = jnp.zeros((nseg, 2 * CMP_HIDDEN), F32)
        for p in range(CMP_STRIDE):
            rows = raw_ref[j, pl.ds(p, nseg, stride=CMP_STRIDE), :]
            first = first + _dot((rows + pos_ref[p:p + 1, :]).astype(BF16), w1_ref[p])
            q = CMP_STRIDE + p
            second = second + _dot((rows + pos_ref[q:q + 1, :]).astype(BF16), w1_ref[q])
        hid = _silu(first + pltpu.roll(second, nseg - 1, 0) + b1_ref[...]).astype(BF16)
        for gi in range(2):
            g = 2 * (j % per_kv) + gi
            hg = hid[:, gi * CMP_HIDDEN:(gi + 1) * CMP_HIDDEN]
            if is_k:
                kc_ref[:, g * HEAD_DIM:(g + 1) * HEAD_DIM] = _dot(hg, w2k_ref[...]).astype(kc_ref.dtype)
            else:
                vct_ref[g * HEAD_DIM:(g + 1) * HEAD_DIM, :] = _dot_nt(w2vt_ref[...], hg).astype(vct_ref.dtype)


def _compress(raw, batch, posk, posv, w1k, b1k, w2k, w1v, b1v, w2vt):
    n_slab, t, lanes = raw.shape
    seq = t // batch
    nseg = seq // CMP_STRIDE
    assert n_slab * lanes == 2 * KV_WIDTH and lanes == 2 * HEAD_DIM
    const = lambda arr: pl.BlockSpec(arr.shape, lambda i: (0,) * arr.ndim, pipeline_mode=pl.Buffered(1))
    return pl.pallas_call(
        _compress_kernel,
        grid=(batch,),
        in_specs=[pl.BlockSpec((n_slab, seq, lanes), lambda i: (0, i, 0)),
                  const(posk), const(posv), const(w1k), const(b1k), const(w2k),
                  const(w1v), const(b1v), const(w2vt)],
        out_specs=[pl.BlockSpec((None, nseg, KV_WIDTH), lambda i: (i, 0, 0)),
                   pl.BlockSpec((None, KV_WIDTH, nseg), lambda i: (i, 0, 0))],
        out_shape=[jax.ShapeDtypeStruct((batch, nseg, KV_WIDTH), BF16),
                   jax.ShapeDtypeStruct((batch, KV_WIDTH, nseg), BF16)],
        compiler_params=pltpu.CompilerParams(dimension_semantics=("parallel",),
                                             vmem_limit_bytes=VMEM_LIMIT),
        name="compress",
    )(raw, posk, posv, w1k, b1k, w2k, w1v, b1v, w2vt)


def _nsa_kernel(qt_ref, gt_ref, kc_ref, vct_ref, kslc_ref, vslct_ref, kswa_ref, vswat_ref,
                tsel_ref, tcmp_ref, ovt_ref, far_ref, out_ref, guard_ref,
                qbd_ref, qaug_ref, m_ref, acc_ref, ot_ref, s_ref, p_ref, viol_ref, work_ref, chosen_ref,
                *, n_qb, n_cmp_rows, n_sel, top_n, exact):
    qb = pl.program_id(1)
    q0 = qb * Q_BLOCK

    @pl.when(qb == 0)
    def _():
        qbd_ref[...] = jnp.zeros_like(qbd_ref)
        qaug_ref[...] = jnp.zeros_like(qaug_ref)

    for g in range(NSA_GROUPS):
        for r in range(NSA_HPG):
            h = g * NSA_HPG + r
            q_head = qt_ref[h * HEAD_DIM:(h + 1) * HEAD_DIM, :]
            qbd_ref[g * HEAD_DIM:(g + 1) * HEAD_DIM, h * Q_BLOCK:(h + 1) * Q_BLOCK] = q_head
            qaug_ref[g, 0:HEAD_DIM, r * Q_BLOCK:(r + 1) * Q_BLOCK] = q_head

    def gslice(g):
        return slice(g * GCOLS, (g + 1) * GCOLS)

    def values_with_ones(vt_tiles, g):
        vt = jnp.concatenate([t[g * HEAD_DIM:(g + 1) * HEAD_DIM, :] for t in vt_tiles], axis=1)
        return jnp.concatenate([vt, jnp.ones((SUM_ROWS, vt.shape[1]), BF16)], axis=0)

    def normalized(a):
        return a[0:HEAD_DIM, :] * (1.0 / a[HEAD_DIM:HEAD_DIM + 1, :])

    def self_logit(kt):
        cols = []
        for h in range(NSA_HEADS):
            g = h // NSA_HPG
            prod = qt_ref[h * HEAD_DIM:(h + 1) * HEAD_DIM, :].astype(F32) * kt[g * HEAD_DIM:(g + 1) * HEAD_DIM, :]
            cols.append(jnp.sum(prod, axis=0, keepdims=True))
        return jnp.concatenate(cols, axis=1) - far_ref[...]

    diag_rows = pl.ds(pl.multiple_of(q0, Q_BLOCK), Q_BLOCK)

    u0 = pl.multiple_of(8 * (n_qb - 1 - qb), 8)
    j_iota = lax.broadcasted_iota(jnp.int32, (n_sel, Q_BLOCK), 0)
    j_f32 = j_iota.astype(F32)
    t_pos = q0 + lax.broadcasted_iota(jnp.int32, (n_sel, Q_BLOCK), 1)
    cur = t_pos // SEL_BLOCK
    valid = j_iota <= cur
    forced = (j_iota == 0) | (j_iota == cur) | (j_iota == cur - 1)

    logits = [_dot(kc_ref[...], qbd_ref[:, gslice(g)]) for g in range(NSA_GROUPS)]
    for g in range(NSA_GROUPS):
        s = logits[g] + tcmp_ref[pl.ds(u0, n_cmp_rows), gslice(g)]
        m = jnp.maximum(jnp.max(s, axis=0, keepdims=True), 0.1 * NEG)
        p_ref[0:n_cmp_rows, gslice(g)] = jnp.exp2(s - m).astype(BF16)
    o_cmp, imps = [], []
    for g in range(NSA_GROUPS):
        e = p_ref[0:n_cmp_rows, gslice(g)]
        raw = _dot(values_with_ones([vct_ref[...]], g), e)
        total = raw[HEAD_DIM:HEAD_DIM + 1, :]
        inv = 1.0 / jnp.where(total == 0.0, 1.0, total)
        o_cmp.append(raw[0:HEAD_DIM, :] * inv)
        imp_heads = _dot(ovt_ref[...], e) * inv
        imp = imp_heads[:, 0:Q_BLOCK]
        for r in range(1, NSA_HPG):
            imp = imp + imp_heads[:, r * Q_BLOCK:(r + 1) * Q_BLOCK]
        imps.append(imp)

    def window(exact):
        first_tile = jnp.maximum(qb - (WIN_TILES - 1), 0)
        kw = kswa_ref[pl.ds(pl.multiple_of(first_tile * LANES, LANES), WIN_TILES * LANES), :]
        win_vt_tiles = [vswat_ref[first_tile + t] for t in range(WIN_TILES)]
        win_ids = []
        for t in range(WIN_TILES):
            i = qb - (first_tile + t)
            win_ids.append(jnp.where(i < 0, TILE_MASKED, jnp.where(i == WIN_TILES - 1, TILE_WIN_OLD, i)))
        logits = [_dot(kw, qbd_ref[:, gslice(g)]) for g in range(NSA_GROUPS)]
        if not exact:
            m_fix = self_logit(kswa_ref[diag_rows, :].astype(F32).T)
        for g in range(NSA_GROUPS):
            m8 = None
            for t in range(WIN_TILES):
                rows = slice(t * LANES, (t + 1) * LANES)
                v = logits[g][rows, :] + tsel_ref[win_ids[t], :, gslice(g)]
                if exact:
                    s_ref[rows, gslice(g)] = v
                    vm = jnp.max(v.reshape(LANES // 8, 8, GCOLS), axis=0)
                else:
                    pt = jnp.exp2(v - m_fix[:, gslice(g)]).astype(BF16)
                    p_ref[rows, gslice(g)] = pt
                    vm = jnp.max(pt.reshape(LANES // 16, 16, GCOLS), axis=0)
                m8 = vm if m8 is None else jnp.maximum(m8, vm)
            if exact:
                m = jnp.max(m8, axis=0, keepdims=True)
                p_ref[:, gslice(g)] = jnp.exp2(s_ref[:, gslice(g)] - m).astype(BF16)
            else:
                viol_ref[:, gslice(g)] = jnp.maximum(viol_ref[:, gslice(g)],
                                                     jnp.max(m8.astype(F32), axis=0, keepdims=True))
        for g in range(NSA_GROUPS):
            acc_ref[g] = _dot(values_with_ones(win_vt_tiles, g), p_ref[:, gslice(g)])

    quota = top_n - (1 + (cur >= 1).astype(jnp.int32) + (cur >= 2).astype(jnp.int32))
    for g in range(NSA_GROUPS):
        work_ref[g] = jnp.where(forced, -2.0, jnp.where(valid, imps[g], -1.0))
        chosen_ref[g] = jnp.where(forced, 1.0, 0.0)

    def pick_round(index, limited):
        for g in range(NSA_GROUPS):
            work = work_ref[g]
            best = jnp.max(work, axis=0, keepdims=True)
            first = jnp.min(jnp.where(work == best, j_f32, float(n_sel)), axis=0, keepdims=True)
            hit = j_f32 == first
            if limited:
                hit = hit & (quota > index)
            chosen_ref[g] = jnp.where(hit, 1.0, chosen_ref[g])
            work_ref[g] = jnp.where(hit, -2.0, work)

    max_forced = 3
    for index in range(top_n - max_forced):
        pick_round(index, False)

    @pl.when(q0 < (max_forced - 1) * SEL_BLOCK)
    def _():
        for index in range(top_n - max_forced, top_n - 1):
            pick_round(index, True)

    for g in range(NSA_GROUPS):
        addm = jnp.where((chosen_ref[g] > 0.5) & valid, 0.0, NEG).astype(BF16)
        qaug_ref[g, HEAD_DIM:HEAD_DIM + n_sel, :] = jnp.concatenate([addm] * NSA_HPG, axis=1)

    def sel_step(r, near, exact):
        key_rows = pl.ds(pl.multiple_of(r * STEP, STEP), STEP)
        vt_tiles = [vslct_ref[STEP_TILES * r + t] for t in range(STEP_TILES)]
        if near:
            tile_ids = []
            for t in range(STEP_TILES):
                i = qb - (STEP_TILES * r + t)
                tile_ids.append(jnp.where(i < 0, TILE_MASKED, jnp.minimum(i, TILE_FAR)))
        logits = [_dot(kslc_ref[key_rows, g * AUG_DIM:(g + 1) * AUG_DIM], qaug_ref[g])
                  for g in range(NSA_GROUPS)]
        alphas = []
        for g in range(NSA_GROUPS):
            m_old = m_ref[:, gslice(g)]
            m8 = None
            for t in range(STEP_TILES):
                rows = slice(t * LANES, (t + 1) * LANES)
                v = logits[g][rows, :]
                if near:
                    v = v + tsel_ref[tile_ids[t], :, gslice(g)]
                if exact:
                    s_ref[rows, gslice(g)] = v
                    vm = jnp.max(v.reshape(LANES // 8, 8, GCOLS), axis=0)
                else:
                    pt = jnp.exp2(v - m_old).astype(BF16)
                    p_ref[rows, gslice(g)] = pt
                    vm = jnp.max(pt.reshape(LANES // 16, 16, GCOLS), axis=0)
                m8 = vm if m8 is None else jnp.maximum(m8, vm)
            if exact:
                m_new = jnp.maximum(m_old, jnp.max(m8, axis=0, keepdims=True))
                alphas.append(jnp.exp2(m_old - m_new))
                m_ref[:, gslice(g)] = m_new
                for t in range(STEP_TILES):
                    rows = slice(t * LANES, (t + 1) * LANES)
                    p_ref[rows, gslice(g)] = jnp.exp2(s_ref[rows, gslice(g)] - m_new).astype(BF16)
            else:
                viol_ref[:, gslice(g)] = jnp.maximum(viol_ref[:, gslice(g)],
                                                     jnp.max(m8.astype(F32), axis=0, keepdims=True))
        for g in range(NSA_GROUPS):
            pv = _dot(values_with_ones(vt_tiles, g), p_ref[0:STEP, gslice(g)])
            if exact:
                acc_ref[g] = acc_ref[g] * alphas[g] + pv
            else:
                acc_ref[g] = acc_ref[g] + pv

    r_diag = qb // STEP_TILES
    n_far = jnp.maximum((qb - (N_NEAR - 1)) // STEP_TILES, 0)

    def selected(exact):
        if exact:
            m_ref[...] = jnp.full(m_ref.shape, NEG, F32)
        else:
            kt = kslc_ref[diag_rows, :].astype(F32).T
            m_ref[...] = self_logit(jnp.concatenate(
                [kt[g * AUG_DIM:g * AUG_DIM + HEAD_DIM, :] for g in range(NSA_GROUPS)], axis=0))
        acc_ref[...] = jnp.zeros_like(acc_ref)

        def near_body(i, carry):
            sel_step(r_diag - i, True, exact)
            return carry

        def far_body(r, carry):
            sel_step(r, False, exact)
            return carry

        lax.fori_loop(0, r_diag - n_far + 1, near_body, 0)
        lax.fori_loop(0, n_far, far_body, 0)

    viol_ref[...] = jnp.zeros_like(viol_ref)
    selected(exact)
    o_slc = [normalized(acc_ref[g]) for g in range(NSA_GROUPS)]
    window(exact)
    o_swa = [normalized(acc_ref[g]) for g in range(NSA_GROUPS)]
    guard_ref[...] = jnp.broadcast_to(jnp.max(viol_ref[...], axis=1, keepdims=True), guard_ref.shape)

    gates = _sigmoid(gt_ref[...])
    for g in range(NSA_GROUPS):
        def gate_row(c):
            return jnp.concatenate(
                [gates[c * NSA_HEADS + g * NSA_HPG + r:c * NSA_HEADS + g * NSA_HPG + r + 1, :]
                 for r in range(NSA_HPG)], axis=1)
        og = gate_row(0) * o_cmp[g] + gate_row(1) * o_slc[g] + gate_row(2) * o_swa[g]
        for r in range(NSA_HPG):
            h = g * NSA_HPG + r
            ot_ref[h * HEAD_DIM:(h + 1) * HEAD_DIM, :] = og[:, r * Q_BLOCK:(r + 1) * Q_BLOCK]
    out_ref[...] = ot_ref[...].T


def _nsa(qt, gt, kc, vct, kslc, vslct, kswa, vswat, tsel, tcmp, ovt, far, seq, exact):
    b = kc.shape[0]
    n_qb = seq // Q_BLOCK
    assert n_qb % STEP_TILES == 0 and n_qb >= WIN_TILES and WIN_TILES >= STEP_TILES
    n_cmp_rows = kc.shape[1]
    n_sel = seq // SEL_BLOCK
    assert n_sel <= AUG_DIM - HEAD_DIM
    top_n = min(SEL_TOPN, n_sel)
    assert top_n > 3
    n_gate = gt.shape[2]
    const = lambda arr: pl.BlockSpec(arr.shape, lambda i, j: (0,) * arr.ndim,
                                     pipeline_mode=pl.Buffered(1))
    kern = functools.partial(_nsa_kernel, n_qb=n_qb, n_cmp_rows=n_cmp_rows, n_sel=n_sel, top_n=top_n,
                             exact=exact)
    return pl.pallas_call(
        kern,
        grid=(b, n_qb),
        in_specs=[pl.BlockSpec((None, None, NSA_WIDTH, Q_BLOCK), lambda i, j: (i, j, 0, 0)),
                  pl.BlockSpec((None, None, n_gate, Q_BLOCK), lambda i, j: (i, j, 0, 0)),
                  pl.BlockSpec((None, n_cmp_rows, KV_WIDTH), lambda i, j: (i, 0, 0)),
                  pl.BlockSpec((None, KV_WIDTH, n_cmp_rows), lambda i, j: (i, 0, 0)),
                  pl.BlockSpec((None, seq, NSA_GROUPS * AUG_DIM), lambda i, j: (i, 0, 0)),
                  pl.BlockSpec((None, n_qb, KV_WIDTH, Q_BLOCK), lambda i, j: (i, 0, 0, 0)),
                  pl.BlockSpec((None, seq, KV_WIDTH), lambda i, j: (i, 0, 0)),
                  pl.BlockSpec((None, n_qb, KV_WIDTH, Q_BLOCK), lambda i, j: (i, 0, 0, 0)),
                  const(tsel), const(tcmp), const(ovt), const(far)],
        out_specs=[pl.BlockSpec((None, Q_BLOCK, NSA_WIDTH), lambda i, j: (i, j, 0)),
                   pl.BlockSpec((None, None, 8, LANES), lambda i, j: (i, j, 0, 0))],
        out_shape=[jax.ShapeDtypeStruct((b, seq, NSA_WIDTH), F32),
                   jax.ShapeDtypeStruct((b, n_qb, 8, LANES), F32)],
        scratch_shapes=[pltpu.VMEM((KV_WIDTH, QCOLS), BF16),
                        pltpu.VMEM((NSA_GROUPS, AUG_DIM, GCOLS), BF16),
                        pltpu.VMEM((1, QCOLS), F32),
                        pltpu.VMEM((NSA_GROUPS, HEAD_DIM + SUM_ROWS, GCOLS), F32),
                        pltpu.VMEM((NSA_WIDTH, Q_BLOCK), F32),
                        pltpu.VMEM((WIN_TILES * LANES, QCOLS), F32),
                        pltpu.VMEM((WIN_TILES * LANES, QCOLS), BF16),
                        pltpu.VMEM((1, QCOLS), F32),
                        pltpu.VMEM((NSA_GROUPS, n_sel, Q_BLOCK), F32),
                        pltpu.VMEM((NSA_GROUPS, n_sel, Q_BLOCK), F32)],
        compiler_params=pltpu.CompilerParams(dimension_semantics=("parallel", "arbitrary"),
                                             vmem_limit_bytes=VMEM_LIMIT),
        name="nsa_exact" if exact else "nsa",
    )(qt, gt, kc, vct, kslc, vslct, kswa, vswat, tsel, tcmp, ovt, far)


def _ssd_kernel(xbc_ref, dt_ref, convw_ref, convb_ref, dtb_ref, a_ref, dskip_ref, expand_ref, y_ref,
                xp_ref, tail_ref, state_ref, slab_ref):
    L = SSM_CHUNK
    NV = L // 8
    halo = CONV_WIDTH - 1
    c = pl.program_id(1)

    @pl.when(c == 0)
    def _():
        tail_ref[...] = jnp.zeros_like(tail_ref)
        state_ref[...] = jnp.zeros_like(state_ref)

    n_slab = CONV_DIM // LANES
    for s in range(n_slab):
        slab_ref[s] = xbc_ref[:, s * LANES:(s + 1) * LANES]
    slab_ref[n_slab] = jnp.concatenate([dt_ref[...], jnp.zeros((L, LANES - SSM_HEADS), F32)], axis=1)
    for v in range(NV):
        for s in range(n_slab):
            xp_ref[(halo + v) * 8:(halo + v + 1) * 8, s * LANES:(s + 1) * LANES] = \
                slab_ref[s, pl.ds(v, 8, stride=NV), :]
    sub = lax.broadcasted_iota(jnp.int32, (8, CONV_DIM), 0)
    for w in range(halo):
        cur = xp_ref[(NV + w) * 8:(NV + w + 1) * 8, :]
        prev = tail_ref[w * 8:(w + 1) * 8, :]
        xp_ref[w * 8:(w + 1) * 8, :] = pltpu.roll(jnp.where(sub == 7, prev, cur), 1, 0)
    tail_ref[...] = xp_ref[NV * 8:(NV + halo) * 8, :]
    conv = convb_ref[...]
    for k in range(CONV_WIDTH):
        conv = conv + convw_ref[k:k + 1, :] * xp_ref[8 * k:8 * k + L, :]
    xbc = _silu(conv)
    xs = xbc[:, 0:SSM_WIDTH]
    bm = xbc[:, SSM_WIDTH:SSM_WIDTH + SSM_GROUPS * SSM_STATE]
    cm = xbc[:, SSM_WIDTH + SSM_GROUPS * SSM_STATE:]

    dt_raw = jnp.concatenate([slab_ref[n_slab, pl.ds(v, 8, stride=NV), :] for v in range(NV)], axis=0)
    dt = _softplus(dt_raw[:, 0:SSM_HEADS] + dtb_ref[...])
    a_row = -jnp.exp(a_ref[...])
    row_i = lax.broadcasted_iota(jnp.int32, (L, L), 0)
    col_j = lax.broadcasted_iota(jnp.int32, (L, L), 1)
    token = lambda r: (r % 8) * NV + r // 8
    tril = token(row_i) >= token(col_j)
    tril01 = jnp.where(tril, 1.0, 0.0).astype(BF16)
    cs = _dot_exact_rhs(tril01, dt * a_row)
    cst = jnp.concatenate([cs, jnp.zeros((L, L - SSM_HEADS), F32)], axis=1).T

    expand = expand_ref[...]
    cs_x = _dot_exact_lhs(cs, expand)
    dt_x = _dot_exact_lhs(dt, expand)
    total_x = cs_x[L - 1:L, :]
    xdt = (xs * dt_x).astype(BF16)
    xw = (xs * (jnp.exp(total_x - cs_x) * dt_x)).astype(BF16)
    decay_out = jnp.exp(cs_x)
    decay_state = jnp.exp(total_x)

    lane = lax.broadcasted_iota(jnp.int32, (L, LANES), 1)
    gw = SSM_HPG * SSM_HEAD_DIM
    for g in range(SSM_GROUPS):
        bg = bm[:, g * SSM_STATE:(g + 1) * SSM_STATE]
        cg = cm[:, g * SSM_STATE:(g + 1) * SSM_STATE].astype(BF16)
        cb = _dot_nt(cg, bg.astype(BF16))
        st = state_ref[g]
        y_g = _dot(cg, st.astype(BF16)) * decay_out[:, g * gw:(g + 1) * gw]
        state_ref[g] = st * decay_state[:, g * gw:(g + 1) * gw] + \
            _dot(bg.T.astype(BF16), xw[:, g * gw:(g + 1) * gw])
        pieces = []
        for pair in range(SSM_HPG // 2):
            ws = []
            for hh in range(2):
                h = g * SSM_HPG + 2 * pair + hh
                diff = cs[:, h:h + 1] - cst[h:h + 1, :]
                decay = jnp.where(tril, jnp.exp(jnp.where(tril, diff, 0.0)), 0.0)
                ws.append((cb * decay).astype(BF16))
            w_pair = jnp.concatenate(ws, axis=1)
            lo = g * gw + pair * LANES
            slab = xdt[:, lo:lo + LANES]
            zero = jnp.zeros_like(slab)
            x_bd = jnp.concatenate([jnp.where(lane < SSM_HEAD_DIM, slab, zero),
                                    jnp.where(lane >= SSM_HEAD_DIM, slab, zero)], axis=0)
            pieces.append(_dot(w_pair, x_bd))
        y_g = y_g + jnp.concatenate(pieces, axis=1) + \
            dskip_ref[:, g * gw:(g + 1) * gw] * xs[:, g * gw:(g + 1) * gw]
        for s in range(gw // LANES):
            for v in range(NV):
                slab_ref[s, pl.ds(v, 8, stride=NV), :] = y_g[v * 8:(v + 1) * 8, s * LANES:(s + 1) * LANES]
        for s in range(gw // LANES):
            y_ref[:, g * gw + s * LANES:g * gw + (s + 1) * LANES] = slab_ref[s]


def _ssd(xbc, dt, convw, convb, dtb, a_log, dskip_x, expand, seq):
    b = xbc.shape[0]
    nc = seq // SSM_CHUNK
    halo_rows = (CONV_WIDTH - 1) * 8
    const = lambda arr: pl.BlockSpec(arr.shape, lambda i, j: (0,) * arr.ndim)
    return pl.pallas_call(
        _ssd_kernel,
        grid=(b, nc),
        in_specs=[pl.BlockSpec((None, SSM_CHUNK, CONV_DIM), lambda i, j: (i, j, 0)),
                  pl.BlockSpec((None, SSM_CHUNK, SSM_HEADS), lambda i, j: (i, j, 0)),
                  const(convw), const(convb), const(dtb), const(a_log), const(dskip_x), const(expand)],
        out_specs=pl.BlockSpec((None, SSM_CHUNK, SSM_WIDTH), lambda i, j: (i, j, 0)),
        out_shape=jax.ShapeDtypeStruct((b, seq, SSM_WIDTH), F32),
        scratch_shapes=[pltpu.VMEM((SSM_CHUNK + halo_rows, CONV_DIM), F32),
                        pltpu.VMEM((halo_rows, CONV_DIM), F32),
                        pltpu.VMEM((SSM_GROUPS, SSM_STATE, SSM_HPG * SSM_HEAD_DIM), F32),
                        pltpu.VMEM((CONV_DIM // LANES + 1, SSM_CHUNK, LANES), F32)],
        compiler_params=pltpu.CompilerParams(dimension_semantics=("parallel", "arbitrary"),
                                             vmem_limit_bytes=VMEM_LIMIT),
        name="ssd",
    )(xbc, dt, convw, convb, dtb, a_log, dskip_x, expand)


def _epilogue_kernel(x_ref, o_ref, y_ref, nw_ref, wz1_ref, wz2_ref, wmg_ref, won_ref, wos_ref, wo_ref,
                     snw_ref, fnw_ref, out_ref):
    x = x_ref[...]
    ms = jnp.mean(x * x, axis=-1, keepdims=True)
    xn = (x * lax.rsqrt(ms + NORM_EPS) * nw_ref[...]).astype(BF16)
    u = (o_ref[...] * _silu(_dot(xn, wz1_ref[...]))).astype(BF16)
    h_nsa = _dot(u, won_ref[...])
    hh = y_ref[...] * _silu(_dot(xn, wz2_ref[...]))
    gw = SSM_WIDTH // SSM_GROUPS
    parts = []
    for g in range(SSM_GROUPS):
        hg = hh[:, g * gw:(g + 1) * gw]
        hg = hg * lax.rsqrt(jnp.mean(hg * hg, axis=-1, keepdims=True) + NORM_EPS)
        parts.append((hg * snw_ref[:, g * gw:(g + 1) * gw]).astype(BF16))
    h_ssm = _dot(jnp.concatenate(parts, axis=1), wos_ref[...])
    gate = _sigmoid(_dot(xn, wmg_ref[...]))
    mix = (gate[:, 0:D_MODEL] * h_nsa + gate[:, D_MODEL:] * h_ssm).astype(BF16)
    r = x + _dot(mix, wo_ref[...])
    ms2 = jnp.mean(r * r, axis=-1, keepdims=True)
    out_ref[...] = r * lax.rsqrt(ms2 + NORM_EPS) * fnw_ref[...]


def _epilogue(x2, o2, y2, nw, wz1, wz2, wmg, won, wos, wo, snw, fnw, tm):
    t, d = x2.shape
    assert t % tm == 0
    const = lambda arr: pl.BlockSpec(arr.shape, lambda i: (0,) * arr.ndim, pipeline_mode=pl.Buffered(1))
    return pl.pallas_call(
        _epilogue_kernel,
        grid=(t // tm,),
        in_specs=[pl.BlockSpec((tm, d), lambda i: (i, 0)),
                  pl.BlockSpec((tm, NSA_WIDTH), lambda i: (i, 0)),
                  pl.BlockSpec((tm, SSM_WIDTH), lambda i: (i, 0)),
                  const(nw), const(wz1), const(wz2), const(wmg), const(won), const(wos), const(wo),
                  const(snw), const(fnw)],
        out_specs=pl.BlockSpec((tm, d), lambda i: (i, 0)),
        out_shape=jax.ShapeDtypeStruct((t, d), F32),
        compiler_params=pltpu.CompilerParams(dimension_semantics=("parallel",),
                                             vmem_limit_bytes=VMEM_LIMIT),
        name="epilogue",
    )(x2, o2, y2, nw, wz1, wz2, wmg, won, wos, wo, snw, fnw)


def _table_kernel(rb_ref, bucket_ref, out_ref):
    bk = bucket_ref[...]
    for h in range(NSA_HEADS):
        acc = jnp.full(bk.shape, NEG, F32)
        for b in range(REL_BUCKETS):
            acc = jnp.where(bk == b, rb_ref[b, h], acc)
        out_ref[:, h * Q_BLOCK:(h + 1) * Q_BLOCK] = acc


def _bias_table(rb, buckets, tr):
    rows = buckets.shape[0]
    assert rows % tr == 0
    return pl.pallas_call(
        _table_kernel,
        grid=(rows // tr,),
        in_specs=[pl.BlockSpec(memory_space=pltpu.SMEM),
                  pl.BlockSpec((tr, Q_BLOCK), lambda i: (i, 0))],
        out_specs=pl.BlockSpec((tr, QCOLS), lambda i: (i, 0)),
        out_shape=jax.ShapeDtypeStruct((rows, QCOLS), F32),
        compiler_params=pltpu.CompilerParams(dimension_semantics=("parallel",)),
        name="bias_table",
    )(rb, buckets)


def _bucket_tables(seq):
    n_qb = seq // Q_BLOCK
    key = np.arange(LANES)[:, None]
    tok = np.arange(Q_BLOCK)[None, :]
    tiles = [_t5_bucket_np(LANES * o + tok - key) for o in range(N_NEAR)]
    tiles.append(np.full((LANES, Q_BLOCK), REL_BUCKETS - 1, np.int32))
    dwin = WINDOW + tok - key
    tiles.append(_t5_bucket_np(np.where(dwin < WINDOW, dwin, -1)))
    tiles.append(np.full((LANES, Q_BLOCK), MASKED_BUCKET, np.int32))
    assert len(tiles) == N_TILES and LANES * N_NEAR - (LANES - 1) >= 790
    n_cmp_rows = seq // CMP_STRIDE
    rows = 8 * (n_qb - 1) + n_cmp_rows
    rows_pad = -(-rows // LANES) * LANES
    u = np.arange(rows_pad)[:, None]
    cmp_tbl = _t5_bucket_np(tok - CMP_STRIDE * u + Q_BLOCK * (n_qb - 1) - (CMP_BLOCK - 1))
    return np.concatenate(tiles, axis=0), cmp_tbl


def _block_onehot(seq, tm):
    pos = np.arange(seq)
    pat = np.zeros((seq, NSA_GROUPS, AUG_DIM), np.float32)
    pat[pos, :, HEAD_DIM + pos // SEL_BLOCK] = 1.0
    return jnp.asarray(pat.reshape(seq // tm, tm, NSA_GROUPS * AUG_DIM), dtype=BF16)


def _overlap_t(seq):
    n_cmp_rows = seq // CMP_STRIDE
    n_sel = seq // SEL_BLOCK
    c_start = np.arange(n_cmp_rows)[None, :] * CMP_STRIDE
    s_start = np.arange(n_sel)[:, None] * SEL_BLOCK
    ov = (c_start < s_start + SEL_BLOCK) & (c_start + CMP_BLOCK > s_start)
    ov[:, n_cmp_rows - 1] = False
    return jnp.asarray(ov, dtype=BF16)


def _layer(x, norm_w, w_in, cmp_pos_k, cmp_pos_v, cmp_k_w1, cmp_k_b1, cmp_k_w2, cmp_v_w1, cmp_v_b1,
           cmp_v_w2, conv_w, conv_b, dt_bias, a_log, d_skip, ssm_norm_w, w_out_nsa, w_out_ssm, w_out,
           rel_bias, out_norm_w):
    b, s, d = x.shape
    assert d == D_MODEL and s % STEP == 0 and s >= 2 * WINDOW
    t = b * s
    n_qb = s // Q_BLOCK
    cols = _column_offsets()
    wcol = lambda name: w_in[:, cols[name][0]:cols[name][1]]
    x2 = x.reshape(t, d)
    nw = norm_w.reshape(1, d).astype(F32)

    gate_w = wcol('nsa_gate').reshape(d, NSA_HEADS, 3).transpose(0, 2, 1).reshape(d, 3 * NSA_HEADS)
    w_kslc = jnp.pad(wcol('k_slc').reshape(d, NSA_GROUPS, HEAD_DIM),
                     ((0, 0), (0, 0), (0, AUG_DIM - HEAD_DIM))).reshape(d, NSA_GROUPS * AUG_DIM)
    wn = jnp.concatenate([w_kslc, wcol('k_swa'), wcol('k_cmp'), wcol('v_cmp'), wcol('xbc'),
                          wcol('dt')], axis=1).astype(BF16)
    wt = jnp.concatenate([wcol('q') * (HEAD_DIM ** -0.5 * LOG2E), wcol('v_slc'), wcol('v_swa'), gate_w],
                         axis=1).T.astype(BF16)
    kslc, kswa, cc, xbc, dt, qt, vslct, vswat, gt = _proj(x2, nw, wn, wt, _block_onehot(s, STEP), STEP)
    kslc = kslc.reshape(b, s, NSA_GROUPS * AUG_DIM)
    kswa = kswa.reshape(b, s, KV_WIDTH)
    xbc = xbc.reshape(b, s, CONV_DIM)
    dt = dt.reshape(b, s, SSM_HEADS)
    qt = qt.reshape(b, n_qb, NSA_WIDTH, Q_BLOCK)
    vslct = vslct.reshape(b, n_qb, KV_WIDTH, Q_BLOCK)
    vswat = vswat.reshape(b, n_qb, KV_WIDTH, Q_BLOCK)
    gt = gt.reshape(b, n_qb, 3 * NSA_HEADS, Q_BLOCK)

    def pair_diag(w1):
        w = w1.reshape(CMP_BLOCK, HEAD_DIM, CMP_HIDDEN)
        z = jnp.zeros_like(w)
        return jnp.concatenate([jnp.concatenate([w, z], axis=2),
                                jnp.concatenate([z, w], axis=2)], axis=1).astype(BF16)

    twice = lambda a: jnp.tile(a.astype(F32), (1, 2))
    kc, vct = _compress(cc, b, twice(cmp_pos_k), twice(cmp_pos_v),
                        pair_diag(cmp_k_w1), twice(cmp_k_b1.reshape(1, -1)), cmp_k_w2.astype(BF16),
                        pair_diag(cmp_v_w1), twice(cmp_v_b1.reshape(1, -1)), cmp_v_w2.T.astype(BF16))

    sel_buckets, cmp_buckets = _bucket_tables(s)
    rb = rel_bias.astype(F32) * LOG2E
    rb_far = rb[REL_BUCKETS - 1:REL_BUCKETS, :]
    tsel = _bias_table(rb - rb_far, jnp.asarray(sel_buckets), LANES).reshape(N_TILES, LANES, QCOLS)
    tcmp = _bias_table(rb, jnp.asarray(cmp_buckets), LANES)
    far = jnp.repeat(rb_far.reshape(-1), Q_BLOCK).reshape(1, QCOLS)
    nsa_args = (qt, gt, kc, vct, kslc, vslct, kswa, vswat, tsel, tcmp, _overlap_t(s), far, s)
    o_nsa, guard = _nsa(*nsa_args, exact=False)
    o_nsa = lax.cond(jnp.logical_not(jnp.max(guard) <= 2.0 ** MAX_EXP2_EXCESS),
                     lambda: _nsa(*nsa_args, exact=True)[0], lambda: o_nsa)

    expand = jnp.asarray(np.kron(np.eye(SSM_HEADS), np.ones((1, SSM_HEAD_DIM))), dtype=BF16)
    y = _ssd(xbc, dt, conv_w.astype(F32), conv_b.reshape(1, -1).astype(F32),
             dt_bias.reshape(1, -1).astype(F32), a_log.reshape(1, -1).astype(F32),
             jnp.repeat(d_skip.astype(F32), SSM_HEAD_DIM).reshape(1, -1), expand, s)

    out = _epilogue(x2, o_nsa.reshape(t, NSA_WIDTH), y.reshape(t, SSM_WIDTH), nw,
                    wcol('z_nsa').astype(BF16), wcol('z_ssm').astype(BF16), wcol('merge_gate').astype(BF16),
                    w_out_nsa.astype(BF16), w_out_ssm.astype(BF16), w_out.astype(BF16),
                    ssm_norm_w.reshape(1, -1).astype(F32), out_norm_w.reshape(1, -1).astype(F32),
                    256 if t % 256 == 0 else Q_BLOCK)
    return out.reshape(b, s, d)


def kernel(x, norm_w, w_in, cmp_pos_k, cmp_pos_v, cmp_k_w1, cmp_k_b1, cmp_k_w2, cmp_v_w1, cmp_v_b1, cmp_v_w2,
           conv_w, conv_b, dt_bias, a_log, d_skip, ssm_norm_w, w_out_nsa, w_out_ssm, w_out, rel_bias,
           final_norm_w):
    depth = norm_w.shape[0]
    assert depth == 1, "the epilogue fuses the final norm into the single layer"
    return _layer(x, norm_w[0], w_in[0], cmp_pos_k[0], cmp_pos_v[0], cmp_k_w1[0], cmp_k_b1[0], cmp_k_w2[0],
                  cmp_v_w1[0], cmp_v_b1[0], cmp_v_w2[0], conv_w[0], conv_b[0], dt_bias[0], a_log[0],
                  d_skip[0], ssm_norm_w[0], w_out_nsa[0], w_out_ssm[0], w_out[0], rel_bias, final_norm_w)
```

```python
import functools
import math

import numpy as np
import jax
import jax.numpy as jnp
from jax import lax
from jax.experimental import pallas as pl
from jax.experimental.pallas import tpu as pltpu

F32 = jnp.float32
BF16 = jnp.bfloat16

D_MODEL = 1024
NSA_HEADS = 16
NSA_GROUPS = 4
NSA_HPG = NSA_HEADS // NSA_GROUPS
HEAD_DIM = 64
NSA_WIDTH = NSA_HEADS * HEAD_DIM
KV_WIDTH = NSA_GROUPS * HEAD_DIM
CMP_BLOCK = 32
CMP_STRIDE = 16
CMP_HIDDEN = 4 * HEAD_DIM
SEL_BLOCK = 64
SEL_TOPN = 8
WINDOW = 512
Q_BLOCK = 128
FORCE_SCORE = 1.0e4
REL_BUCKETS = 32
SSM_WIDTH = 2 * D_MODEL
SSM_HEAD_DIM = 64
SSM_HEADS = SSM_WIDTH // SSM_HEAD_DIM
SSM_GROUPS = 4
SSM_HPG = SSM_HEADS // SSM_GROUPS
SSM_STATE = 128
CONV_WIDTH = 4
SSM_CHUNK = 128
CONV_DIM = SSM_WIDTH + 2 * SSM_GROUPS * SSM_STATE
NORM_EPS = 1e-6

NEG = -1e30
LOG2E = math.log2(math.e)
LANES = 128
QCOLS = NSA_HEADS * Q_BLOCK
GCOLS = NSA_HPG * Q_BLOCK
AUG_DIM = 2 * HEAD_DIM
N_NEAR = 8
TILE_FAR = N_NEAR
TILE_WIN_OLD = N_NEAR + 1
TILE_MASKED = N_NEAR + 2
N_TILES = N_NEAR + 3
MASKED_BUCKET = REL_BUCKETS
SUM_ROWS = 16
MAX_EXP2_EXCESS = 100.0
STEP_TILES = 4
STEP = STEP_TILES * LANES
WIN_TILES = WINDOW // LANES + 1
VMEM_LIMIT = 56 * 1024 * 1024


def _column_offsets():
    sizes = (('q', NSA_WIDTH), ('k_cmp', KV_WIDTH), ('v_cmp', KV_WIDTH), ('k_slc', KV_WIDTH),
             ('v_slc', KV_WIDTH), ('k_swa', KV_WIDTH), ('v_swa', KV_WIDTH), ('nsa_gate', 3 * NSA_HEADS),
             ('z_nsa', NSA_WIDTH), ('z_ssm', SSM_WIDTH), ('xbc', CONV_DIM), ('dt', SSM_HEADS),
             ('merge_gate', 2 * D_MODEL))
    out, lo = {}, 0
    for name, n in sizes:
        out[name] = (lo, lo + n)
        lo += n
    return out


def _t5_bucket_np(dist):
    dist = np.asarray(dist, dtype=np.int64)
    d = np.maximum(dist, 0)
    max_exact = REL_BUCKETS // 2
    large = np.full(d.shape, max_exact, dtype=np.int64)
    d8 = d.astype(object) ** 8
    for k in range(1, REL_BUCKETS - max_exact):
        large = large + (d8 >= 2 ** (32 + 3 * k)).astype(np.int64)
    bucket = np.where(d < max_exact, d, np.minimum(large, REL_BUCKETS - 1))
    return np.where(dist < 0, MASKED_BUCKET, bucket).astype(np.int32)


def _sigmoid(x):
    return 1.0 / (1.0 + jnp.exp(-x))


def _silu(x):
    return x * _sigmoid(x)


def _softplus(x):
    return jnp.maximum(x, 0.0) + jnp.log1p(jnp.exp(-jnp.abs(x)))


def _dot(a, b):
    return jnp.dot(a, b, preferred_element_type=F32)


def _dot_nt(a, b):
    return lax.dot_general(a, b, (((1,), (1,)), ((), ())), preferred_element_type=F32)


def _split3(x):
    hi = x.astype(BF16)
    r1 = x - hi.astype(F32)
    mid = r1.astype(BF16)
    lo = (r1 - mid.astype(F32)).astype(BF16)
    return hi, mid, lo


def _dot_exact_lhs(x, w01):
    hi, mid, lo = _split3(x)
    return _dot(hi, w01) + _dot(mid, w01) + _dot(lo, w01)


def _dot_exact_rhs(w01, x):
    hi, mid, lo = _split3(x)
    return _dot(w01, hi) + _dot(w01, mid) + _dot(w01, lo)


_NAT_OUTS = (('k_slc', NSA_GROUPS * AUG_DIM, BF16), ('k_swa', KV_WIDTH, BF16), ('kv_cmp', 2 * KV_WIDTH, F32),
             ('xbc', CONV_DIM, F32), ('dt', SSM_HEADS, F32))
_TR_OUTS = (('q', NSA_WIDTH, BF16), ('v_slc', KV_WIDTH, BF16), ('v_swa', KV_WIDTH, BF16),
            ('gate', 3 * NSA_HEADS, F32))


def _proj_kernel(x_ref, nw_ref, wn_ref, wt_ref, onehot_ref, *out_refs, n_sub):
    x = x_ref[...]
    ms = jnp.mean(x * x, axis=-1, keepdims=True)
    xn = (x * lax.rsqrt(ms + NORM_EPS) * nw_ref[...]).astype(BF16)
    lo = 0
    for (name, n, _), o_ref in zip(_NAT_OUTS, out_refs[:len(_NAT_OUTS)]):
        res = _dot(xn, wn_ref[:, lo:lo + n])
        if name == 'k_slc':
            res = res + onehot_ref[...]
        if name == 'kv_cmp':
            for j in range(n // LANES):
                o_ref[j] = res[:, j * LANES:(j + 1) * LANES].astype(o_ref.dtype)
        else:
            o_ref[...] = res.astype(o_ref.dtype)
        lo += n
    lo = 0
    for (_, n, _), o_ref in zip(_TR_OUTS, out_refs[len(_NAT_OUTS):]):
        res = _dot_nt(wt_ref[lo:lo + n, :], xn)
        for s in range(n_sub):
            o_ref[s] = res[:, s * LANES:(s + 1) * LANES].astype(o_ref.dtype)
        lo += n


def _proj(x2, nw, wn, wt, onehot, tm):
    t, d = x2.shape
    assert t % tm == 0 and tm % LANES == 0 and onehot.shape[1] == tm
    n_sub = tm // LANES
    n_pat = onehot.shape[0]
    const = lambda arr: pl.BlockSpec(arr.shape, lambda i: (0,) * arr.ndim, pipeline_mode=pl.Buffered(1))
    slabbed = lambda name: name == 'kv_cmp'
    out_specs = [pl.BlockSpec((n // LANES, tm, LANES), lambda i: (0, i, 0)) if slabbed(name)
                 else pl.BlockSpec((tm, n), lambda i: (i, 0)) for name, n, _ in _NAT_OUTS] + \
                [pl.BlockSpec((n_sub, n, LANES), lambda i: (i, 0, 0)) for _, n, _ in _TR_OUTS]
    out_shape = [jax.ShapeDtypeStruct((n // LANES, t, LANES) if slabbed(name) else (t, n), dt)
                 for name, n, dt in _NAT_OUTS] + \
                [jax.ShapeDtypeStruct((t // LANES, n, LANES), dt) for _, n, dt in _TR_OUTS]
    return pl.pallas_call(
        functools.partial(_proj_kernel, n_sub=n_sub),
        grid=(t // tm,),
        in_specs=[pl.BlockSpec((tm, d), lambda i: (i, 0)), const(nw), const(wn), const(wt),
                  pl.BlockSpec((None, tm, onehot.shape[2]), lambda i: (i % n_pat, 0, 0))],
        out_specs=out_specs,
        out_shape=out_shape,
        compiler_params=pltpu.CompilerParams(dimension_semantics=("parallel",),
                                             vmem_limit_bytes=VMEM_LIMIT),
        name="proj",
    )(x2, nw, wn, wt, onehot)


def _compress_kernel(raw_ref, posk_ref, posv_ref, w1k_ref, b1k_ref, w2k_ref,
                     w1v_ref, b1v_ref, w2vt_ref, kc_ref, vct_ref):
    nseg = kc_ref.shape[0]
    n_slab = raw_ref.shape[0]
    per_kv = n_slab // 2
    for j in range(n_slab):
        is_k = j < per_kv
        pos_ref, w1_ref, b1_ref = (posk_ref, w1k_ref, b1k_ref) if is_k else (posv_ref, w1v_ref, b1v_ref)
        first = jnp.zeros((nseg, 2 * CMP_HIDDEN), F32)
        second = jnp.zeros((nseg, 2 * CMP_HIDDEN), F32)
        for p in range(CMP_STRIDE):
            rows = raw_ref[j, pl.ds(p, nseg, stride=CMP_STRIDE), :]
            first = first + _dot((rows + pos_ref[p:p + 1, :]).astype(BF16), w1_ref[p])
            q = CMP_STRIDE + p
            second = second + _dot((rows + pos_ref[q:q + 1, :]).astype(BF16), w1_ref[q])
        hid = _silu(first + pltpu.roll(second, nseg - 1, 0) + b1_ref[...]).astype(BF16)
        for gi in range(2):
            g = 2 * (j % per_kv) + gi
            hg = hid[:, gi * CMP_HIDDEN:(gi + 1) * CMP_HIDDEN]
            if is_k:
                kc_ref[:, g * HEAD_DIM:(g + 1) * HEAD_DIM] = _dot(hg, w2k_ref[...]).astype(kc_ref.dtype)
            else:
                vct_ref[g * HEAD_DIM:(g + 1) * HEAD_DIM, :] = _dot_nt(w2vt_ref[...], hg).astype(vct_ref.dtype)


def _compress(raw, batch, posk, posv, w1k, b1k, w2k, w1v, b1v, w2vt):
    n_slab, t, lanes = raw.shape
    seq = t // batch
    nseg = seq // CMP_STRIDE
    assert n_slab * lanes == 2 * KV_WIDTH and lanes == 2 * HEAD_DIM
    const = lambda arr: pl.BlockSpec(arr.shape, lambda i: (0,) * arr.ndim, pipeline_mode=pl.Buffered(1))
    return pl.pallas_call(
        _compress_kernel,
        grid=(batch,),
        in_specs=[pl.BlockSpec((n_slab, seq, lanes), lambda i: (0, i, 0)),
                  const(posk), const(posv), const(w1k), const(b1k), const(w2k),
                  const(w1v), const(b1v), const(w2vt)],
        out_specs=[pl.BlockSpec((None, nseg, KV_WIDTH), lambda i: (i, 0, 0)),
                   pl.BlockSpec((None, KV_WIDTH, nseg), lambda i: (i, 0, 0))],
        out_shape=[jax.ShapeDtypeStruct((batch, nseg, KV_WIDTH), BF16),
                   jax.ShapeDtypeStruct((batch, KV_WIDTH, nseg), BF16)],
        compiler_params=pltpu.CompilerParams(dimension_semantics=("parallel",),
                                             vmem_limit_bytes=VMEM_LIMIT),
        name="compress",
    )(raw, posk, posv, w1k, b1k, w2k, w1v, b1v, w2vt)


def _nsa_kernel(qt_ref, gt_ref, kc_ref, vct_ref, kslc_ref, vslct_ref, kswa_ref, vswat_ref,
                tsel_ref, tcmp_ref, ovt_ref, far_ref, out_ref, guard_ref,
                qbd_ref, qaug_ref, m_ref, acc_ref, ot_ref, s_ref, p_ref, viol_ref, work_ref, chosen_ref,
                *, n_qb, n_cmp_rows, n_sel, top_n, exact):
    qb = pl.program_id(1)
    q0 = qb * Q_BLOCK

    @pl.when(qb == 0)
    def _():
        qbd_ref[...] = jnp.zeros_like(qbd_ref)
        qaug_ref[...] = jnp.zeros_like(qaug_ref)

    for g in range(NSA_GROUPS):
        for r in range(NSA_HPG):
            h = g * NSA_HPG + r
            q_head = qt_ref[h * HEAD_DIM:(h + 1) * HEAD_DIM, :]
            qbd_ref[g * HEAD_DIM:(g + 1) * HEAD_DIM, h * Q_BLOCK:(h + 1) * Q_BLOCK] = q_head
            qaug_ref[g, 0:HEAD_DIM, r * Q_BLOCK:(r + 1) * Q_BLOCK] = q_head

    def gslice(g):
        return slice(g * GCOLS, (g + 1) * GCOLS)

    def values_with_ones(vt_tiles, g):
        vt = jnp.concatenate([t[g * HEAD_DIM:(g + 1) * HEAD_DIM, :] for t in vt_tiles], axis=1)
        return jnp.concatenate([vt, jnp.ones((SUM_ROWS, vt.shape[1]), BF16)], axis=0)

    def normalized(a):
        return a[0:HEAD_DIM, :] * (1.0 / a[HEAD_DIM:HEAD_DIM + 1, :])

    def self_logit(kt):
        cols = []
        for h in range(NSA_HEADS):
            g = h // NSA_HPG
            prod = qt_ref[h * HEAD_DIM:(h + 1) * HEAD_DIM, :].astype(F32) * kt[g * HEAD_DIM:(g + 1) * HEAD_DIM, :]
            cols.append(jnp.sum(prod, axis=0, keepdims=True))
        return jnp.concatenate(cols, axis=1) - far_ref[...]

    diag_rows = pl.ds(pl.multiple_of(q0, Q_BLOCK), Q_BLOCK)

    u0 = pl.multiple_of(8 * (n_qb - 1 - qb), 8)
    j_iota = lax.broadcasted_iota(jnp.int32, (n_sel, Q_BLOCK), 0)
    j_f32 = j_iota.astype(F32)
    t_pos = q0 + lax.broadcasted_iota(jnp.int32, (n_sel, Q_BLOCK), 1)
    cur = t_pos // SEL_BLOCK
    valid = j_iota <= cur
    forced = (j_iota == 0) | (j_iota == cur) | (j_iota == cur - 1)

    logits = [_dot(kc_ref[...], qbd_ref[:, gslice(g)]) for g in range(NSA_GROUPS)]
    for g in range(NSA_GROUPS):
        s = logits[g] + tcmp_ref[pl.ds(u0, n_cmp_rows), gslice(g)]
        m = jnp.maximum(jnp.max(s, axis=0, keepdims=True), 0.1 * NEG)
        p_ref[0:n_cmp_rows, gslice(g)] = jnp.exp2(s - m).astype(BF16)
    o_cmp, imps = [], []
    for g in range(NSA_GROUPS):
        e = p_ref[0:n_cmp_rows, gslice(g)]
        raw = _dot(values_with_ones([vct_ref[...]], g), e)
        total = raw[HEAD_DIM:HEAD_DIM + 1, :]
        inv = 1.0 / jnp.where(total == 0.0, 1.0, total)
        o_cmp.append(raw[0:HEAD_DIM, :] * inv)
        imp_heads = _dot(ovt_ref[...], e) * inv
        imp = imp_heads[:, 0:Q_BLOCK]
        for r in range(1, NSA_HPG):
            imp = imp + imp_heads[:, r * Q_BLOCK:(r + 1) * Q_BLOCK]
        imps.append(imp)

    def window(exact):
        first_tile = jnp.maximum(qb - (WIN_TILES - 1), 0)
        kw = kswa_ref[pl.ds(pl.multiple_of(first_tile * LANES, LANES), WIN_TILES * LANES), :]
        win_vt_tiles = [vswat_ref[first_tile + t] for t in range(WIN_TILES)]
        win_ids = []
        for t in range(WIN_TILES):
            i = qb - (first_tile + t)
            win_ids.append(jnp.where(i < 0, TILE_MASKED, jnp.where(i == WIN_TILES - 1, TILE_WIN_OLD, i)))
        logits = [_dot(kw, qbd_ref[:, gslice(g)]) for g in range(NSA_GROUPS)]
        if not exact:
            m_fix = self_logit(kswa_ref[diag_rows, :].astype(F32).T)
        for g in range(NSA_GROUPS):
            m8 = None
            for t in range(WIN_TILES):
                rows = slice(t * LANES, (t + 1) * LANES)
                v = logits[g][rows, :] + tsel_ref[win_ids[t], :, gslice(g)]
                if exact:
                    s_ref[rows, gslice(g)] = v
                    vm = jnp.max(v.reshape(LANES // 8, 8, GCOLS), axis=0)
                else:
                    pt = jnp.exp2(v - m_fix[:, gslice(g)]).astype(BF16)
                    p_ref[rows, gslice(g)] = pt
                    vm = jnp.max(pt.reshape(LANES // 16, 16, GCOLS), axis=0)
                m8 = vm if m8 is None else jnp.maximum(m8, vm)
            if exact:
                m = jnp.max(m8, axis=0, keepdims=True)
                p_ref[0:WIN_TILES * LANES, gslice(g)] = jnp.exp2(s_ref[:, gslice(g)] - m).astype(BF16)
            else:
                viol_ref[:, gslice(g)] = jnp.maximum(viol_ref[:, gslice(g)],
                                                     jnp.max(m8.astype(F32), axis=0, keepdims=True))
        for g in range(NSA_GROUPS):
            acc_ref[g] = _dot(values_with_ones(win_vt_tiles, g), p_ref[0:WIN_TILES * LANES, gslice(g)])

    quota = top_n - (1 + (cur >= 1).astype(jnp.int32) + (cur >= 2).astype(jnp.int32))
    for g in range(NSA_GROUPS):
        work_ref[g] = jnp.where(forced, -2.0, jnp.where(valid, imps[g], -1.0))
        chosen_ref[g] = jnp.where(forced, 1.0, 0.0)

    def pick_round(index, limited):
        for g in range(NSA_GROUPS):
            work = work_ref[g]
            best = jnp.max(work, axis=0, keepdims=True)
            first = jnp.min(jnp.where(work == best, j_f32, float(n_sel)), axis=0, keepdims=True)
            hit = j_f32 == first
            if limited:
                hit = hit & (quota > index)
            chosen_ref[g] = jnp.where(hit, 1.0, chosen_ref[g])
            work_ref[g] = jnp.where(hit, -2.0, work)

    max_forced = 3
    for index in range(top_n - max_forced):
        pick_round(index, False)

    @pl.when(q0 < (max_forced - 1) * SEL_BLOCK)
    def _():
        for index in range(top_n - max_forced, top_n - 1):
            pick_round(index, True)

    for g in range(NSA_GROUPS):
        addm = jnp.where((chosen_ref[g] > 0.5) & valid, 0.0, NEG).astype(BF16)
        qaug_ref[g, HEAD_DIM:HEAD_DIM + n_sel, :] = jnp.concatenate([addm] * NSA_HPG, axis=1)

    def sel_logits(tile0, n_tiles):
        key_rows = pl.ds(pl.multiple_of(tile0 * LANES, LANES), n_tiles * LANES)
        return [_dot(kslc_ref[key_rows, g * AUG_DIM:(g + 1) * AUG_DIM], qaug_ref[g])
                for g in range(NSA_GROUPS)]

    def sel_update(tile0, n_tiles, near, exact, logits, p_row0):
        vt_tiles = [vslct_ref[tile0 + t] for t in range(n_tiles)]
        if near:
            tile_ids = []
            for t in range(n_tiles):
                i = qb - (tile0 + t)
                tile_ids.append(jnp.where(i < 0, TILE_MASKED, jnp.minimum(i, TILE_FAR)))
        alphas = []
        for g in range(NSA_GROUPS):
            m_old = m_ref[:, gslice(g)]
            m8 = None
            for t in range(n_tiles):
                rows = slice(t * LANES, (t + 1) * LANES)
                v = logits[g][rows, :]
                if near:
                    v = v + tsel_ref[tile_ids[t], :, gslice(g)]
                if exact:
                    s_ref[rows, gslice(g)] = v
                    vm = jnp.max(v.reshape(LANES // 8, 8, GCOLS), axis=0)
                else:
                    pt = jnp.exp2(v - m_old).astype(BF16)
                    p_ref[p_row0 + t * LANES:p_row0 + (t + 1) * LANES, gslice(g)] = pt
                    vm = jnp.max(pt.reshape(LANES // 16, 16, GCOLS), axis=0)
                m8 = vm if m8 is None else jnp.maximum(m8, vm)
            if exact:
                m_new = jnp.maximum(m_old, jnp.max(m8, axis=0, keepdims=True))
                alphas.append(jnp.exp2(m_old - m_new))
                m_ref[:, gslice(g)] = m_new
                for t in range(n_tiles):
                    rows = slice(t * LANES, (t + 1) * LANES)
                    p_ref[p_row0 + t * LANES:p_row0 + (t + 1) * LANES, gslice(g)] = \
                        jnp.exp2(s_ref[rows, gslice(g)] - m_new).astype(BF16)
            else:
                viol_ref[:, gslice(g)] = jnp.maximum(viol_ref[:, gslice(g)],
                                                     jnp.max(m8.astype(F32), axis=0, keepdims=True))
        for g in range(NSA_GROUPS):
            pv = _dot(values_with_ones(vt_tiles, g), p_ref[p_row0:p_row0 + n_tiles * LANES, gslice(g)])
            if exact:
                acc_ref[g] = acc_ref[g] * alphas[g] + pv
            else:
                acc_ref[g] = acc_ref[g] + pv

    def sel_step(tile0, n_tiles, near, exact):
        sel_update(tile0, n_tiles, near, exact, sel_logits(tile0, n_tiles), 0)

    def sel_two_steps(tile_a, tile_b, near):
        la, lb = sel_logits(tile_a, STEP_TILES), sel_logits(tile_b, STEP_TILES)
        sel_update(tile_a, STEP_TILES, near, False, la, 0)
        sel_update(tile_b, STEP_TILES, near, False, lb, STEP)

    r_diag = qb // STEP_TILES
    n_far = jnp.maximum((qb - (N_NEAR - 1)) // STEP_TILES, 0)
    half = STEP_TILES // 2

    def selected(exact):
        if exact:
            m_ref[...] = jnp.full(m_ref.shape, NEG, F32)
        else:
            kt = kslc_ref[diag_rows, :].astype(F32).T
            m_ref[...] = self_logit(jnp.concatenate(
                [kt[g * AUG_DIM:g * AUG_DIM + HEAD_DIM, :] for g in range(NSA_GROUPS)], axis=0))
        acc_ref[...] = jnp.zeros_like(acc_ref)

        @pl.when(qb % STEP_TILES < half)
        def _():
            sel_step(STEP_TILES * r_diag, half, True, exact)

        @pl.when(qb % STEP_TILES >= half)
        def _():
            sel_step(STEP_TILES * r_diag, STEP_TILES, True, exact)

        def near_body(i, carry):
            sel_step(STEP_TILES * (r_diag - i), STEP_TILES, True, exact)
            return carry

        def far_body(r, carry):
            sel_step(STEP_TILES * r, STEP_TILES, False, exact)
            return carry

        def far_pair_body(k, carry):
            sel_two_steps(STEP_TILES * 2 * k, STEP_TILES * (2 * k + 1), False)
            return carry

        lax.fori_loop(1, r_diag - n_far + 1, near_body, 0)
        if exact:
            lax.fori_loop(0, n_far, far_body, 0)
        else:
            @pl.when(n_far % 2 == 1)
            def _():
                sel_step(STEP_TILES * (n_far - 1), STEP_TILES, False, False)

            lax.fori_loop(0, n_far // 2, far_pair_body, 0)

    viol_ref[...] = jnp.zeros_like(viol_ref)
    selected(exact)
    o_slc = [normalized(acc_ref[g]) for g in range(NSA_GROUPS)]
    window(exact)
    o_swa = [normalized(acc_ref[g]) for g in range(NSA_GROUPS)]
    guard_ref[...] = jnp.broadcast_to(jnp.max(viol_ref[...], axis=1, keepdims=True), guard_ref.shape)

    gates = _sigmoid(gt_ref[...])
    for g in range(NSA_GROUPS):
        def gate_row(c):
            return jnp.concatenate(
                [gates[c * NSA_HEADS + g * NSA_HPG + r:c * NSA_HEADS + g * NSA_HPG + r + 1, :]
                 for r in range(NSA_HPG)], axis=1)
        og = gate_row(0) * o_cmp[g] + gate_row(1) * o_slc[g] + gate_row(2) * o_swa[g]
        for r in range(NSA_HPG):
            h = g * NSA_HPG + r
            ot_ref[h * HEAD_DIM:(h + 1) * HEAD_DIM, :] = og[:, r * Q_BLOCK:(r + 1) * Q_BLOCK]
    out_ref[...] = ot_ref[...].T


def _nsa(qt, gt, kc, vct, kslc, vslct, kswa, vswat, tsel, tcmp, ovt, far, seq, exact):
    b = kc.shape[0]
    n_qb = seq // Q_BLOCK
    assert n_qb % STEP_TILES == 0 and n_qb >= WIN_TILES and WIN_TILES >= STEP_TILES
    n_cmp_rows = kc.shape[1]
    n_sel = seq // SEL_BLOCK
    assert n_sel <= AUG_DIM - HEAD_DIM
    top_n = min(SEL_TOPN, n_sel)
    assert top_n > 3
    n_gate = gt.shape[2]
    const = lambda arr: pl.BlockSpec(arr.shape, lambda i, j: (0,) * arr.ndim,
                                     pipeline_mode=pl.Buffered(1))
    kern = functools.partial(_nsa_kernel, n_qb=n_qb, n_cmp_rows=n_cmp_rows, n_sel=n_sel, top_n=top_n,
                             exact=exact)
    return pl.pallas_call(
        kern,
        grid=(b, n_qb),
        in_specs=[pl.BlockSpec((None, None, NSA_WIDTH, Q_BLOCK), lambda i, j: (i, j, 0, 0)),
                  pl.BlockSpec((None, None, n_gate, Q_BLOCK), lambda i, j: (i, j, 0, 0)),
                  pl.BlockSpec((None, n_cmp_rows, KV_WIDTH), lambda i, j: (i, 0, 0)),
                  pl.BlockSpec((None, KV_WIDTH, n_cmp_rows), lambda i, j: (i, 0, 0)),
                  pl.BlockSpec((None, seq, NSA_GROUPS * AUG_DIM), lambda i, j: (i, 0, 0)),
                  pl.BlockSpec((None, n_qb, KV_WIDTH, Q_BLOCK), lambda i, j: (i, 0, 0, 0)),
                  pl.BlockSpec((None, seq, KV_WIDTH), lambda i, j: (i, 0, 0)),
                  pl.BlockSpec((None, n_qb, KV_WIDTH, Q_BLOCK), lambda i, j: (i, 0, 0, 0)),
                  const(tsel), const(tcmp), const(ovt), const(far)],
        out_specs=[pl.BlockSpec((None, Q_BLOCK, NSA_WIDTH), lambda i, j: (i, j, 0)),
                   pl.BlockSpec((None, None, 8, LANES), lambda i, j: (i, j, 0, 0))],
        out_shape=[jax.ShapeDtypeStruct((b, seq, NSA_WIDTH), F32),
                   jax.ShapeDtypeStruct((b, n_qb, 8, LANES), F32)],
        scratch_shapes=[pltpu.VMEM((KV_WIDTH, QCOLS), BF16),
                        pltpu.VMEM((NSA_GROUPS, AUG_DIM, GCOLS), BF16),
                        pltpu.VMEM((1, QCOLS), F32),
                        pltpu.VMEM((NSA_GROUPS, HEAD_DIM + SUM_ROWS, GCOLS), F32),
                        pltpu.VMEM((NSA_WIDTH, Q_BLOCK), F32),
                        pltpu.VMEM((WIN_TILES * LANES, QCOLS), F32),
                        pltpu.VMEM((max(WIN_TILES * LANES, 2 * STEP), QCOLS), BF16),
                        pltpu.VMEM((1, QCOLS), F32),
                        pltpu.VMEM((NSA_GROUPS, n_sel, Q_BLOCK), F32),
                        pltpu.VMEM((NSA_GROUPS, n_sel, Q_BLOCK), F32)],
        compiler_params=pltpu.CompilerParams(dimension_semantics=("parallel", "arbitrary"),
                                             vmem_limit_bytes=VMEM_LIMIT),
        name="nsa_exact" if exact else "nsa",
    )(qt, gt, kc, vct, kslc, vslct, kswa, vswat, tsel, tcmp, ovt, far)


def _ssd_kernel(xbc_ref, dt_ref, convw_ref, convb_ref, dtb_ref, a_ref, dskip_ref, expand_ref, y_ref,
                xp_ref, tail_ref, state_ref, slab_ref):
    L = SSM_CHUNK
    NV = L // 8
    halo = CONV_WIDTH - 1
    c = pl.program_id(1)

    @pl.when(c == 0)
    def _():
        tail_ref[...] = jnp.zeros_like(tail_ref)
        state_ref[...] = jnp.zeros_like(state_ref)

    n_slab = CONV_DIM // LANES
    for s in range(n_slab):
        slab_ref[s] = xbc_ref[:, s * LANES:(s + 1) * LANES]
    slab_ref[n_slab] = jnp.concatenate([dt_ref[...], jnp.zeros((L, LANES - SSM_HEADS), F32)], axis=1)
    for v in range(NV):
        for s in range(n_slab):
            xp_ref[(halo + v) * 8:(halo + v + 1) * 8, s * LANES:(s + 1) * LANES] = \
                slab_ref[s, pl.ds(v, 8, stride=NV), :]
    sub = lax.broadcasted_iota(jnp.int32, (8, CONV_DIM), 0)
    for w in range(halo):
        cur = xp_ref[(NV + w) * 8:(NV + w + 1) * 8, :]
        prev = tail_ref[w * 8:(w + 1) * 8, :]
        xp_ref[w * 8:(w + 1) * 8, :] = pltpu.roll(jnp.where(sub == 7, prev, cur), 1, 0)
    tail_ref[...] = xp_ref[NV * 8:(NV + halo) * 8, :]
    conv = convb_ref[...]
    for k in range(CONV_WIDTH):
        conv = conv + convw_ref[k:k + 1, :] * xp_ref[8 * k:8 * k + L, :]
    xbc = _silu(conv)
    xs = xbc[:, 0:SSM_WIDTH]
    bm = xbc[:, SSM_WIDTH:SSM_WIDTH + SSM_GROUPS * SSM_STATE]
    cm = xbc[:, SSM_WIDTH + SSM_GROUPS * SSM_STATE:]

    dt_raw = jnp.concatenate([slab_ref[n_slab, pl.ds(v, 8, stride=NV), :] for v in range(NV)], axis=0)
    dt = _softplus(dt_raw[:, 0:SSM_HEADS] + dtb_ref[...])
    a_row = -jnp.exp(a_ref[...])
    row_i = lax.broadcasted_iota(jnp.int32, (L, L), 0)
    col_j = lax.broadcasted_iota(jnp.int32, (L, L), 1)
    token = lambda r: (r % 8) * NV + r // 8
    tril = token(row_i) >= token(col_j)
    tril01 = jnp.where(tril, 1.0, 0.0).astype(BF16)
    cs = _dot_exact_rhs(tril01, dt * a_row)
    cst = jnp.concatenate([cs, jnp.zeros((L, L - SSM_HEADS), F32)], axis=1).T

    expand = expand_ref[...]
    cs_x = _dot_exact_lhs(cs, expand)
    dt_x = _dot_exact_lhs(dt, expand)
    total_x = cs_x[L - 1:L, :]
    xdt = (xs * dt_x).astype(BF16)
    xw = (xs * (jnp.exp(total_x - cs_x) * dt_x)).astype(BF16)
    decay_out = jnp.exp(cs_x)
    decay_state = jnp.exp(total_x)

    lane = lax.broadcasted_iota(jnp.int32, (L, LANES), 1)
    gw = SSM_HPG * SSM_HEAD_DIM
    for g in range(SSM_GROUPS):
        bg = bm[:, g * SSM_STATE:(g + 1) * SSM_STATE]
        cg = cm[:, g * SSM_STATE:(g + 1) * SSM_STATE].astype(BF16)
        cb = _dot_nt(cg, bg.astype(BF16))
        st = state_ref[g]
        y_g = _dot(cg, st.astype(BF16)) * decay_out[:, g * gw:(g + 1) * gw]
        state_ref[g] = st * decay_state[:, g * gw:(g + 1) * gw] + \
            _dot(bg.T.astype(BF16), xw[:, g * gw:(g + 1) * gw])
        pieces = []
        for pair in range(SSM_HPG // 2):
            ws = []
            for hh in range(2):
                h = g * SSM_HPG + 2 * pair + hh
                diff = cs[:, h:h + 1] - cst[h:h + 1, :]
                decay = jnp.where(tril, jnp.exp(jnp.where(tril, diff, 0.0)), 0.0)
                ws.append((cb * decay).astype(BF16))
            w_pair = jnp.concatenate(ws, axis=1)
            lo = g * gw + pair * LANES
            slab = xdt[:, lo:lo + LANES]
            zero = jnp.zeros_like(slab)
            x_bd = jnp.concatenate([jnp.where(lane < SSM_HEAD_DIM, slab, zero),
                                    jnp.where(lane >= SSM_HEAD_DIM, slab, zero)], axis=0)
            pieces.append(_dot(w_pair, x_bd))
        y_g = y_g + jnp.concatenate(pieces, axis=1) + \
            dskip_ref[:, g * gw:(g + 1) * gw] * xs[:, g * gw:(g + 1) * gw]
        for s in range(gw // LANES):
            for v in range(NV):
                slab_ref[s, pl.ds(v, 8, stride=NV), :] = y_g[v * 8:(v + 1) * 8, s * LANES:(s + 1) * LANES]
        for s in range(gw // LANES):
            y_ref[:, g * gw + s * LANES:g * gw + (s + 1) * LANES] = slab_ref[s]


def _ssd(xbc, dt, convw, convb, dtb, a_log, dskip_x, expand, seq):
    b = xbc.shape[0]
    nc = seq // SSM_CHUNK
    halo_rows = (CONV_WIDTH - 1) * 8
    const = lambda arr: pl.BlockSpec(arr.shape, lambda i, j: (0,) * arr.ndim)
    return pl.pallas_call(
        _ssd_kernel,
        grid=(b, nc),
        in_specs=[pl.BlockSpec((None, SSM_CHUNK, CONV_DIM), lambda i, j: (i, j, 0)),
                  pl.BlockSpec((None, SSM_CHUNK, SSM_HEADS), lambda i, j: (i, j, 0)),
                  const(convw), const(convb), const(dtb), const(a_log), const(dskip_x), const(expand)],
        out_specs=pl.BlockSpec((None, SSM_CHUNK, SSM_WIDTH), lambda i, j: (i, j, 0)),
        out_shape=jax.ShapeDtypeStruct((b, seq, SSM_WIDTH), F32),
        scratch_shapes=[pltpu.VMEM((SSM_CHUNK + halo_rows, CONV_DIM), F32),
                        pltpu.VMEM((halo_rows, CONV_DIM), F32),
                        pltpu.VMEM((SSM_GROUPS, SSM_STATE, SSM_HPG * SSM_HEAD_DIM), F32),
                        pltpu.VMEM((CONV_DIM // LANES + 1, SSM_CHUNK, LANES), F32)],
        compiler_params=pltpu.CompilerParams(dimension_semantics=("parallel", "arbitrary"),
                                             vmem_limit_bytes=VMEM_LIMIT),
        name="ssd",
    )(xbc, dt, convw, convb, dtb, a_log, dskip_x, expand)


def _epilogue_kernel(x_ref, o_ref, y_ref, nw_ref, wz1_ref, wz2_ref, wmg_ref, won_ref, wos_ref, wo_ref,
                     snw_ref, fnw_ref, out_ref):
    x = x_ref[...]
    ms = jnp.mean(x * x, axis=-1, keepdims=True)
    xn = (x * lax.rsqrt(ms + NORM_EPS) * nw_ref[...]).astype(BF16)
    u = (o_ref[...] * _silu(_dot(xn, wz1_ref[...]))).astype(BF16)
    h_nsa = _dot(u, won_ref[...])
    hh = y_ref[...] * _silu(_dot(xn, wz2_ref[...]))
    gw = SSM_WIDTH // SSM_GROUPS
    parts = []
    for g in range(SSM_GROUPS):
        hg = hh[:, g * gw:(g + 1) * gw]
        hg = hg * lax.rsqrt(jnp.mean(hg * hg, axis=-1, keepdims=True) + NORM_EPS)
        parts.append((hg * snw_ref[:, g * gw:(g + 1) * gw]).astype(BF16))
    h_ssm = _dot(jnp.concatenate(parts, axis=1), wos_ref[...])
    gate = _sigmoid(_dot(xn, wmg_ref[...]))
    mix = (gate[:, 0:D_MODEL] * h_nsa + gate[:, D_MODEL:] * h_ssm).astype(BF16)
    r = x + _dot(mix, wo_ref[...])
    ms2 = jnp.mean(r * r, axis=-1, keepdims=True)
    out_ref[...] = r * lax.rsqrt(ms2 + NORM_EPS) * fnw_ref[...]


def _epilogue(x2, o2, y2, nw, wz1, wz2, wmg, won, wos, wo, snw, fnw, tm):
    t, d = x2.shape
    assert t % tm == 0
    const = lambda arr: pl.BlockSpec(arr.shape, lambda i: (0,) * arr.ndim, pipeline_mode=pl.Buffered(1))
    return pl.pallas_call(
        _epilogue_kernel,
        grid=(t // tm,),
        in_specs=[pl.BlockSpec((tm, d), lambda i: (i, 0)),
                  pl.BlockSpec((tm, NSA_WIDTH), lambda i: (i, 0)),
                  pl.BlockSpec((tm, SSM_WIDTH), lambda i: (i, 0)),
                  const(nw), const(wz1), const(wz2), const(wmg), const(won), const(wos), const(wo),
                  const(snw), const(fnw)],
        out_specs=pl.BlockSpec((tm, d), lambda i: (i, 0)),
        out_shape=jax.ShapeDtypeStruct((t, d), F32),
        compiler_params=pltpu.CompilerParams(dimension_semantics=("parallel",),
                                             vmem_limit_bytes=VMEM_LIMIT),
        name="epilogue",
    )(x2, o2, y2, nw, wz1, wz2, wmg, won, wos, wo, snw, fnw)


def _table_kernel(rb_ref, bucket_ref, out_ref):
    bk = bucket_ref[...]
    for h in range(NSA_HEADS):
        acc = jnp.full(bk.shape, NEG, F32)
        for b in range(REL_BUCKETS):
            acc = jnp.where(bk == b, rb_ref[b, h], acc)
        out_ref[:, h * Q_BLOCK:(h + 1) * Q_BLOCK] = acc


def _bias_table(rb, buckets, tr):
    rows = buckets.shape[0]
    assert rows % tr == 0
    return pl.pallas_call(
        _table_kernel,
        grid=(rows // tr,),
        in_specs=[pl.BlockSpec(memory_space=pltpu.SMEM),
                  pl.BlockSpec((tr, Q_BLOCK), lambda i: (i, 0))],
        out_specs=pl.BlockSpec((tr, QCOLS), lambda i: (i, 0)),
        out_shape=jax.ShapeDtypeStruct((rows, QCOLS), F32),
        compiler_params=pltpu.CompilerParams(dimension_semantics=("parallel",)),
        name="bias_table",
    )(rb, buckets)


def _bucket_tables(seq):
    n_qb = seq // Q_BLOCK
    key = np.arange(LANES)[:, None]
    tok = np.arange(Q_BLOCK)[None, :]
    tiles = [_t5_bucket_np(LANES * o + tok - key) for o in range(N_NEAR)]
    tiles.append(np.full((LANES, Q_BLOCK), REL_BUCKETS - 1, np.int32))
    dwin = WINDOW + tok - key
    tiles.append(_t5_bucket_np(np.where(dwin < WINDOW, dwin, -1)))
    tiles.append(np.full((LANES, Q_BLOCK), MASKED_BUCKET, np.int32))
    assert len(tiles) == N_TILES and LANES * N_NEAR - (LANES - 1) >= 790
    n_cmp_rows = seq // CMP_STRIDE
    rows = 8 * (n_qb - 1) + n_cmp_rows
    rows_pad = -(-rows // LANES) * LANES
    u = np.arange(rows_pad)[:, None]
    cmp_tbl = _t5_bucket_np(tok - CMP_STRIDE * u + Q_BLOCK * (n_qb - 1) - (CMP_BLOCK - 1))
    return np.concatenate(tiles, axis=0), cmp_tbl


def _block_onehot(seq, tm):
    pos = np.arange(seq)
    pat = np.zeros((seq, NSA_GROUPS, AUG_DIM), np.float32)
    pat[pos, :, HEAD_DIM + pos // SEL_BLOCK] = 1.0
    return jnp.asarray(pat.reshape(seq // tm, tm, NSA_GROUPS * AUG_DIM), dtype=BF16)


def _overlap_t(seq):
    n_cmp_rows = seq // CMP_STRIDE
    n_sel = seq // SEL_BLOCK
    c_start = np.arange(n_cmp_rows)[None, :] * CMP_STRIDE
    s_start = np.arange(n_sel)[:, None] * SEL_BLOCK
    ov = (c_start < s_start + SEL_BLOCK) & (c_start + CMP_BLOCK > s_start)
    ov[:, n_cmp_rows - 1] = False
    return jnp.asarray(ov, dtype=BF16)


def _layer(x, norm_w, w_in, cmp_pos_k, cmp_pos_v, cmp_k_w1, cmp_k_b1, cmp_k_w2, cmp_v_w1, cmp_v_b1,
           cmp_v_w2, conv_w, conv_b, dt_bias, a_log, d_skip, ssm_norm_w, w_out_nsa, w_out_ssm, w_out,
           rel_bias, out_norm_w):
    b, s, d = x.shape
    assert d == D_MODEL and s % STEP == 0 and s >= 2 * WINDOW
    t = b * s
    n_qb = s // Q_BLOCK
    cols = _column_offsets()
    wcol = lambda name: w_in[:, cols[name][0]:cols[name][1]]
    x2 = x.reshape(t, d)
    nw = norm_w.reshape(1, d).astype(F32)

    gate_w = wcol('nsa_gate').reshape(d, NSA_HEADS, 3).transpose(0, 2, 1).reshape(d, 3 * NSA_HEADS)
    w_kslc = jnp.pad(wcol('k_slc').reshape(d, NSA_GROUPS, HEAD_DIM),
                     ((0, 0), (0, 0), (0, AUG_DIM - HEAD_DIM))).reshape(d, NSA_GROUPS * AUG_DIM)
    wn = jnp.concatenate([w_kslc, wcol('k_swa'), wcol('k_cmp'), wcol('v_cmp'), wcol('xbc'),
                          wcol('dt')], axis=1).astype(BF16)
    wt = jnp.concatenate([wcol('q') * (HEAD_DIM ** -0.5 * LOG2E), wcol('v_slc'), wcol('v_swa'), gate_w],
                         axis=1).T.astype(BF16)
    kslc, kswa, cc, xbc, dt, qt, vslct, vswat, gt = _proj(x2, nw, wn, wt, _block_onehot(s, STEP), STEP)
    kslc = kslc.reshape(b, s, NSA_GROUPS * AUG_DIM)
    kswa = kswa.reshape(b, s, KV_WIDTH)
    xbc = xbc.reshape(b, s, CONV_DIM)
    dt = dt.reshape(b, s, SSM_HEADS)
    qt = qt.reshape(b, n_qb, NSA_WIDTH, Q_BLOCK)
    vslct = vslct.reshape(b, n_qb, KV_WIDTH, Q_BLOCK)
    vswat = vswat.reshape(b, n_qb, KV_WIDTH, Q_BLOCK)
    gt = gt.reshape(b, n_qb, 3 * NSA_HEADS, Q_BLOCK)

    def pair_diag(w1):
        w = w1.reshape(CMP_BLOCK, HEAD_DIM, CMP_HIDDEN)
        z = jnp.zeros_like(w)
        return jnp.concatenate([jnp.concatenate([w, z], axis=2),
                                jnp.concatenate([z, w], axis=2)], axis=1).astype(BF16)

    twice = lambda a: jnp.tile(a.astype(F32), (1, 2))
    kc, vct = _compress(cc, b, twice(cmp_pos_k), twice(cmp_pos_v),
                        pair_diag(cmp_k_w1), twice(cmp_k_b1.reshape(1, -1)), cmp_k_w2.astype(BF16),
                        pair_diag(cmp_v_w1), twice(cmp_v_b1.reshape(1, -1)), cmp_v_w2.T.astype(BF16))

    sel_buckets, cmp_buckets = _bucket_tables(s)
    rb = rel_bias.astype(F32) * LOG2E
    rb_far = rb[REL_BUCKETS - 1:REL_BUCKETS, :]
    tsel = _bias_table(rb - rb_far, jnp.asarray(sel_buckets), LANES).reshape(N_TILES, LANES, QCOLS)
    tcmp = _bias_table(rb, jnp.asarray(cmp_buckets), LANES)
    far = jnp.repeat(rb_far.reshape(-1), Q_BLOCK).reshape(1, QCOLS)
    nsa_args = (qt, gt, kc, vct, kslc, vslct, kswa, vswat, tsel, tcmp, _overlap_t(s), far, s)
    o_nsa, guard = _nsa(*nsa_args, exact=False)
    o_nsa = lax.cond(jnp.logical_not(jnp.max(guard) <= 2.0 ** MAX_EXP2_EXCESS),
                     lambda: _nsa(*nsa_args, exact=True)[0], lambda: o_nsa)

    expand = jnp.asarray(np.kron(np.eye(SSM_HEADS), np.ones((1, SSM_HEAD_DIM))), dtype=BF16)
    y = _ssd(xbc, dt, conv_w.astype(F32), conv_b.reshape(1, -1).astype(F32),
             dt_bias.reshape(1, -1).astype(F32), a_log.reshape(1, -1).astype(F32),
             jnp.repeat(d_skip.astype(F32), SSM_HEAD_DIM).reshape(1, -1), expand, s)

    out = _epilogue(x2, o_nsa.reshape(t, NSA_WIDTH), y.reshape(t, SSM_WIDTH), nw,
                    wcol('z_nsa').astype(BF16), wcol('z_ssm').astype(BF16), wcol('merge_gate').astype(BF16),
                    w_out_nsa.astype(BF16), w_out_ssm.astype(BF16), w_out.astype(BF16),
                    ssm_norm_w.reshape(1, -1).astype(F32), out_norm_w.reshape(1, -1).astype(F32),
                    256 if t % 256 == 0 else Q_BLOCK)
    return out.reshape(b, s, d)


def kernel(x, norm_w, w_in, cmp_pos_k, cmp_pos_v, cmp_k_w1, cmp_k_b1, cmp_k_w2, cmp_v_w1, cmp_v_b1, cmp_v_w2,
           conv_w, conv_b, dt_bias, a_log, d_skip, ssm_norm_w, w_out_nsa, w_out_ssm, w_out, rel_bias,
           final_norm_w):
    depth = norm_w.shape[0]
    assert depth == 1, "the epilogue fuses the final norm into the single layer"
    return _layer(x, norm_w[0], w_in[0], cmp_pos_k[0], cmp_pos_v[0], cmp_k_w1[0], cmp_k_b1[0], cmp_k_w2[0],
                  cmp_v_w1[0], cmp_v_b1[0], cmp_v_w2[0], conv_w[0], conv_b[0], dt_bias[0], a_log[0],
                  d_skip[0], ssm_norm_w[0], w_out_nsa[0], w_out_ssm[0], w_out[0], rel_bias, final_norm_w)
```

```python
import functools
import math

import numpy as np
import jax
import jax.numpy as jnp
from jax import lax
from jax.experimental import pallas as pl
from jax.experimental.pallas import tpu as pltpu

F32 = jnp.float32
BF16 = jnp.bfloat16

D_MODEL = 1024
NSA_HEADS = 16
NSA_GROUPS = 4
NSA_HPG = NSA_HEADS // NSA_GROUPS
HEAD_DIM = 64
NSA_WIDTH = NSA_HEADS * HEAD_DIM
KV_WIDTH = NSA_GROUPS * HEAD_DIM
CMP_BLOCK = 32
CMP_STRIDE = 16
CMP_HIDDEN = 4 * HEAD_DIM
SEL_BLOCK = 64
SEL_TOPN = 8
WINDOW = 512
Q_BLOCK = 128
FORCE_SCORE = 1.0e4
REL_BUCKETS = 32
SSM_WIDTH = 2 * D_MODEL
SSM_HEAD_DIM = 64
SSM_HEADS = SSM_WIDTH // SSM_HEAD_DIM
SSM_GROUPS = 4
SSM_HPG = SSM_HEADS // SSM_GROUPS
SSM_STATE = 128
CONV_WIDTH = 4
SSM_CHUNK = 128
CONV_DIM = SSM_WIDTH + 2 * SSM_GROUPS * SSM_STATE
NORM_EPS = 1e-6

NEG = -1e30
LOG2E = math.log2(math.e)
LANES = 128
QCOLS = NSA_HEADS * Q_BLOCK
GCOLS = NSA_HPG * Q_BLOCK
AUG_DIM = 2 * HEAD_DIM
N_NEAR = 8
TILE_FAR = N_NEAR
TILE_WIN_OLD = N_NEAR + 1
TILE_MASKED = N_NEAR + 2
N_TILES = N_NEAR + 3
MASKED_BUCKET = REL_BUCKETS
SUM_ROWS = 16
MAX_EXP2_EXCESS = 100.0
STEP_TILES = 4
STEP = STEP_TILES * LANES
WIN_TILES = WINDOW // LANES + 1
SSD_CHUNKS_PER_STEP = 2
VMEM_LIMIT = 56 * 1024 * 1024


def _column_offsets():
    sizes = (('q', NSA_WIDTH), ('k_cmp', KV_WIDTH), ('v_cmp', KV_WIDTH), ('k_slc', KV_WIDTH),
             ('v_slc', KV_WIDTH), ('k_swa', KV_WIDTH), ('v_swa', KV_WIDTH), ('nsa_gate', 3 * NSA_HEADS),
             ('z_nsa', NSA_WIDTH), ('z_ssm', SSM_WIDTH), ('xbc', CONV_DIM), ('dt', SSM_HEADS),
             ('merge_gate', 2 * D_MODEL))
    out, lo = {}, 0
    for name, n in sizes:
        out[name] = (lo, lo + n)
        lo += n
    return out


def _t5_bucket_np(dist):
    dist = np.asarray(dist, dtype=np.int64)
    d = np.maximum(dist, 0)
    max_exact = REL_BUCKETS // 2
    large = np.full(d.shape, max_exact, dtype=np.int64)
    d8 = d.astype(object) ** 8
    for k in range(1, REL_BUCKETS - max_exact):
        large = large + (d8 >= 2 ** (32 + 3 * k)).astype(np.int64)
    bucket = np.where(d < max_exact, d, np.minimum(large, REL_BUCKETS - 1))
    return np.where(dist < 0, MASKED_BUCKET, bucket).astype(np.int32)


def _sigmoid(x):
    return 1.0 / (1.0 + jnp.exp(-x))


def _silu(x):
    return x * _sigmoid(x)


def _softplus(x):
    return jnp.maximum(x, 0.0) + jnp.log1p(jnp.exp(-jnp.abs(x)))


def _dot(a, b):
    return jnp.dot(a, b, preferred_element_type=F32)


def _dot_nt(a, b):
    return lax.dot_general(a, b, (((1,), (1,)), ((), ())), preferred_element_type=F32)


def _split3(x):
    hi = x.astype(BF16)
    r1 = x - hi.astype(F32)
    mid = r1.astype(BF16)
    lo = (r1 - mid.astype(F32)).astype(BF16)
    return hi, mid, lo


def _dot_exact_lhs(x, w01):
    hi, mid, lo = _split3(x)
    return _dot(hi, w01) + _dot(mid, w01) + _dot(lo, w01)


def _dot_exact_rhs(w01, x):
    hi, mid, lo = _split3(x)
    return _dot(w01, hi) + _dot(w01, mid) + _dot(w01, lo)


_NAT_OUTS = (('k_slc', NSA_GROUPS * AUG_DIM, BF16), ('k_swa', KV_WIDTH, BF16), ('kv_cmp', 2 * KV_WIDTH, F32),
             ('xbc', CONV_DIM, F32), ('dt', SSM_HEADS, F32))
_TR_OUTS = (('q', NSA_WIDTH, BF16), ('v_slc', KV_WIDTH, BF16), ('v_swa', KV_WIDTH, BF16),
            ('gate', 3 * NSA_HEADS, F32))


def _proj_kernel(x_ref, nw_ref, wn_ref, wt_ref, onehot_ref, *out_refs, n_sub):
    x = x_ref[...]
    ms = jnp.mean(x * x, axis=-1, keepdims=True)
    xn = (x * lax.rsqrt(ms + NORM_EPS) * nw_ref[...]).astype(BF16)
    lo = 0
    for (name, n, _), o_ref in zip(_NAT_OUTS, out_refs[:len(_NAT_OUTS)]):
        res = _dot(xn, wn_ref[:, lo:lo + n])
        if name == 'k_slc':
            res = res + onehot_ref[...]
        if name == 'kv_cmp':
            for j in range(n // LANES):
                o_ref[j] = res[:, j * LANES:(j + 1) * LANES].astype(o_ref.dtype)
        else:
            o_ref[...] = res.astype(o_ref.dtype)
        lo += n
    lo = 0
    for (_, n, _), o_ref in zip(_TR_OUTS, out_refs[len(_NAT_OUTS):]):
        res = _dot_nt(wt_ref[lo:lo + n, :], xn)
        for s in range(n_sub):
            o_ref[s] = res[:, s * LANES:(s + 1) * LANES].astype(o_ref.dtype)
        lo += n


def _proj(x2, nw, wn, wt, onehot, tm):
    t, d = x2.shape
    assert t % tm == 0 and tm % LANES == 0 and onehot.shape[1] == tm
    n_sub = tm // LANES
    n_pat = onehot.shape[0]
    const = lambda arr: pl.BlockSpec(arr.shape, lambda i: (0,) * arr.ndim, pipeline_mode=pl.Buffered(1))
    slabbed = lambda name: name == 'kv_cmp'
    out_specs = [pl.BlockSpec((n // LANES, tm, LANES), lambda i: (0, i, 0)) if slabbed(name)
                 else pl.BlockSpec((tm, n), lambda i: (i, 0)) for name, n, _ in _NAT_OUTS] + \
                [pl.BlockSpec((n_sub, n, LANES), lambda i: (i, 0, 0)) for _, n, _ in _TR_OUTS]
    out_shape = [jax.ShapeDtypeStruct((n // LANES, t, LANES) if slabbed(name) else (t, n), dt)
                 for name, n, dt in _NAT_OUTS] + \
                [jax.ShapeDtypeStruct((t // LANES, n, LANES), dt) for _, n, dt in _TR_OUTS]
    return pl.pallas_call(
        functools.partial(_proj_kernel, n_sub=n_sub),
        grid=(t // tm,),
        in_specs=[pl.BlockSpec((tm, d), lambda i: (i, 0)), const(nw), const(wn), const(wt),
                  pl.BlockSpec((None, tm, onehot.shape[2]), lambda i: (i % n_pat, 0, 0))],
        out_specs=out_specs,
        out_shape=out_shape,
        compiler_params=pltpu.CompilerParams(dimension_semantics=("parallel",),
                                             vmem_limit_bytes=VMEM_LIMIT),
        name="proj",
    )(x2, nw, wn, wt, onehot)


def _compress_kernel(raw_ref, posk_ref, posv_ref, w1k_ref, b1k_ref, w2k_ref,
                     w1v_ref, b1v_ref, w2vt_ref, kc_ref, vct_ref):
    nseg = kc_ref.shape[0]
    n_slab = raw_ref.shape[0]
    per_kv = n_slab // 2
    for j in range(n_slab):
        is_k = j < per_kv
        pos_ref, w1_ref, b1_ref = (posk_ref, w1k_ref, b1k_ref) if is_k else (posv_ref, w1v_ref, b1v_ref)
        first = jnp.zeros((nseg, 2 * CMP_HIDDEN), F32)
        second = jnp.zeros((nseg, 2 * CMP_HIDDEN), F32)
        for p in range(CMP_STRIDE):
            rows = raw_ref[j, pl.ds(p, nseg, stride=CMP_STRIDE), :]
            first = first + _dot((rows + pos_ref[p:p + 1, :]).astype(BF16), w1_ref[p])
            q = CMP_STRIDE + p
            second = second + _dot((rows + pos_ref[q:q + 1, :]).astype(BF16), w1_ref[q])
        hid = _silu(first + pltpu.roll(second, nseg - 1, 0) + b1_ref[...]).astype(BF16)
        for gi in range(2):
            g = 2 * (j % per_kv) + gi
            hg = hid[:, gi * CMP_HIDDEN:(gi + 1) * CMP_HIDDEN]
            if is_k:
                kc_ref[:, g * HEAD_DIM:(g + 1) * HEAD_DIM] = _dot(hg, w2k_ref[...]).astype(kc_ref.dtype)
            else:
                vct_ref[g * HEAD_DIM:(g + 1) * HEAD_DIM, :] = _dot_nt(w2vt_ref[...], hg).astype(vct_ref.dtype)


def _compress(raw, batch, posk, posv, w1k, b1k, w2k, w1v, b1v, w2vt):
    n_slab, t, lanes = raw.shape
    seq = t // batch
    nseg = seq // CMP_STRIDE
    assert n_slab * lanes == 2 * KV_WIDTH and lanes == 2 * HEAD_DIM
    const = lambda arr: pl.BlockSpec(arr.shape, lambda i: (0,) * arr.ndim, pipeline_mode=pl.Buffered(1))
    return pl.pallas_call(
        _compress_kernel,
        grid=(batch,),
        in_specs=[pl.BlockSpec((n_slab, seq, lanes), lambda i: (0, i, 0)),
                  const(posk), const(posv), const(w1k), const(b1k), const(w2k),
                  const(w1v), const(b1v), const(w2vt)],
        out_specs=[pl.BlockSpec((None, nseg, KV_WIDTH), lambda i: (i, 0, 0)),
                   pl.BlockSpec((None, KV_WIDTH, nseg), lambda i: (i, 0, 0))],
        out_shape=[jax.ShapeDtypeStruct((batch, nseg, KV_WIDTH), BF16),
                   jax.ShapeDtypeStruct((batch, KV_WIDTH, nseg), BF16)],
        compiler_params=pltpu.CompilerParams(dimension_semantics=("parallel",),
                                             vmem_limit_bytes=VMEM_LIMIT),
        name="compress",
    )(raw, posk, posv, w1k, b1k, w2k, w1v, b1v, w2vt)


def _nsa_kernel(qt_ref, gt_ref, kc_ref, vct_ref, kslc_ref, vslct_ref, kswa_ref, vswat_ref,
                tsel_ref, tcmp_ref, ovt_ref, far_ref, out_ref, guard_ref,
                qbd_ref, qaug_ref, m_ref, acc_ref, ot_ref, s_ref, p_ref, viol_ref, work_ref, chosen_ref,
                *, n_qb, n_cmp_rows, n_sel, top_n, exact):
    qb = pl.program_id(1)
    q0 = qb * Q_BLOCK

    @pl.when(qb == 0)
    def _():
        qbd_ref[...] = jnp.zeros_like(qbd_ref)
        qaug_ref[...] = jnp.zeros_like(qaug_ref)

    for g in range(NSA_GROUPS):
        for r in range(NSA_HPG):
            h = g * NSA_HPG + r
            q_head = qt_ref[h * HEAD_DIM:(h + 1) * HEAD_DIM, :]
            qbd_ref[g * HEAD_DIM:(g + 1) * HEAD_DIM, h * Q_BLOCK:(h + 1) * Q_BLOCK] = q_head
            qaug_ref[g, 0:HEAD_DIM, r * Q_BLOCK:(r + 1) * Q_BLOCK] = q_head

    def gslice(g):
        return slice(g * GCOLS, (g + 1) * GCOLS)

    def values_with_ones(vt_tiles, g):
        vt = jnp.concatenate([t[g * HEAD_DIM:(g + 1) * HEAD_DIM, :] for t in vt_tiles], axis=1)
        return jnp.concatenate([vt, jnp.ones((SUM_ROWS, vt.shape[1]), BF16)], axis=0)

    def normalized(a):
        return a[0:HEAD_DIM, :] * (1.0 / a[HEAD_DIM:HEAD_DIM + 1, :])

    def self_logit(kt):
        cols = []
        for h in range(NSA_HEADS):
            g = h // NSA_HPG
            prod = qt_ref[h * HEAD_DIM:(h + 1) * HEAD_DIM, :].astype(F32) * kt[g * HEAD_DIM:(g + 1) * HEAD_DIM, :]
            cols.append(jnp.sum(prod, axis=0, keepdims=True))
        return jnp.concatenate(cols, axis=1) - far_ref[...]

    diag_rows = pl.ds(pl.multiple_of(q0, Q_BLOCK), Q_BLOCK)

    u0 = pl.multiple_of(8 * (n_qb - 1 - qb), 8)
    j_iota = lax.broadcasted_iota(jnp.int32, (n_sel, Q_BLOCK), 0)
    j_f32 = j_iota.astype(F32)
    t_pos = q0 + lax.broadcasted_iota(jnp.int32, (n_sel, Q_BLOCK), 1)
    cur = t_pos // SEL_BLOCK
    valid = j_iota <= cur
    forced = (j_iota == 0) | (j_iota == cur) | (j_iota == cur - 1)

    logits = [_dot(kc_ref[...], qbd_ref[:, gslice(g)]) for g in range(NSA_GROUPS)]
    for g in range(NSA_GROUPS):
        s = logits[g] + tcmp_ref[pl.ds(u0, n_cmp_rows), gslice(g)]
        m = jnp.maximum(jnp.max(s, axis=0, keepdims=True), 0.1 * NEG)
        p_ref[0:n_cmp_rows, gslice(g)] = jnp.exp2(s - m).astype(BF16)
    o_cmp, imps = [], []
    for g in range(NSA_GROUPS):
        e = p_ref[0:n_cmp_rows, gslice(g)]
        raw = _dot(values_with_ones([vct_ref[...]], g), e)
        total = raw[HEAD_DIM:HEAD_DIM + 1, :]
        inv = 1.0 / jnp.where(total == 0.0, 1.0, total)
        o_cmp.append(raw[0:HEAD_DIM, :] * inv)
        imp_heads = _dot(ovt_ref[...], e) * inv
        imp = imp_heads[:, 0:Q_BLOCK]
        for r in range(1, NSA_HPG):
            imp = imp + imp_heads[:, r * Q_BLOCK:(r + 1) * Q_BLOCK]
        imps.append(imp)

    def window(exact):
        first_tile = jnp.maximum(qb - (WIN_TILES - 1), 0)
        kw = kswa_ref[pl.ds(pl.multiple_of(first_tile * LANES, LANES), WIN_TILES * LANES), :]
        win_vt_tiles = [vswat_ref[first_tile + t] for t in range(WIN_TILES)]
        win_ids = []
        for t in range(WIN_TILES):
            i = qb - (first_tile + t)
            win_ids.append(jnp.where(i < 0, TILE_MASKED, jnp.where(i == WIN_TILES - 1, TILE_WIN_OLD, i)))
        logits = [_dot(kw, qbd_ref[:, gslice(g)]) for g in range(NSA_GROUPS)]
        if not exact:
            m_fix = self_logit(kswa_ref[diag_rows, :].astype(F32).T)
        for g in range(NSA_GROUPS):
            m8 = None
            for t in range(WIN_TILES):
                rows = slice(t * LANES, (t + 1) * LANES)
                v = logits[g][rows, :] + tsel_ref[win_ids[t], :, gslice(g)]
                if exact:
                    s_ref[rows, gslice(g)] = v
                    vm = jnp.max(v.reshape(LANES // 8, 8, GCOLS), axis=0)
                else:
                    pt = jnp.exp2(v - m_fix[:, gslice(g)]).astype(BF16)
                    p_ref[rows, gslice(g)] = pt
                    vm = jnp.max(pt.reshape(LANES // 16, 16, GCOLS), axis=0)
                m8 = vm if m8 is None else jnp.maximum(m8, vm)
            if exact:
                m = jnp.max(m8, axis=0, keepdims=True)
                p_ref[0:WIN_TILES * LANES, gslice(g)] = jnp.exp2(s_ref[:, gslice(g)] - m).astype(BF16)
            else:
                viol_ref[:, gslice(g)] = jnp.maximum(viol_ref[:, gslice(g)],
                                                     jnp.max(m8.astype(F32), axis=0, keepdims=True))
        for g in range(NSA_GROUPS):
            acc_ref[g] = _dot(values_with_ones(win_vt_tiles, g), p_ref[0:WIN_TILES * LANES, gslice(g)])

    quota = top_n - (1 + (cur >= 1).astype(jnp.int32) + (cur >= 2).astype(jnp.int32))
    for g in range(NSA_GROUPS):
        work_ref[g] = jnp.where(forced, -2.0, jnp.where(valid, imps[g], -1.0))
        chosen_ref[g] = jnp.where(forced, 1.0, 0.0)

    def pick_round(index, limited):
        for g in range(NSA_GROUPS):
            work = work_ref[g]
            best = jnp.max(work, axis=0, keepdims=True)
            first = jnp.min(jnp.where(work == best, j_f32, float(n_sel)), axis=0, keepdims=True)
            hit = j_f32 == first
            if limited:
                hit = hit & (quota > index)
            chosen_ref[g] = jnp.where(hit, 1.0, chosen_ref[g])
            work_ref[g] = jnp.where(hit, -2.0, work)

    max_forced = 3
    for index in range(top_n - max_forced):
        pick_round(index, False)

    @pl.when(q0 < (max_forced - 1) * SEL_BLOCK)
    def _():
        for index in range(top_n - max_forced, top_n - 1):
            pick_round(index, True)

    for g in range(NSA_GROUPS):
        addm = jnp.where((chosen_ref[g] > 0.5) & valid, 0.0, NEG).astype(BF16)
        qaug_ref[g, HEAD_DIM:HEAD_DIM + n_sel, :] = jnp.concatenate([addm] * NSA_HPG, axis=1)

    def sel_logits(tile0, n_tiles):
        key_rows = pl.ds(pl.multiple_of(tile0 * LANES, LANES), n_tiles * LANES)
        return [_dot(kslc_ref[key_rows, g * AUG_DIM:(g + 1) * AUG_DIM], qaug_ref[g])
                for g in range(NSA_GROUPS)]

    def sel_update(tile0, n_tiles, near, exact, logits, p_row0):
        vt_tiles = [vslct_ref[tile0 + t] for t in range(n_tiles)]
        if near:
            tile_ids = []
            for t in range(n_tiles):
                i = qb - (tile0 + t)
                tile_ids.append(jnp.where(i < 0, TILE_MASKED, jnp.minimum(i, TILE_FAR)))
        alphas = []
        for g in range(NSA_GROUPS):
            m_old = m_ref[:, gslice(g)]
            m8 = None
            for t in range(n_tiles):
                rows = slice(t * LANES, (t + 1) * LANES)
                v = logits[g][rows, :]
                if near:
                    v = v + tsel_ref[tile_ids[t], :, gslice(g)]
                if exact:
                    s_ref[rows, gslice(g)] = v
                    vm = jnp.max(v.reshape(LANES // 8, 8, GCOLS), axis=0)
                else:
                    pt = jnp.exp2(v - m_old).astype(BF16)
                    p_ref[p_row0 + t * LANES:p_row0 + (t + 1) * LANES, gslice(g)] = pt
                    vm = jnp.max(pt.reshape(LANES // 16, 16, GCOLS), axis=0)
                m8 = vm if m8 is None else jnp.maximum(m8, vm)
            if exact:
                m_new = jnp.maximum(m_old, jnp.max(m8, axis=0, keepdims=True))
                alphas.append(jnp.exp2(m_old - m_new))
                m_ref[:, gslice(g)] = m_new
                for t in range(n_tiles):
                    rows = slice(t * LANES, (t + 1) * LANES)
                    p_ref[p_row0 + t * LANES:p_row0 + (t + 1) * LANES, gslice(g)] = \
                        jnp.exp2(s_ref[rows, gslice(g)] - m_new).astype(BF16)
            else:
                viol_ref[:, gslice(g)] = jnp.maximum(viol_ref[:, gslice(g)],
                                                     jnp.max(m8.astype(F32), axis=0, keepdims=True))
        for g in range(NSA_GROUPS):
            pv = _dot(values_with_ones(vt_tiles, g), p_ref[p_row0:p_row0 + n_tiles * LANES, gslice(g)])
            if exact:
                acc_ref[g] = acc_ref[g] * alphas[g] + pv
            else:
                acc_ref[g] = acc_ref[g] + pv

    def sel_step(tile0, n_tiles, near, exact):
        sel_update(tile0, n_tiles, near, exact, sel_logits(tile0, n_tiles), 0)

    def sel_two_steps(tile_a, tile_b, near):
        la, lb = sel_logits(tile_a, STEP_TILES), sel_logits(tile_b, STEP_TILES)
        sel_update(tile_a, STEP_TILES, near, False, la, 0)
        sel_update(tile_b, STEP_TILES, near, False, lb, STEP)

    r_diag = qb // STEP_TILES
    n_far = jnp.maximum((qb - (N_NEAR - 1)) // STEP_TILES, 0)
    half = STEP_TILES // 2

    def selected(exact):
        if exact:
            m_ref[...] = jnp.full(m_ref.shape, NEG, F32)
        else:
            kt = kslc_ref[diag_rows, :].astype(F32).T
            m_ref[...] = self_logit(jnp.concatenate(
                [kt[g * AUG_DIM:g * AUG_DIM + HEAD_DIM, :] for g in range(NSA_GROUPS)], axis=0))
        acc_ref[...] = jnp.zeros_like(acc_ref)

        @pl.when(qb % STEP_TILES < half)
        def _():
            sel_step(STEP_TILES * r_diag, half, True, exact)

        @pl.when(qb % STEP_TILES >= half)
        def _():
            sel_step(STEP_TILES * r_diag, STEP_TILES, True, exact)

        def near_body(i, carry):
            sel_step(STEP_TILES * (r_diag - i), STEP_TILES, True, exact)
            return carry

        def far_body(r, carry):
            sel_step(STEP_TILES * r, STEP_TILES, False, exact)
            return carry

        def far_pair_body(k, carry):
            sel_two_steps(STEP_TILES * 2 * k, STEP_TILES * (2 * k + 1), False)
            return carry

        lax.fori_loop(1, r_diag - n_far + 1, near_body, 0)
        if exact:
            lax.fori_loop(0, n_far, far_body, 0)
        else:
            @pl.when(n_far % 2 == 1)
            def _():
                sel_step(STEP_TILES * (n_far - 1), STEP_TILES, False, False)

            lax.fori_loop(0, n_far // 2, far_pair_body, 0)

    viol_ref[...] = jnp.zeros_like(viol_ref)
    selected(exact)
    o_slc = [normalized(acc_ref[g]) for g in range(NSA_GROUPS)]
    window(exact)
    o_swa = [normalized(acc_ref[g]) for g in range(NSA_GROUPS)]
    guard_ref[...] = jnp.broadcast_to(jnp.max(viol_ref[...], axis=1, keepdims=True), guard_ref.shape)

    gates = _sigmoid(gt_ref[...])
    for g in range(NSA_GROUPS):
        def gate_row(c):
            return jnp.concatenate(
                [gates[c * NSA_HEADS + g * NSA_HPG + r:c * NSA_HEADS + g * NSA_HPG + r + 1, :]
                 for r in range(NSA_HPG)], axis=1)
        og = gate_row(0) * o_cmp[g] + gate_row(1) * o_slc[g] + gate_row(2) * o_swa[g]
        for r in range(NSA_HPG):
            h = g * NSA_HPG + r
            ot_ref[h * HEAD_DIM:(h + 1) * HEAD_DIM, :] = og[:, r * Q_BLOCK:(r + 1) * Q_BLOCK]
    out_ref[...] = ot_ref[...].T


def _nsa(qt, gt, kc, vct, kslc, vslct, kswa, vswat, tsel, tcmp, ovt, far, seq, exact):
    b = kc.shape[0]
    n_qb = seq // Q_BLOCK
    assert n_qb % STEP_TILES == 0 and n_qb >= WIN_TILES and WIN_TILES >= STEP_TILES
    n_cmp_rows = kc.shape[1]
    n_sel = seq // SEL_BLOCK
    assert n_sel <= AUG_DIM - HEAD_DIM
    top_n = min(SEL_TOPN, n_sel)
    assert top_n > 3
    n_gate = gt.shape[2]
    const = lambda arr: pl.BlockSpec(arr.shape, lambda i, j: (0,) * arr.ndim,
                                     pipeline_mode=pl.Buffered(1))
    kern = functools.partial(_nsa_kernel, n_qb=n_qb, n_cmp_rows=n_cmp_rows, n_sel=n_sel, top_n=top_n,
                             exact=exact)
    return pl.pallas_call(
        kern,
        grid=(b, n_qb),
        in_specs=[pl.BlockSpec((None, None, NSA_WIDTH, Q_BLOCK), lambda i, j: (i, j, 0, 0)),
                  pl.BlockSpec((None, None, n_gate, Q_BLOCK), lambda i, j: (i, j, 0, 0)),
                  pl.BlockSpec((None, n_cmp_rows, KV_WIDTH), lambda i, j: (i, 0, 0)),
                  pl.BlockSpec((None, KV_WIDTH, n_cmp_rows), lambda i, j: (i, 0, 0)),
                  pl.BlockSpec((None, seq, NSA_GROUPS * AUG_DIM), lambda i, j: (i, 0, 0)),
                  pl.BlockSpec((None, n_qb, KV_WIDTH, Q_BLOCK), lambda i, j: (i, 0, 0, 0)),
                  pl.BlockSpec((None, seq, KV_WIDTH), lambda i, j: (i, 0, 0)),
                  pl.BlockSpec((None, n_qb, KV_WIDTH, Q_BLOCK), lambda i, j: (i, 0, 0, 0)),
                  const(tsel), const(tcmp), const(ovt), const(far)],
        out_specs=[pl.BlockSpec((None, Q_BLOCK, NSA_WIDTH), lambda i, j: (i, j, 0)),
                   pl.BlockSpec((None, None, 8, LANES), lambda i, j: (i, j, 0, 0))],
        out_shape=[jax.ShapeDtypeStruct((b, seq, NSA_WIDTH), F32),
                   jax.ShapeDtypeStruct((b, n_qb, 8, LANES), F32)],
        scratch_shapes=[pltpu.VMEM((KV_WIDTH, QCOLS), BF16),
                        pltpu.VMEM((NSA_GROUPS, AUG_DIM, GCOLS), BF16),
                        pltpu.VMEM((1, QCOLS), F32),
                        pltpu.VMEM((NSA_GROUPS, HEAD_DIM + SUM_ROWS, GCOLS), F32),
                        pltpu.VMEM((NSA_WIDTH, Q_BLOCK), F32),
                        pltpu.VMEM((WIN_TILES * LANES, QCOLS), F32),
                        pltpu.VMEM((max(WIN_TILES * LANES, 2 * STEP), QCOLS), BF16),
                        pltpu.VMEM((1, QCOLS), F32),
                        pltpu.VMEM((NSA_GROUPS, n_sel, Q_BLOCK), F32),
                        pltpu.VMEM((NSA_GROUPS, n_sel, Q_BLOCK), F32)],
        compiler_params=pltpu.CompilerParams(dimension_semantics=("parallel", "arbitrary"),
                                             vmem_limit_bytes=VMEM_LIMIT),
        name="nsa_exact" if exact else "nsa",
    )(qt, gt, kc, vct, kslc, vslct, kswa, vswat, tsel, tcmp, ovt, far)


def _ssd_kernel(xbc_ref, dt_ref, convw_ref, convb_ref, dtb_ref, a_ref, dskip_ref, expand_ref, y_ref,
                xp_ref, tail_ref, state_ref, slab_ref):
    @pl.when(pl.program_id(1) == 0)
    def _():
        tail_ref[...] = jnp.zeros_like(tail_ref)
        state_ref[...] = jnp.zeros_like(state_ref)

    n_chunks = xbc_ref.shape[0] // SSM_CHUNK
    fronts = [_ssd_front(ci, xbc_ref, dt_ref, convw_ref, convb_ref, dtb_ref, a_ref, expand_ref,
                         xp_ref.at[ci], tail_ref, slab_ref.at[ci]) for ci in range(n_chunks)]
    for ci in range(n_chunks):
        _ssd_back(ci, fronts[ci], dskip_ref, y_ref, state_ref, slab_ref.at[ci])


def _ssd_front(ci, xbc_ref, dt_ref, convw_ref, convb_ref, dtb_ref, a_ref, expand_ref, xp_ref, tail_ref, slab_ref):
    L = SSM_CHUNK
    NV = L // 8
    halo = CONV_WIDTH - 1
    tok = slice(ci * L, (ci + 1) * L)

    n_slab = CONV_DIM // LANES
    for s in range(n_slab):
        slab_ref[s] = xbc_ref[tok, s * LANES:(s + 1) * LANES]
    slab_ref[n_slab] = jnp.concatenate([dt_ref[tok, :], jnp.zeros((L, LANES - SSM_HEADS), F32)], axis=1)
    for v in range(NV):
        for s in range(n_slab):
            xp_ref[(halo + v) * 8:(halo + v + 1) * 8, s * LANES:(s + 1) * LANES] = \
                slab_ref[s, pl.ds(v, 8, stride=NV), :]
    sub = lax.broadcasted_iota(jnp.int32, (8, CONV_DIM), 0)
    for w in range(halo):
        cur = xp_ref[(NV + w) * 8:(NV + w + 1) * 8, :]
        prev = tail_ref[w * 8:(w + 1) * 8, :]
        xp_ref[w * 8:(w + 1) * 8, :] = pltpu.roll(jnp.where(sub == 7, prev, cur), 1, 0)
    tail_ref[...] = xp_ref[NV * 8:(NV + halo) * 8, :]
    conv = convb_ref[...]
    for k in range(CONV_WIDTH):
        conv = conv + convw_ref[k:k + 1, :] * xp_ref[8 * k:8 * k + L, :]
    xbc = _silu(conv)
    xs = xbc[:, 0:SSM_WIDTH]
    bm = xbc[:, SSM_WIDTH:SSM_WIDTH + SSM_GROUPS * SSM_STATE]
    cm = xbc[:, SSM_WIDTH + SSM_GROUPS * SSM_STATE:]

    dt_raw = jnp.concatenate([slab_ref[n_slab, pl.ds(v, 8, stride=NV), :] for v in range(NV)], axis=0)
    dt = _softplus(dt_raw[:, 0:SSM_HEADS] + dtb_ref[...])
    a_row = -jnp.exp(a_ref[...])
    row_i = lax.broadcasted_iota(jnp.int32, (L, L), 0)
    col_j = lax.broadcasted_iota(jnp.int32, (L, L), 1)
    token = lambda r: (r % 8) * NV + r // 8
    tril = token(row_i) >= token(col_j)
    tril01 = jnp.where(tril, 1.0, 0.0).astype(BF16)
    cs = _dot_exact_rhs(tril01, dt * a_row)
    cst = jnp.concatenate([cs, jnp.zeros((L, L - SSM_HEADS), F32)], axis=1).T

    expand = expand_ref[...]
    cs_x = _dot_exact_lhs(cs, expand)
    dt_x = _dot_exact_lhs(dt, expand)
    total_x = cs_x[L - 1:L, :]
    xdt = (xs * dt_x).astype(BF16)
    xw = (xs * (jnp.exp(total_x - cs_x) * dt_x)).astype(BF16)
    decay_out = jnp.exp(cs_x)
    decay_state = jnp.exp(total_x)
    return xs, bm, cm, cs, cst, tril, xdt, xw, decay_out, decay_state


def _ssd_back(ci, front, dskip_ref, y_ref, state_ref, slab_ref):
    xs, bm, cm, cs, cst, tril, xdt, xw, decay_out, decay_state = front
    L = SSM_CHUNK
    NV = L // 8
    tok = slice(ci * L, (ci + 1) * L)
    lane = lax.broadcasted_iota(jnp.int32, (L, LANES), 1)
    gw = SSM_HPG * SSM_HEAD_DIM
    for g in range(SSM_GROUPS):
        bg = bm[:, g * SSM_STATE:(g + 1) * SSM_STATE]
        cg = cm[:, g * SSM_STATE:(g + 1) * SSM_STATE].astype(BF16)
        cb = _dot_nt(cg, bg.astype(BF16))
        st = state_ref[g]
        y_g = _dot(cg, st.astype(BF16)) * decay_out[:, g * gw:(g + 1) * gw]
        state_ref[g] = st * decay_state[:, g * gw:(g + 1) * gw] + \
            _dot(bg.T.astype(BF16), xw[:, g * gw:(g + 1) * gw])
        pieces = []
        for pair in range(SSM_HPG // 2):
            ws = []
            for hh in range(2):
                h = g * SSM_HPG + 2 * pair + hh
                diff = cs[:, h:h + 1] - cst[h:h + 1, :]
                decay = jnp.where(tril, jnp.exp(jnp.where(tril, diff, 0.0)), 0.0)
                ws.append((cb * decay).astype(BF16))
            w_pair = jnp.concatenate(ws, axis=1)
            lo = g * gw + pair * LANES
            slab = xdt[:, lo:lo + LANES]
            zero = jnp.zeros_like(slab)
            x_bd = jnp.concatenate([jnp.where(lane < SSM_HEAD_DIM, slab, zero),
                                    jnp.where(lane >= SSM_HEAD_DIM, slab, zero)], axis=0)
            pieces.append(_dot(w_pair, x_bd))
        y_g = y_g + jnp.concatenate(pieces, axis=1) + \
            dskip_ref[:, g * gw:(g + 1) * gw] * xs[:, g * gw:(g + 1) * gw]
        for s in range(gw // LANES):
            for v in range(NV):
                slab_ref[s, pl.ds(v, 8, stride=NV), :] = y_g[v * 8:(v + 1) * 8, s * LANES:(s + 1) * LANES]
        for s in range(gw // LANES):
            y_ref[tok, g * gw + s * LANES:g * gw + (s + 1) * LANES] = slab_ref[s]


def _ssd(xbc, dt, convw, convb, dtb, a_log, dskip_x, expand, seq):
    b = xbc.shape[0]
    rows = SSD_CHUNKS_PER_STEP * SSM_CHUNK
    assert seq % rows == 0
    halo_rows = (CONV_WIDTH - 1) * 8
    const = lambda arr: pl.BlockSpec(arr.shape, lambda i, j: (0,) * arr.ndim)
    return pl.pallas_call(
        _ssd_kernel,
        grid=(b, seq // rows),
        in_specs=[pl.BlockSpec((None, rows, CONV_DIM), lambda i, j: (i, j, 0)),
                  pl.BlockSpec((None, rows, SSM_HEADS), lambda i, j: (i, j, 0)),
                  const(convw), const(convb), const(dtb), const(a_log), const(dskip_x), const(expand)],
        out_specs=pl.BlockSpec((None, rows, SSM_WIDTH), lambda i, j: (i, j, 0)),
        out_shape=jax.ShapeDtypeStruct((b, seq, SSM_WIDTH), F32),
        scratch_shapes=[pltpu.VMEM((SSD_CHUNKS_PER_STEP, SSM_CHUNK + halo_rows, CONV_DIM), F32),
                        pltpu.VMEM((halo_rows, CONV_DIM), F32),
                        pltpu.VMEM((SSM_GROUPS, SSM_STATE, SSM_HPG * SSM_HEAD_DIM), F32),
                        pltpu.VMEM((SSD_CHUNKS_PER_STEP, CONV_DIM // LANES + 1, SSM_CHUNK, LANES), F32)],
        compiler_params=pltpu.CompilerParams(dimension_semantics=("parallel", "arbitrary"),
                                             vmem_limit_bytes=VMEM_LIMIT),
        name="ssd",
    )(xbc, dt, convw, convb, dtb, a_log, dskip_x, expand)


def _epilogue_kernel(x_ref, o_ref, y_ref, nw_ref, wz1_ref, wz2_ref, wmg_ref, won_ref, wos_ref, wo_ref,
                     snw_ref, fnw_ref, out_ref):
    x = x_ref[...]
    ms = jnp.mean(x * x, axis=-1, keepdims=True)
    xn = (x * lax.rsqrt(ms + NORM_EPS) * nw_ref[...]).astype(BF16)
    u = (o_ref[...] * _silu(_dot(xn, wz1_ref[...]))).astype(BF16)
    h_nsa = _dot(u, won_ref[...])
    hh = y_ref[...] * _silu(_dot(xn, wz2_ref[...]))
    gw = SSM_WIDTH // SSM_GROUPS
    parts = []
    for g in range(SSM_GROUPS):
        hg = hh[:, g * gw:(g + 1) * gw]
        hg = hg * lax.rsqrt(jnp.mean(hg * hg, axis=-1, keepdims=True) + NORM_EPS)
        parts.append((hg * snw_ref[:, g * gw:(g + 1) * gw]).astype(BF16))
    h_ssm = _dot(jnp.concatenate(parts, axis=1), wos_ref[...])
    gate = _sigmoid(_dot(xn, wmg_ref[...]))
    mix = (gate[:, 0:D_MODEL] * h_nsa + gate[:, D_MODEL:] * h_ssm).astype(BF16)
    r = x + _dot(mix, wo_ref[...])
    ms2 = jnp.mean(r * r, axis=-1, keepdims=True)
    out_ref[...] = r * lax.rsqrt(ms2 + NORM_EPS) * fnw_ref[...]


def _epilogue(x2, o2, y2, nw, wz1, wz2, wmg, won, wos, wo, snw, fnw, tm):
    t, d = x2.shape
    assert t % tm == 0
    const = lambda arr: pl.BlockSpec(arr.shape, lambda i: (0,) * arr.ndim, pipeline_mode=pl.Buffered(1))
    return pl.pallas_call(
        _epilogue_kernel,
        grid=(t // tm,),
        in_specs=[pl.BlockSpec((tm, d), lambda i: (i, 0)),
                  pl.BlockSpec((tm, NSA_WIDTH), lambda i: (i, 0)),
                  pl.BlockSpec((tm, SSM_WIDTH), lambda i: (i, 0)),
                  const(nw), const(wz1), const(wz2), const(wmg), const(won), const(wos), const(wo),
                  const(snw), const(fnw)],
        out_specs=pl.BlockSpec((tm, d), lambda i: (i, 0)),
        out_shape=jax.ShapeDtypeStruct((t, d), F32),
        compiler_params=pltpu.CompilerParams(dimension_semantics=("parallel",),
                                             vmem_limit_bytes=VMEM_LIMIT),
        name="epilogue",
    )(x2, o2, y2, nw, wz1, wz2, wmg, won, wos, wo, snw, fnw)


def _table_kernel(rb_ref, bucket_ref, out_ref):
    bk = bucket_ref[...]
    for h in range(NSA_HEADS):
        acc = jnp.full(bk.shape, NEG, F32)
        for b in range(REL_BUCKETS):
            acc = jnp.where(bk == b, rb_ref[b, h], acc)
        out_ref[:, h * Q_BLOCK:(h + 1) * Q_BLOCK] = acc


def _bias_table(rb, buckets, tr):
    rows = buckets.shape[0]
    assert rows % tr == 0
    return pl.pallas_call(
        _table_kernel,
        grid=(rows // tr,),
        in_specs=[pl.BlockSpec(memory_space=pltpu.SMEM),
                  pl.BlockSpec((tr, Q_BLOCK), lambda i: (i, 0))],
        out_specs=pl.BlockSpec((tr, QCOLS), lambda i: (i, 0)),
        out_shape=jax.ShapeDtypeStruct((rows, QCOLS), F32),
        compiler_params=pltpu.CompilerParams(dimension_semantics=("parallel",)),
        name="bias_table",
    )(rb, buckets)


def _bucket_tables(seq):
    n_qb = seq // Q_BLOCK
    key = np.arange(LANES)[:, None]
    tok = np.arange(Q_BLOCK)[None, :]
    tiles = [_t5_bucket_np(LANES * o + tok - key) for o in range(N_NEAR)]
    tiles.append(np.full((LANES, Q_BLOCK), REL_BUCKETS - 1, np.int32))
    dwin = WINDOW + tok - key
    tiles.append(_t5_bucket_np(np.where(dwin < WINDOW, dwin, -1)))
    tiles.append(np.full((LANES, Q_BLOCK), MASKED_BUCKET, np.int32))
    assert len(tiles) == N_TILES and LANES * N_NEAR - (LANES - 1) >= 790
    n_cmp_rows = seq // CMP_STRIDE
    rows = 8 * (n_qb - 1) + n_cmp_rows
    rows_pad = -(-rows // LANES) * LANES
    u = np.arange(rows_pad)[:, None]
    cmp_tbl = _t5_bucket_np(tok - CMP_STRIDE * u + Q_BLOCK * (n_qb - 1) - (CMP_BLOCK - 1))
    return np.concatenate(tiles, axis=0), cmp_tbl


def _block_onehot(seq, tm):
    pos = np.arange(seq)
    pat = np.zeros((seq, NSA_GROUPS, AUG_DIM), np.float32)
    pat[pos, :, HEAD_DIM + pos // SEL_BLOCK] = 1.0
    return jnp.asarray(pat.reshape(seq // tm, tm, NSA_GROUPS * AUG_DIM), dtype=BF16)


def _overlap_t(seq):
    n_cmp_rows = seq // CMP_STRIDE
    n_sel = seq // SEL_BLOCK
    c_start = np.arange(n_cmp_rows)[None, :] * CMP_STRIDE
    s_start = np.arange(n_sel)[:, None] * SEL_BLOCK
    ov = (c_start < s_start + SEL_BLOCK) & (c_start + CMP_BLOCK > s_start)
    ov[:, n_cmp_rows - 1] = False
    return jnp.asarray(ov, dtype=BF16)


def _layer(x, norm_w, w_in, cmp_pos_k, cmp_pos_v, cmp_k_w1, cmp_k_b1, cmp_k_w2, cmp_v_w1, cmp_v_b1,
           cmp_v_w2, conv_w, conv_b, dt_bias, a_log, d_skip, ssm_norm_w, w_out_nsa, w_out_ssm, w_out,
           rel_bias, out_norm_w):
    b, s, d = x.shape
    assert d == D_MODEL and s % STEP == 0 and s >= 2 * WINDOW
    t = b * s
    n_qb = s // Q_BLOCK
    cols = _column_offsets()
    wcol = lambda name: w_in[:, cols[name][0]:cols[name][1]]
    x2 = x.reshape(t, d)
    nw = norm_w.reshape(1, d).astype(F32)

    gate_w = wcol('nsa_gate').reshape(d, NSA_HEADS, 3).transpose(0, 2, 1).reshape(d, 3 * NSA_HEADS)
    w_kslc = jnp.pad(wcol('k_slc').reshape(d, NSA_GROUPS, HEAD_DIM),
                     ((0, 0), (0, 0), (0, AUG_DIM - HEAD_DIM))).reshape(d, NSA_GROUPS * AUG_DIM)
    wn = jnp.concatenate([w_kslc, wcol('k_swa'), wcol('k_cmp'), wcol('v_cmp'), wcol('xbc'),
                          wcol('dt')], axis=1).astype(BF16)
    wt = jnp.concatenate([wcol('q') * (HEAD_DIM ** -0.5 * LOG2E), wcol('v_slc'), wcol('v_swa'), gate_w],
                         axis=1).T.astype(BF16)
    kslc, kswa, cc, xbc, dt, qt, vslct, vswat, gt = _proj(x2, nw, wn, wt, _block_onehot(s, STEP), STEP)
    kslc = kslc.reshape(b, s, NSA_GROUPS * AUG_DIM)
    kswa = kswa.reshape(b, s, KV_WIDTH)
    xbc = xbc.reshape(b, s, CONV_DIM)
    dt = dt.reshape(b, s, SSM_HEADS)
    qt = qt.reshape(b, n_qb, NSA_WIDTH, Q_BLOCK)
    vslct = vslct.reshape(b, n_qb, KV_WIDTH, Q_BLOCK)
    vswat = vswat.reshape(b, n_qb, KV_WIDTH, Q_BLOCK)
    gt = gt.reshape(b, n_qb, 3 * NSA_HEADS, Q_BLOCK)

    def pair_diag(w1):
        w = w1.reshape(CMP_BLOCK, HEAD_DIM, CMP_HIDDEN)
        z = jnp.zeros_like(w)
        return jnp.concatenate([jnp.concatenate([w, z], axis=2),
                                jnp.concatenate([z, w], axis=2)], axis=1).astype(BF16)

    twice = lambda a: jnp.tile(a.astype(F32), (1, 2))
    kc, vct = _compress(cc, b, twice(cmp_pos_k), twice(cmp_pos_v),
                        pair_diag(cmp_k_w1), twice(cmp_k_b1.reshape(1, -1)), cmp_k_w2.astype(BF16),
                        pair_diag(cmp_v_w1), twice(cmp_v_b1.reshape(1, -1)), cmp_v_w2.T.astype(BF16))

    sel_buckets, cmp_buckets = _bucket_tables(s)
    rb = rel_bias.astype(F32) * LOG2E
    rb_far = rb[REL_BUCKETS - 1:REL_BUCKETS, :]
    tsel = _bias_table(rb - rb_far, jnp.asarray(sel_buckets), LANES).reshape(N_TILES, LANES, QCOLS)
    tcmp = _bias_table(rb, jnp.asarray(cmp_buckets), LANES)
    far = jnp.repeat(rb_far.reshape(-1), Q_BLOCK).reshape(1, QCOLS)
    nsa_args = (qt, gt, kc, vct, kslc, vslct, kswa, vswat, tsel, tcmp, _overlap_t(s), far, s)
    o_nsa, guard = _nsa(*nsa_args, exact=False)
    o_nsa = lax.cond(jnp.logical_not(jnp.max(guard) <= 2.0 ** MAX_EXP2_EXCESS),
                     lambda: _nsa(*nsa_args, exact=True)[0], lambda: o_nsa)

    expand = jnp.asarray(np.kron(np.eye(SSM_HEADS), np.ones((1, SSM_HEAD_DIM))), dtype=BF16)
    y = _ssd(xbc, dt, conv_w.astype(F32), conv_b.reshape(1, -1).astype(F32),
             dt_bias.reshape(1, -1).astype(F32), a_log.reshape(1, -1).astype(F32),
             jnp.repeat(d_skip.astype(F32), SSM_HEAD_DIM).reshape(1, -1), expand, s)

    out = _epilogue(x2, o_nsa.reshape(t, NSA_WIDTH), y.reshape(t, SSM_WIDTH), nw,
                    wcol('z_nsa').astype(BF16), wcol('z_ssm').astype(BF16), wcol('merge_gate').astype(BF16),
                    w_out_nsa.astype(BF16), w_out_ssm.astype(BF16), w_out.astype(BF16),
                    ssm_norm_w.reshape(1, -1).astype(F32), out_norm_w.reshape(1, -1).astype(F32),
                    256 if t % 256 == 0 else Q_BLOCK)
    return out.reshape(b, s, d)


def kernel(x, norm_w, w_in, cmp_pos_k, cmp_pos_v, cmp_k_w1, cmp_k_b1, cmp_k_w2, cmp_v_w1, cmp_v_b1, cmp_v_w2,
           conv_w, conv_b, dt_bias, a_log, d_skip, ssm_norm_w, w_out_nsa, w_out_ssm, w_out, rel_bias,
           final_norm_w):
    depth = norm_w.shape[0]
    assert depth == 1, "the epilogue fuses the final norm into the single layer"
    return _layer(x, norm_w[0], w_in[0], cmp_pos_k[0], cmp_pos_v[0], cmp_k_w1[0], cmp_k_b1[0], cmp_k_w2[0],
                  cmp_v_w1[0], cmp_v_b1[0], cmp_v_w2[0], conv_w[0], conv_b[0], dt_bias[0], a_log[0],
                  d_skip[0], ssm_norm_w[0], w_out_nsa[0], w_out_ssm[0], w_out[0], rel_bias, final_norm_w)
```

```python
import functools
import math

import numpy as np
import jax
import jax.numpy as jnp
from jax import lax
from jax.experimental import pallas as pl
from jax.experimental.pallas import tpu as pltpu

F32 = jnp.float32
BF16 = jnp.bfloat16

D_MODEL = 1024
NSA_HEADS = 16
NSA_GROUPS = 4
NSA_HPG = NSA_HEADS // NSA_GROUPS
HEAD_DIM = 64
NSA_WIDTH = NSA_HEADS * HEAD_DIM
KV_WIDTH = NSA_GROUPS * HEAD_DIM
CMP_BLOCK = 32
CMP_STRIDE = 16
CMP_HIDDEN = 4 * HEAD_DIM
SEL_BLOCK = 64
SEL_TOPN = 8
WINDOW = 512
Q_BLOCK = 128
FORCE_SCORE = 1.0e4
REL_BUCKETS = 32
SSM_WIDTH = 2 * D_MODEL
SSM_HEAD_DIM = 64
SSM_HEADS = SSM_WIDTH // SSM_HEAD_DIM
SSM_GROUPS = 4
SSM_HPG = SSM_HEADS // SSM_GROUPS
SSM_STATE = 128
CONV_WIDTH = 4
SSM_CHUNK = 128
CONV_DIM = SSM_WIDTH + 2 * SSM_GROUPS * SSM_STATE
NORM_EPS = 1e-6

NEG = -1e30
LOG2E = math.log2(math.e)
LANES = 128
QCOLS = NSA_HEADS * Q_BLOCK
GCOLS = NSA_HPG * Q_BLOCK
AUG_DIM = 2 * HEAD_DIM
N_NEAR = 8
TILE_FAR = N_NEAR
TILE_WIN_OLD = N_NEAR + 1
TILE_MASKED = N_NEAR + 2
N_TILES = N_NEAR + 3
MASKED_BUCKET = REL_BUCKETS
SUM_ROWS = 16
MAX_EXP2_EXCESS = 100.0
STEP_TILES = 4
STEP = STEP_TILES * LANES
WIN_TILES = WINDOW // LANES + 1
SSD_CHUNKS_PER_STEP = 2
VMEM_LIMIT = 56 * 1024 * 1024


def _column_offsets():
    sizes = (('q', NSA_WIDTH), ('k_cmp', KV_WIDTH), ('v_cmp', KV_WIDTH), ('k_slc', KV_WIDTH),
             ('v_slc', KV_WIDTH), ('k_swa', KV_WIDTH), ('v_swa', KV_WIDTH), ('nsa_gate', 3 * NSA_HEADS),
             ('z_nsa', NSA_WIDTH), ('z_ssm', SSM_WIDTH), ('xbc', CONV_DIM), ('dt', SSM_HEADS),
             ('merge_gate', 2 * D_MODEL))
    out, lo = {}, 0
    for name, n in sizes:
        out[name] = (lo, lo + n)
        lo += n
    return out


def _t5_bucket_np(dist):
    dist = np.asarray(dist, dtype=np.int64)
    d = np.maximum(dist, 0)
    max_exact = REL_BUCKETS // 2
    large = np.full(d.shape, max_exact, dtype=np.int64)
    d8 = d.astype(object) ** 8
    for k in range(1, REL_BUCKETS - max_exact):
        large = large + (d8 >= 2 ** (32 + 3 * k)).astype(np.int64)
    bucket = np.where(d < max_exact, d, np.minimum(large, REL_BUCKETS - 1))
    return np.where(dist < 0, MASKED_BUCKET, bucket).astype(np.int32)


def _sigmoid(x):
    return 1.0 / (1.0 + jnp.exp(-x))


def _silu(x):
    return x * _sigmoid(x)


def _softplus(x):
    return jnp.maximum(x, 0.0) + jnp.log1p(jnp.exp(-jnp.abs(x)))


def _dot(a, b):
    return jnp.dot(a, b, preferred_element_type=F32)


def _dot_nt(a, b):
    return lax.dot_general(a, b, (((1,), (1,)), ((), ())), preferred_element_type=F32)


def _split3(x):
    hi = x.astype(BF16)
    r1 = x - hi.astype(F32)
    mid = r1.astype(BF16)
    lo = (r1 - mid.astype(F32)).astype(BF16)
    return hi, mid, lo


def _dot_exact_lhs(x, w01):
    hi, mid, lo = _split3(x)
    return _dot(hi, w01) + _dot(mid, w01) + _dot(lo, w01)


def _dot_exact_rhs(w01, x):
    hi, mid, lo = _split3(x)
    return _dot(w01, hi) + _dot(w01, mid) + _dot(w01, lo)


_NAT_OUTS = (('k_slc', NSA_GROUPS * AUG_DIM, BF16), ('k_swa', KV_WIDTH, BF16), ('kv_cmp', 2 * KV_WIDTH, F32),
             ('xbc', CONV_DIM, F32), ('dt', SSM_HEADS, F32))
_TR_OUTS = (('q', NSA_WIDTH, BF16), ('v_slc', KV_WIDTH, BF16), ('v_swa', KV_WIDTH, BF16),
            ('gate', 3 * NSA_HEADS, F32))


def _proj_kernel(x_ref, nw_ref, wn_ref, wt_ref, onehot_ref, *out_refs, n_sub):
    x = x_ref[...]
    ms = jnp.mean(x * x, axis=-1, keepdims=True)
    xn = (x * lax.rsqrt(ms + NORM_EPS) * nw_ref[...]).astype(BF16)
    lo = 0
    for (name, n, _), o_ref in zip(_NAT_OUTS, out_refs[:len(_NAT_OUTS)]):
        res = _dot(xn, wn_ref[:, lo:lo + n])
        if name == 'k_slc':
            res = res + onehot_ref[...]
        if name == 'kv_cmp':
            for j in range(n // LANES):
                o_ref[j] = res[:, j * LANES:(j + 1) * LANES].astype(o_ref.dtype)
        else:
            o_ref[...] = res.astype(o_ref.dtype)
        lo += n
    lo = 0
    for (_, n, _), o_ref in zip(_TR_OUTS, out_refs[len(_NAT_OUTS):]):
        res = _dot_nt(wt_ref[lo:lo + n, :], xn)
        for s in range(n_sub):
            o_ref[s] = res[:, s * LANES:(s + 1) * LANES].astype(o_ref.dtype)
        lo += n


def _proj(x2, nw, wn, wt, onehot, tm):
    t, d = x2.shape
    assert t % tm == 0 and tm % LANES == 0 and onehot.shape[1] == tm
    n_sub = tm // LANES
    n_pat = onehot.shape[0]
    const = lambda arr: pl.BlockSpec(arr.shape, lambda i: (0,) * arr.ndim, pipeline_mode=pl.Buffered(1))
    slabbed = lambda name: name == 'kv_cmp'
    out_specs = [pl.BlockSpec((n // LANES, tm, LANES), lambda i: (0, i, 0)) if slabbed(name)
                 else pl.BlockSpec((tm, n), lambda i: (i, 0)) for name, n, _ in _NAT_OUTS] + \
                [pl.BlockSpec((n_sub, n, LANES), lambda i: (i, 0, 0)) for _, n, _ in _TR_OUTS]
    out_shape = [jax.ShapeDtypeStruct((n // LANES, t, LANES) if slabbed(name) else (t, n), dt)
                 for name, n, dt in _NAT_OUTS] + \
                [jax.ShapeDtypeStruct((t // LANES, n, LANES), dt) for _, n, dt in _TR_OUTS]
    return pl.pallas_call(
        functools.partial(_proj_kernel, n_sub=n_sub),
        grid=(t // tm,),
        in_specs=[pl.BlockSpec((tm, d), lambda i: (i, 0)), const(nw), const(wn), const(wt),
                  pl.BlockSpec((None, tm, onehot.shape[2]), lambda i: (i % n_pat, 0, 0))],
        out_specs=out_specs,
        out_shape=out_shape,
        compiler_params=pltpu.CompilerParams(dimension_semantics=("parallel",),
                                             vmem_limit_bytes=VMEM_LIMIT),
        name="proj",
    )(x2, nw, wn, wt, onehot)


def _compress_kernel(raw_ref, posk_ref, posv_ref, w1k_ref, b1k_ref, w2k_ref,
                     w1v_ref, b1v_ref, w2vt_ref, kc_ref, vct_ref):
    nseg = kc_ref.shape[0]
    n_slab = raw_ref.shape[0]
    per_kv = n_slab // 2
    for j in range(n_slab):
        is_k = j < per_kv
        pos_ref, w1_ref, b1_ref = (posk_ref, w1k_ref, b1k_ref) if is_k else (posv_ref, w1v_ref, b1v_ref)
        first = jnp.zeros((nseg, 2 * CMP_HIDDEN), F32)
        second = jnp.zeros((nseg, 2 * CMP_HIDDEN), F32)
        for p in range(CMP_STRIDE):
            rows = raw_ref[j, pl.ds(p, nseg, stride=CMP_STRIDE), :]
            first = first + _dot((rows + pos_ref[p:p + 1, :]).astype(BF16), w1_ref[p])
            q = CMP_STRIDE + p
            second = second + _dot((rows + pos_ref[q:q + 1, :]).astype(BF16), w1_ref[q])
        hid = _silu(first + pltpu.roll(second, nseg - 1, 0) + b1_ref[...]).astype(BF16)
        for gi in range(2):
            g = 2 * (j % per_kv) + gi
            hg = hid[:, gi * CMP_HIDDEN:(gi + 1) * CMP_HIDDEN]
            if is_k:
                kc_ref[:, g * HEAD_DIM:(g + 1) * HEAD_DIM] = _dot(hg, w2k_ref[...]).astype(kc_ref.dtype)
            else:
                vct_ref[g * HEAD_DIM:(g + 1) * HEAD_DIM, :] = _dot_nt(w2vt_ref[...], hg).astype(vct_ref.dtype)


def _compress(raw, batch, posk, posv, w1k, b1k, w2k, w1v, b1v, w2vt):
    n_slab, t, lanes = raw.shape
    seq = t // batch
    nseg = seq // CMP_STRIDE
    assert n_slab * lanes == 2 * KV_WIDTH and lanes == 2 * HEAD_DIM
    const = lambda arr: pl.BlockSpec(arr.shape, lambda i: (0,) * arr.ndim, pipeline_mode=pl.Buffered(1))
    return pl.pallas_call(
        _compress_kernel,
        grid=(batch,),
        in_specs=[pl.BlockSpec((n_slab, seq, lanes), lambda i: (0, i, 0)),
                  const(posk), const(posv), const(w1k), const(b1k), const(w2k),
                  const(w1v), const(b1v), const(w2vt)],
        out_specs=[pl.BlockSpec((None, nseg, KV_WIDTH), lambda i: (i, 0, 0)),
                   pl.BlockSpec((None, KV_WIDTH, nseg), lambda i: (i, 0, 0))],
        out_shape=[jax.ShapeDtypeStruct((batch, nseg, KV_WIDTH), BF16),
                   jax.ShapeDtypeStruct((batch, KV_WIDTH, nseg), BF16)],
        compiler_params=pltpu.CompilerParams(dimension_semantics=("parallel",),
                                             vmem_limit_bytes=VMEM_LIMIT),
        name="compress",
    )(raw, posk, posv, w1k, b1k, w2k, w1v, b1v, w2vt)


def _nsa_kernel(qt_ref, gt_ref, kc_ref, vct_ref, kslc_ref, vslct_ref, kswa_ref, vswat_ref,
                tsel_ref, tcmp_ref, ovt_ref, far_ref, out_ref, guard_ref,
                qbd_ref, qaug_ref, m_ref, acc_ref, ot_ref, s_ref, p_ref, viol_ref, work_ref, chosen_ref,
                ocmp_ref, *, n_qb, n_cmp_rows, n_sel, top_n, exact):
    qb = pl.program_id(1)
    q0 = qb * Q_BLOCK

    @pl.when(qb == 0)
    def _():
        qbd_ref[...] = jnp.zeros_like(qbd_ref)
        qaug_ref[...] = jnp.zeros_like(qaug_ref)

    for g in range(NSA_GROUPS):
        for r in range(NSA_HPG):
            h = g * NSA_HPG + r
            q_head = qt_ref[h * HEAD_DIM:(h + 1) * HEAD_DIM, :]
            qbd_ref[g * HEAD_DIM:(g + 1) * HEAD_DIM, h * Q_BLOCK:(h + 1) * Q_BLOCK] = q_head
            qaug_ref[g, 0:HEAD_DIM, r * Q_BLOCK:(r + 1) * Q_BLOCK] = q_head

    def gslice(g):
        return slice(g * GCOLS, (g + 1) * GCOLS)

    def values_with_ones(vt_tiles, g):
        vt = jnp.concatenate([t[g * HEAD_DIM:(g + 1) * HEAD_DIM, :] for t in vt_tiles], axis=1)
        return jnp.concatenate([vt, jnp.ones((SUM_ROWS, vt.shape[1]), BF16)], axis=0)

    def normalized(a):
        return a[0:HEAD_DIM, :] * (1.0 / a[HEAD_DIM:HEAD_DIM + 1, :])

    def self_logit(kt):
        cols = []
        for h in range(NSA_HEADS):
            g = h // NSA_HPG
            prod = qt_ref[h * HEAD_DIM:(h + 1) * HEAD_DIM, :].astype(F32) * kt[g * HEAD_DIM:(g + 1) * HEAD_DIM, :]
            cols.append(jnp.sum(prod, axis=0, keepdims=True))
        return jnp.concatenate(cols, axis=1) - far_ref[...]

    diag_rows = pl.ds(pl.multiple_of(q0, Q_BLOCK), Q_BLOCK)

    u0 = pl.multiple_of(8 * (n_qb - 1 - qb), 8)

    def compress_and_select(n_rows, n_blk):
        j_iota = lax.broadcasted_iota(jnp.int32, (n_blk, Q_BLOCK), 0)
        j_f32 = j_iota.astype(F32)
        t_pos = q0 + lax.broadcasted_iota(jnp.int32, (n_blk, Q_BLOCK), 1)
        cur = t_pos // SEL_BLOCK
        valid = j_iota <= cur
        forced = (j_iota == 0) | (j_iota == cur) | (j_iota == cur - 1)

        logits = [_dot(kc_ref[0:n_rows, :], qbd_ref[:, gslice(g)]) for g in range(NSA_GROUPS)]
        for g in range(NSA_GROUPS):
            s = logits[g] + tcmp_ref[pl.ds(u0, n_rows), gslice(g)]
            m = jnp.maximum(jnp.max(s, axis=0, keepdims=True), 0.1 * NEG)
            p_ref[0:n_rows, gslice(g)] = jnp.exp2(s - m).astype(BF16)
        imps = []
        for g in range(NSA_GROUPS):
            e = p_ref[0:n_rows, gslice(g)]
            raw = _dot(values_with_ones([vct_ref[:, 0:n_rows]], g), e)
            total = raw[HEAD_DIM:HEAD_DIM + 1, :]
            inv = 1.0 / jnp.where(total == 0.0, 1.0, total)
            ocmp_ref[g] = raw[0:HEAD_DIM, :] * inv
            imp_heads = _dot(ovt_ref[0:n_blk, 0:n_rows], e) * inv
            imp = imp_heads[:, 0:Q_BLOCK]
            for r in range(1, NSA_HPG):
                imp = imp + imp_heads[:, r * Q_BLOCK:(r + 1) * Q_BLOCK]
            imps.append(imp)

        quota = top_n - (1 + (cur >= 1).astype(jnp.int32) + (cur >= 2).astype(jnp.int32))
        for g in range(NSA_GROUPS):
            work_ref[g, 0:n_blk, :] = jnp.where(forced, -2.0, jnp.where(valid, imps[g], -1.0))
            chosen_ref[g, 0:n_blk, :] = jnp.where(forced, 1.0, 0.0)

        def pick_round(index, limited):
            for g in range(NSA_GROUPS):
                work = work_ref[g, 0:n_blk, :]
                best = jnp.max(work, axis=0, keepdims=True)
                first = jnp.min(jnp.where(work == best, j_f32, float(n_blk)), axis=0, keepdims=True)
                hit = j_f32 == first
                if limited:
                    hit = hit & (quota > index)
                chosen_ref[g, 0:n_blk, :] = jnp.where(hit, 1.0, chosen_ref[g, 0:n_blk, :])
                work_ref[g, 0:n_blk, :] = jnp.where(hit, -2.0, work)

        max_forced = 3
        for index in range(top_n - max_forced):
            pick_round(index, False)

        @pl.when(q0 < (max_forced - 1) * SEL_BLOCK)
        def _():
            for index in range(top_n - max_forced, top_n - 1):
                pick_round(index, True)

        for g in range(NSA_GROUPS):
            addm = jnp.where((chosen_ref[g, 0:n_blk, :] > 0.5) & valid, 0.0, NEG).astype(BF16)
            qaug_ref[g, HEAD_DIM:HEAD_DIM + n_blk, :] = jnp.concatenate([addm] * NSA_HPG, axis=1)
            if n_blk < n_sel:
                qaug_ref[g, HEAD_DIM + n_blk:HEAD_DIM + n_sel, :] = jnp.full((n_sel - n_blk, GCOLS), NEG, BF16)

    first_half = qb < n_qb // 2

    @pl.when(first_half)
    def _():
        compress_and_select(n_cmp_rows // 2, n_sel // 2)

    @pl.when(jnp.logical_not(first_half))
    def _():
        compress_and_select(n_cmp_rows, n_sel)

    def window(exact):
        first_tile = jnp.maximum(qb - (WIN_TILES - 1), 0)
        kw = kswa_ref[pl.ds(pl.multiple_of(first_tile * LANES, LANES), WIN_TILES * LANES), :]
        win_vt_tiles = [vswat_ref[first_tile + t] for t in range(WIN_TILES)]
        win_ids = []
        for t in range(WIN_TILES):
            i = qb - (first_tile + t)
            win_ids.append(jnp.where(i < 0, TILE_MASKED, jnp.where(i == WIN_TILES - 1, TILE_WIN_OLD, i)))
        logits = [_dot(kw, qbd_ref[:, gslice(g)]) for g in range(NSA_GROUPS)]
        if not exact:
            m_fix = self_logit(kswa_ref[diag_rows, :].astype(F32).T)
        for g in range(NSA_GROUPS):
            m8 = None
            for t in range(WIN_TILES):
                rows = slice(t * LANES, (t + 1) * LANES)
                v = logits[g][rows, :] + tsel_ref[win_ids[t], :, gslice(g)]
                if exact:
                    s_ref[rows, gslice(g)] = v
                    vm = jnp.max(v.reshape(LANES // 8, 8, GCOLS), axis=0)
                else:
                    pt = jnp.exp2(v - m_fix[:, gslice(g)]).astype(BF16)
                    p_ref[rows, gslice(g)] = pt
                    vm = jnp.max(pt.reshape(LANES // 16, 16, GCOLS), axis=0)
                m8 = vm if m8 is None else jnp.maximum(m8, vm)
            if exact:
                m = jnp.max(m8, axis=0, keepdims=True)
                p_ref[0:WIN_TILES * LANES, gslice(g)] = jnp.exp2(s_ref[:, gslice(g)] - m).astype(BF16)
            else:
                viol_ref[:, gslice(g)] = jnp.maximum(viol_ref[:, gslice(g)],
                                                     jnp.max(m8.astype(F32), axis=0, keepdims=True))
        for g in range(NSA_GROUPS):
            acc_ref[g] = _dot(values_with_ones(win_vt_tiles, g), p_ref[0:WIN_TILES * LANES, gslice(g)])

    def sel_logits(tile0, n_tiles):
        key_rows = pl.ds(pl.multiple_of(tile0 * LANES, LANES), n_tiles * LANES)
        return [_dot(kslc_ref[key_rows, g * AUG_DIM:(g + 1) * AUG_DIM], qaug_ref[g])
                for g in range(NSA_GROUPS)]

    def sel_update(tile0, n_tiles, near, exact, logits, p_row0):
        vt_tiles = [vslct_ref[tile0 + t] for t in range(n_tiles)]
        if near:
            tile_ids = []
            for t in range(n_tiles):
                i = qb - (tile0 + t)
                tile_ids.append(jnp.where(i < 0, TILE_MASKED, jnp.minimum(i, TILE_FAR)))
        alphas = []
        for g in range(NSA_GROUPS):
            m_old = m_ref[:, gslice(g)]
            m8 = None
            for t in range(n_tiles):
                rows = slice(t * LANES, (t + 1) * LANES)
                v = logits[g][rows, :]
                if near:
                    v = v + tsel_ref[tile_ids[t], :, gslice(g)]
                if exact:
                    s_ref[rows, gslice(g)] = v
                    vm = jnp.max(v.reshape(LANES // 8, 8, GCOLS), axis=0)
                else:
                    pt = jnp.exp2(v - m_old).astype(BF16)
                    p_ref[p_row0 + t * LANES:p_row0 + (t + 1) * LANES, gslice(g)] = pt
                    vm = jnp.max(pt.reshape(LANES // 16, 16, GCOLS), axis=0)
                m8 = vm if m8 is None else jnp.maximum(m8, vm)
            if exact:
                m_new = jnp.maximum(m_old, jnp.max(m8, axis=0, keepdims=True))
                alphas.append(jnp.exp2(m_old - m_new))
                m_ref[:, gslice(g)] = m_new
                for t in range(n_tiles):
                    rows = slice(t * LANES, (t + 1) * LANES)
                    p_ref[p_row0 + t * LANES:p_row0 + (t + 1) * LANES, gslice(g)] = \
                        jnp.exp2(s_ref[rows, gslice(g)] - m_new).astype(BF16)
            else:
                viol_ref[:, gslice(g)] = jnp.maximum(viol_ref[:, gslice(g)],
                                                     jnp.max(m8.astype(F32), axis=0, keepdims=True))
        for g in range(NSA_GROUPS):
            pv = _dot(values_with_ones(vt_tiles, g), p_ref[p_row0:p_row0 + n_tiles * LANES, gslice(g)])
            if exact:
                acc_ref[g] = acc_ref[g] * alphas[g] + pv
            else:
                acc_ref[g] = acc_ref[g] + pv

    def sel_step(tile0, n_tiles, near, exact):
        sel_update(tile0, n_tiles, near, exact, sel_logits(tile0, n_tiles), 0)

    def sel_two_steps(tile_a, tile_b, near):
        la, lb = sel_logits(tile_a, STEP_TILES), sel_logits(tile_b, STEP_TILES)
        sel_update(tile_a, STEP_TILES, near, False, la, 0)
        sel_update(tile_b, STEP_TILES, near, False, lb, STEP)

    r_diag = qb // STEP_TILES
    n_far = jnp.maximum((qb - (N_NEAR - 1)) // STEP_TILES, 0)
    half = STEP_TILES // 2

    def selected(exact):
        if exact:
            m_ref[...] = jnp.full(m_ref.shape, NEG, F32)
        else:
            kt = kslc_ref[diag_rows, :].astype(F32).T
            m_ref[...] = self_logit(jnp.concatenate(
                [kt[g * AUG_DIM:g * AUG_DIM + HEAD_DIM, :] for g in range(NSA_GROUPS)], axis=0))
        acc_ref[...] = jnp.zeros_like(acc_ref)

        @pl.when(qb % STEP_TILES < half)
        def _():
            sel_step(STEP_TILES * r_diag, half, True, exact)

        @pl.when(qb % STEP_TILES >= half)
        def _():
            sel_step(STEP_TILES * r_diag, STEP_TILES, True, exact)

        def near_body(i, carry):
            sel_step(STEP_TILES * (r_diag - i), STEP_TILES, True, exact)
            return carry

        def far_body(r, carry):
            sel_step(STEP_TILES * r, STEP_TILES, False, exact)
            return carry

        def far_pair_body(k, carry):
            sel_two_steps(STEP_TILES * 2 * k, STEP_TILES * (2 * k + 1), False)
            return carry

        lax.fori_loop(1, r_diag - n_far + 1, near_body, 0)
        if exact:
            lax.fori_loop(0, n_far, far_body, 0)
        else:
            @pl.when(n_far % 2 == 1)
            def _():
                sel_step(STEP_TILES * (n_far - 1), STEP_TILES, False, False)

            lax.fori_loop(0, n_far // 2, far_pair_body, 0)

    viol_ref[...] = jnp.zeros_like(viol_ref)
    selected(exact)
    o_slc = [normalized(acc_ref[g]) for g in range(NSA_GROUPS)]
    window(exact)
    o_swa = [normalized(acc_ref[g]) for g in range(NSA_GROUPS)]
    guard_ref[...] = jnp.broadcast_to(jnp.max(viol_ref[...], axis=1, keepdims=True), guard_ref.shape)

    gates = _sigmoid(gt_ref[...])
    for g in range(NSA_GROUPS):
        def gate_row(c):
            return jnp.concatenate(
                [gates[c * NSA_HEADS + g * NSA_HPG + r:c * NSA_HEADS + g * NSA_HPG + r + 1, :]
                 for r in range(NSA_HPG)], axis=1)
        og = gate_row(0) * ocmp_ref[g] + gate_row(1) * o_slc[g] + gate_row(2) * o_swa[g]
        for r in range(NSA_HPG):
            h = g * NSA_HPG + r
            ot_ref[h * HEAD_DIM:(h + 1) * HEAD_DIM, :] = og[:, r * Q_BLOCK:(r + 1) * Q_BLOCK]
    out_ref[...] = ot_ref[...].T


def _nsa(qt, gt, kc, vct, kslc, vslct, kswa, vswat, tsel, tcmp, ovt, far, seq, exact):
    b = kc.shape[0]
    n_qb = seq // Q_BLOCK
    assert n_qb % STEP_TILES == 0 and n_qb >= WIN_TILES and WIN_TILES >= STEP_TILES
    n_cmp_rows = kc.shape[1]
    n_sel = seq // SEL_BLOCK
    assert n_sel <= AUG_DIM - HEAD_DIM
    top_n = min(SEL_TOPN, n_sel)
    assert top_n > 3
    n_gate = gt.shape[2]
    const = lambda arr: pl.BlockSpec(arr.shape, lambda i, j: (0,) * arr.ndim,
                                     pipeline_mode=pl.Buffered(1))
    kern = functools.partial(_nsa_kernel, n_qb=n_qb, n_cmp_rows=n_cmp_rows, n_sel=n_sel, top_n=top_n,
                             exact=exact)
    return pl.pallas_call(
        kern,
        grid=(b, n_qb),
        in_specs=[pl.BlockSpec((None, None, NSA_WIDTH, Q_BLOCK), lambda i, j: (i, j, 0, 0)),
                  pl.BlockSpec((None, None, n_gate, Q_BLOCK), lambda i, j: (i, j, 0, 0)),
                  pl.BlockSpec((None, n_cmp_rows, KV_WIDTH), lambda i, j: (i, 0, 0)),
                  pl.BlockSpec((None, KV_WIDTH, n_cmp_rows), lambda i, j: (i, 0, 0)),
                  pl.BlockSpec((None, seq, NSA_GROUPS * AUG_DIM), lambda i, j: (i, 0, 0)),
                  pl.BlockSpec((None, n_qb, KV_WIDTH, Q_BLOCK), lambda i, j: (i, 0, 0, 0)),
                  pl.BlockSpec((None, seq, KV_WIDTH), lambda i, j: (i, 0, 0)),
                  pl.BlockSpec((None, n_qb, KV_WIDTH, Q_BLOCK), lambda i, j: (i, 0, 0, 0)),
                  const(tsel), const(tcmp), const(ovt), const(far)],
        out_specs=[pl.BlockSpec((None, Q_BLOCK, NSA_WIDTH), lambda i, j: (i, j, 0)),
                   pl.BlockSpec((None, None, 8, LANES), lambda i, j: (i, j, 0, 0))],
        out_shape=[jax.ShapeDtypeStruct((b, seq, NSA_WIDTH), F32),
                   jax.ShapeDtypeStruct((b, n_qb, 8, LANES), F32)],
        scratch_shapes=[pltpu.VMEM((KV_WIDTH, QCOLS), BF16),
                        pltpu.VMEM((NSA_GROUPS, AUG_DIM, GCOLS), BF16),
                        pltpu.VMEM((1, QCOLS), F32),
                        pltpu.VMEM((NSA_GROUPS, HEAD_DIM + SUM_ROWS, GCOLS), F32),
                        pltpu.VMEM((NSA_WIDTH, Q_BLOCK), F32),
                        pltpu.VMEM((WIN_TILES * LANES, QCOLS), F32),
                        pltpu.VMEM((max(WIN_TILES * LANES, 2 * STEP), QCOLS), BF16),
                        pltpu.VMEM((1, QCOLS), F32),
                        pltpu.VMEM((NSA_GROUPS, n_sel, Q_BLOCK), F32),
                        pltpu.VMEM((NSA_GROUPS, n_sel, Q_BLOCK), F32),
                        pltpu.VMEM((NSA_GROUPS, HEAD_DIM, GCOLS), F32)],
        compiler_params=pltpu.CompilerParams(dimension_semantics=("parallel", "arbitrary"),
                                             vmem_limit_bytes=VMEM_LIMIT),
        name="nsa_exact" if exact else "nsa",
    )(qt, gt, kc, vct, kslc, vslct, kswa, vswat, tsel, tcmp, ovt, far)


def _ssd_kernel(xbc_ref, dt_ref, convw_ref, convb_ref, dtb_ref, a_ref, dskip_ref, expand_ref, y_ref,
                xp_ref, tail_ref, state_ref, slab_ref):
    @pl.when(pl.program_id(1) == 0)
    def _():
        tail_ref[...] = jnp.zeros_like(tail_ref)
        state_ref[...] = jnp.zeros_like(state_ref)

    n_chunks = xbc_ref.shape[0] // SSM_CHUNK
    fronts = [_ssd_front(ci, xbc_ref, dt_ref, convw_ref, convb_ref, dtb_ref, a_ref, expand_ref,
                         xp_ref.at[ci], tail_ref, slab_ref.at[ci]) for ci in range(n_chunks)]
    for ci in range(n_chunks):
        _ssd_back(ci, fronts[ci], dskip_ref, y_ref, state_ref, slab_ref.at[ci])


def _ssd_front(ci, xbc_ref, dt_ref, convw_ref, convb_ref, dtb_ref, a_ref, expand_ref, xp_ref, tail_ref, slab_ref):
    L = SSM_CHUNK
    NV = L // 8
    halo = CONV_WIDTH - 1
    tok = slice(ci * L, (ci + 1) * L)

    n_slab = CONV_DIM // LANES
    for s in range(n_slab):
        slab_ref[s] = xbc_ref[tok, s * LANES:(s + 1) * LANES]
    slab_ref[n_slab] = jnp.concatenate([dt_ref[tok, :], jnp.zeros((L, LANES - SSM_HEADS), F32)], axis=1)
    for v in range(NV):
        for s in range(n_slab):
            xp_ref[(halo + v) * 8:(halo + v + 1) * 8, s * LANES:(s + 1) * LANES] = \
                slab_ref[s, pl.ds(v, 8, stride=NV), :]
    sub = lax.broadcasted_iota(jnp.int32, (8, CONV_DIM), 0)
    for w in range(halo):
        cur = xp_ref[(NV + w) * 8:(NV + w + 1) * 8, :]
        prev = tail_ref[w * 8:(w + 1) * 8, :]
        xp_ref[w * 8:(w + 1) * 8, :] = pltpu.roll(jnp.where(sub == 7, prev, cur), 1, 0)
    tail_ref[...] = xp_ref[NV * 8:(NV + halo) * 8, :]
    conv = convb_ref[...]
    for k in range(CONV_WIDTH):
        conv = conv + convw_ref[k:k + 1, :] * xp_ref[8 * k:8 * k + L, :]
    xbc = _silu(conv)
    xs = xbc[:, 0:SSM_WIDTH]
    bm = xbc[:, SSM_WIDTH:SSM_WIDTH + SSM_GROUPS * SSM_STATE]
    cm = xbc[:, SSM_WIDTH + SSM_GROUPS * SSM_STATE:]

    dt_raw = jnp.concatenate([slab_ref[n_slab, pl.ds(v, 8, stride=NV), :] for v in range(NV)], axis=0)
    dt = _softplus(dt_raw[:, 0:SSM_HEADS] + dtb_ref[...])
    a_row = -jnp.exp(a_ref[...])
    row_i = lax.broadcasted_iota(jnp.int32, (L, L), 0)
    col_j = lax.broadcasted_iota(jnp.int32, (L, L), 1)
    token = lambda r: (r % 8) * NV + r // 8
    tril = token(row_i) >= token(col_j)
    tril01 = jnp.where(tril, 1.0, 0.0).astype(BF16)
    cs = _dot_exact_rhs(tril01, dt * a_row)
    cst = jnp.concatenate([cs, jnp.zeros((L, L - SSM_HEADS), F32)], axis=1).T

    expand = expand_ref[...]
    cs_x = _dot_exact_lhs(cs, expand)
    dt_x = _dot_exact_lhs(dt, expand)
    total_x = cs_x[L - 1:L, :]
    xdt = (xs * dt_x).astype(BF16)
    xw = (xs * (jnp.exp(total_x - cs_x) * dt_x)).astype(BF16)
    decay_out = jnp.exp(cs_x)
    decay_state = jnp.exp(total_x)
    return xs, bm, cm, cs, cst, tril, xdt, xw, decay_out, decay_state


def _ssd_back(ci, front, dskip_ref, y_ref, state_ref, slab_ref):
    xs, bm, cm, cs, cst, tril, xdt, xw, decay_out, decay_state = front
    L = SSM_CHUNK
    NV = L // 8
    tok = slice(ci * L, (ci + 1) * L)
    lane = lax.broadcasted_iota(jnp.int32, (L, LANES), 1)
    gw = SSM_HPG * SSM_HEAD_DIM
    for g in range(SSM_GROUPS):
        bg = bm[:, g * SSM_STATE:(g + 1) * SSM_STATE]
        cg = cm[:, g * SSM_STATE:(g + 1) * SSM_STATE].astype(BF16)
        cb = _dot_nt(cg, bg.astype(BF16))
        st = state_ref[g]
        y_g = _dot(cg, st.astype(BF16)) * decay_out[:, g * gw:(g + 1) * gw]
        state_ref[g] = st * decay_state[:, g * gw:(g + 1) * gw] + \
            _dot(bg.T.astype(BF16), xw[:, g * gw:(g + 1) * gw])
        pieces = []
        for pair in range(SSM_HPG // 2):
            ws = []
            for hh in range(2):
                h = g * SSM_HPG + 2 * pair + hh
                diff = cs[:, h:h + 1] - cst[h:h + 1, :]
                decay = jnp.where(tril, jnp.exp(jnp.where(tril, diff, 0.0)), 0.0)
                ws.append((cb * decay).astype(BF16))
            w_pair = jnp.concatenate(ws, axis=1)
            lo = g * gw + pair * LANES
            slab = xdt[:, lo:lo + LANES]
            zero = jnp.zeros_like(slab)
            x_bd = jnp.concatenate([jnp.where(lane < SSM_HEAD_DIM, slab, zero),
                                    jnp.where(lane >= SSM_HEAD_DIM, slab, zero)], axis=0)
            pieces.append(_dot(w_pair, x_bd))
        y_g = y_g + jnp.concatenate(pieces, axis=1) + \
            dskip_ref[:, g * gw:(g + 1) * gw] * xs[:, g * gw:(g + 1) * gw]
        for s in range(gw // LANES):
            for v in range(NV):
                slab_ref[s, pl.ds(v, 8, stride=NV), :] = y_g[v * 8:(v + 1) * 8, s * LANES:(s + 1) * LANES]
        for s in range(gw // LANES):
            y_ref[tok, g * gw + s * LANES:g * gw + (s + 1) * LANES] = slab_ref[s]


def _ssd(xbc, dt, convw, convb, dtb, a_log, dskip_x, expand, seq):
    b = xbc.shape[0]
    rows = SSD_CHUNKS_PER_STEP * SSM_CHUNK
    assert seq % rows == 0
    halo_rows = (CONV_WIDTH - 1) * 8
    const = lambda arr: pl.BlockSpec(arr.shape, lambda i, j: (0,) * arr.ndim)
    return pl.pallas_call(
        _ssd_kernel,
        grid=(b, seq // rows),
        in_specs=[pl.BlockSpec((None, rows, CONV_DIM), lambda i, j: (i, j, 0)),
                  pl.BlockSpec((None, rows, SSM_HEADS), lambda i, j: (i, j, 0)),
                  const(convw), const(convb), const(dtb), const(a_log), const(dskip_x), const(expand)],
        out_specs=pl.BlockSpec((None, rows, SSM_WIDTH), lambda i, j: (i, j, 0)),
        out_shape=jax.ShapeDtypeStruct((b, seq, SSM_WIDTH), F32),
        scratch_shapes=[pltpu.VMEM((SSD_CHUNKS_PER_STEP, SSM_CHUNK + halo_rows, CONV_DIM), F32),
                        pltpu.VMEM((halo_rows, CONV_DIM), F32),
                        pltpu.VMEM((SSM_GROUPS, SSM_STATE, SSM_HPG * SSM_HEAD_DIM), F32),
                        pltpu.VMEM((SSD_CHUNKS_PER_STEP, CONV_DIM // LANES + 1, SSM_CHUNK, LANES), F32)],
        compiler_params=pltpu.CompilerParams(dimension_semantics=("parallel", "arbitrary"),
                                             vmem_limit_bytes=VMEM_LIMIT),
        name="ssd",
    )(xbc, dt, convw, convb, dtb, a_log, dskip_x, expand)


def _epilogue_kernel(x_ref, o_ref, y_ref, nw_ref, wz1_ref, wz2_ref, wmg_ref, won_ref, wos_ref, wo_ref,
                     snw_ref, fnw_ref, out_ref):
    x = x_ref[...]
    ms = jnp.mean(x * x, axis=-1, keepdims=True)
    xn = (x * lax.rsqrt(ms + NORM_EPS) * nw_ref[...]).astype(BF16)
    u = (o_ref[...] * _silu(_dot(xn, wz1_ref[...]))).astype(BF16)
    h_nsa = _dot(u, won_ref[...])
    hh = y_ref[...] * _silu(_dot(xn, wz2_ref[...]))
    gw = SSM_WIDTH // SSM_GROUPS
    parts = []
    for g in range(SSM_GROUPS):
        hg = hh[:, g * gw:(g + 1) * gw]
        hg = hg * lax.rsqrt(jnp.mean(hg * hg, axis=-1, keepdims=True) + NORM_EPS)
        parts.append((hg * snw_ref[:, g * gw:(g + 1) * gw]).astype(BF16))
    h_ssm = _dot(jnp.concatenate(parts, axis=1), wos_ref[...])
    gate = _sigmoid(_dot(xn, wmg_ref[...]))
    mix = (gate[:, 0:D_MODEL] * h_nsa + gate[:, D_MODEL:] * h_ssm).astype(BF16)
    r = x + _dot(mix, wo_ref[...])
    ms2 = jnp.mean(r * r, axis=-1, keepdims=True)
    out_ref[...] = r * lax.rsqrt(ms2 + NORM_EPS) * fnw_ref[...]


def _epilogue(x2, o2, y2, nw, wz1, wz2, wmg, won, wos, wo, snw, fnw, tm):
    t, d = x2.shape
    assert t % tm == 0
    const = lambda arr: pl.BlockSpec(arr.shape, lambda i: (0,) * arr.ndim, pipeline_mode=pl.Buffered(1))
    return pl.pallas_call(
        _epilogue_kernel,
        grid=(t // tm,),
        in_specs=[pl.BlockSpec((tm, d), lambda i: (i, 0)),
                  pl.BlockSpec((tm, NSA_WIDTH), lambda i: (i, 0)),
                  pl.BlockSpec((tm, SSM_WIDTH), lambda i: (i, 0)),
                  const(nw), const(wz1), const(wz2), const(wmg), const(won), const(wos), const(wo),
                  const(snw), const(fnw)],
        out_specs=pl.BlockSpec((tm, d), lambda i: (i, 0)),
        out_shape=jax.ShapeDtypeStruct((t, d), F32),
        compiler_params=pltpu.CompilerParams(dimension_semantics=("parallel",),
                                             vmem_limit_bytes=VMEM_LIMIT),
        name="epilogue",
    )(x2, o2, y2, nw, wz1, wz2, wmg, won, wos, wo, snw, fnw)


def _table_kernel(rb_ref, bucket_ref, out_ref):
    bk = bucket_ref[...]
    for h in range(NSA_HEADS):
        acc = jnp.full(bk.shape, NEG, F32)
        for b in range(REL_BUCKETS):
            acc = jnp.where(bk == b, rb_ref[b, h], acc)
        out_ref[:, h * Q_BLOCK:(h + 1) * Q_BLOCK] = acc


def _bias_table(rb, buckets, tr):
    rows = buckets.shape[0]
    assert rows % tr == 0
    return pl.pallas_call(
        _table_kernel,
        grid=(rows // tr,),
        in_specs=[pl.BlockSpec(memory_space=pltpu.SMEM),
                  pl.BlockSpec((tr, Q_BLOCK), lambda i: (i, 0))],
        out_specs=pl.BlockSpec((tr, QCOLS), lambda i: (i, 0)),
        out_shape=jax.ShapeDtypeStruct((rows, QCOLS), F32),
        compiler_params=pltpu.CompilerParams(dimension_semantics=("parallel",)),
        name="bias_table",
    )(rb, buckets)


def _bucket_tables(seq):
    n_qb = seq // Q_BLOCK
    key = np.arange(LANES)[:, None]
    tok = np.arange(Q_BLOCK)[None, :]
    tiles = [_t5_bucket_np(LANES * o + tok - key) for o in range(N_NEAR)]
    tiles.append(np.full((LANES, Q_BLOCK), REL_BUCKETS - 1, np.int32))
    dwin = WINDOW + tok - key
    tiles.append(_t5_bucket_np(np.where(dwin < WINDOW, dwin, -1)))
    tiles.append(np.full((LANES, Q_BLOCK), MASKED_BUCKET, np.int32))
    assert len(tiles) == N_TILES and LANES * N_NEAR - (LANES - 1) >= 790
    n_cmp_rows = seq // CMP_STRIDE
    rows = 8 * (n_qb - 1) + n_cmp_rows
    rows_pad = -(-rows // LANES) * LANES
    u = np.arange(rows_pad)[:, None]
    cmp_tbl = _t5_bucket_np(tok - CMP_STRIDE * u + Q_BLOCK * (n_qb - 1) - (CMP_BLOCK - 1))
    return np.concatenate(tiles, axis=0), cmp_tbl


def _block_onehot(seq, tm):
    pos = np.arange(seq)
    pat = np.zeros((seq, NSA_GROUPS, AUG_DIM), np.float32)
    pat[pos, :, HEAD_DIM + pos // SEL_BLOCK] = 1.0
    return jnp.asarray(pat.reshape(seq // tm, tm, NSA_GROUPS * AUG_DIM), dtype=BF16)


def _overlap_t(seq):
    n_cmp_rows = seq // CMP_STRIDE
    n_sel = seq // SEL_BLOCK
    c_start = np.arange(n_cmp_rows)[None, :] * CMP_STRIDE
    s_start = np.arange(n_sel)[:, None] * SEL_BLOCK
    ov = (c_start < s_start + SEL_BLOCK) & (c_start + CMP_BLOCK > s_start)
    ov[:, n_cmp_rows - 1] = False
    return jnp.asarray(ov, dtype=BF16)


def _layer(x, norm_w, w_in, cmp_pos_k, cmp_pos_v, cmp_k_w1, cmp_k_b1, cmp_k_w2, cmp_v_w1, cmp_v_b1,
           cmp_v_w2, conv_w, conv_b, dt_bias, a_log, d_skip, ssm_norm_w, w_out_nsa, w_out_ssm, w_out,
           rel_bias, out_norm_w):
    b, s, d = x.shape
    assert d == D_MODEL and s % STEP == 0 and s >= 2 * WINDOW
    t = b * s
    n_qb = s // Q_BLOCK
    cols = _column_offsets()
    wcol = lambda name: w_in[:, cols[name][0]:cols[name][1]]
    x2 = x.reshape(t, d)
    nw = norm_w.reshape(1, d).astype(F32)

    gate_w = wcol('nsa_gate').reshape(d, NSA_HEADS, 3).transpose(0, 2, 1).reshape(d, 3 * NSA_HEADS)
    w_kslc = jnp.pad(wcol('k_slc').reshape(d, NSA_GROUPS, HEAD_DIM),
                     ((0, 0), (0, 0), (0, AUG_DIM - HEAD_DIM))).reshape(d, NSA_GROUPS * AUG_DIM)
    wn = jnp.concatenate([w_kslc, wcol('k_swa'), wcol('k_cmp'), wcol('v_cmp'), wcol('xbc'),
                          wcol('dt')], axis=1).astype(BF16)
    wt = jnp.concatenate([wcol('q') * (HEAD_DIM ** -0.5 * LOG2E), wcol('v_slc'), wcol('v_swa'), gate_w],
                         axis=1).T.astype(BF16)
    kslc, kswa, cc, xbc, dt, qt, vslct, vswat, gt = _proj(x2, nw, wn, wt, _block_onehot(s, STEP), STEP)
    kslc = kslc.reshape(b, s, NSA_GROUPS * AUG_DIM)
    kswa = kswa.reshape(b, s, KV_WIDTH)
    xbc = xbc.reshape(b, s, CONV_DIM)
    dt = dt.reshape(b, s, SSM_HEADS)
    qt = qt.reshape(b, n_qb, NSA_WIDTH, Q_BLOCK)
    vslct = vslct.reshape(b, n_qb, KV_WIDTH, Q_BLOCK)
    vswat = vswat.reshape(b, n_qb, KV_WIDTH, Q_BLOCK)
    gt = gt.reshape(b, n_qb, 3 * NSA_HEADS, Q_BLOCK)

    def pair_diag(w1):
        w = w1.reshape(CMP_BLOCK, HEAD_DIM, CMP_HIDDEN)
        z = jnp.zeros_like(w)
        return jnp.concatenate([jnp.concatenate([w, z], axis=2),
                                jnp.concatenate([z, w], axis=2)], axis=1).astype(BF16)

    twice = lambda a: jnp.tile(a.astype(F32), (1, 2))
    kc, vct = _compress(cc, b, twice(cmp_pos_k), twice(cmp_pos_v),
                        pair_diag(cmp_k_w1), twice(cmp_k_b1.reshape(1, -1)), cmp_k_w2.astype(BF16),
                        pair_diag(cmp_v_w1), twice(cmp_v_b1.reshape(1, -1)), cmp_v_w2.T.astype(BF16))

    sel_buckets, cmp_buckets = _bucket_tables(s)
    rb = rel_bias.astype(F32) * LOG2E
    rb_far = rb[REL_BUCKETS - 1:REL_BUCKETS, :]
    tsel = _bias_table(rb - rb_far, jnp.asarray(sel_buckets), LANES).reshape(N_TILES, LANES, QCOLS)
    tcmp = _bias_table(rb, jnp.asarray(cmp_buckets), LANES)
    far = jnp.repeat(rb_far.reshape(-1), Q_BLOCK).reshape(1, QCOLS)
    nsa_args = (qt, gt, kc, vct, kslc, vslct, kswa, vswat, tsel, tcmp, _overlap_t(s), far, s)
    o_nsa, guard = _nsa(*nsa_args, exact=False)
    o_nsa = lax.cond(jnp.logical_not(jnp.max(guard) <= 2.0 ** MAX_EXP2_EXCESS),
                     lambda: _nsa(*nsa_args, exact=True)[0], lambda: o_nsa)

    expand = jnp.asarray(np.kron(np.eye(SSM_HEADS), np.ones((1, SSM_HEAD_DIM))), dtype=BF16)
    y = _ssd(xbc, dt, conv_w.astype(F32), conv_b.reshape(1, -1).astype(F32),
             dt_bias.reshape(1, -1).astype(F32), a_log.reshape(1, -1).astype(F32),
             jnp.repeat(d_skip.astype(F32), SSM_HEAD_DIM).reshape(1, -1), expand, s)

    out = _epilogue(x2, o_nsa.reshape(t, NSA_WIDTH), y.reshape(t, SSM_WIDTH), nw,
                    wcol('z_nsa').astype(BF16), wcol('z_ssm').astype(BF16), wcol('merge_gate').astype(BF16),
                    w_out_nsa.astype(BF16), w_out_ssm.astype(BF16), w_out.astype(BF16),
                    ssm_norm_w.reshape(1, -1).astype(F32), out_norm_w.reshape(1, -1).astype(F32),
                    256 if t % 256 == 0 else Q_BLOCK)
    return out.reshape(b, s, d)


def kernel(x, norm_w, w_in, cmp_pos_k, cmp_pos_v, cmp_k_w1, cmp_k_b1, cmp_k_w2, cmp_v_w1, cmp_v_b1, cmp_v_w2,
           conv_w, conv_b, dt_bias, a_log, d_skip, ssm_norm_w, w_out_nsa, w_out_ssm, w_out, rel_bias,
           final_norm_w):
    depth = norm_w.shape[0]
    assert depth == 1, "the epilogue fuses the final norm into the single layer"
    return _layer(x, norm_w[0], w_in[0], cmp_pos_k[0], cmp_pos_v[0], cmp_k_w1[0], cmp_k_b1[0], cmp_k_w2[0],
                  cmp_v_w1[0], cmp_v_b1[0], cmp_v_w2[0], conv_w[0], conv_b[0], dt_bias[0], a_log[0],
                  d_skip[0], ssm_norm_w[0], w_out_nsa[0], w_out_ssm[0], w_out[0], rel_bias, final_norm_w)
```

```python
import functools
import math

import numpy as np
import jax
import jax.numpy as jnp
from jax import lax
from jax.experimental import pallas as pl
from jax.experimental.pallas import tpu as pltpu

F32 = jnp.float32
BF16 = jnp.bfloat16

D_MODEL = 1024
NSA_HEADS = 16
NSA_GROUPS = 4
NSA_HPG = NSA_HEADS // NSA_GROUPS
HEAD_DIM = 64
NSA_WIDTH = NSA_HEADS * HEAD_DIM
KV_WIDTH = NSA_GROUPS * HEAD_DIM
CMP_BLOCK = 32
CMP_STRIDE = 16
CMP_HIDDEN = 4 * HEAD_DIM
SEL_BLOCK = 64
SEL_TOPN = 8
WINDOW = 512
Q_BLOCK = 128
FORCE_SCORE = 1.0e4
REL_BUCKETS = 32
SSM_WIDTH = 2 * D_MODEL
SSM_HEAD_DIM = 64
SSM_HEADS = SSM_WIDTH // SSM_HEAD_DIM
SSM_GROUPS = 4
SSM_HPG = SSM_HEADS // SSM_GROUPS
SSM_STATE = 128
CONV_WIDTH = 4
SSM_CHUNK = 128
CONV_DIM = SSM_WIDTH + 2 * SSM_GROUPS * SSM_STATE
NORM_EPS = 1e-6

NEG = -1e30
LOG2E = math.log2(math.e)
LANES = 128
QCOLS = NSA_HEADS * Q_BLOCK
GCOLS = NSA_HPG * Q_BLOCK
AUG_DIM = 2 * HEAD_DIM
N_NEAR = 8
TILE_FAR = N_NEAR
TILE_WIN_OLD = N_NEAR + 1
TILE_MASKED = N_NEAR + 2
N_TILES = N_NEAR + 3
MASKED_BUCKET = REL_BUCKETS
SUM_ROWS = 16
MAX_EXP2_EXCESS = 100.0
STEP_TILES = 4
STEP = STEP_TILES * LANES
WIN_TILES = WINDOW // LANES + 1
SSD_CHUNKS_PER_STEP = 2
VMEM_LIMIT = 56 * 1024 * 1024


def _column_offsets():
    sizes = (('q', NSA_WIDTH), ('k_cmp', KV_WIDTH), ('v_cmp', KV_WIDTH), ('k_slc', KV_WIDTH),
             ('v_slc', KV_WIDTH), ('k_swa', KV_WIDTH), ('v_swa', KV_WIDTH), ('nsa_gate', 3 * NSA_HEADS),
             ('z_nsa', NSA_WIDTH), ('z_ssm', SSM_WIDTH), ('xbc', CONV_DIM), ('dt', SSM_HEADS),
             ('merge_gate', 2 * D_MODEL))
    out, lo = {}, 0
    for name, n in sizes:
        out[name] = (lo, lo + n)
        lo += n
    return out


def _t5_bucket_np(dist):
    dist = np.asarray(dist, dtype=np.int64)
    d = np.maximum(dist, 0)
    max_exact = REL_BUCKETS // 2
    large = np.full(d.shape, max_exact, dtype=np.int64)
    d8 = d.astype(object) ** 8
    for k in range(1, REL_BUCKETS - max_exact):
        large = large + (d8 >= 2 ** (32 + 3 * k)).astype(np.int64)
    bucket = np.where(d < max_exact, d, np.minimum(large, REL_BUCKETS - 1))
    return np.where(dist < 0, MASKED_BUCKET, bucket).astype(np.int32)


def _sigmoid(x):
    return 1.0 / (1.0 + jnp.exp(-x))


def _silu(x):
    return x * _sigmoid(x)


def _softplus(x):
    return jnp.maximum(x, 0.0) + jnp.log1p(jnp.exp(-jnp.abs(x)))


def _dot(a, b):
    return jnp.dot(a, b, preferred_element_type=F32)


def _dot_nt(a, b):
    return lax.dot_general(a, b, (((1,), (1,)), ((), ())), preferred_element_type=F32)


def _split3(x):
    hi = x.astype(BF16)
    r1 = x - hi.astype(F32)
    mid = r1.astype(BF16)
    lo = (r1 - mid.astype(F32)).astype(BF16)
    return hi, mid, lo


def _dot_exact_lhs(x, w01):
    hi, mid, lo = _split3(x)
    return _dot(hi, w01) + _dot(mid, w01) + _dot(lo, w01)


def _dot_exact_rhs(w01, x):
    hi, mid, lo = _split3(x)
    return _dot(w01, hi) + _dot(w01, mid) + _dot(w01, lo)


_NAT_OUTS = (('k_slc', NSA_GROUPS * AUG_DIM, BF16), ('k_swa', KV_WIDTH, BF16), ('kv_cmp', 2 * KV_WIDTH, F32),
             ('xbc', CONV_DIM, F32), ('dt', SSM_HEADS, F32))
_TR_OUTS = (('q', NSA_WIDTH, BF16), ('v_slc', KV_WIDTH, BF16), ('v_swa', KV_WIDTH, BF16),
            ('gate', 3 * NSA_HEADS, F32))


def _proj_kernel(x_ref, nw_ref, wn_ref, wt_ref, onehot_ref, *out_refs, n_sub):
    x = x_ref[...]
    ms = jnp.mean(x * x, axis=-1, keepdims=True)
    xn = (x * lax.rsqrt(ms + NORM_EPS) * nw_ref[...]).astype(BF16)
    lo = 0
    for (name, n, _), o_ref in zip(_NAT_OUTS, out_refs[:len(_NAT_OUTS)]):
        res = _dot(xn, wn_ref[:, lo:lo + n])
        if name == 'k_slc':
            res = res + onehot_ref[...]
        if name == 'kv_cmp':
            for j in range(n // LANES):
                o_ref[j] = res[:, j * LANES:(j + 1) * LANES].astype(o_ref.dtype)
        else:
            o_ref[...] = res.astype(o_ref.dtype)
        lo += n
    lo = 0
    for (_, n, _), o_ref in zip(_TR_OUTS, out_refs[len(_NAT_OUTS):]):
        res = _dot_nt(wt_ref[lo:lo + n, :], xn)
        for s in range(n_sub):
            o_ref[s] = res[:, s * LANES:(s + 1) * LANES].astype(o_ref.dtype)
        lo += n


def _proj(x2, nw, wn, wt, onehot, tm):
    t, d = x2.shape
    assert t % tm == 0 and tm % LANES == 0 and onehot.shape[1] == tm
    n_sub = tm // LANES
    n_pat = onehot.shape[0]
    const = lambda arr: pl.BlockSpec(arr.shape, lambda i: (0,) * arr.ndim, pipeline_mode=pl.Buffered(1))
    slabbed = lambda name: name == 'kv_cmp'
    out_specs = [pl.BlockSpec((n // LANES, tm, LANES), lambda i: (0, i, 0)) if slabbed(name)
                 else pl.BlockSpec((tm, n), lambda i: (i, 0)) for name, n, _ in _NAT_OUTS] + \
                [pl.BlockSpec((n_sub, n, LANES), lambda i: (i, 0, 0)) for _, n, _ in _TR_OUTS]
    out_shape = [jax.ShapeDtypeStruct((n // LANES, t, LANES) if slabbed(name) else (t, n), dt)
                 for name, n, dt in _NAT_OUTS] + \
                [jax.ShapeDtypeStruct((t // LANES, n, LANES), dt) for _, n, dt in _TR_OUTS]
    return pl.pallas_call(
        functools.partial(_proj_kernel, n_sub=n_sub),
        grid=(t // tm,),
        in_specs=[pl.BlockSpec((tm, d), lambda i: (i, 0)), const(nw), const(wn), const(wt),
                  pl.BlockSpec((None, tm, onehot.shape[2]), lambda i: (i % n_pat, 0, 0))],
        out_specs=out_specs,
        out_shape=out_shape,
        compiler_params=pltpu.CompilerParams(dimension_semantics=("parallel",),
                                             vmem_limit_bytes=VMEM_LIMIT),
        name="proj",
    )(x2, nw, wn, wt, onehot)


def _compress_kernel(raw_ref, posk_ref, posv_ref, w1k_ref, b1k_ref, w2k_ref,
                     w1v_ref, b1v_ref, w2vt_ref, kc_ref, vct_ref):
    nseg = kc_ref.shape[0]
    n_slab = raw_ref.shape[0]
    per_kv = n_slab // 2
    for j in range(n_slab):
        is_k = j < per_kv
        pos_ref, w1_ref, b1_ref = (posk_ref, w1k_ref, b1k_ref) if is_k else (posv_ref, w1v_ref, b1v_ref)
        first = jnp.zeros((nseg, 2 * CMP_HIDDEN), F32)
        second = jnp.zeros((nseg, 2 * CMP_HIDDEN), F32)
        for p in range(CMP_STRIDE):
            rows = raw_ref[j, pl.ds(p, nseg, stride=CMP_STRIDE), :]
            first = first + _dot((rows + pos_ref[p:p + 1, :]).astype(BF16), w1_ref[p])
            q = CMP_STRIDE + p
            second = second + _dot((rows + pos_ref[q:q + 1, :]).astype(BF16), w1_ref[q])
        hid = _silu(first + pltpu.roll(second, nseg - 1, 0) + b1_ref[...]).astype(BF16)
        for gi in range(2):
            g = 2 * (j % per_kv) + gi
            hg = hid[:, gi * CMP_HIDDEN:(gi + 1) * CMP_HIDDEN]
            if is_k:
                kc_ref[:, g * HEAD_DIM:(g + 1) * HEAD_DIM] = _dot(hg, w2k_ref[...]).astype(kc_ref.dtype)
            else:
                vct_ref[g * HEAD_DIM:(g + 1) * HEAD_DIM, :] = _dot_nt(w2vt_ref[...], hg).astype(vct_ref.dtype)


def _compress(raw, batch, posk, posv, w1k, b1k, w2k, w1v, b1v, w2vt):
    n_slab, t, lanes = raw.shape
    seq = t // batch
    nseg = seq // CMP_STRIDE
    assert n_slab * lanes == 2 * KV_WIDTH and lanes == 2 * HEAD_DIM
    const = lambda arr: pl.BlockSpec(arr.shape, lambda i: (0,) * arr.ndim, pipeline_mode=pl.Buffered(1))
    return pl.pallas_call(
        _compress_kernel,
        grid=(batch,),
        in_specs=[pl.BlockSpec((n_slab, seq, lanes), lambda i: (0, i, 0)),
                  const(posk), const(posv), const(w1k), const(b1k), const(w2k),
                  const(w1v), const(b1v), const(w2vt)],
        out_specs=[pl.BlockSpec((None, nseg, KV_WIDTH), lambda i: (i, 0, 0)),
                   pl.BlockSpec((None, KV_WIDTH, nseg), lambda i: (i, 0, 0))],
        out_shape=[jax.ShapeDtypeStruct((batch, nseg, KV_WIDTH), BF16),
                   jax.ShapeDtypeStruct((batch, KV_WIDTH, nseg), BF16)],
        compiler_params=pltpu.CompilerParams(dimension_semantics=("parallel",),
                                             vmem_limit_bytes=VMEM_LIMIT),
        name="compress",
    )(raw, posk, posv, w1k, b1k, w2k, w1v, b1v, w2vt)


def _nsa_kernel(qt_ref, gt_ref, kc_ref, vct_ref, kslc_ref, vslct_ref, kswa_ref, vswat_ref,
                tsel_ref, tcmp_ref, ovt_ref, far_ref, out_ref, guard_ref,
                qbd_ref, qaug_ref, m_ref, acc_ref, ot_ref, s_ref, p_ref, viol_ref, work_ref, chosen_ref,
                ocmp_ref, *, n_qb, n_cmp_rows, n_sel, top_n, exact):
    qb = pl.program_id(1)
    q0 = qb * Q_BLOCK

    @pl.when(qb == 0)
    def _():
        qbd_ref[...] = jnp.zeros_like(qbd_ref)
        qaug_ref[...] = jnp.zeros_like(qaug_ref)

    for g in range(NSA_GROUPS):
        for r in range(NSA_HPG):
            h = g * NSA_HPG + r
            q_head = qt_ref[h * HEAD_DIM:(h + 1) * HEAD_DIM, :]
            qbd_ref[g * HEAD_DIM:(g + 1) * HEAD_DIM, h * Q_BLOCK:(h + 1) * Q_BLOCK] = q_head
            qaug_ref[g, 0:HEAD_DIM, r * Q_BLOCK:(r + 1) * Q_BLOCK] = q_head

    def gslice(g):
        return slice(g * GCOLS, (g + 1) * GCOLS)

    def values_with_ones(vt_tiles, g):
        vt = jnp.concatenate([t[g * HEAD_DIM:(g + 1) * HEAD_DIM, :] for t in vt_tiles], axis=1)
        return jnp.concatenate([vt, jnp.ones((SUM_ROWS, vt.shape[1]), BF16)], axis=0)

    def normalized(a):
        return a[0:HEAD_DIM, :] * (1.0 / a[HEAD_DIM:HEAD_DIM + 1, :])

    def self_logit(kt):
        cols = []
        for h in range(NSA_HEADS):
            g = h // NSA_HPG
            prod = qt_ref[h * HEAD_DIM:(h + 1) * HEAD_DIM, :].astype(F32) * kt[g * HEAD_DIM:(g + 1) * HEAD_DIM, :]
            cols.append(jnp.sum(prod, axis=0, keepdims=True))
        return jnp.concatenate(cols, axis=1) - far_ref[...]

    diag_rows = pl.ds(pl.multiple_of(q0, Q_BLOCK), Q_BLOCK)

    u0 = pl.multiple_of(8 * (n_qb - 1 - qb), 8)

    def compress_and_select(n_rows, n_blk):
        j_iota = lax.broadcasted_iota(jnp.int32, (n_blk, Q_BLOCK), 0)
        j_f32 = j_iota.astype(F32)
        t_pos = q0 + lax.broadcasted_iota(jnp.int32, (n_blk, Q_BLOCK), 1)
        cur = t_pos // SEL_BLOCK
        valid = j_iota <= cur
        forced = (j_iota == 0) | (j_iota == cur) | (j_iota == cur - 1)

        logits = [_dot(kc_ref[0:n_rows, :], qbd_ref[:, gslice(g)]) for g in range(NSA_GROUPS)]
        for g in range(NSA_GROUPS):
            s = logits[g] + tcmp_ref[pl.ds(u0, n_rows), gslice(g)]
            m = jnp.maximum(jnp.max(s, axis=0, keepdims=True), 0.1 * NEG)
            p_ref[0:n_rows, gslice(g)] = jnp.exp2(s - m).astype(BF16)
        imps = []
        for g in range(NSA_GROUPS):
            e = p_ref[0:n_rows, gslice(g)]
            raw = _dot(values_with_ones([vct_ref[:, 0:n_rows]], g), e)
            total = raw[HEAD_DIM:HEAD_DIM + 1, :]
            inv = 1.0 / jnp.where(total == 0.0, 1.0, total)
            ocmp_ref[g] = raw[0:HEAD_DIM, :] * inv
            imp_heads = _dot(ovt_ref[0:n_blk, 0:n_rows], e) * inv
            imp = imp_heads[:, 0:Q_BLOCK]
            for r in range(1, NSA_HPG):
                imp = imp + imp_heads[:, r * Q_BLOCK:(r + 1) * Q_BLOCK]
            imps.append(imp)

        quota = top_n - (1 + (cur >= 1).astype(jnp.int32) + (cur >= 2).astype(jnp.int32))
        for g in range(NSA_GROUPS):
            work_ref[g, 0:n_blk, :] = jnp.where(forced, -2.0, jnp.where(valid, imps[g], -1.0))
            chosen_ref[g, 0:n_blk, :] = jnp.where(forced, 1.0, 0.0)

        def pick_round(index, limited):
            for g in range(NSA_GROUPS):
                work = work_ref[g, 0:n_blk, :]
                best = jnp.max(work, axis=0, keepdims=True)
                first = jnp.min(jnp.where(work == best, j_f32, float(n_blk)), axis=0, keepdims=True)
                hit = j_f32 == first
                if limited:
                    hit = hit & (quota > index)
                chosen_ref[g, 0:n_blk, :] = jnp.where(hit, 1.0, chosen_ref[g, 0:n_blk, :])
                work_ref[g, 0:n_blk, :] = jnp.where(hit, -2.0, work)

        max_forced = 3
        for index in range(top_n - max_forced):
            pick_round(index, False)

        @pl.when(q0 < (max_forced - 1) * SEL_BLOCK)
        def _():
            for index in range(top_n - max_forced, top_n - 1):
                pick_round(index, True)

        for g in range(NSA_GROUPS):
            addm = jnp.where((chosen_ref[g, 0:n_blk, :] > 0.5) & valid, 0.0, NEG).astype(BF16)
            qaug_ref[g, HEAD_DIM:HEAD_DIM + n_blk, :] = jnp.concatenate([addm] * NSA_HPG, axis=1)
            if n_blk < n_sel:
                qaug_ref[g, HEAD_DIM + n_blk:HEAD_DIM + n_sel, :] = jnp.full((n_sel - n_blk, GCOLS), NEG, BF16)

    first_half = qb < n_qb // 2

    @pl.when(first_half)
    def _():
        compress_and_select(n_cmp_rows // 2, n_sel // 2)

    @pl.when(jnp.logical_not(first_half))
    def _():
        compress_and_select(n_cmp_rows, n_sel)

    def window(exact):
        first_tile = jnp.maximum(qb - (WIN_TILES - 1), 0)
        kw = kswa_ref[pl.ds(pl.multiple_of(first_tile * LANES, LANES), WIN_TILES * LANES), :]
        win_vt_tiles = [vswat_ref[first_tile + t] for t in range(WIN_TILES)]
        win_ids = []
        for t in range(WIN_TILES):
            i = qb - (first_tile + t)
            win_ids.append(jnp.where(i < 0, TILE_MASKED, jnp.where(i == WIN_TILES - 1, TILE_WIN_OLD, i)))
        logits = [_dot(kw, qbd_ref[:, gslice(g)]) for g in range(NSA_GROUPS)]
        if not exact:
            m_fix = self_logit(kswa_ref[diag_rows, :].astype(F32).T)
        for g in range(NSA_GROUPS):
            m8 = None
            for t in range(WIN_TILES):
                rows = slice(t * LANES, (t + 1) * LANES)
                v = logits[g][rows, :] + tsel_ref[win_ids[t], :, gslice(g)]
                if exact:
                    s_ref[rows, gslice(g)] = v
                    vm = jnp.max(v.reshape(LANES // 8, 8, GCOLS), axis=0)
                else:
                    pt = jnp.exp2((v - m_fix[:, gslice(g)]).astype(BF16))
                    p_ref[rows, gslice(g)] = pt
                    vm = jnp.max(pt.reshape(LANES // 16, 16, GCOLS), axis=0)
                m8 = vm if m8 is None else jnp.maximum(m8, vm)
            if exact:
                m = jnp.max(m8, axis=0, keepdims=True)
                p_ref[0:WIN_TILES * LANES, gslice(g)] = jnp.exp2(s_ref[:, gslice(g)] - m).astype(BF16)
            else:
                viol_ref[:, gslice(g)] = jnp.maximum(viol_ref[:, gslice(g)],
                                                     jnp.max(m8.astype(F32), axis=0, keepdims=True))
        for g in range(NSA_GROUPS):
            acc_ref[g] = _dot(values_with_ones(win_vt_tiles, g), p_ref[0:WIN_TILES * LANES, gslice(g)])

    def sel_logits(tile0, n_tiles):
        key_rows = pl.ds(pl.multiple_of(tile0 * LANES, LANES), n_tiles * LANES)
        return [_dot(kslc_ref[key_rows, g * AUG_DIM:(g + 1) * AUG_DIM], qaug_ref[g])
                for g in range(NSA_GROUPS)]

    def sel_update(tile0, n_tiles, near, exact, logits, p_row0):
        vt_tiles = [vslct_ref[tile0 + t] for t in range(n_tiles)]
        if near:
            tile_ids = []
            for t in range(n_tiles):
                i = qb - (tile0 + t)
                tile_ids.append(jnp.where(i < 0, TILE_MASKED, jnp.minimum(i, TILE_FAR)))
        alphas = []
        for g in range(NSA_GROUPS):
            m_old = m_ref[:, gslice(g)]
            m8 = None
            for t in range(n_tiles):
                rows = slice(t * LANES, (t + 1) * LANES)
                v = logits[g][rows, :]
                if near:
                    v = v + tsel_ref[tile_ids[t], :, gslice(g)]
                if exact:
                    s_ref[rows, gslice(g)] = v
                    vm = jnp.max(v.reshape(LANES // 8, 8, GCOLS), axis=0)
                else:
                    pt = jnp.exp2((v - m_old).astype(BF16))
                    p_ref[p_row0 + t * LANES:p_row0 + (t + 1) * LANES, gslice(g)] = pt
                    vm = jnp.max(pt.reshape(LANES // 16, 16, GCOLS), axis=0)
                m8 = vm if m8 is None else jnp.maximum(m8, vm)
            if exact:
                m_new = jnp.maximum(m_old, jnp.max(m8, axis=0, keepdims=True))
                alphas.append(jnp.exp2(m_old - m_new))
                m_ref[:, gslice(g)] = m_new
                for t in range(n_tiles):
                    rows = slice(t * LANES, (t + 1) * LANES)
                    p_ref[p_row0 + t * LANES:p_row0 + (t + 1) * LANES, gslice(g)] = \
                        jnp.exp2(s_ref[rows, gslice(g)] - m_new).astype(BF16)
            else:
                viol_ref[:, gslice(g)] = jnp.maximum(viol_ref[:, gslice(g)],
                                                     jnp.max(m8.astype(F32), axis=0, keepdims=True))
        for g in range(NSA_GROUPS):
            pv = _dot(values_with_ones(vt_tiles, g), p_ref[p_row0:p_row0 + n_tiles * LANES, gslice(g)])
            if exact:
                acc_ref[g] = acc_ref[g] * alphas[g] + pv
            else:
                acc_ref[g] = acc_ref[g] + pv

    def sel_step(tile0, n_tiles, near, exact):
        sel_update(tile0, n_tiles, near, exact, sel_logits(tile0, n_tiles), 0)

    def sel_two_steps(tile_a, tile_b, near):
        la, lb = sel_logits(tile_a, STEP_TILES), sel_logits(tile_b, STEP_TILES)
        sel_update(tile_a, STEP_TILES, near, False, la, 0)
        sel_update(tile_b, STEP_TILES, near, False, lb, STEP)

    r_diag = qb // STEP_TILES
    n_far = jnp.maximum((qb - (N_NEAR - 1)) // STEP_TILES, 0)
    half = STEP_TILES // 2

    def selected(exact):
        if exact:
            m_ref[...] = jnp.full(m_ref.shape, NEG, F32)
        else:
            kt = kslc_ref[diag_rows, :].astype(F32).T
            m_ref[...] = self_logit(jnp.concatenate(
                [kt[g * AUG_DIM:g * AUG_DIM + HEAD_DIM, :] for g in range(NSA_GROUPS)], axis=0))
        acc_ref[...] = jnp.zeros_like(acc_ref)

        @pl.when(qb % STEP_TILES < half)
        def _():
            sel_step(STEP_TILES * r_diag, half, True, exact)

        @pl.when(qb % STEP_TILES >= half)
        def _():
            sel_step(STEP_TILES * r_diag, STEP_TILES, True, exact)

        def near_body(i, carry):
            sel_step(STEP_TILES * (r_diag - i), STEP_TILES, True, exact)
            return carry

        def far_body(r, carry):
            sel_step(STEP_TILES * r, STEP_TILES, False, exact)
            return carry

        def far_pair_body(k, carry):
            sel_two_steps(STEP_TILES * 2 * k, STEP_TILES * (2 * k + 1), False)
            return carry

        lax.fori_loop(1, r_diag - n_far + 1, near_body, 0)
        if exact:
            lax.fori_loop(0, n_far, far_body, 0)
        else:
            @pl.when(n_far % 2 == 1)
            def _():
                sel_step(STEP_TILES * (n_far - 1), STEP_TILES, False, False)

            lax.fori_loop(0, n_far // 2, far_pair_body, 0)

    viol_ref[...] = jnp.zeros_like(viol_ref)
    selected(exact)
    o_slc = [normalized(acc_ref[g]) for g in range(NSA_GROUPS)]
    window(exact)
    o_swa = [normalized(acc_ref[g]) for g in range(NSA_GROUPS)]
    guard_ref[...] = jnp.broadcast_to(jnp.max(viol_ref[...], axis=1, keepdims=True), guard_ref.shape)

    gates = _sigmoid(gt_ref[...])
    for g in range(NSA_GROUPS):
        def gate_row(c):
            return jnp.concatenate(
                [gates[c * NSA_HEADS + g * NSA_HPG + r:c * NSA_HEADS + g * NSA_HPG + r + 1, :]
                 for r in range(NSA_HPG)], axis=1)
        og = gate_row(0) * ocmp_ref[g] + gate_row(1) * o_slc[g] + gate_row(2) * o_swa[g]
        for r in range(NSA_HPG):
            h = g * NSA_HPG + r
            ot_ref[h * HEAD_DIM:(h + 1) * HEAD_DIM, :] = og[:, r * Q_BLOCK:(r + 1) * Q_BLOCK]
    out_ref[...] = ot_ref[...].T


def _nsa(qt, gt, kc, vct, kslc, vslct, kswa, vswat, tsel, tcmp, ovt, far, seq, exact):
    b = kc.shape[0]
    n_qb = seq // Q_BLOCK
    assert n_qb % STEP_TILES == 0 and n_qb >= WIN_TILES and WIN_TILES >= STEP_TILES
    n_cmp_rows = kc.shape[1]
    n_sel = seq // SEL_BLOCK
    assert n_sel <= AUG_DIM - HEAD_DIM
    top_n = min(SEL_TOPN, n_sel)
    assert top_n > 3
    n_gate = gt.shape[2]
    const = lambda arr: pl.BlockSpec(arr.shape, lambda i, j: (0,) * arr.ndim,
                                     pipeline_mode=pl.Buffered(1))
    kern = functools.partial(_nsa_kernel, n_qb=n_qb, n_cmp_rows=n_cmp_rows, n_sel=n_sel, top_n=top_n,
                             exact=exact)
    return pl.pallas_call(
        kern,
        grid=(b, n_qb),
        in_specs=[pl.BlockSpec((None, None, NSA_WIDTH, Q_BLOCK), lambda i, j: (i, j, 0, 0)),
                  pl.BlockSpec((None, None, n_gate, Q_BLOCK), lambda i, j: (i, j, 0, 0)),
                  pl.BlockSpec((None, n_cmp_rows, KV_WIDTH), lambda i, j: (i, 0, 0)),
                  pl.BlockSpec((None, KV_WIDTH, n_cmp_rows), lambda i, j: (i, 0, 0)),
                  pl.BlockSpec((None, seq, NSA_GROUPS * AUG_DIM), lambda i, j: (i, 0, 0)),
                  pl.BlockSpec((None, n_qb, KV_WIDTH, Q_BLOCK), lambda i, j: (i, 0, 0, 0)),
                  pl.BlockSpec((None, seq, KV_WIDTH), lambda i, j: (i, 0, 0)),
                  pl.BlockSpec((None, n_qb, KV_WIDTH, Q_BLOCK), lambda i, j: (i, 0, 0, 0)),
                  const(tsel), const(tcmp), const(ovt), const(far)],
        out_specs=[pl.BlockSpec((None, Q_BLOCK, NSA_WIDTH), lambda i, j: (i, j, 0)),
                   pl.BlockSpec((None, None, 8, LANES), lambda i, j: (i, j, 0, 0))],
        out_shape=[jax.ShapeDtypeStruct((b, seq, NSA_WIDTH), F32),
                   jax.ShapeDtypeStruct((b, n_qb, 8, LANES), F32)],
        scratch_shapes=[pltpu.VMEM((KV_WIDTH, QCOLS), BF16),
                        pltpu.VMEM((NSA_GROUPS, AUG_DIM, GCOLS), BF16),
                        pltpu.VMEM((1, QCOLS), F32),
                        pltpu.VMEM((NSA_GROUPS, HEAD_DIM + SUM_ROWS, GCOLS), F32),
                        pltpu.VMEM((NSA_WIDTH, Q_BLOCK), F32),
                        pltpu.VMEM((WIN_TILES * LANES, QCOLS), F32),
                        pltpu.VMEM((max(WIN_TILES * LANES, 2 * STEP), QCOLS), BF16),
                        pltpu.VMEM((1, QCOLS), F32),
                        pltpu.VMEM((NSA_GROUPS, n_sel, Q_BLOCK), F32),
                        pltpu.VMEM((NSA_GROUPS, n_sel, Q_BLOCK), F32),
                        pltpu.VMEM((NSA_GROUPS, HEAD_DIM, GCOLS), F32)],
        compiler_params=pltpu.CompilerParams(dimension_semantics=("parallel", "arbitrary"),
                                             vmem_limit_bytes=VMEM_LIMIT),
        name="nsa_exact" if exact else "nsa",
    )(qt, gt, kc, vct, kslc, vslct, kswa, vswat, tsel, tcmp, ovt, far)


def _ssd_kernel(xbc_ref, dt_ref, convw_ref, convb_ref, dtb_ref, a_ref, dskip_ref, expand_ref, y_ref,
                xp_ref, tail_ref, state_ref, slab_ref):
    @pl.when(pl.program_id(1) == 0)
    def _():
        tail_ref[...] = jnp.zeros_like(tail_ref)
        state_ref[...] = jnp.zeros_like(state_ref)

    n_chunks = xbc_ref.shape[0] // SSM_CHUNK
    fronts = [_ssd_front(ci, xbc_ref, dt_ref, convw_ref, convb_ref, dtb_ref, a_ref, expand_ref,
                         xp_ref.at[ci], tail_ref, slab_ref.at[ci]) for ci in range(n_chunks)]
    for ci in range(n_chunks):
        _ssd_back(ci, fronts[ci], dskip_ref, y_ref, state_ref, slab_ref.at[ci])


def _ssd_front(ci, xbc_ref, dt_ref, convw_ref, convb_ref, dtb_ref, a_ref, expand_ref, xp_ref, tail_ref, slab_ref):
    L = SSM_CHUNK
    NV = L // 8
    halo = CONV_WIDTH - 1
    tok = slice(ci * L, (ci + 1) * L)

    n_slab = CONV_DIM // LANES
    for s in range(n_slab):
        slab_ref[s] = xbc_ref[tok, s * LANES:(s + 1) * LANES]
    slab_ref[n_slab] = jnp.concatenate([dt_ref[tok, :], jnp.zeros((L, LANES - SSM_HEADS), F32)], axis=1)
    for v in range(NV):
        for s in range(n_slab):
            xp_ref[(halo + v) * 8:(halo + v + 1) * 8, s * LANES:(s + 1) * LANES] = \
                slab_ref[s, pl.ds(v, 8, stride=NV), :]
    sub = lax.broadcasted_iota(jnp.int32, (8, CONV_DIM), 0)
    for w in range(halo):
        cur = xp_ref[(NV + w) * 8:(NV + w + 1) * 8, :]
        prev = tail_ref[w * 8:(w + 1) * 8, :]
        xp_ref[w * 8:(w + 1) * 8, :] = pltpu.roll(jnp.where(sub == 7, prev, cur), 1, 0)
    tail_ref[...] = xp_ref[NV * 8:(NV + halo) * 8, :]
    conv = convb_ref[...]
    for k in range(CONV_WIDTH):
        conv = conv + convw_ref[k:k + 1, :] * xp_ref[8 * k:8 * k + L, :]
    xbc = _silu(conv)
    xs = xbc[:, 0:SSM_WIDTH]
    bm = xbc[:, SSM_WIDTH:SSM_WIDTH + SSM_GROUPS * SSM_STATE]
    cm = xbc[:, SSM_WIDTH + SSM_GROUPS * SSM_STATE:]

    dt_raw = jnp.concatenate([slab_ref[n_slab, pl.ds(v, 8, stride=NV), :] for v in range(NV)], axis=0)
    dt = _softplus(dt_raw[:, 0:SSM_HEADS] + dtb_ref[...])
    a_row = -jnp.exp(a_ref[...])
    row_i = lax.broadcasted_iota(jnp.int32, (L, L), 0)
    col_j = lax.broadcasted_iota(jnp.int32, (L, L), 1)
    token = lambda r: (r % 8) * NV + r // 8
    tril = token(row_i) >= token(col_j)
    tril01 = jnp.where(tril, 1.0, 0.0).astype(BF16)
    cs = _dot_exact_rhs(tril01, dt * a_row)
    cst = jnp.concatenate([cs, jnp.zeros((L, L - SSM_HEADS), F32)], axis=1).T

    expand = expand_ref[...]
    cs_x = _dot_exact_lhs(cs, expand)
    dt_x = _dot_exact_lhs(dt, expand)
    total_x = cs_x[L - 1:L, :]
    xdt = (xs * dt_x).astype(BF16)
    xw = (xs * (jnp.exp(total_x - cs_x) * dt_x)).astype(BF16)
    decay_out = jnp.exp(cs_x)
    decay_state = jnp.exp(total_x)
    return xs, bm, cm, cs, cst, tril, xdt, xw, decay_out, decay_state


def _ssd_back(ci, front, dskip_ref, y_ref, state_ref, slab_ref):
    xs, bm, cm, cs, cst, tril, xdt, xw, decay_out, decay_state = front
    L = SSM_CHUNK
    NV = L // 8
    tok = slice(ci * L, (ci + 1) * L)
    lane = lax.broadcasted_iota(jnp.int32, (L, LANES), 1)
    gw = SSM_HPG * SSM_HEAD_DIM
    for g in range(SSM_GROUPS):
        bg = bm[:, g * SSM_STATE:(g + 1) * SSM_STATE]
        cg = cm[:, g * SSM_STATE:(g + 1) * SSM_STATE].astype(BF16)
        cb = _dot_nt(cg, bg.astype(BF16))
        st = state_ref[g]
        y_g = _dot(cg, st.astype(BF16)) * decay_out[:, g * gw:(g + 1) * gw]
        state_ref[g] = st * decay_state[:, g * gw:(g + 1) * gw] + \
            _dot(bg.T.astype(BF16), xw[:, g * gw:(g + 1) * gw])
        pieces = []
        for pair in range(SSM_HPG // 2):
            ws = []
            for hh in range(2):
                h = g * SSM_HPG + 2 * pair + hh
                diff = cs[:, h:h + 1] - cst[h:h + 1, :]
                decay = jnp.where(tril, jnp.exp(jnp.where(tril, diff, 0.0)), 0.0)
                ws.append((cb * decay).astype(BF16))
            w_pair = jnp.concatenate(ws, axis=1)
            lo = g * gw + pair * LANES
            slab = xdt[:, lo:lo + LANES]
            zero = jnp.zeros_like(slab)
            x_bd = jnp.concatenate([jnp.where(lane < SSM_HEAD_DIM, slab, zero),
                                    jnp.where(lane >= SSM_HEAD_DIM, slab, zero)], axis=0)
            pieces.append(_dot(w_pair, x_bd))
        y_g = y_g + jnp.concatenate(pieces, axis=1) + \
            dskip_ref[:, g * gw:(g + 1) * gw] * xs[:, g * gw:(g + 1) * gw]
        for s in range(gw // LANES):
            for v in range(NV):
                slab_ref[s, pl.ds(v, 8, stride=NV), :] = y_g[v * 8:(v + 1) * 8, s * LANES:(s + 1) * LANES]
        for s in range(gw // LANES):
            y_ref[tok, g * gw + s * LANES:g * gw + (s + 1) * LANES] = slab_ref[s]


def _ssd(xbc, dt, convw, convb, dtb, a_log, dskip_x, expand, seq):
    b = xbc.shape[0]
    rows = SSD_CHUNKS_PER_STEP * SSM_CHUNK
    assert seq % rows == 0
    halo_rows = (CONV_WIDTH - 1) * 8
    const = lambda arr: pl.BlockSpec(arr.shape, lambda i, j: (0,) * arr.ndim)
    return pl.pallas_call(
        _ssd_kernel,
        grid=(b, seq // rows),
        in_specs=[pl.BlockSpec((None, rows, CONV_DIM), lambda i, j: (i, j, 0)),
                  pl.BlockSpec((None, rows, SSM_HEADS), lambda i, j: (i, j, 0)),
                  const(convw), const(convb), const(dtb), const(a_log), const(dskip_x), const(expand)],
        out_specs=pl.BlockSpec((None, rows, SSM_WIDTH), lambda i, j: (i, j, 0)),
        out_shape=jax.ShapeDtypeStruct((b, seq, SSM_WIDTH), F32),
        scratch_shapes=[pltpu.VMEM((SSD_CHUNKS_PER_STEP, SSM_CHUNK + halo_rows, CONV_DIM), F32),
                        pltpu.VMEM((halo_rows, CONV_DIM), F32),
                        pltpu.VMEM((SSM_GROUPS, SSM_STATE, SSM_HPG * SSM_HEAD_DIM), F32),
                        pltpu.VMEM((SSD_CHUNKS_PER_STEP, CONV_DIM // LANES + 1, SSM_CHUNK, LANES), F32)],
        compiler_params=pltpu.CompilerParams(dimension_semantics=("parallel", "arbitrary"),
                                             vmem_limit_bytes=VMEM_LIMIT),
        name="ssd",
    )(xbc, dt, convw, convb, dtb, a_log, dskip_x, expand)


def _epilogue_kernel(x_ref, o_ref, y_ref, nw_ref, wz1_ref, wz2_ref, wmg_ref, won_ref, wos_ref, wo_ref,
                     snw_ref, fnw_ref, out_ref):
    x = x_ref[...]
    ms = jnp.mean(x * x, axis=-1, keepdims=True)
    xn = (x * lax.rsqrt(ms + NORM_EPS) * nw_ref[...]).astype(BF16)
    u = (o_ref[...] * _silu(_dot(xn, wz1_ref[...]))).astype(BF16)
    h_nsa = _dot(u, won_ref[...])
    hh = y_ref[...] * _silu(_dot(xn, wz2_ref[...]))
    gw = SSM_WIDTH // SSM_GROUPS
    parts = []
    for g in range(SSM_GROUPS):
        hg = hh[:, g * gw:(g + 1) * gw]
        hg = hg * lax.rsqrt(jnp.mean(hg * hg, axis=-1, keepdims=True) + NORM_EPS)
        parts.append((hg * snw_ref[:, g * gw:(g + 1) * gw]).astype(BF16))
    h_ssm = _dot(jnp.concatenate(parts, axis=1), wos_ref[...])
    gate = _sigmoid(_dot(xn, wmg_ref[...]))
    mix = (gate[:, 0:D_MODEL] * h_nsa + gate[:, D_MODEL:] * h_ssm).astype(BF16)
    r = x + _dot(mix, wo_ref[...])
    ms2 = jnp.mean(r * r, axis=-1, keepdims=True)
    out_ref[...] = r * lax.rsqrt(ms2 + NORM_EPS) * fnw_ref[...]


def _epilogue(x2, o2, y2, nw, wz1, wz2, wmg, won, wos, wo, snw, fnw, tm):
    t, d = x2.shape
    assert t % tm == 0
    const = lambda arr: pl.BlockSpec(arr.shape, lambda i: (0,) * arr.ndim, pipeline_mode=pl.Buffered(1))
    return pl.pallas_call(
        _epilogue_kernel,
        grid=(t // tm,),
        in_specs=[pl.BlockSpec((tm, d), lambda i: (i, 0)),
                  pl.BlockSpec((tm, NSA_WIDTH), lambda i: (i, 0)),
                  pl.BlockSpec((tm, SSM_WIDTH), lambda i: (i, 0)),
                  const(nw), const(wz1), const(wz2), const(wmg), const(won), const(wos), const(wo),
                  const(snw), const(fnw)],
        out_specs=pl.BlockSpec((tm, d), lambda i: (i, 0)),
        out_shape=jax.ShapeDtypeStruct((t, d), F32),
        compiler_params=pltpu.CompilerParams(dimension_semantics=("parallel",),
                                             vmem_limit_bytes=VMEM_LIMIT),
        name="epilogue",
    )(x2, o2, y2, nw, wz1, wz2, wmg, won, wos, wo, snw, fnw)


def _table_kernel(rb_ref, bucket_ref, out_ref):
    bk = bucket_ref[...]
    for h in range(NSA_HEADS):
        acc = jnp.full(bk.shape, NEG, F32)
        for b in range(REL_BUCKETS):
            acc = jnp.where(bk == b, rb_ref[b, h], acc)
        out_ref[:, h * Q_BLOCK:(h + 1) * Q_BLOCK] = acc


def _bias_table(rb, buckets, tr):
    rows = buckets.shape[0]
    assert rows % tr == 0
    return pl.pallas_call(
        _table_kernel,
        grid=(rows // tr,),
        in_specs=[pl.BlockSpec(memory_space=pltpu.SMEM),
                  pl.BlockSpec((tr, Q_BLOCK), lambda i: (i, 0))],
        out_specs=pl.BlockSpec((tr, QCOLS), lambda i: (i, 0)),
        out_shape=jax.ShapeDtypeStruct((rows, QCOLS), F32),
        compiler_params=pltpu.CompilerParams(dimension_semantics=("parallel",)),
        name="bias_table",
    )(rb, buckets)


def _bucket_tables(seq):
    n_qb = seq // Q_BLOCK
    key = np.arange(LANES)[:, None]
    tok = np.arange(Q_BLOCK)[None, :]
    tiles = [_t5_bucket_np(LANES * o + tok - key) for o in range(N_NEAR)]
    tiles.append(np.full((LANES, Q_BLOCK), REL_BUCKETS - 1, np.int32))
    dwin = WINDOW + tok - key
    tiles.append(_t5_bucket_np(np.where(dwin < WINDOW, dwin, -1)))
    tiles.append(np.full((LANES, Q_BLOCK), MASKED_BUCKET, np.int32))
    assert len(tiles) == N_TILES and LANES * N_NEAR - (LANES - 1) >= 790
    n_cmp_rows = seq // CMP_STRIDE
    rows = 8 * (n_qb - 1) + n_cmp_rows
    rows_pad = -(-rows // LANES) * LANES
    u = np.arange(rows_pad)[:, None]
    cmp_tbl = _t5_bucket_np(tok - CMP_STRIDE * u + Q_BLOCK * (n_qb - 1) - (CMP_BLOCK - 1))
    return np.concatenate(tiles, axis=0), cmp_tbl


def _block_onehot(seq, tm):
    pos = np.arange(seq)
    pat = np.zeros((seq, NSA_GROUPS, AUG_DIM), np.float32)
    pat[pos, :, HEAD_DIM + pos // SEL_BLOCK] = 1.0
    return jnp.asarray(pat.reshape(seq // tm, tm, NSA_GROUPS * AUG_DIM), dtype=BF16)


def _overlap_t(seq):
    n_cmp_rows = seq // CMP_STRIDE
    n_sel = seq // SEL_BLOCK
    c_start = np.arange(n_cmp_rows)[None, :] * CMP_STRIDE
    s_start = np.arange(n_sel)[:, None] * SEL_BLOCK
    ov = (c_start < s_start + SEL_BLOCK) & (c_start + CMP_BLOCK > s_start)
    ov[:, n_cmp_rows - 1] = False
    return jnp.asarray(ov, dtype=BF16)


def _layer(x, norm_w, w_in, cmp_pos_k, cmp_pos_v, cmp_k_w1, cmp_k_b1, cmp_k_w2, cmp_v_w1, cmp_v_b1,
           cmp_v_w2, conv_w, conv_b, dt_bias, a_log, d_skip, ssm_norm_w, w_out_nsa, w_out_ssm, w_out,
           rel_bias, out_norm_w):
    b, s, d = x.shape
    assert d == D_MODEL and s % STEP == 0 and s >= 2 * WINDOW
    t = b * s
    n_qb = s // Q_BLOCK
    cols = _column_offsets()
    wcol = lambda name: w_in[:, cols[name][0]:cols[name][1]]
    x2 = x.reshape(t, d)
    nw = norm_w.reshape(1, d).astype(F32)

    gate_w = wcol('nsa_gate').reshape(d, NSA_HEADS, 3).transpose(0, 2, 1).reshape(d, 3 * NSA_HEADS)
    w_kslc = jnp.pad(wcol('k_slc').reshape(d, NSA_GROUPS, HEAD_DIM),
                     ((0, 0), (0, 0), (0, AUG_DIM - HEAD_DIM))).reshape(d, NSA_GROUPS * AUG_DIM)
    wn = jnp.concatenate([w_kslc, wcol('k_swa'), wcol('k_cmp'), wcol('v_cmp'), wcol('xbc'),
                          wcol('dt')], axis=1).astype(BF16)
    wt = jnp.concatenate([wcol('q') * (HEAD_DIM ** -0.5 * LOG2E), wcol('v_slc'), wcol('v_swa'), gate_w],
                         axis=1).T.astype(BF16)
    kslc, kswa, cc, xbc, dt, qt, vslct, vswat, gt = _proj(x2, nw, wn, wt, _block_onehot(s, STEP), STEP)
    kslc = kslc.reshape(b, s, NSA_GROUPS * AUG_DIM)
    kswa = kswa.reshape(b, s, KV_WIDTH)
    xbc = xbc.reshape(b, s, CONV_DIM)
    dt = dt.reshape(b, s, SSM_HEADS)
    qt = qt.reshape(b, n_qb, NSA_WIDTH, Q_BLOCK)
    vslct = vslct.reshape(b, n_qb, KV_WIDTH, Q_BLOCK)
    vswat = vswat.reshape(b, n_qb, KV_WIDTH, Q_BLOCK)
    gt = gt.reshape(b, n_qb, 3 * NSA_HEADS, Q_BLOCK)

    def pair_diag(w1):
        w = w1.reshape(CMP_BLOCK, HEAD_DIM, CMP_HIDDEN)
        z = jnp.zeros_like(w)
        return jnp.concatenate([jnp.concatenate([w, z], axis=2),
                                jnp.concatenate([z, w], axis=2)], axis=1).astype(BF16)

    twice = lambda a: jnp.tile(a.astype(F32), (1, 2))
    kc, vct = _compress(cc, b, twice(cmp_pos_k), twice(cmp_pos_v),
                        pair_diag(cmp_k_w1), twice(cmp_k_b1.reshape(1, -1)), cmp_k_w2.astype(BF16),
                        pair_diag(cmp_v_w1), twice(cmp_v_b1.reshape(1, -1)), cmp_v_w2.T.astype(BF16))

    sel_buckets, cmp_buckets = _bucket_tables(s)
    rb = rel_bias.astype(F32) * LOG2E
    rb_far = rb[REL_BUCKETS - 1:REL_BUCKETS, :]
    tsel = _bias_table(rb - rb_far, jnp.asarray(sel_buckets), LANES).reshape(N_TILES, LANES, QCOLS)
    tcmp = _bias_table(rb, jnp.asarray(cmp_buckets), LANES)
    far = jnp.repeat(rb_far.reshape(-1), Q_BLOCK).reshape(1, QCOLS)
    nsa_args = (qt, gt, kc, vct, kslc, vslct, kswa, vswat, tsel, tcmp, _overlap_t(s), far, s)
    o_nsa, guard = _nsa(*nsa_args, exact=False)
    o_nsa = lax.cond(jnp.logical_not(jnp.max(guard) <= 2.0 ** MAX_EXP2_EXCESS),
                     lambda: _nsa(*nsa_args, exact=True)[0], lambda: o_nsa)

    expand = jnp.asarray(np.kron(np.eye(SSM_HEADS), np.ones((1, SSM_HEAD_DIM))), dtype=BF16)
    y = _ssd(xbc, dt, conv_w.astype(F32), conv_b.reshape(1, -1).astype(F32),
             dt_bias.reshape(1, -1).astype(F32), a_log.reshape(1, -1).astype(F32),
             jnp.repeat(d_skip.astype(F32), SSM_HEAD_DIM).reshape(1, -1), expand, s)

    out = _epilogue(x2, o_nsa.reshape(t, NSA_WIDTH), y.reshape(t, SSM_WIDTH), nw,
                    wcol('z_nsa').astype(BF16), wcol('z_ssm').astype(BF16), wcol('merge_gate').astype(BF16),
                    w_out_nsa.astype(BF16), w_out_ssm.astype(BF16), w_out.astype(BF16),
                    ssm_norm_w.reshape(1, -1).astype(F32), out_norm_w.reshape(1, -1).astype(F32),
                    256 if t % 256 == 0 else Q_BLOCK)
    return out.reshape(b, s, d)


def kernel(x, norm_w, w_in, cmp_pos_k, cmp_pos_v, cmp_k_w1, cmp_k_b1, cmp_k_w2, cmp_v_w1, cmp_v_b1, cmp_v_w2,
           conv_w, conv_b, dt_bias, a_log, d_skip, ssm_norm_w, w_out_nsa, w_out_ssm, w_out, rel_bias,
           final_norm_w):
    depth = norm_w.shape[0]
    assert depth == 1, "the epilogue fuses the final norm into the single layer"
    return _layer(x, norm_w[0], w_in[0], cmp_pos_k[0], cmp_pos_v[0], cmp_k_w1[0], cmp_k_b1[0], cmp_k_w2[0],
                  cmp_v_w1[0], cmp_v_b1[0], cmp_v_w2[0], conv_w[0], conv_b[0], dt_bias[0], a_log[0],
                  d_skip[0], ssm_norm_w[0], w_out_nsa[0], w_out_ssm[0], w_out[0], rel_bias, final_norm_w)
```

```python
import functools
import math

import numpy as np
import jax
import jax.numpy as jnp
from jax import lax
from jax.experimental import pallas as pl
from jax.experimental.pallas import tpu as pltpu

F32 = jnp.float32
BF16 = jnp.bfloat16

D_MODEL = 1024
NSA_HEADS = 16
NSA_GROUPS = 4
NSA_HPG = NSA_HEADS // NSA_GROUPS
HEAD_DIM = 64
NSA_WIDTH = NSA_HEADS * HEAD_DIM
KV_WIDTH = NSA_GROUPS * HEAD_DIM
CMP_BLOCK = 32
CMP_STRIDE = 16
CMP_HIDDEN = 4 * HEAD_DIM
SEL_BLOCK = 64
SEL_TOPN = 8
WINDOW = 512
Q_BLOCK = 128
FORCE_SCORE = 1.0e4
REL_BUCKETS = 32
SSM_WIDTH = 2 * D_MODEL
SSM_HEAD_DIM = 64
SSM_HEADS = SSM_WIDTH // SSM_HEAD_DIM
SSM_GROUPS = 4
SSM_HPG = SSM_HEADS // SSM_GROUPS
SSM_STATE = 128
CONV_WIDTH = 4
SSM_CHUNK = 128
CONV_DIM = SSM_WIDTH + 2 * SSM_GROUPS * SSM_STATE
NORM_EPS = 1e-6

NEG = -1e30
LOG2E = math.log2(math.e)
LANES = 128
QCOLS = NSA_HEADS * Q_BLOCK
GCOLS = NSA_HPG * Q_BLOCK
AUG_DIM = 2 * HEAD_DIM
N_NEAR = 8
TILE_FAR = N_NEAR
TILE_WIN_OLD = N_NEAR + 1
TILE_MASKED = N_NEAR + 2
N_TILES = N_NEAR + 3
MASKED_BUCKET = REL_BUCKETS
SUM_ROWS = 16
MAX_EXP2_EXCESS = 100.0
STEP_TILES = 4
STEP = STEP_TILES * LANES
WIN_TILES = WINDOW // LANES + 1
SSD_CHUNKS_PER_STEP = 2
QK_LOOKAHEAD = 2
VMEM_LIMIT = 56 * 1024 * 1024


def _column_offsets():
    sizes = (('q', NSA_WIDTH), ('k_cmp', KV_WIDTH), ('v_cmp', KV_WIDTH), ('k_slc', KV_WIDTH),
             ('v_slc', KV_WIDTH), ('k_swa', KV_WIDTH), ('v_swa', KV_WIDTH), ('nsa_gate', 3 * NSA_HEADS),
             ('z_nsa', NSA_WIDTH), ('z_ssm', SSM_WIDTH), ('xbc', CONV_DIM), ('dt', SSM_HEADS),
             ('merge_gate', 2 * D_MODEL))
    out, lo = {}, 0
    for name, n in sizes:
        out[name] = (lo, lo + n)
        lo += n
    return out


def _t5_bucket_np(dist):
    dist = np.asarray(dist, dtype=np.int64)
    d = np.maximum(dist, 0)
    max_exact = REL_BUCKETS // 2
    large = np.full(d.shape, max_exact, dtype=np.int64)
    d8 = d.astype(object) ** 8
    for k in range(1, REL_BUCKETS - max_exact):
        large = large + (d8 >= 2 ** (32 + 3 * k)).astype(np.int64)
    bucket = np.where(d < max_exact, d, np.minimum(large, REL_BUCKETS - 1))
    return np.where(dist < 0, MASKED_BUCKET, bucket).astype(np.int32)


def _sigmoid(x):
    return 1.0 / (1.0 + jnp.exp(-x))


def _silu(x):
    return x * _sigmoid(x)


def _softplus(x):
    return jnp.maximum(x, 0.0) + jnp.log1p(jnp.exp(-jnp.abs(x)))


def _dot(a, b):
    return jnp.dot(a, b, preferred_element_type=F32)


def _dot_nt(a, b):
    return lax.dot_general(a, b, (((1,), (1,)), ((), ())), preferred_element_type=F32)


def _split3(x):
    hi = x.astype(BF16)
    r1 = x - hi.astype(F32)
    mid = r1.astype(BF16)
    lo = (r1 - mid.astype(F32)).astype(BF16)
    return hi, mid, lo


def _dot_exact_lhs(x, w01):
    hi, mid, lo = _split3(x)
    return _dot(hi, w01) + _dot(mid, w01) + _dot(lo, w01)


def _dot_exact_rhs(w01, x):
    hi, mid, lo = _split3(x)
    return _dot(w01, hi) + _dot(w01, mid) + _dot(w01, lo)


_NAT_OUTS = (('k_slc', NSA_GROUPS * AUG_DIM, BF16), ('k_swa', KV_WIDTH, BF16), ('kv_cmp', 2 * KV_WIDTH, F32),
             ('xbc', CONV_DIM, F32), ('dt', SSM_HEADS, F32))
_TR_OUTS = (('q', NSA_WIDTH, BF16), ('v_slc', KV_WIDTH, BF16), ('v_swa', KV_WIDTH, BF16),
            ('gate', 3 * NSA_HEADS, F32))


def _proj_kernel(x_ref, nw_ref, wn_ref, wt_ref, onehot_ref, *out_refs, n_sub):
    x = x_ref[...]
    ms = jnp.mean(x * x, axis=-1, keepdims=True)
    xn = (x * lax.rsqrt(ms + NORM_EPS) * nw_ref[...]).astype(BF16)
    lo = 0
    for (name, n, _), o_ref in zip(_NAT_OUTS, out_refs[:len(_NAT_OUTS)]):
        res = _dot(xn, wn_ref[:, lo:lo + n])
        if name == 'k_slc':
            res = res + onehot_ref[...]
        if name == 'kv_cmp':
            for j in range(n // LANES):
                o_ref[j] = res[:, j * LANES:(j + 1) * LANES].astype(o_ref.dtype)
        else:
            o_ref[...] = res.astype(o_ref.dtype)
        lo += n
    lo = 0
    for (_, n, _), o_ref in zip(_TR_OUTS, out_refs[len(_NAT_OUTS):]):
        res = _dot_nt(wt_ref[lo:lo + n, :], xn)
        for s in range(n_sub):
            o_ref[s] = res[:, s * LANES:(s + 1) * LANES].astype(o_ref.dtype)
        lo += n


def _proj(x2, nw, wn, wt, onehot, tm):
    t, d = x2.shape
    assert t % tm == 0 and tm % LANES == 0 and onehot.shape[1] == tm
    n_sub = tm // LANES
    n_pat = onehot.shape[0]
    const = lambda arr: pl.BlockSpec(arr.shape, lambda i: (0,) * arr.ndim, pipeline_mode=pl.Buffered(1))
    slabbed = lambda name: name == 'kv_cmp'
    out_specs = [pl.BlockSpec((n // LANES, tm, LANES), lambda i: (0, i, 0)) if slabbed(name)
                 else pl.BlockSpec((tm, n), lambda i: (i, 0)) for name, n, _ in _NAT_OUTS] + \
                [pl.BlockSpec((n_sub, n, LANES), lambda i: (i, 0, 0)) for _, n, _ in _TR_OUTS]
    out_shape = [jax.ShapeDtypeStruct((n // LANES, t, LANES) if slabbed(name) else (t, n), dt)
                 for name, n, dt in _NAT_OUTS] + \
                [jax.ShapeDtypeStruct((t // LANES, n, LANES), dt) for _, n, dt in _TR_OUTS]
    return pl.pallas_call(
        functools.partial(_proj_kernel, n_sub=n_sub),
        grid=(t // tm,),
        in_specs=[pl.BlockSpec((tm, d), lambda i: (i, 0)), const(nw), const(wn), const(wt),
                  pl.BlockSpec((None, tm, onehot.shape[2]), lambda i: (i % n_pat, 0, 0))],
        out_specs=out_specs,
        out_shape=out_shape,
        compiler_params=pltpu.CompilerParams(dimension_semantics=("parallel",),
                                             vmem_limit_bytes=VMEM_LIMIT),
        name="proj",
    )(x2, nw, wn, wt, onehot)


def _compress_kernel(raw_ref, posk_ref, posv_ref, w1k_ref, b1k_ref, w2k_ref,
                     w1v_ref, b1v_ref, w2vt_ref, kc_ref, vct_ref):
    nseg = kc_ref.shape[0]
    n_slab = raw_ref.shape[0]
    per_kv = n_slab // 2
    for j in range(n_slab):
        is_k = j < per_kv
        pos_ref, w1_ref, b1_ref = (posk_ref, w1k_ref, b1k_ref) if is_k else (posv_ref, w1v_ref, b1v_ref)
        first = jnp.zeros((nseg, 2 * CMP_HIDDEN), F32)
        second = jnp.zeros((nseg, 2 * CMP_HIDDEN), F32)
        for p in range(CMP_STRIDE):
            rows = raw_ref[j, pl.ds(p, nseg, stride=CMP_STRIDE), :]
            first = first + _dot((rows + pos_ref[p:p + 1, :]).astype(BF16), w1_ref[p])
            q = CMP_STRIDE + p
            second = second + _dot((rows + pos_ref[q:q + 1, :]).astype(BF16), w1_ref[q])
        hid = _silu(first + pltpu.roll(second, nseg - 1, 0) + b1_ref[...]).astype(BF16)
        for gi in range(2):
            g = 2 * (j % per_kv) + gi
            hg = hid[:, gi * CMP_HIDDEN:(gi + 1) * CMP_HIDDEN]
            if is_k:
                kc_ref[:, g * HEAD_DIM:(g + 1) * HEAD_DIM] = _dot(hg, w2k_ref[...]).astype(kc_ref.dtype)
            else:
                vct_ref[g * HEAD_DIM:(g + 1) * HEAD_DIM, :] = _dot_nt(w2vt_ref[...], hg).astype(vct_ref.dtype)


def _compress(raw, batch, posk, posv, w1k, b1k, w2k, w1v, b1v, w2vt):
    n_slab, t, lanes = raw.shape
    seq = t // batch
    nseg = seq // CMP_STRIDE
    assert n_slab * lanes == 2 * KV_WIDTH and lanes == 2 * HEAD_DIM
    const = lambda arr: pl.BlockSpec(arr.shape, lambda i: (0,) * arr.ndim, pipeline_mode=pl.Buffered(1))
    return pl.pallas_call(
        _compress_kernel,
        grid=(batch,),
        in_specs=[pl.BlockSpec((n_slab, seq, lanes), lambda i: (0, i, 0)),
                  const(posk), const(posv), const(w1k), const(b1k), const(w2k),
                  const(w1v), const(b1v), const(w2vt)],
        out_specs=[pl.BlockSpec((None, nseg, KV_WIDTH), lambda i: (i, 0, 0)),
                   pl.BlockSpec((None, KV_WIDTH, nseg), lambda i: (i, 0, 0))],
        out_shape=[jax.ShapeDtypeStruct((batch, nseg, KV_WIDTH), BF16),
                   jax.ShapeDtypeStruct((batch, KV_WIDTH, nseg), BF16)],
        compiler_params=pltpu.CompilerParams(dimension_semantics=("parallel",),
                                             vmem_limit_bytes=VMEM_LIMIT),
        name="compress",
    )(raw, posk, posv, w1k, b1k, w2k, w1v, b1v, w2vt)


def _nsa_kernel(qt_ref, gt_ref, kc_ref, vct_ref, kslc_ref, vslct_ref, kswa_ref, vswat_ref,
                tsel_ref, tcmp_ref, ovt_ref, far_ref, out_ref, guard_ref,
                qbd_ref, qaug_ref, m_ref, acc_ref, ot_ref, s_ref, p_ref, viol_ref, work_ref, chosen_ref,
                ocmp_ref, *, n_qb, n_cmp_rows, n_sel, top_n, exact):
    qb = pl.program_id(1)
    q0 = qb * Q_BLOCK

    @pl.when(qb == 0)
    def _():
        qbd_ref[...] = jnp.zeros_like(qbd_ref)
        qaug_ref[...] = jnp.zeros_like(qaug_ref)

    for g in range(NSA_GROUPS):
        for r in range(NSA_HPG):
            h = g * NSA_HPG + r
            q_head = qt_ref[h * HEAD_DIM:(h + 1) * HEAD_DIM, :]
            qbd_ref[g * HEAD_DIM:(g + 1) * HEAD_DIM, h * Q_BLOCK:(h + 1) * Q_BLOCK] = q_head
            qaug_ref[g, 0:HEAD_DIM, r * Q_BLOCK:(r + 1) * Q_BLOCK] = q_head

    def gslice(g):
        return slice(g * GCOLS, (g + 1) * GCOLS)

    def values_with_ones(vt_tiles, g):
        vt = jnp.concatenate([t[g * HEAD_DIM:(g + 1) * HEAD_DIM, :] for t in vt_tiles], axis=1)
        return jnp.concatenate([vt, jnp.ones((SUM_ROWS, vt.shape[1]), BF16)], axis=0)

    def normalized(a):
        return a[0:HEAD_DIM, :] * (1.0 / a[HEAD_DIM:HEAD_DIM + 1, :])

    def self_logit(kt):
        cols = []
        for h in range(NSA_HEADS):
            g = h // NSA_HPG
            prod = qt_ref[h * HEAD_DIM:(h + 1) * HEAD_DIM, :].astype(F32) * kt[g * HEAD_DIM:(g + 1) * HEAD_DIM, :]
            cols.append(jnp.sum(prod, axis=0, keepdims=True))
        return jnp.concatenate(cols, axis=1) - far_ref[...]

    diag_rows = pl.ds(pl.multiple_of(q0, Q_BLOCK), Q_BLOCK)

    u0 = pl.multiple_of(8 * (n_qb - 1 - qb), 8)

    def compress_and_select(n_rows, n_blk):
        j_iota = lax.broadcasted_iota(jnp.int32, (n_blk, Q_BLOCK), 0)
        j_f32 = j_iota.astype(F32)
        t_pos = q0 + lax.broadcasted_iota(jnp.int32, (n_blk, Q_BLOCK), 1)
        cur = t_pos // SEL_BLOCK
        valid = j_iota <= cur
        forced = (j_iota == 0) | (j_iota == cur) | (j_iota == cur - 1)

        logits = [_dot(kc_ref[0:n_rows, :], qbd_ref[:, gslice(g)]) for g in range(NSA_GROUPS)]
        for g in range(NSA_GROUPS):
            s = logits[g] + tcmp_ref[pl.ds(u0, n_rows), gslice(g)]
            m = jnp.maximum(jnp.max(s, axis=0, keepdims=True), 0.1 * NEG)
            p_ref[0:n_rows, gslice(g)] = jnp.exp2(s - m).astype(BF16)
        imps = []
        for g in range(NSA_GROUPS):
            e = p_ref[0:n_rows, gslice(g)]
            raw = _dot(values_with_ones([vct_ref[:, 0:n_rows]], g), e)
            total = raw[HEAD_DIM:HEAD_DIM + 1, :]
            inv = 1.0 / jnp.where(total == 0.0, 1.0, total)
            ocmp_ref[g] = raw[0:HEAD_DIM, :] * inv
            imp_heads = _dot(ovt_ref[0:n_blk, 0:n_rows], e) * inv
            imp = imp_heads[:, 0:Q_BLOCK]
            for r in range(1, NSA_HPG):
                imp = imp + imp_heads[:, r * Q_BLOCK:(r + 1) * Q_BLOCK]
            imps.append(imp)

        quota = top_n - (1 + (cur >= 1).astype(jnp.int32) + (cur >= 2).astype(jnp.int32))
        for g in range(NSA_GROUPS):
            work_ref[g, 0:n_blk, :] = jnp.where(forced, -2.0, jnp.where(valid, imps[g], -1.0))
            chosen_ref[g, 0:n_blk, :] = jnp.where(forced, 1.0, 0.0)

        def pick_round(index, limited):
            for g in range(NSA_GROUPS):
                work = work_ref[g, 0:n_blk, :]
                best = jnp.max(work, axis=0, keepdims=True)
                first = jnp.min(jnp.where(work == best, j_f32, float(n_blk)), axis=0, keepdims=True)
                hit = j_f32 == first
                if limited:
                    hit = hit & (quota > index)
                chosen_ref[g, 0:n_blk, :] = jnp.where(hit, 1.0, chosen_ref[g, 0:n_blk, :])
                work_ref[g, 0:n_blk, :] = jnp.where(hit, -2.0, work)

        max_forced = 3
        for index in range(top_n - max_forced):
            pick_round(index, False)

        @pl.when(q0 < (max_forced - 1) * SEL_BLOCK)
        def _():
            for index in range(top_n - max_forced, top_n - 1):
                pick_round(index, True)

        for g in range(NSA_GROUPS):
            addm = jnp.where((chosen_ref[g, 0:n_blk, :] > 0.5) & valid, 0.0, NEG).astype(BF16)
            qaug_ref[g, HEAD_DIM:HEAD_DIM + n_blk, :] = jnp.concatenate([addm] * NSA_HPG, axis=1)
            if n_blk < n_sel:
                qaug_ref[g, HEAD_DIM + n_blk:HEAD_DIM + n_sel, :] = jnp.full((n_sel - n_blk, GCOLS), NEG, BF16)

    first_half = qb < n_qb // 2

    @pl.when(first_half)
    def _():
        compress_and_select(n_cmp_rows // 2, n_sel // 2)

    @pl.when(jnp.logical_not(first_half))
    def _():
        compress_and_select(n_cmp_rows, n_sel)

    def pipelined(items, logits_of, finish):
        pending = {i: logits_of(items[i]) for i in range(min(QK_LOOKAHEAD, len(items)))}
        for i, item in enumerate(items):
            finish(item, pending.pop(i))
            if i + QK_LOOKAHEAD < len(items):
                pending[i + QK_LOOKAHEAD] = logits_of(items[i + QK_LOOKAHEAD])

    def window(exact):
        first_tile = jnp.maximum(qb - (WIN_TILES - 1), 0)
        kw = kswa_ref[pl.ds(pl.multiple_of(first_tile * LANES, LANES), WIN_TILES * LANES), :]
        win_vt_tiles = [vswat_ref[first_tile + t] for t in range(WIN_TILES)]
        win_ids = []
        for t in range(WIN_TILES):
            i = qb - (first_tile + t)
            win_ids.append(jnp.where(i < 0, TILE_MASKED, jnp.where(i == WIN_TILES - 1, TILE_WIN_OLD, i)))
        if not exact:
            m_fix = self_logit(kswa_ref[diag_rows, :].astype(F32).T)

        def finish(g, logits):
            m8 = None
            for t in range(WIN_TILES):
                rows = slice(t * LANES, (t + 1) * LANES)
                v = logits[rows, :] + tsel_ref[win_ids[t], :, gslice(g)]
                if exact:
                    s_ref[rows, gslice(g)] = v
                    vm = jnp.max(v.reshape(LANES // 8, 8, GCOLS), axis=0)
                else:
                    pt = jnp.exp2((v - m_fix[:, gslice(g)]).astype(BF16))
                    p_ref[rows, gslice(g)] = pt
                    vm = jnp.max(pt.reshape(LANES // 16, 16, GCOLS), axis=0)
                m8 = vm if m8 is None else jnp.maximum(m8, vm)
            if exact:
                m = jnp.max(m8, axis=0, keepdims=True)
                p_ref[0:WIN_TILES * LANES, gslice(g)] = jnp.exp2(s_ref[:, gslice(g)] - m).astype(BF16)
            else:
                viol_ref[:, gslice(g)] = jnp.maximum(viol_ref[:, gslice(g)],
                                                     jnp.max(m8.astype(F32), axis=0, keepdims=True))
            acc_ref[g] = _dot(values_with_ones(win_vt_tiles, g), p_ref[0:WIN_TILES * LANES, gslice(g)])

        pipelined(list(range(NSA_GROUPS)), lambda g: _dot(kw, qbd_ref[:, gslice(g)]), finish)

    def sel_steps(steps, near, exact):
        def logits_of(item):
            (tile0, n_tiles, _), g = item
            key_rows = pl.ds(pl.multiple_of(tile0 * LANES, LANES), n_tiles * LANES)
            return _dot(kslc_ref[key_rows, g * AUG_DIM:(g + 1) * AUG_DIM], qaug_ref[g])

        def finish(item, logits):
            (tile0, n_tiles, p_row0), g = item
            vt_tiles = [vslct_ref[tile0 + t] for t in range(n_tiles)]
            m_old = m_ref[:, gslice(g)]
            m8 = None
            for t in range(n_tiles):
                rows = slice(t * LANES, (t + 1) * LANES)
                v = logits[rows, :]
                if near:
                    i = qb - (tile0 + t)
                    tile_id = jnp.where(i < 0, TILE_MASKED, jnp.minimum(i, TILE_FAR))
                    v = v + tsel_ref[tile_id, :, gslice(g)]
                if exact:
                    s_ref[rows, gslice(g)] = v
                    vm = jnp.max(v.reshape(LANES // 8, 8, GCOLS), axis=0)
                else:
                    pt = jnp.exp2((v - m_old).astype(BF16))
                    p_ref[p_row0 + t * LANES:p_row0 + (t + 1) * LANES, gslice(g)] = pt
                    vm = jnp.max(pt.reshape(LANES // 16, 16, GCOLS), axis=0)
                m8 = vm if m8 is None else jnp.maximum(m8, vm)
            if exact:
                m_new = jnp.maximum(m_old, jnp.max(m8, axis=0, keepdims=True))
                alpha = jnp.exp2(m_old - m_new)
                m_ref[:, gslice(g)] = m_new
                for t in range(n_tiles):
                    rows = slice(t * LANES, (t + 1) * LANES)
                    p_ref[p_row0 + t * LANES:p_row0 + (t + 1) * LANES, gslice(g)] = \
                        jnp.exp2(s_ref[rows, gslice(g)] - m_new).astype(BF16)
            else:
                viol_ref[:, gslice(g)] = jnp.maximum(viol_ref[:, gslice(g)],
                                                     jnp.max(m8.astype(F32), axis=0, keepdims=True))
            pv = _dot(values_with_ones(vt_tiles, g), p_ref[p_row0:p_row0 + n_tiles * LANES, gslice(g)])
            if exact:
                acc_ref[g] = acc_ref[g] * alpha + pv
            else:
                acc_ref[g] = acc_ref[g] + pv

        pipelined([(step, g) for step in steps for g in range(NSA_GROUPS)], logits_of, finish)

    def sel_step(tile0, n_tiles, near, exact):
        sel_steps([(tile0, n_tiles, 0)], near, exact)

    def sel_two_steps(tile_a, tile_b, near):
        sel_steps([(tile_a, STEP_TILES, 0), (tile_b, STEP_TILES, STEP)], near, False)

    r_diag = qb // STEP_TILES
    n_far = jnp.maximum((qb - (N_NEAR - 1)) // STEP_TILES, 0)
    half = STEP_TILES // 2

    def selected(exact):
        if exact:
            m_ref[...] = jnp.full(m_ref.shape, NEG, F32)
        else:
            kt = kslc_ref[diag_rows, :].astype(F32).T
            m_ref[...] = self_logit(jnp.concatenate(
                [kt[g * AUG_DIM:g * AUG_DIM + HEAD_DIM, :] for g in range(NSA_GROUPS)], axis=0))
        acc_ref[...] = jnp.zeros_like(acc_ref)

        @pl.when(qb % STEP_TILES < half)
        def _():
            sel_step(STEP_TILES * r_diag, half, True, exact)

        @pl.when(qb % STEP_TILES >= half)
        def _():
            sel_step(STEP_TILES * r_diag, STEP_TILES, True, exact)

        def near_body(i, carry):
            sel_step(STEP_TILES * (r_diag - i), STEP_TILES, True, exact)
            return carry

        def far_body(r, carry):
            sel_step(STEP_TILES * r, STEP_TILES, False, exact)
            return carry

        def far_pair_body(k, carry):
            sel_two_steps(STEP_TILES * 2 * k, STEP_TILES * (2 * k + 1), False)
            return carry

        lax.fori_loop(1, r_diag - n_far + 1, near_body, 0)
        if exact:
            lax.fori_loop(0, n_far, far_body, 0)
        else:
            @pl.when(n_far % 2 == 1)
            def _():
                sel_step(STEP_TILES * (n_far - 1), STEP_TILES, False, False)

            lax.fori_loop(0, n_far // 2, far_pair_body, 0)

    viol_ref[...] = jnp.zeros_like(viol_ref)
    selected(exact)
    o_slc = [normalized(acc_ref[g]) for g in range(NSA_GROUPS)]
    window(exact)
    o_swa = [normalized(acc_ref[g]) for g in range(NSA_GROUPS)]
    guard_ref[...] = jnp.broadcast_to(jnp.max(viol_ref[...], axis=1, keepdims=True), guard_ref.shape)

    gates = _sigmoid(gt_ref[...])
    for g in range(NSA_GROUPS):
        def gate_row(c):
            return jnp.concatenate(
                [gates[c * NSA_HEADS + g * NSA_HPG + r:c * NSA_HEADS + g * NSA_HPG + r + 1, :]
                 for r in range(NSA_HPG)], axis=1)
        og = gate_row(0) * ocmp_ref[g] + gate_row(1) * o_slc[g] + gate_row(2) * o_swa[g]
        for r in range(NSA_HPG):
            h = g * NSA_HPG + r
            ot_ref[h * HEAD_DIM:(h + 1) * HEAD_DIM, :] = og[:, r * Q_BLOCK:(r + 1) * Q_BLOCK]
    out_ref[...] = ot_ref[...].T


def _nsa(qt, gt, kc, vct, kslc, vslct, kswa, vswat, tsel, tcmp, ovt, far, seq, exact):
    b = kc.shape[0]
    n_qb = seq // Q_BLOCK
    assert n_qb % STEP_TILES == 0 and n_qb >= WIN_TILES and WIN_TILES >= STEP_TILES
    n_cmp_rows = kc.shape[1]
    n_sel = seq // SEL_BLOCK
    assert n_sel <= AUG_DIM - HEAD_DIM
    top_n = min(SEL_TOPN, n_sel)
    assert top_n > 3
    n_gate = gt.shape[2]
    const = lambda arr: pl.BlockSpec(arr.shape, lambda i, j: (0,) * arr.ndim,
                                     pipeline_mode=pl.Buffered(1))
    kern = functools.partial(_nsa_kernel, n_qb=n_qb, n_cmp_rows=n_cmp_rows, n_sel=n_sel, top_n=top_n,
                             exact=exact)
    return pl.pallas_call(
        kern,
        grid=(b, n_qb),
        in_specs=[pl.BlockSpec((None, None, NSA_WIDTH, Q_BLOCK), lambda i, j: (i, j, 0, 0)),
                  pl.BlockSpec((None, None, n_gate, Q_BLOCK), lambda i, j: (i, j, 0, 0)),
                  pl.BlockSpec((None, n_cmp_rows, KV_WIDTH), lambda i, j: (i, 0, 0)),
                  pl.BlockSpec((None, KV_WIDTH, n_cmp_rows), lambda i, j: (i, 0, 0)),
                  pl.BlockSpec((None, seq, NSA_GROUPS * AUG_DIM), lambda i, j: (i, 0, 0)),
                  pl.BlockSpec((None, n_qb, KV_WIDTH, Q_BLOCK), lambda i, j: (i, 0, 0, 0)),
                  pl.BlockSpec((None, seq, KV_WIDTH), lambda i, j: (i, 0, 0)),
                  pl.BlockSpec((None, n_qb, KV_WIDTH, Q_BLOCK), lambda i, j: (i, 0, 0, 0)),
                  const(tsel), const(tcmp), const(ovt), const(far)],
        out_specs=[pl.BlockSpec((None, Q_BLOCK, NSA_WIDTH), lambda i, j: (i, j, 0)),
                   pl.BlockSpec((None, None, 8, LANES), lambda i, j: (i, j, 0, 0))],
        out_shape=[jax.ShapeDtypeStruct((b, seq, NSA_WIDTH), F32),
                   jax.ShapeDtypeStruct((b, n_qb, 8, LANES), F32)],
        scratch_shapes=[pltpu.VMEM((KV_WIDTH, QCOLS), BF16),
                        pltpu.VMEM((NSA_GROUPS, AUG_DIM, GCOLS), BF16),
                        pltpu.VMEM((1, QCOLS), F32),
                        pltpu.VMEM((NSA_GROUPS, HEAD_DIM + SUM_ROWS, GCOLS), F32),
                        pltpu.VMEM((NSA_WIDTH, Q_BLOCK), F32),
                        pltpu.VMEM((WIN_TILES * LANES, QCOLS), F32),
                        pltpu.VMEM((max(WIN_TILES * LANES, 2 * STEP), QCOLS), BF16),
                        pltpu.VMEM((1, QCOLS), F32),
                        pltpu.VMEM((NSA_GROUPS, n_sel, Q_BLOCK), F32),
                        pltpu.VMEM((NSA_GROUPS, n_sel, Q_BLOCK), F32),
                        pltpu.VMEM((NSA_GROUPS, HEAD_DIM, GCOLS), F32)],
        compiler_params=pltpu.CompilerParams(dimension_semantics=("parallel", "arbitrary"),
                                             vmem_limit_bytes=VMEM_LIMIT),
        name="nsa_exact" if exact else "nsa",
    )(qt, gt, kc, vct, kslc, vslct, kswa, vswat, tsel, tcmp, ovt, far)


def _ssd_kernel(xbc_ref, dt_ref, convw_ref, convb_ref, dtb_ref, a_ref, dskip_ref, expand_ref, y_ref,
                xp_ref, tail_ref, state_ref, slab_ref):
    @pl.when(pl.program_id(1) == 0)
    def _():
        tail_ref[...] = jnp.zeros_like(tail_ref)
        state_ref[...] = jnp.zeros_like(state_ref)

    n_chunks = xbc_ref.shape[0] // SSM_CHUNK
    fronts = [_ssd_front(ci, xbc_ref, dt_ref, convw_ref, convb_ref, dtb_ref, a_ref, expand_ref,
                         xp_ref.at[ci], tail_ref, slab_ref.at[ci]) for ci in range(n_chunks)]
    for ci in range(n_chunks):
        _ssd_back(ci, fronts[ci], dskip_ref, y_ref, state_ref, slab_ref.at[ci])


def _ssd_front(ci, xbc_ref, dt_ref, convw_ref, convb_ref, dtb_ref, a_ref, expand_ref, xp_ref, tail_ref, slab_ref):
    L = SSM_CHUNK
    NV = L // 8
    halo = CONV_WIDTH - 1
    tok = slice(ci * L, (ci + 1) * L)

    n_slab = CONV_DIM // LANES
    for s in range(n_slab):
        slab_ref[s] = xbc_ref[tok, s * LANES:(s + 1) * LANES]
    slab_ref[n_slab] = jnp.concatenate([dt_ref[tok, :], jnp.zeros((L, LANES - SSM_HEADS), F32)], axis=1)
    for v in range(NV):
        for s in range(n_slab):
            xp_ref[(halo + v) * 8:(halo + v + 1) * 8, s * LANES:(s + 1) * LANES] = \
                slab_ref[s, pl.ds(v, 8, stride=NV), :]
    sub = lax.broadcasted_iota(jnp.int32, (8, CONV_DIM), 0)
    for w in range(halo):
        cur = xp_ref[(NV + w) * 8:(NV + w + 1) * 8, :]
        prev = tail_ref[w * 8:(w + 1) * 8, :]
        xp_ref[w * 8:(w + 1) * 8, :] = pltpu.roll(jnp.where(sub == 7, prev, cur), 1, 0)
    tail_ref[...] = xp_ref[NV * 8:(NV + halo) * 8, :]
    conv = convb_ref[...]
    for k in range(CONV_WIDTH):
        conv = conv + convw_ref[k:k + 1, :] * xp_ref[8 * k:8 * k + L, :]
    xbc = _silu(conv)
    xs = xbc[:, 0:SSM_WIDTH]
    bm = xbc[:, SSM_WIDTH:SSM_WIDTH + SSM_GROUPS * SSM_STATE]
    cm = xbc[:, SSM_WIDTH + SSM_GROUPS * SSM_STATE:]

    dt_raw = jnp.concatenate([slab_ref[n_slab, pl.ds(v, 8, stride=NV), :] for v in range(NV)], axis=0)
    dt = _softplus(dt_raw[:, 0:SSM_HEADS] + dtb_ref[...])
    a_row = -jnp.exp(a_ref[...])
    row_i = lax.broadcasted_iota(jnp.int32, (L, L), 0)
    col_j = lax.broadcasted_iota(jnp.int32, (L, L), 1)
    token = lambda r: (r % 8) * NV + r // 8
    tril = token(row_i) >= token(col_j)
    tril01 = jnp.where(tril, 1.0, 0.0).astype(BF16)
    cs = _dot_exact_rhs(tril01, dt * a_row)
    cst = jnp.concatenate([cs, jnp.zeros((L, L - SSM_HEADS), F32)], axis=1).T

    expand = expand_ref[...]
    cs_x = _dot_exact_lhs(cs, expand)
    dt_x = _dot_exact_lhs(dt, expand)
    total_x = cs_x[L - 1:L, :]
    xdt = (xs * dt_x).astype(BF16)
    xw = (xs * (jnp.exp(total_x - cs_x) * dt_x)).astype(BF16)
    decay_out = jnp.exp(cs_x)
    decay_state = jnp.exp(total_x)
    return xs, bm, cm, cs, cst, tril, xdt, xw, decay_out, decay_state


def _ssd_back(ci, front, dskip_ref, y_ref, state_ref, slab_ref):
    xs, bm, cm, cs, cst, tril, xdt, xw, decay_out, decay_state = front
    L = SSM_CHUNK
    NV = L // 8
    tok = slice(ci * L, (ci + 1) * L)
    lane = lax.broadcasted_iota(jnp.int32, (L, LANES), 1)
    gw = SSM_HPG * SSM_HEAD_DIM
    for g in range(SSM_GROUPS):
        bg = bm[:, g * SSM_STATE:(g + 1) * SSM_STATE]
        cg = cm[:, g * SSM_STATE:(g + 1) * SSM_STATE].astype(BF16)
        cb = _dot_nt(cg, bg.astype(BF16))
        st = state_ref[g]
        y_g = _dot(cg, st.astype(BF16)) * decay_out[:, g * gw:(g + 1) * gw]
        state_ref[g] = st * decay_state[:, g * gw:(g + 1) * gw] + \
            _dot(bg.T.astype(BF16), xw[:, g * gw:(g + 1) * gw])
        pieces = []
        for pair in range(SSM_HPG // 2):
            ws = []
            for hh in range(2):
                h = g * SSM_HPG + 2 * pair + hh
                diff = cs[:, h:h + 1] - cst[h:h + 1, :]
                decay = jnp.where(tril, jnp.exp(jnp.where(tril, diff, 0.0)), 0.0)
                ws.append((cb * decay).astype(BF16))
            w_pair = jnp.concatenate(ws, axis=1)
            lo = g * gw + pair * LANES
            slab = xdt[:, lo:lo + LANES]
            zero = jnp.zeros_like(slab)
            x_bd = jnp.concatenate([jnp.where(lane < SSM_HEAD_DIM, slab, zero),
                                    jnp.where(lane >= SSM_HEAD_DIM, slab, zero)], axis=0)
            pieces.append(_dot(w_pair, x_bd))
        y_g = y_g + jnp.concatenate(pieces, axis=1) + \
            dskip_ref[:, g * gw:(g + 1) * gw] * xs[:, g * gw:(g + 1) * gw]
        for s in range(gw // LANES):
            for v in range(NV):
                slab_ref[s, pl.ds(v, 8, stride=NV), :] = y_g[v * 8:(v + 1) * 8, s * LANES:(s + 1) * LANES]
        for s in range(gw // LANES):
            y_ref[tok, g * gw + s * LANES:g * gw + (s + 1) * LANES] = slab_ref[s]


def _ssd(xbc, dt, convw, convb, dtb, a_log, dskip_x, expand, seq):
    b = xbc.shape[0]
    rows = SSD_CHUNKS_PER_STEP * SSM_CHUNK
    assert seq % rows == 0
    halo_rows = (CONV_WIDTH - 1) * 8
    const = lambda arr: pl.BlockSpec(arr.shape, lambda i, j: (0,) * arr.ndim)
    return pl.pallas_call(
        _ssd_kernel,
        grid=(b, seq // rows),
        in_specs=[pl.BlockSpec((None, rows, CONV_DIM), lambda i, j: (i, j, 0)),
                  pl.BlockSpec((None, rows, SSM_HEADS), lambda i, j: (i, j, 0)),
                  const(convw), const(convb), const(dtb), const(a_log), const(dskip_x), const(expand)],
        out_specs=pl.BlockSpec((None, rows, SSM_WIDTH), lambda i, j: (i, j, 0)),
        out_shape=jax.ShapeDtypeStruct((b, seq, SSM_WIDTH), F32),
        scratch_shapes=[pltpu.VMEM((SSD_CHUNKS_PER_STEP, SSM_CHUNK + halo_rows, CONV_DIM), F32),
                        pltpu.VMEM((halo_rows, CONV_DIM), F32),
                        pltpu.VMEM((SSM_GROUPS, SSM_STATE, SSM_HPG * SSM_HEAD_DIM), F32),
                        pltpu.VMEM((SSD_CHUNKS_PER_STEP, CONV_DIM // LANES + 1, SSM_CHUNK, LANES), F32)],
        compiler_params=pltpu.CompilerParams(dimension_semantics=("parallel", "arbitrary"),
                                             vmem_limit_bytes=VMEM_LIMIT),
        name="ssd",
    )(xbc, dt, convw, convb, dtb, a_log, dskip_x, expand)


def _epilogue_kernel(x_ref, o_ref, y_ref, nw_ref, wz1_ref, wz2_ref, wmg_ref, won_ref, wos_ref, wo_ref,
                     snw_ref, fnw_ref, out_ref):
    x = x_ref[...]
    ms = jnp.mean(x * x, axis=-1, keepdims=True)
    xn = (x * lax.rsqrt(ms + NORM_EPS) * nw_ref[...]).astype(BF16)
    u = (o_ref[...] * _silu(_dot(xn, wz1_ref[...]))).astype(BF16)
    h_nsa = _dot(u, won_ref[...])
    hh = y_ref[...] * _silu(_dot(xn, wz2_ref[...]))
    gw = SSM_WIDTH // SSM_GROUPS
    parts = []
    for g in range(SSM_GROUPS):
        hg = hh[:, g * gw:(g + 1) * gw]
        hg = hg * lax.rsqrt(jnp.mean(hg * hg, axis=-1, keepdims=True) + NORM_EPS)
        parts.append((hg * snw_ref[:, g * gw:(g + 1) * gw]).astype(BF16))
    h_ssm = _dot(jnp.concatenate(parts, axis=1), wos_ref[...])
    gate = _sigmoid(_dot(xn, wmg_ref[...]))
    mix = (gate[:, 0:D_MODEL] * h_nsa + gate[:, D_MODEL:] * h_ssm).astype(BF16)
    r = x + _dot(mix, wo_ref[...])
    ms2 = jnp.mean(r * r, axis=-1, keepdims=True)
    out_ref[...] = r * lax.rsqrt(ms2 + NORM_EPS) * fnw_ref[...]


def _epilogue(x2, o2, y2, nw, wz1, wz2, wmg, won, wos, wo, snw, fnw, tm):
    t, d = x2.shape
    assert t % tm == 0
    const = lambda arr: pl.BlockSpec(arr.shape, lambda i: (0,) * arr.ndim, pipeline_mode=pl.Buffered(1))
    return pl.pallas_call(
        _epilogue_kernel,
        grid=(t // tm,),
        in_specs=[pl.BlockSpec((tm, d), lambda i: (i, 0)),
                  pl.BlockSpec((tm, NSA_WIDTH), lambda i: (i, 0)),
                  pl.BlockSpec((tm, SSM_WIDTH), lambda i: (i, 0)),
                  const(nw), const(wz1), const(wz2), const(wmg), const(won), const(wos), const(wo),
                  const(snw), const(fnw)],
        out_specs=pl.BlockSpec((tm, d), lambda i: (i, 0)),
        out_shape=jax.ShapeDtypeStruct((t, d), F32),
        compiler_params=pltpu.CompilerParams(dimension_semantics=("parallel",),
                                             vmem_limit_bytes=VMEM_LIMIT),
        name="epilogue",
    )(x2, o2, y2, nw, wz1, wz2, wmg, won, wos, wo, snw, fnw)


def _table_kernel(rb_ref, bucket_ref, out_ref):
    bk = bucket_ref[...]
    for h in range(NSA_HEADS):
        acc = jnp.full(bk.shape, NEG, F32)
        for b in range(REL_BUCKETS):
            acc = jnp.where(bk == b, rb_ref[b, h], acc)
        out_ref[:, h * Q_BLOCK:(h + 1) * Q_BLOCK] = acc


def _bias_table(rb, buckets, tr):
    rows = buckets.shape[0]
    assert rows % tr == 0
    return pl.pallas_call(
        _table_kernel,
        grid=(rows // tr,),
        in_specs=[pl.BlockSpec(memory_space=pltpu.SMEM),
                  pl.BlockSpec((tr, Q_BLOCK), lambda i: (i, 0))],
        out_specs=pl.BlockSpec((tr, QCOLS), lambda i: (i, 0)),
        out_shape=jax.ShapeDtypeStruct((rows, QCOLS), F32),
        compiler_params=pltpu.CompilerParams(dimension_semantics=("parallel",)),
        name="bias_table",
    )(rb, buckets)


def _bucket_tables(seq):
    n_qb = seq // Q_BLOCK
    key = np.arange(LANES)[:, None]
    tok = np.arange(Q_BLOCK)[None, :]
    tiles = [_t5_bucket_np(LANES * o + tok - key) for o in range(N_NEAR)]
    tiles.append(np.full((LANES, Q_BLOCK), REL_BUCKETS - 1, np.int32))
    dwin = WINDOW + tok - key
    tiles.append(_t5_bucket_np(np.where(dwin < WINDOW, dwin, -1)))
    tiles.append(np.full((LANES, Q_BLOCK), MASKED_BUCKET, np.int32))
    assert len(tiles) == N_TILES and LANES * N_NEAR - (LANES - 1) >= 790
    n_cmp_rows = seq // CMP_STRIDE
    rows = 8 * (n_qb - 1) + n_cmp_rows
    rows_pad = -(-rows // LANES) * LANES
    u = np.arange(rows_pad)[:, None]
    cmp_tbl = _t5_bucket_np(tok - CMP_STRIDE * u + Q_BLOCK * (n_qb - 1) - (CMP_BLOCK - 1))
    return np.concatenate(tiles, axis=0), cmp_tbl


def _block_onehot(seq, tm):
    pos = np.arange(seq)
    pat = np.zeros((seq, NSA_GROUPS, AUG_DIM), np.float32)
    pat[pos, :, HEAD_DIM + pos // SEL_BLOCK] = 1.0
    return jnp.asarray(pat.reshape(seq // tm, tm, NSA_GROUPS * AUG_DIM), dtype=BF16)


def _overlap_t(seq):
    n_cmp_rows = seq // CMP_STRIDE
    n_sel = seq // SEL_BLOCK
    c_start = np.arange(n_cmp_rows)[None, :] * CMP_STRIDE
    s_start = np.arange(n_sel)[:, None] * SEL_BLOCK
    ov = (c_start < s_start + SEL_BLOCK) & (c_start + CMP_BLOCK > s_start)
    ov[:, n_cmp_rows - 1] = False
    return jnp.asarray(ov, dtype=BF16)


def _layer(x, norm_w, w_in, cmp_pos_k, cmp_pos_v, cmp_k_w1, cmp_k_b1, cmp_k_w2, cmp_v_w1, cmp_v_b1,
           cmp_v_w2, conv_w, conv_b, dt_bias, a_log, d_skip, ssm_norm_w, w_out_nsa, w_out_ssm, w_out,
           rel_bias, out_norm_w):
    b, s, d = x.shape
    assert d == D_MODEL and s % STEP == 0 and s >= 2 * WINDOW
    t = b * s
    n_qb = s // Q_BLOCK
    cols = _column_offsets()
    wcol = lambda name: w_in[:, cols[name][0]:cols[name][1]]
    x2 = x.reshape(t, d)
    nw = norm_w.reshape(1, d).astype(F32)

    gate_w = wcol('nsa_gate').reshape(d, NSA_HEADS, 3).transpose(0, 2, 1).reshape(d, 3 * NSA_HEADS)
    w_kslc = jnp.pad(wcol('k_slc').reshape(d, NSA_GROUPS, HEAD_DIM),
                     ((0, 0), (0, 0), (0, AUG_DIM - HEAD_DIM))).reshape(d, NSA_GROUPS * AUG_DIM)
    wn = jnp.concatenate([w_kslc, wcol('k_swa'), wcol('k_cmp'), wcol('v_cmp'), wcol('xbc'),
                          wcol('dt')], axis=1).astype(BF16)
    wt = jnp.concatenate([wcol('q') * (HEAD_DIM ** -0.5 * LOG2E), wcol('v_slc'), wcol('v_swa'), gate_w],
                         axis=1).T.astype(BF16)
    kslc, kswa, cc, xbc, dt, qt, vslct, vswat, gt = _proj(x2, nw, wn, wt, _block_onehot(s, STEP), STEP)
    kslc = kslc.reshape(b, s, NSA_GROUPS * AUG_DIM)
    kswa = kswa.reshape(b, s, KV_WIDTH)
    xbc = xbc.reshape(b, s, CONV_DIM)
    dt = dt.reshape(b, s, SSM_HEADS)
    qt = qt.reshape(b, n_qb, NSA_WIDTH, Q_BLOCK)
    vslct = vslct.reshape(b, n_qb, KV_WIDTH, Q_BLOCK)
    vswat = vswat.reshape(b, n_qb, KV_WIDTH, Q_BLOCK)
    gt = gt.reshape(b, n_qb, 3 * NSA_HEADS, Q_BLOCK)

    def pair_diag(w1):
        w = w1.reshape(CMP_BLOCK, HEAD_DIM, CMP_HIDDEN)
        z = jnp.zeros_like(w)
        return jnp.concatenate([jnp.concatenate([w, z], axis=2),
                                jnp.concatenate([z, w], axis=2)], axis=1).astype(BF16)

    twice = lambda a: jnp.tile(a.astype(F32), (1, 2))
    kc, vct = _compress(cc, b, twice(cmp_pos_k), twice(cmp_pos_v),
                        pair_diag(cmp_k_w1), twice(cmp_k_b1.reshape(1, -1)), cmp_k_w2.astype(BF16),
                        pair_diag(cmp_v_w1), twice(cmp_v_b1.reshape(1, -1)), cmp_v_w2.T.astype(BF16))

    sel_buckets, cmp_buckets = _bucket_tables(s)
    rb = rel_bias.astype(F32) * LOG2E
    rb_far = rb[REL_BUCKETS - 1:REL_BUCKETS, :]
    tsel = _bias_table(rb - rb_far, jnp.asarray(sel_buckets), LANES).reshape(N_TILES, LANES, QCOLS)
    tcmp = _bias_table(rb, jnp.asarray(cmp_buckets), LANES)
    far = jnp.repeat(rb_far.reshape(-1), Q_BLOCK).reshape(1, QCOLS)
    nsa_args = (qt, gt, kc, vct, kslc, vslct, kswa, vswat, tsel, tcmp, _overlap_t(s), far, s)
    o_nsa, guard = _nsa(*nsa_args, exact=False)
    o_nsa = lax.cond(jnp.logical_not(jnp.max(guard) <= 2.0 ** MAX_EXP2_EXCESS),
                     lambda: _nsa(*nsa_args, exact=True)[0], lambda: o_nsa)

    expand = jnp.asarray(np.kron(np.eye(SSM_HEADS), np.ones((1, SSM_HEAD_DIM))), dtype=BF16)
    y = _ssd(xbc, dt, conv_w.astype(F32), conv_b.reshape(1, -1).astype(F32),
             dt_bias.reshape(1, -1).astype(F32), a_log.reshape(1, -1).astype(F32),
             jnp.repeat(d_skip.astype(F32), SSM_HEAD_DIM).reshape(1, -1), expand, s)

    out = _epilogue(x2, o_nsa.reshape(t, NSA_WIDTH), y.reshape(t, SSM_WIDTH), nw,
                    wcol('z_nsa').astype(BF16), wcol('z_ssm').astype(BF16), wcol('merge_gate').astype(BF16),
                    w_out_nsa.astype(BF16), w_out_ssm.astype(BF16), w_out.astype(BF16),
                    ssm_norm_w.reshape(1, -1).astype(F32), out_norm_w.reshape(1, -1).astype(F32),
                    256 if t % 256 == 0 else Q_BLOCK)
    return out.reshape(b, s, d)


def kernel(x, norm_w, w_in, cmp_pos_k, cmp_pos_v, cmp_k_w1, cmp_k_b1, cmp_k_w2, cmp_v_w1, cmp_v_b1, cmp_v_w2,
           conv_w, conv_b, dt_bias, a_log, d_skip, ssm_norm_w, w_out_nsa, w_out_ssm, w_out, rel_bias,
           final_norm_w):
    depth = norm_w.shape[0]
    assert depth == 1, "the epilogue fuses the final norm into the single layer"
    return _layer(x, norm_w[0], w_in[0], cmp_pos_k[0], cmp_pos_v[0], cmp_k_w1[0], cmp_k_b1[0], cmp_k_w2[0],
                  cmp_v_w1[0], cmp_v_b1[0], cmp_v_w2[0], conv_w[0], conv_b[0], dt_bias[0], a_log[0],
                  d_skip[0], ssm_norm_w[0], w_out_nsa[0], w_out_ssm[0], w_out[0], rel_bias, final_norm_w)
```

```python
import functools
import math

import numpy as np
import jax
import jax.numpy as jnp
from jax import lax
from jax.experimental import pallas as pl
from jax.experimental.pallas import tpu as pltpu

F32 = jnp.float32
BF16 = jnp.bfloat16

D_MODEL = 1024
NSA_HEADS = 16
NSA_GROUPS = 4
NSA_HPG = NSA_HEADS // NSA_GROUPS
HEAD_DIM = 64
NSA_WIDTH = NSA_HEADS * HEAD_DIM
KV_WIDTH = NSA_GROUPS * HEAD_DIM
CMP_BLOCK = 32
CMP_STRIDE = 16
CMP_HIDDEN = 4 * HEAD_DIM
SEL_BLOCK = 64
SEL_TOPN = 8
WINDOW = 512
Q_BLOCK = 128
FORCE_SCORE = 1.0e4
REL_BUCKETS = 32
SSM_WIDTH = 2 * D_MODEL
SSM_HEAD_DIM = 64
SSM_HEADS = SSM_WIDTH // SSM_HEAD_DIM
SSM_GROUPS = 4
SSM_HPG = SSM_HEADS // SSM_GROUPS
SSM_STATE = 128
CONV_WIDTH = 4
SSM_CHUNK = 128
CONV_DIM = SSM_WIDTH + 2 * SSM_GROUPS * SSM_STATE
NORM_EPS = 1e-6

NEG = -1e30
LOG2E = math.log2(math.e)
LANES = 128
QCOLS = NSA_HEADS * Q_BLOCK
GCOLS = NSA_HPG * Q_BLOCK
AUG_DIM = 2 * HEAD_DIM
N_NEAR = 8
TILE_FAR = N_NEAR
TILE_WIN_OLD = N_NEAR + 1
TILE_MASKED = N_NEAR + 2
N_TILES = N_NEAR + 3
MASKED_BUCKET = REL_BUCKETS
SUM_ROWS = 16
MAX_EXP2_EXCESS = 100.0
STEP_TILES = 4
STEP = STEP_TILES * LANES
WIN_TILES = WINDOW // LANES + 1
SSD_CHUNKS_PER_STEP = 2
QK_LOOKAHEAD = 3
VMEM_LIMIT = 56 * 1024 * 1024


def _column_offsets():
    sizes = (('q', NSA_WIDTH), ('k_cmp', KV_WIDTH), ('v_cmp', KV_WIDTH), ('k_slc', KV_WIDTH),
             ('v_slc', KV_WIDTH), ('k_swa', KV_WIDTH), ('v_swa', KV_WIDTH), ('nsa_gate', 3 * NSA_HEADS),
             ('z_nsa', NSA_WIDTH), ('z_ssm', SSM_WIDTH), ('xbc', CONV_DIM), ('dt', SSM_HEADS),
             ('merge_gate', 2 * D_MODEL))
    out, lo = {}, 0
    for name, n in sizes:
        out[name] = (lo, lo + n)
        lo += n
    return out


def _t5_bucket_np(dist):
    dist = np.asarray(dist, dtype=np.int64)
    d = np.maximum(dist, 0)
    max_exact = REL_BUCKETS // 2
    large = np.full(d.shape, max_exact, dtype=np.int64)
    d8 = d.astype(object) ** 8
    for k in range(1, REL_BUCKETS - max_exact):
        large = large + (d8 >= 2 ** (32 + 3 * k)).astype(np.int64)
    bucket = np.where(d < max_exact, d, np.minimum(large, REL_BUCKETS - 1))
    return np.where(dist < 0, MASKED_BUCKET, bucket).astype(np.int32)


def _sigmoid(x):
    return 1.0 / (1.0 + jnp.exp(-x))


def _silu(x):
    return x * _sigmoid(x)


def _softplus(x):
    return jnp.maximum(x, 0.0) + jnp.log1p(jnp.exp(-jnp.abs(x)))


def _dot(a, b):
    return jnp.dot(a, b, preferred_element_type=F32)


def _dot_nt(a, b):
    return lax.dot_general(a, b, (((1,), (1,)), ((), ())), preferred_element_type=F32)


def _split3(x):
    hi = x.astype(BF16)
    r1 = x - hi.astype(F32)
    mid = r1.astype(BF16)
    lo = (r1 - mid.astype(F32)).astype(BF16)
    return hi, mid, lo


def _dot_exact_lhs(x, w01):
    hi, mid, lo = _split3(x)
    return _dot(hi, w01) + _dot(mid, w01) + _dot(lo, w01)


def _dot_exact_rhs(w01, x):
    hi, mid, lo = _split3(x)
    return _dot(w01, hi) + _dot(w01, mid) + _dot(w01, lo)


_NAT_OUTS = (('k_slc', NSA_GROUPS * AUG_DIM, BF16), ('k_swa', KV_WIDTH, BF16), ('kv_cmp', 2 * KV_WIDTH, F32),
             ('xbc', CONV_DIM, F32), ('dt', SSM_HEADS, F32))
_TR_OUTS = (('q', NSA_WIDTH, BF16), ('v_slc', KV_WIDTH, BF16), ('v_swa', KV_WIDTH, BF16),
            ('gate', 3 * NSA_HEADS, F32))


def _proj_kernel(x_ref, nw_ref, wn_ref, wt_ref, onehot_ref, *out_refs, n_sub):
    x = x_ref[...]
    ms = jnp.mean(x * x, axis=-1, keepdims=True)
    xn = (x * lax.rsqrt(ms + NORM_EPS) * nw_ref[...]).astype(BF16)
    lo = 0
    for (name, n, _), o_ref in zip(_NAT_OUTS, out_refs[:len(_NAT_OUTS)]):
        res = _dot(xn, wn_ref[:, lo:lo + n])
        if name == 'k_slc':
            res = res + onehot_ref[...]
        if name == 'kv_cmp':
            for j in range(n // LANES):
                o_ref[j] = res[:, j * LANES:(j + 1) * LANES].astype(o_ref.dtype)
        else:
            o_ref[...] = res.astype(o_ref.dtype)
        lo += n
    lo = 0
    for (_, n, _), o_ref in zip(_TR_OUTS, out_refs[len(_NAT_OUTS):]):
        res = _dot_nt(wt_ref[lo:lo + n, :], xn)
        for s in range(n_sub):
            o_ref[s] = res[:, s * LANES:(s + 1) * LANES].astype(o_ref.dtype)
        lo += n


def _proj(x2, nw, wn, wt, onehot, tm):
    t, d = x2.shape
    assert t % tm == 0 and tm % LANES == 0 and onehot.shape[1] == tm
    n_sub = tm // LANES
    n_pat = onehot.shape[0]
    const = lambda arr: pl.BlockSpec(arr.shape, lambda i: (0,) * arr.ndim, pipeline_mode=pl.Buffered(1))
    slabbed = lambda name: name == 'kv_cmp'
    out_specs = [pl.BlockSpec((n // LANES, tm, LANES), lambda i: (0, i, 0)) if slabbed(name)
                 else pl.BlockSpec((tm, n), lambda i: (i, 0)) for name, n, _ in _NAT_OUTS] + \
                [pl.BlockSpec((n_sub, n, LANES), lambda i: (i, 0, 0)) for _, n, _ in _TR_OUTS]
    out_shape = [jax.ShapeDtypeStruct((n // LANES, t, LANES) if slabbed(name) else (t, n), dt)
                 for name, n, dt in _NAT_OUTS] + \
                [jax.ShapeDtypeStruct((t // LANES, n, LANES), dt) for _, n, dt in _TR_OUTS]
    return pl.pallas_call(
        functools.partial(_proj_kernel, n_sub=n_sub),
        grid=(t // tm,),
        in_specs=[pl.BlockSpec((tm, d), lambda i: (i, 0)), const(nw), const(wn), const(wt),
                  pl.BlockSpec((None, tm, onehot.shape[2]), lambda i: (i % n_pat, 0, 0))],
        out_specs=out_specs,
        out_shape=out_shape,
        compiler_params=pltpu.CompilerParams(dimension_semantics=("parallel",),
                                             vmem_limit_bytes=VMEM_LIMIT),
        name="proj",
    )(x2, nw, wn, wt, onehot)


def _compress_kernel(raw_ref, posk_ref, posv_ref, w1k_ref, b1k_ref, w2k_ref,
                     w1v_ref, b1v_ref, w2vt_ref, kc_ref, vct_ref):
    nseg = kc_ref.shape[0]
    n_slab = raw_ref.shape[0]
    per_kv = n_slab // 2
    for j in range(n_slab):
        is_k = j < per_kv
        pos_ref, w1_ref, b1_ref = (posk_ref, w1k_ref, b1k_ref) if is_k else (posv_ref, w1v_ref, b1v_ref)
        first = jnp.zeros((nseg, 2 * CMP_HIDDEN), F32)
        second = jnp.zeros((nseg, 2 * CMP_HIDDEN), F32)
        for p in range(CMP_STRIDE):
            rows = raw_ref[j, pl.ds(p, nseg, stride=CMP_STRIDE), :]
            first = first + _dot((rows + pos_ref[p:p + 1, :]).astype(BF16), w1_ref[p])
            q = CMP_STRIDE + p
            second = second + _dot((rows + pos_ref[q:q + 1, :]).astype(BF16), w1_ref[q])
        hid = _silu(first + pltpu.roll(second, nseg - 1, 0) + b1_ref[...]).astype(BF16)
        for gi in range(2):
            g = 2 * (j % per_kv) + gi
            hg = hid[:, gi * CMP_HIDDEN:(gi + 1) * CMP_HIDDEN]
            if is_k:
                kc_ref[:, g * HEAD_DIM:(g + 1) * HEAD_DIM] = _dot(hg, w2k_ref[...]).astype(kc_ref.dtype)
            else:
                vct_ref[g * HEAD_DIM:(g + 1) * HEAD_DIM, :] = _dot_nt(w2vt_ref[...], hg).astype(vct_ref.dtype)


def _compress(raw, batch, posk, posv, w1k, b1k, w2k, w1v, b1v, w2vt):
    n_slab, t, lanes = raw.shape
    seq = t // batch
    nseg = seq // CMP_STRIDE
    assert n_slab * lanes == 2 * KV_WIDTH and lanes == 2 * HEAD_DIM
    const = lambda arr: pl.BlockSpec(arr.shape, lambda i: (0,) * arr.ndim, pipeline_mode=pl.Buffered(1))
    return pl.pallas_call(
        _compress_kernel,
        grid=(batch,),
        in_specs=[pl.BlockSpec((n_slab, seq, lanes), lambda i: (0, i, 0)),
                  const(posk), const(posv), const(w1k), const(b1k), const(w2k),
                  const(w1v), const(b1v), const(w2vt)],
        out_specs=[pl.BlockSpec((None, nseg, KV_WIDTH), lambda i: (i, 0, 0)),
                   pl.BlockSpec((None, KV_WIDTH, nseg), lambda i: (i, 0, 0))],
        out_shape=[jax.ShapeDtypeStruct((batch, nseg, KV_WIDTH), BF16),
                   jax.ShapeDtypeStruct((batch, KV_WIDTH, nseg), BF16)],
        compiler_params=pltpu.CompilerParams(dimension_semantics=("parallel",),
                                             vmem_limit_bytes=VMEM_LIMIT),
        name="compress",
    )(raw, posk, posv, w1k, b1k, w2k, w1v, b1v, w2vt)


def _nsa_kernel(qt_ref, gt_ref, kc_ref, vct_ref, kslc_ref, vslct_ref, kswa_ref, vswat_ref,
                tsel_ref, tcmp_ref, ovt_ref, far_ref, out_ref, guard_ref,
                qbd_ref, qaug_ref, m_ref, acc_ref, ot_ref, s_ref, p_ref, viol_ref, work_ref, chosen_ref,
                ocmp_ref, *, n_qb, n_cmp_rows, n_sel, top_n, exact):
    qb = pl.program_id(1)
    q0 = qb * Q_BLOCK

    @pl.when(qb == 0)
    def _():
        qbd_ref[...] = jnp.zeros_like(qbd_ref)
        qaug_ref[...] = jnp.zeros_like(qaug_ref)

    for g in range(NSA_GROUPS):
        for r in range(NSA_HPG):
            h = g * NSA_HPG + r
            q_head = qt_ref[h * HEAD_DIM:(h + 1) * HEAD_DIM, :]
            qbd_ref[g * HEAD_DIM:(g + 1) * HEAD_DIM, h * Q_BLOCK:(h + 1) * Q_BLOCK] = q_head
            qaug_ref[g, 0:HEAD_DIM, r * Q_BLOCK:(r + 1) * Q_BLOCK] = q_head

    def gslice(g):
        return slice(g * GCOLS, (g + 1) * GCOLS)

    def values_with_ones(vt_tiles, g):
        vt = jnp.concatenate([t[g * HEAD_DIM:(g + 1) * HEAD_DIM, :] for t in vt_tiles], axis=1)
        return jnp.concatenate([vt, jnp.ones((SUM_ROWS, vt.shape[1]), BF16)], axis=0)

    def normalized(a):
        return a[0:HEAD_DIM, :] * (1.0 / a[HEAD_DIM:HEAD_DIM + 1, :])

    def self_logit(kt):
        cols = []
        for h in range(NSA_HEADS):
            g = h // NSA_HPG
            prod = qt_ref[h * HEAD_DIM:(h + 1) * HEAD_DIM, :].astype(F32) * kt[g * HEAD_DIM:(g + 1) * HEAD_DIM, :]
            cols.append(jnp.sum(prod, axis=0, keepdims=True))
        return jnp.concatenate(cols, axis=1) - far_ref[...]

    diag_rows = pl.ds(pl.multiple_of(q0, Q_BLOCK), Q_BLOCK)

    u0 = pl.multiple_of(8 * (n_qb - 1 - qb), 8)

    def compress_and_select(n_rows, n_blk):
        j_iota = lax.broadcasted_iota(jnp.int32, (n_blk, Q_BLOCK), 0)
        j_f32 = j_iota.astype(F32)
        t_pos = q0 + lax.broadcasted_iota(jnp.int32, (n_blk, Q_BLOCK), 1)
        cur = t_pos // SEL_BLOCK
        valid = j_iota <= cur
        forced = (j_iota == 0) | (j_iota == cur) | (j_iota == cur - 1)

        logits = [_dot(kc_ref[0:n_rows, :], qbd_ref[:, gslice(g)]) for g in range(NSA_GROUPS)]
        for g in range(NSA_GROUPS):
            s = logits[g] + tcmp_ref[pl.ds(u0, n_rows), gslice(g)]
            m = jnp.maximum(jnp.max(s, axis=0, keepdims=True), 0.1 * NEG)
            p_ref[0:n_rows, gslice(g)] = jnp.exp2(s - m).astype(BF16)
        imps = []
        for g in range(NSA_GROUPS):
            e = p_ref[0:n_rows, gslice(g)]
            raw = _dot(values_with_ones([vct_ref[:, 0:n_rows]], g), e)
            total = raw[HEAD_DIM:HEAD_DIM + 1, :]
            inv = 1.0 / jnp.where(total == 0.0, 1.0, total)
            ocmp_ref[g] = raw[0:HEAD_DIM, :] * inv
            imp_heads = _dot(ovt_ref[0:n_blk, 0:n_rows], e) * inv
            imp = imp_heads[:, 0:Q_BLOCK]
            for r in range(1, NSA_HPG):
                imp = imp + imp_heads[:, r * Q_BLOCK:(r + 1) * Q_BLOCK]
            imps.append(imp)

        quota = top_n - (1 + (cur >= 1).astype(jnp.int32) + (cur >= 2).astype(jnp.int32))
        for g in range(NSA_GROUPS):
            work_ref[g, 0:n_blk, :] = jnp.where(forced, -2.0, jnp.where(valid, imps[g], -1.0))
            chosen_ref[g, 0:n_blk, :] = jnp.where(forced, 1.0, 0.0)

        def pick_round(index, limited):
            for g in range(NSA_GROUPS):
                work = work_ref[g, 0:n_blk, :]
                best = jnp.max(work, axis=0, keepdims=True)
                first = jnp.min(jnp.where(work == best, j_f32, float(n_blk)), axis=0, keepdims=True)
                hit = j_f32 == first
                if limited:
                    hit = hit & (quota > index)
                chosen_ref[g, 0:n_blk, :] = jnp.where(hit, 1.0, chosen_ref[g, 0:n_blk, :])
                work_ref[g, 0:n_blk, :] = jnp.where(hit, -2.0, work)

        max_forced = 3
        for index in range(top_n - max_forced):
            pick_round(index, False)

        @pl.when(q0 < (max_forced - 1) * SEL_BLOCK)
        def _():
            for index in range(top_n - max_forced, top_n - 1):
                pick_round(index, True)

        for g in range(NSA_GROUPS):
            addm = jnp.where((chosen_ref[g, 0:n_blk, :] > 0.5) & valid, 0.0, NEG).astype(BF16)
            qaug_ref[g, HEAD_DIM:HEAD_DIM + n_blk, :] = jnp.concatenate([addm] * NSA_HPG, axis=1)
            if n_blk < n_sel:
                qaug_ref[g, HEAD_DIM + n_blk:HEAD_DIM + n_sel, :] = jnp.full((n_sel - n_blk, GCOLS), NEG, BF16)

    first_half = qb < n_qb // 2

    @pl.when(first_half)
    def _():
        compress_and_select(n_cmp_rows // 2, n_sel // 2)

    @pl.when(jnp.logical_not(first_half))
    def _():
        compress_and_select(n_cmp_rows, n_sel)

    def pipelined(items, logits_of, finish):
        pending = {i: logits_of(items[i]) for i in range(min(QK_LOOKAHEAD, len(items)))}
        for i, item in enumerate(items):
            finish(item, pending.pop(i))
            if i + QK_LOOKAHEAD < len(items):
                pending[i + QK_LOOKAHEAD] = logits_of(items[i + QK_LOOKAHEAD])

    def window(exact):
        first_tile = jnp.maximum(qb - (WIN_TILES - 1), 0)
        kw = kswa_ref[pl.ds(pl.multiple_of(first_tile * LANES, LANES), WIN_TILES * LANES), :]
        win_vt_tiles = [vswat_ref[first_tile + t] for t in range(WIN_TILES)]
        win_ids = []
        for t in range(WIN_TILES):
            i = qb - (first_tile + t)
            win_ids.append(jnp.where(i < 0, TILE_MASKED, jnp.where(i == WIN_TILES - 1, TILE_WIN_OLD, i)))
        if not exact:
            m_fix = self_logit(kswa_ref[diag_rows, :].astype(F32).T)

        def finish(g, logits):
            m8 = None
            for t in range(WIN_TILES):
                rows = slice(t * LANES, (t + 1) * LANES)
                v = logits[rows, :] + tsel_ref[win_ids[t], :, gslice(g)]
                if exact:
                    s_ref[rows, gslice(g)] = v
                    vm = jnp.max(v.reshape(LANES // 8, 8, GCOLS), axis=0)
                else:
                    pt = jnp.exp2((v - m_fix[:, gslice(g)]).astype(BF16))
                    p_ref[rows, gslice(g)] = pt
                    vm = jnp.max(pt.reshape(LANES // 16, 16, GCOLS), axis=0)
                m8 = vm if m8 is None else jnp.maximum(m8, vm)
            if exact:
                m = jnp.max(m8, axis=0, keepdims=True)
                p_ref[0:WIN_TILES * LANES, gslice(g)] = jnp.exp2(s_ref[:, gslice(g)] - m).astype(BF16)
            else:
                viol_ref[:, gslice(g)] = jnp.maximum(viol_ref[:, gslice(g)],
                                                     jnp.max(m8.astype(F32), axis=0, keepdims=True))
            acc_ref[g] = _dot(values_with_ones(win_vt_tiles, g), p_ref[0:WIN_TILES * LANES, gslice(g)])

        pipelined(list(range(NSA_GROUPS)), lambda g: _dot(kw, qbd_ref[:, gslice(g)]), finish)

    def sel_steps(steps, near, exact):
        def logits_of(item):
            (tile0, n_tiles, _), g = item
            key_rows = pl.ds(pl.multiple_of(tile0 * LANES, LANES), n_tiles * LANES)
            return _dot(kslc_ref[key_rows, g * AUG_DIM:(g + 1) * AUG_DIM], qaug_ref[g])

        def finish(item, logits):
            (tile0, n_tiles, p_row0), g = item
            vt_tiles = [vslct_ref[tile0 + t] for t in range(n_tiles)]
            m_old = m_ref[:, gslice(g)]
            m8 = None
            for t in range(n_tiles):
                rows = slice(t * LANES, (t + 1) * LANES)
                v = logits[rows, :]
                if near:
                    i = qb - (tile0 + t)
                    tile_id = jnp.where(i < 0, TILE_MASKED, jnp.minimum(i, TILE_FAR))
                    v = v + tsel_ref[tile_id, :, gslice(g)]
                if exact:
                    s_ref[rows, gslice(g)] = v
                    vm = jnp.max(v.reshape(LANES // 8, 8, GCOLS), axis=0)
                else:
                    pt = jnp.exp2((v - m_old).astype(BF16))
                    p_ref[p_row0 + t * LANES:p_row0 + (t + 1) * LANES, gslice(g)] = pt
                    vm = jnp.max(pt.reshape(LANES // 16, 16, GCOLS), axis=0)
                m8 = vm if m8 is None else jnp.maximum(m8, vm)
            if exact:
                m_new = jnp.maximum(m_old, jnp.max(m8, axis=0, keepdims=True))
                alpha = jnp.exp2(m_old - m_new)
                m_ref[:, gslice(g)] = m_new
                for t in range(n_tiles):
                    rows = slice(t * LANES, (t + 1) * LANES)
                    p_ref[p_row0 + t * LANES:p_row0 + (t + 1) * LANES, gslice(g)] = \
                        jnp.exp2(s_ref[rows, gslice(g)] - m_new).astype(BF16)
            else:
                viol_ref[:, gslice(g)] = jnp.maximum(viol_ref[:, gslice(g)],
                                                     jnp.max(m8.astype(F32), axis=0, keepdims=True))
            pv = _dot(values_with_ones(vt_tiles, g), p_ref[p_row0:p_row0 + n_tiles * LANES, gslice(g)])
            if exact:
                acc_ref[g] = acc_ref[g] * alpha + pv
            else:
                acc_ref[g] = acc_ref[g] + pv

        pipelined([(step, g) for step in steps for g in range(NSA_GROUPS)], logits_of, finish)

    def sel_step(tile0, n_tiles, near, exact):
        sel_steps([(tile0, n_tiles, 0)], near, exact)

    def sel_two_steps(tile_a, tile_b, near):
        sel_steps([(tile_a, STEP_TILES, 0), (tile_b, STEP_TILES, STEP)], near, False)

    r_diag = qb // STEP_TILES
    n_far = jnp.maximum((qb - (N_NEAR - 1)) // STEP_TILES, 0)
    half = STEP_TILES // 2

    def selected(exact):
        if exact:
            m_ref[...] = jnp.full(m_ref.shape, NEG, F32)
        else:
            kt = kslc_ref[diag_rows, :].astype(F32).T
            m_ref[...] = self_logit(jnp.concatenate(
                [kt[g * AUG_DIM:g * AUG_DIM + HEAD_DIM, :] for g in range(NSA_GROUPS)], axis=0))
        acc_ref[...] = jnp.zeros_like(acc_ref)

        @pl.when(qb % STEP_TILES < half)
        def _():
            sel_step(STEP_TILES * r_diag, half, True, exact)

        @pl.when(qb % STEP_TILES >= half)
        def _():
            sel_step(STEP_TILES * r_diag, STEP_TILES, True, exact)

        def near_body(i, carry):
            sel_step(STEP_TILES * (r_diag - i), STEP_TILES, True, exact)
            return carry

        def far_body(r, carry):
            sel_step(STEP_TILES * r, STEP_TILES, False, exact)
            return carry

        def far_pair_body(k, carry):
            sel_two_steps(STEP_TILES * 2 * k, STEP_TILES * (2 * k + 1), False)
            return carry

        lax.fori_loop(1, r_diag - n_far + 1, near_body, 0)
        if exact:
            lax.fori_loop(0, n_far, far_body, 0)
        else:
            @pl.when(n_far % 2 == 1)
            def _():
                sel_step(STEP_TILES * (n_far - 1), STEP_TILES, False, False)

            lax.fori_loop(0, n_far // 2, far_pair_body, 0)

    viol_ref[...] = jnp.zeros_like(viol_ref)
    selected(exact)
    o_slc = [normalized(acc_ref[g]) for g in range(NSA_GROUPS)]
    window(exact)
    o_swa = [normalized(acc_ref[g]) for g in range(NSA_GROUPS)]
    guard_ref[...] = jnp.broadcast_to(jnp.max(viol_ref[...], axis=1, keepdims=True), guard_ref.shape)

    gates = _sigmoid(gt_ref[...])
    for g in range(NSA_GROUPS):
        def gate_row(c):
            return jnp.concatenate(
                [gates[c * NSA_HEADS + g * NSA_HPG + r:c * NSA_HEADS + g * NSA_HPG + r + 1, :]
                 for r in range(NSA_HPG)], axis=1)
        og = gate_row(0) * ocmp_ref[g] + gate_row(1) * o_slc[g] + gate_row(2) * o_swa[g]
        for r in range(NSA_HPG):
            h = g * NSA_HPG + r
            ot_ref[h * HEAD_DIM:(h + 1) * HEAD_DIM, :] = og[:, r * Q_BLOCK:(r + 1) * Q_BLOCK]
    out_ref[...] = ot_ref[...].T


def _nsa(qt, gt, kc, vct, kslc, vslct, kswa, vswat, tsel, tcmp, ovt, far, seq, exact):
    b = kc.shape[0]
    n_qb = seq // Q_BLOCK
    assert n_qb % STEP_TILES == 0 and n_qb >= WIN_TILES and WIN_TILES >= STEP_TILES
    n_cmp_rows = kc.shape[1]
    n_sel = seq // SEL_BLOCK
    assert n_sel <= AUG_DIM - HEAD_DIM
    top_n = min(SEL_TOPN, n_sel)
    assert top_n > 3
    n_gate = gt.shape[2]
    const = lambda arr: pl.BlockSpec(arr.shape, lambda i, j: (0,) * arr.ndim,
                                     pipeline_mode=pl.Buffered(1))
    kern = functools.partial(_nsa_kernel, n_qb=n_qb, n_cmp_rows=n_cmp_rows, n_sel=n_sel, top_n=top_n,
                             exact=exact)
    return pl.pallas_call(
        kern,
        grid=(b, n_qb),
        in_specs=[pl.BlockSpec((None, None, NSA_WIDTH, Q_BLOCK), lambda i, j: (i, j, 0, 0)),
                  pl.BlockSpec((None, None, n_gate, Q_BLOCK), lambda i, j: (i, j, 0, 0)),
                  pl.BlockSpec((None, n_cmp_rows, KV_WIDTH), lambda i, j: (i, 0, 0)),
                  pl.BlockSpec((None, KV_WIDTH, n_cmp_rows), lambda i, j: (i, 0, 0)),
                  pl.BlockSpec((None, seq, NSA_GROUPS * AUG_DIM), lambda i, j: (i, 0, 0)),
                  pl.BlockSpec((None, n_qb, KV_WIDTH, Q_BLOCK), lambda i, j: (i, 0, 0, 0)),
                  pl.BlockSpec((None, seq, KV_WIDTH), lambda i, j: (i, 0, 0)),
                  pl.BlockSpec((None, n_qb, KV_WIDTH, Q_BLOCK), lambda i, j: (i, 0, 0, 0)),
                  const(tsel), const(tcmp), const(ovt), const(far)],
        out_specs=[pl.BlockSpec((None, Q_BLOCK, NSA_WIDTH), lambda i, j: (i, j, 0)),
                   pl.BlockSpec((None, None, 8, LANES), lambda i, j: (i, j, 0, 0))],
        out_shape=[jax.ShapeDtypeStruct((b, seq, NSA_WIDTH), F32),
                   jax.ShapeDtypeStruct((b, n_qb, 8, LANES), F32)],
        scratch_shapes=[pltpu.VMEM((KV_WIDTH, QCOLS), BF16),
                        pltpu.VMEM((NSA_GROUPS, AUG_DIM, GCOLS), BF16),
                        pltpu.VMEM((1, QCOLS), F32),
                        pltpu.VMEM((NSA_GROUPS, HEAD_DIM + SUM_ROWS, GCOLS), F32),
                        pltpu.VMEM((NSA_WIDTH, Q_BLOCK), F32),
                        pltpu.VMEM((WIN_TILES * LANES, QCOLS), F32),
                        pltpu.VMEM((max(WIN_TILES * LANES, 2 * STEP), QCOLS), BF16),
                        pltpu.VMEM((1, QCOLS), F32),
                        pltpu.VMEM((NSA_GROUPS, n_sel, Q_BLOCK), F32),
                        pltpu.VMEM((NSA_GROUPS, n_sel, Q_BLOCK), F32),
                        pltpu.VMEM((NSA_GROUPS, HEAD_DIM, GCOLS), F32)],
        compiler_params=pltpu.CompilerParams(dimension_semantics=("parallel", "arbitrary"),
                                             vmem_limit_bytes=VMEM_LIMIT),
        name="nsa_exact" if exact else "nsa",
    )(qt, gt, kc, vct, kslc, vslct, kswa, vswat, tsel, tcmp, ovt, far)


def _ssd_kernel(xbc_ref, dt_ref, convw_ref, convb_ref, dtb_ref, a_ref, dskip_ref, expand_ref, y_ref,
                xp_ref, tail_ref, state_ref, slab_ref):
    @pl.when(pl.program_id(1) == 0)
    def _():
        tail_ref[...] = jnp.zeros_like(tail_ref)
        state_ref[...] = jnp.zeros_like(state_ref)

    n_chunks = xbc_ref.shape[0] // SSM_CHUNK
    fronts = [_ssd_front(ci, xbc_ref, dt_ref, convw_ref, convb_ref, dtb_ref, a_ref, expand_ref,
                         xp_ref.at[ci], tail_ref, slab_ref.at[ci]) for ci in range(n_chunks)]
    for ci in range(n_chunks):
        _ssd_back(ci, fronts[ci], dskip_ref, y_ref, state_ref, slab_ref.at[ci])


def _ssd_front(ci, xbc_ref, dt_ref, convw_ref, convb_ref, dtb_ref, a_ref, expand_ref, xp_ref, tail_ref, slab_ref):
    L = SSM_CHUNK
    NV = L // 8
    halo = CONV_WIDTH - 1
    tok = slice(ci * L, (ci + 1) * L)

    n_slab = CONV_DIM // LANES
    for s in range(n_slab):
        slab_ref[s] = xbc_ref[tok, s * LANES:(s + 1) * LANES]
    slab_ref[n_slab] = jnp.concatenate([dt_ref[tok, :], jnp.zeros((L, LANES - SSM_HEADS), F32)], axis=1)
    for v in range(NV):
        for s in range(n_slab):
            xp_ref[(halo + v) * 8:(halo + v + 1) * 8, s * LANES:(s + 1) * LANES] = \
                slab_ref[s, pl.ds(v, 8, stride=NV), :]
    sub = lax.broadcasted_iota(jnp.int32, (8, CONV_DIM), 0)
    for w in range(halo):
        cur = xp_ref[(NV + w) * 8:(NV + w + 1) * 8, :]
        prev = tail_ref[w * 8:(w + 1) * 8, :]
        xp_ref[w * 8:(w + 1) * 8, :] = pltpu.roll(jnp.where(sub == 7, prev, cur), 1, 0)
    tail_ref[...] = xp_ref[NV * 8:(NV + halo) * 8, :]
    conv = convb_ref[...]
    for k in range(CONV_WIDTH):
        conv = conv + convw_ref[k:k + 1, :] * xp_ref[8 * k:8 * k + L, :]
    xbc = _silu(conv)
    xs = xbc[:, 0:SSM_WIDTH]
    bm = xbc[:, SSM_WIDTH:SSM_WIDTH + SSM_GROUPS * SSM_STATE]
    cm = xbc[:, SSM_WIDTH + SSM_GROUPS * SSM_STATE:]

    dt_raw = jnp.concatenate([slab_ref[n_slab, pl.ds(v, 8, stride=NV), :] for v in range(NV)], axis=0)
    dt = _softplus(dt_raw[:, 0:SSM_HEADS] + dtb_ref[...])
    a_row = -jnp.exp(a_ref[...])
    row_i = lax.broadcasted_iota(jnp.int32, (L, L), 0)
    col_j = lax.broadcasted_iota(jnp.int32, (L, L), 1)
    token = lambda r: (r % 8) * NV + r // 8
    tril = token(row_i) >= token(col_j)
    tril01 = jnp.where(tril, 1.0, 0.0).astype(BF16)
    cs = _dot_exact_rhs(tril01, dt * a_row)
    cst = jnp.concatenate([cs, jnp.zeros((L, L - SSM_HEADS), F32)], axis=1).T

    expand = expand_ref[...]
    cs_x = _dot_exact_lhs(cs, expand)
    dt_x = _dot_exact_lhs(dt, expand)
    total_x = cs_x[L - 1:L, :]
    xdt = (xs * dt_x).astype(BF16)
    xw = (xs * (jnp.exp(total_x - cs_x) * dt_x)).astype(BF16)
    decay_out = jnp.exp(cs_x)
    decay_state = jnp.exp(total_x)
    return xs, bm, cm, cs, cst, tril, xdt, xw, decay_out, decay_state


def _ssd_back(ci, front, dskip_ref, y_ref, state_ref, slab_ref):
    xs, bm, cm, cs, cst, tril, xdt, xw, decay_out, decay_state = front
    L = SSM_CHUNK
    NV = L // 8
    tok = slice(ci * L, (ci + 1) * L)
    lane = lax.broadcasted_iota(jnp.int32, (L, LANES), 1)
    gw = SSM_HPG * SSM_HEAD_DIM
    for g in range(SSM_GROUPS):
        bg = bm[:, g * SSM_STATE:(g + 1) * SSM_STATE]
        cg = cm[:, g * SSM_STATE:(g + 1) * SSM_STATE].astype(BF16)
        cb = _dot_nt(cg, bg.astype(BF16))
        st = state_ref[g]
        y_g = _dot(cg, st.astype(BF16)) * decay_out[:, g * gw:(g + 1) * gw]
        state_ref[g] = st * decay_state[:, g * gw:(g + 1) * gw] + \
            _dot(bg.T.astype(BF16), xw[:, g * gw:(g + 1) * gw])
        pieces = []
        for pair in range(SSM_HPG // 2):
            ws = []
            for hh in range(2):
                h = g * SSM_HPG + 2 * pair + hh
                diff = cs[:, h:h + 1] - cst[h:h + 1, :]
                decay = jnp.where(tril, jnp.exp(jnp.where(tril, diff, 0.0)), 0.0)
                ws.append((cb * decay).astype(BF16))
            w_pair = jnp.concatenate(ws, axis=1)
            lo = g * gw + pair * LANES
            slab = xdt[:, lo:lo + LANES]
            zero = jnp.zeros_like(slab)
            x_bd = jnp.concatenate([jnp.where(lane < SSM_HEAD_DIM, slab, zero),
                                    jnp.where(lane >= SSM_HEAD_DIM, slab, zero)], axis=0)
            pieces.append(_dot(w_pair, x_bd))
        y_g = y_g + jnp.concatenate(pieces, axis=1) + \
            dskip_ref[:, g * gw:(g + 1) * gw] * xs[:, g * gw:(g + 1) * gw]
        for s in range(gw // LANES):
            for v in range(NV):
                slab_ref[s, pl.ds(v, 8, stride=NV), :] = y_g[v * 8:(v + 1) * 8, s * LANES:(s + 1) * LANES]
        for s in range(gw // LANES):
            y_ref[tok, g * gw + s * LANES:g * gw + (s + 1) * LANES] = slab_ref[s]


def _ssd(xbc, dt, convw, convb, dtb, a_log, dskip_x, expand, seq):
    b = xbc.shape[0]
    rows = SSD_CHUNKS_PER_STEP * SSM_CHUNK
    assert seq % rows == 0
    halo_rows = (CONV_WIDTH - 1) * 8
    const = lambda arr: pl.BlockSpec(arr.shape, lambda i, j: (0,) * arr.ndim)
    return pl.pallas_call(
        _ssd_kernel,
        grid=(b, seq // rows),
        in_specs=[pl.BlockSpec((None, rows, CONV_DIM), lambda i, j: (i, j, 0)),
                  pl.BlockSpec((None, rows, SSM_HEADS), lambda i, j: (i, j, 0)),
                  const(convw), const(convb), const(dtb), const(a_log), const(dskip_x), const(expand)],
        out_specs=pl.BlockSpec((None, rows, SSM_WIDTH), lambda i, j: (i, j, 0)),
        out_shape=jax.ShapeDtypeStruct((b, seq, SSM_WIDTH), F32),
        scratch_shapes=[pltpu.VMEM((SSD_CHUNKS_PER_STEP, SSM_CHUNK + halo_rows, CONV_DIM), F32),
                        pltpu.VMEM((halo_rows, CONV_DIM), F32),
                        pltpu.VMEM((SSM_GROUPS, SSM_STATE, SSM_HPG * SSM_HEAD_DIM), F32),
                        pltpu.VMEM((SSD_CHUNKS_PER_STEP, CONV_DIM // LANES + 1, SSM_CHUNK, LANES), F32)],
        compiler_params=pltpu.CompilerParams(dimension_semantics=("parallel", "arbitrary"),
                                             vmem_limit_bytes=VMEM_LIMIT),
        name="ssd",
    )(xbc, dt, convw, convb, dtb, a_log, dskip_x, expand)


def _epilogue_kernel(x_ref, o_ref, y_ref, nw_ref, wz1_ref, wz2_ref, wmg_ref, won_ref, wos_ref, wo_ref,
                     snw_ref, fnw_ref, out_ref):
    x = x_ref[...]
    ms = jnp.mean(x * x, axis=-1, keepdims=True)
    xn = (x * lax.rsqrt(ms + NORM_EPS) * nw_ref[...]).astype(BF16)
    u = (o_ref[...] * _silu(_dot(xn, wz1_ref[...]))).astype(BF16)
    h_nsa = _dot(u, won_ref[...])
    hh = y_ref[...] * _silu(_dot(xn, wz2_ref[...]))
    gw = SSM_WIDTH // SSM_GROUPS
    parts = []
    for g in range(SSM_GROUPS):
        hg = hh[:, g * gw:(g + 1) * gw]
        hg = hg * lax.rsqrt(jnp.mean(hg * hg, axis=-1, keepdims=True) + NORM_EPS)
        parts.append((hg * snw_ref[:, g * gw:(g + 1) * gw]).astype(BF16))
    h_ssm = _dot(jnp.concatenate(parts, axis=1), wos_ref[...])
    gate = _sigmoid(_dot(xn, wmg_ref[...]))
    mix = (gate[:, 0:D_MODEL] * h_nsa + gate[:, D_MODEL:] * h_ssm).astype(BF16)
    r = x + _dot(mix, wo_ref[...])
    ms2 = jnp.mean(r * r, axis=-1, keepdims=True)
    out_ref[...] = r * lax.rsqrt(ms2 + NORM_EPS) * fnw_ref[...]


def _epilogue(x2, o2, y2, nw, wz1, wz2, wmg, won, wos, wo, snw, fnw, tm):
    t, d = x2.shape
    assert t % tm == 0
    const = lambda arr: pl.BlockSpec(arr.shape, lambda i: (0,) * arr.ndim, pipeline_mode=pl.Buffered(1))
    return pl.pallas_call(
        _epilogue_kernel,
        grid=(t // tm,),
        in_specs=[pl.BlockSpec((tm, d), lambda i: (i, 0)),
                  pl.BlockSpec((tm, NSA_WIDTH), lambda i: (i, 0)),
                  pl.BlockSpec((tm, SSM_WIDTH), lambda i: (i, 0)),
                  const(nw), const(wz1), const(wz2), const(wmg), const(won), const(wos), const(wo),
                  const(snw), const(fnw)],
        out_specs=pl.BlockSpec((tm, d), lambda i: (i, 0)),
        out_shape=jax.ShapeDtypeStruct((t, d), F32),
        compiler_params=pltpu.CompilerParams(dimension_semantics=("parallel",),
                                             vmem_limit_bytes=VMEM_LIMIT),
        name="epilogue",
    )(x2, o2, y2, nw, wz1, wz2, wmg, won, wos, wo, snw, fnw)


def _table_kernel(rb_ref, bucket_ref, out_ref):
    bk = bucket_ref[...]
    for h in range(NSA_HEADS):
        acc = jnp.full(bk.shape, NEG, F32)
        for b in range(REL_BUCKETS):
            acc = jnp.where(bk == b, rb_ref[b, h], acc)
        out_ref[:, h * Q_BLOCK:(h + 1) * Q_BLOCK] = acc


def _bias_table(rb, buckets, tr):
    rows = buckets.shape[0]
    assert rows % tr == 0
    return pl.pallas_call(
        _table_kernel,
        grid=(rows // tr,),
        in_specs=[pl.BlockSpec(memory_space=pltpu.SMEM),
                  pl.BlockSpec((tr, Q_BLOCK), lambda i: (i, 0))],
        out_specs=pl.BlockSpec((tr, QCOLS), lambda i: (i, 0)),
        out_shape=jax.ShapeDtypeStruct((rows, QCOLS), F32),
        compiler_params=pltpu.CompilerParams(dimension_semantics=("parallel",)),
        name="bias_table",
    )(rb, buckets)


def _bucket_tables(seq):
    n_qb = seq // Q_BLOCK
    key = np.arange(LANES)[:, None]
    tok = np.arange(Q_BLOCK)[None, :]
    tiles = [_t5_bucket_np(LANES * o + tok - key) for o in range(N_NEAR)]
    tiles.append(np.full((LANES, Q_BLOCK), REL_BUCKETS - 1, np.int32))
    dwin = WINDOW + tok - key
    tiles.append(_t5_bucket_np(np.where(dwin < WINDOW, dwin, -1)))
    tiles.append(np.full((LANES, Q_BLOCK), MASKED_BUCKET, np.int32))
    assert len(tiles) == N_TILES and LANES * N_NEAR - (LANES - 1) >= 790
    n_cmp_rows = seq // CMP_STRIDE
    rows = 8 * (n_qb - 1) + n_cmp_rows
    rows_pad = -(-rows // LANES) * LANES
    u = np.arange(rows_pad)[:, None]
    cmp_tbl = _t5_bucket_np(tok - CMP_STRIDE * u + Q_BLOCK * (n_qb - 1) - (CMP_BLOCK - 1))
    return np.concatenate(tiles, axis=0), cmp_tbl


def _block_onehot(seq, tm):
    pos = np.arange(seq)
    pat = np.zeros((seq, NSA_GROUPS, AUG_DIM), np.float32)
    pat[pos, :, HEAD_DIM + pos // SEL_BLOCK] = 1.0
    return jnp.asarray(pat.reshape(seq // tm, tm, NSA_GROUPS * AUG_DIM), dtype=BF16)


def _overlap_t(seq):
    n_cmp_rows = seq // CMP_STRIDE
    n_sel = seq // SEL_BLOCK
    c_start = np.arange(n_cmp_rows)[None, :] * CMP_STRIDE
    s_start = np.arange(n_sel)[:, None] * SEL_BLOCK
    ov = (c_start < s_start + SEL_BLOCK) & (c_start + CMP_BLOCK > s_start)
    ov[:, n_cmp_rows - 1] = False
    return jnp.asarray(ov, dtype=BF16)


def _layer(x, norm_w, w_in, cmp_pos_k, cmp_pos_v, cmp_k_w1, cmp_k_b1, cmp_k_w2, cmp_v_w1, cmp_v_b1,
           cmp_v_w2, conv_w, conv_b, dt_bias, a_log, d_skip, ssm_norm_w, w_out_nsa, w_out_ssm, w_out,
           rel_bias, out_norm_w):
    b, s, d = x.shape
    assert d == D_MODEL and s % STEP == 0 and s >= 2 * WINDOW
    t = b * s
    n_qb = s // Q_BLOCK
    cols = _column_offsets()
    wcol = lambda name: w_in[:, cols[name][0]:cols[name][1]]
    x2 = x.reshape(t, d)
    nw = norm_w.reshape(1, d).astype(F32)

    gate_w = wcol('nsa_gate').reshape(d, NSA_HEADS, 3).transpose(0, 2, 1).reshape(d, 3 * NSA_HEADS)
    w_kslc = jnp.pad(wcol('k_slc').reshape(d, NSA_GROUPS, HEAD_DIM),
                     ((0, 0), (0, 0), (0, AUG_DIM - HEAD_DIM))).reshape(d, NSA_GROUPS * AUG_DIM)
    wn = jnp.concatenate([w_kslc, wcol('k_swa'), wcol('k_cmp'), wcol('v_cmp'), wcol('xbc'),
                          wcol('dt')], axis=1).astype(BF16)
    wt = jnp.concatenate([wcol('q') * (HEAD_DIM ** -0.5 * LOG2E), wcol('v_slc'), wcol('v_swa'), gate_w],
                         axis=1).T.astype(BF16)
    kslc, kswa, cc, xbc, dt, qt, vslct, vswat, gt = _proj(x2, nw, wn, wt, _block_onehot(s, STEP), STEP)
    kslc = kslc.reshape(b, s, NSA_GROUPS * AUG_DIM)
    kswa = kswa.reshape(b, s, KV_WIDTH)
    xbc = xbc.reshape(b, s, CONV_DIM)
    dt = dt.reshape(b, s, SSM_HEADS)
    qt = qt.reshape(b, n_qb, NSA_WIDTH, Q_BLOCK)
    vslct = vslct.reshape(b, n_qb, KV_WIDTH, Q_BLOCK)
    vswat = vswat.reshape(b, n_qb, KV_WIDTH, Q_BLOCK)
    gt = gt.reshape(b, n_qb, 3 * NSA_HEADS, Q_BLOCK)

    def pair_diag(w1):
        w = w1.reshape(CMP_BLOCK, HEAD_DIM, CMP_HIDDEN)
        z = jnp.zeros_like(w)
        return jnp.concatenate([jnp.concatenate([w, z], axis=2),
                                jnp.concatenate([z, w], axis=2)], axis=1).astype(BF16)

    twice = lambda a: jnp.tile(a.astype(F32), (1, 2))
    kc, vct = _compress(cc, b, twice(cmp_pos_k), twice(cmp_pos_v),
                        pair_diag(cmp_k_w1), twice(cmp_k_b1.reshape(1, -1)), cmp_k_w2.astype(BF16),
                        pair_diag(cmp_v_w1), twice(cmp_v_b1.reshape(1, -1)), cmp_v_w2.T.astype(BF16))

    sel_buckets, cmp_buckets = _bucket_tables(s)
    rb = rel_bias.astype(F32) * LOG2E
    rb_far = rb[REL_BUCKETS - 1:REL_BUCKETS, :]
    tsel = _bias_table(rb - rb_far, jnp.asarray(sel_buckets), LANES).reshape(N_TILES, LANES, QCOLS)
    tcmp = _bias_table(rb, jnp.asarray(cmp_buckets), LANES)
    far = jnp.repeat(rb_far.reshape(-1), Q_BLOCK).reshape(1, QCOLS)
    nsa_args = (qt, gt, kc, vct, kslc, vslct, kswa, vswat, tsel, tcmp, _overlap_t(s), far, s)
    o_nsa, guard = _nsa(*nsa_args, exact=False)
    o_nsa = lax.cond(jnp.logical_not(jnp.max(guard) <= 2.0 ** MAX_EXP2_EXCESS),
                     lambda: _nsa(*nsa_args, exact=True)[0], lambda: o_nsa)

    expand = jnp.asarray(np.kron(np.eye(SSM_HEADS), np.ones((1, SSM_HEAD_DIM))), dtype=BF16)
    y = _ssd(xbc, dt, conv_w.astype(F32), conv_b.reshape(1, -1).astype(F32),
             dt_bias.reshape(1, -1).astype(F32), a_log.reshape(1, -1).astype(F32),
             jnp.repeat(d_skip.astype(F32), SSM_HEAD_DIM).reshape(1, -1), expand, s)

    out = _epilogue(x2, o_nsa.reshape(t, NSA_WIDTH), y.reshape(t, SSM_WIDTH), nw,
                    wcol('z_nsa').astype(BF16), wcol('z_ssm').astype(BF16), wcol('merge_gate').astype(BF16),
                    w_out_nsa.astype(BF16), w_out_ssm.astype(BF16), w_out.astype(BF16),
                    ssm_norm_w.reshape(1, -1).astype(F32), out_norm_w.reshape(1, -1).astype(F32),
                    256 if t % 256 == 0 else Q_BLOCK)
    return out.reshape(b, s, d)


def kernel(x, norm_w, w_in, cmp_pos_k, cmp_pos_v, cmp_k_w1, cmp_k_b1, cmp_k_w2, cmp_v_w1, cmp_v_b1, cmp_v_w2,
           conv_w, conv_b, dt_bias, a_log, d_skip, ssm_norm_w, w_out_nsa, w_out_ssm, w_out, rel_bias,
           final_norm_w):
    depth = norm_w.shape[0]
    assert depth == 1, "the epilogue fuses the final norm into the single layer"
    return _layer(x, norm_w[0], w_in[0], cmp_pos_k[0], cmp_pos_v[0], cmp_k_w1[0], cmp_k_b1[0], cmp_k_w2[0],
                  cmp_v_w1[0], cmp_v_b1[0], cmp_v_w2[0], conv_w[0], conv_b[0], dt_bias[0], a_log[0],
                  d_skip[0], ssm_norm_w[0], w_out_nsa[0], w_out_ssm[0], w_out[0], rel_bias, final_norm_w)
```

```python
import functools
import math

import numpy as np
import jax
import jax.numpy as jnp
from jax import lax
from jax.experimental import pallas as pl
from jax.experimental.pallas import tpu as pltpu

F32 = jnp.float32
BF16 = jnp.bfloat16

D_MODEL = 1024
NSA_HEADS = 16
NSA_GROUPS = 4
NSA_HPG = NSA_HEADS // NSA_GROUPS
HEAD_DIM = 64
NSA_WIDTH = NSA_HEADS * HEAD_DIM
KV_WIDTH = NSA_GROUPS * HEAD_DIM
CMP_BLOCK = 32
CMP_STRIDE = 16
CMP_HIDDEN = 4 * HEAD_DIM
SEL_BLOCK = 64
SEL_TOPN = 8
WINDOW = 512
Q_BLOCK = 128
FORCE_SCORE = 1.0e4
REL_BUCKETS = 32
SSM_WIDTH = 2 * D_MODEL
SSM_HEAD_DIM = 64
SSM_HEADS = SSM_WIDTH // SSM_HEAD_DIM
SSM_GROUPS = 4
SSM_HPG = SSM_HEADS // SSM_GROUPS
SSM_STATE = 128
CONV_WIDTH = 4
SSM_CHUNK = 128
CONV_DIM = SSM_WIDTH + 2 * SSM_GROUPS * SSM_STATE
NORM_EPS = 1e-6

NEG = -1e30
LOG2E = math.log2(math.e)
LANES = 128
QCOLS = NSA_HEADS * Q_BLOCK
GCOLS = NSA_HPG * Q_BLOCK
AUG_DIM = 2 * HEAD_DIM
N_NEAR = 8
TILE_FAR = N_NEAR
TILE_WIN_OLD = N_NEAR + 1
TILE_MASKED = N_NEAR + 2
N_TILES = N_NEAR + 3
MASKED_BUCKET = REL_BUCKETS
SUM_ROWS = 16
MAX_EXP2_EXCESS = 100.0
STEP_TILES = 4
STEP = STEP_TILES * LANES
WIN_TILES = WINDOW // LANES + 1
SSD_CHUNKS_PER_STEP = 4
QK_LOOKAHEAD = 2
VMEM_LIMIT = 56 * 1024 * 1024


def _column_offsets():
    sizes = (('q', NSA_WIDTH), ('k_cmp', KV_WIDTH), ('v_cmp', KV_WIDTH), ('k_slc', KV_WIDTH),
             ('v_slc', KV_WIDTH), ('k_swa', KV_WIDTH), ('v_swa', KV_WIDTH), ('nsa_gate', 3 * NSA_HEADS),
             ('z_nsa', NSA_WIDTH), ('z_ssm', SSM_WIDTH), ('xbc', CONV_DIM), ('dt', SSM_HEADS),
             ('merge_gate', 2 * D_MODEL))
    out, lo = {}, 0
    for name, n in sizes:
        out[name] = (lo, lo + n)
        lo += n
    return out


def _t5_bucket_np(dist):
    dist = np.asarray(dist, dtype=np.int64)
    d = np.maximum(dist, 0)
    max_exact = REL_BUCKETS // 2
    large = np.full(d.shape, max_exact, dtype=np.int64)
    d8 = d.astype(object) ** 8
    for k in range(1, REL_BUCKETS - max_exact):
        large = large + (d8 >= 2 ** (32 + 3 * k)).astype(np.int64)
    bucket = np.where(d < max_exact, d, np.minimum(large, REL_BUCKETS - 1))
    return np.where(dist < 0, MASKED_BUCKET, bucket).astype(np.int32)


def _sigmoid(x):
    return 1.0 / (1.0 + jnp.exp(-x))


def _silu(x):
    return x * _sigmoid(x)


def _softplus(x):
    return jnp.maximum(x, 0.0) + jnp.log1p(jnp.exp(-jnp.abs(x)))


def _dot(a, b):
    return jnp.dot(a, b, preferred_element_type=F32)


def _dot_nt(a, b):
    return lax.dot_general(a, b, (((1,), (1,)), ((), ())), preferred_element_type=F32)


def _split3(x):
    hi = x.astype(BF16)
    r1 = x - hi.astype(F32)
    mid = r1.astype(BF16)
    lo = (r1 - mid.astype(F32)).astype(BF16)
    return hi, mid, lo


def _dot_exact_lhs(x, w01):
    hi, mid, lo = _split3(x)
    return _dot(hi, w01) + _dot(mid, w01) + _dot(lo, w01)


def _dot_exact_rhs(w01, x):
    hi, mid, lo = _split3(x)
    return _dot(w01, hi) + _dot(w01, mid) + _dot(w01, lo)


_NAT_OUTS = (('k_slc', NSA_GROUPS * AUG_DIM, BF16), ('k_swa', KV_WIDTH, BF16), ('kv_cmp', 2 * KV_WIDTH, F32),
             ('xbc', CONV_DIM, F32), ('dt', SSM_HEADS, F32))
_TR_OUTS = (('q', NSA_WIDTH, BF16), ('v_slc', KV_WIDTH, BF16), ('v_swa', KV_WIDTH, BF16),
            ('gate', 3 * NSA_HEADS, F32))


def _proj_kernel(x_ref, nw_ref, wn_ref, wt_ref, onehot_ref, *out_refs, n_sub):
    x = x_ref[...]
    ms = jnp.mean(x * x, axis=-1, keepdims=True)
    xn = (x * lax.rsqrt(ms + NORM_EPS) * nw_ref[...]).astype(BF16)
    lo = 0
    for (name, n, _), o_ref in zip(_NAT_OUTS, out_refs[:len(_NAT_OUTS)]):
        res = _dot(xn, wn_ref[:, lo:lo + n])
        if name == 'k_slc':
            res = res + onehot_ref[...]
        if name == 'kv_cmp':
            for j in range(n // LANES):
                o_ref[j] = res[:, j * LANES:(j + 1) * LANES].astype(o_ref.dtype)
        else:
            o_ref[...] = res.astype(o_ref.dtype)
        lo += n
    lo = 0
    for (_, n, _), o_ref in zip(_TR_OUTS, out_refs[len(_NAT_OUTS):]):
        res = _dot_nt(wt_ref[lo:lo + n, :], xn)
        for s in range(n_sub):
            o_ref[s] = res[:, s * LANES:(s + 1) * LANES].astype(o_ref.dtype)
        lo += n


def _proj(x2, nw, wn, wt, onehot, tm):
    t, d = x2.shape
    assert t % tm == 0 and tm % LANES == 0 and onehot.shape[1] == tm
    n_sub = tm // LANES
    n_pat = onehot.shape[0]
    const = lambda arr: pl.BlockSpec(arr.shape, lambda i: (0,) * arr.ndim, pipeline_mode=pl.Buffered(1))
    slabbed = lambda name: name == 'kv_cmp'
    out_specs = [pl.BlockSpec((n // LANES, tm, LANES), lambda i: (0, i, 0)) if slabbed(name)
                 else pl.BlockSpec((tm, n), lambda i: (i, 0)) for name, n, _ in _NAT_OUTS] + \
                [pl.BlockSpec((n_sub, n, LANES), lambda i: (i, 0, 0)) for _, n, _ in _TR_OUTS]
    out_shape = [jax.ShapeDtypeStruct((n // LANES, t, LANES) if slabbed(name) else (t, n), dt)
                 for name, n, dt in _NAT_OUTS] + \
                [jax.ShapeDtypeStruct((t // LANES, n, LANES), dt) for _, n, dt in _TR_OUTS]
    return pl.pallas_call(
        functools.partial(_proj_kernel, n_sub=n_sub),
        grid=(t // tm,),
        in_specs=[pl.BlockSpec((tm, d), lambda i: (i, 0)), const(nw), const(wn), const(wt),
                  pl.BlockSpec((None, tm, onehot.shape[2]), lambda i: (i % n_pat, 0, 0))],
        out_specs=out_specs,
        out_shape=out_shape,
        compiler_params=pltpu.CompilerParams(dimension_semantics=("parallel",),
                                             vmem_limit_bytes=VMEM_LIMIT),
        name="proj",
    )(x2, nw, wn, wt, onehot)


def _compress_kernel(raw_ref, posk_ref, posv_ref, w1k_ref, b1k_ref, w2k_ref,
                     w1v_ref, b1v_ref, w2vt_ref, kc_ref, vct_ref):
    nseg = kc_ref.shape[0]
    n_slab = raw_ref.shape[0]
    per_kv = n_slab // 2
    for j in range(n_slab):
        is_k = j < per_kv
        pos_ref, w1_ref, b1_ref = (posk_ref, w1k_ref, b1k_ref) if is_k else (posv_ref, w1v_ref, b1v_ref)
        first = jnp.zeros((nseg, 2 * CMP_HIDDEN), F32)
        second = jnp.zeros((nseg, 2 * CMP_HIDDEN), F32)
        for p in range(CMP_STRIDE):
            rows = raw_ref[j, pl.ds(p, nseg, stride=CMP_STRIDE), :]
            first = first + _dot((rows + pos_ref[p:p + 1, :]).astype(BF16), w1_ref[p])
            q = CMP_STRIDE + p
            second = second + _dot((rows + pos_ref[q:q + 1, :]).astype(BF16), w1_ref[q])
        hid = _silu(first + pltpu.roll(second, nseg - 1, 0) + b1_ref[...]).astype(BF16)
        for gi in range(2):
            g = 2 * (j % per_kv) + gi
            hg = hid[:, gi * CMP_HIDDEN:(gi + 1) * CMP_HIDDEN]
            if is_k:
                kc_ref[:, g * HEAD_DIM:(g + 1) * HEAD_DIM] = _dot(hg, w2k_ref[...]).astype(kc_ref.dtype)
            else:
                vct_ref[g * HEAD_DIM:(g + 1) * HEAD_DIM, :] = _dot_nt(w2vt_ref[...], hg).astype(vct_ref.dtype)


def _compress(raw, batch, posk, posv, w1k, b1k, w2k, w1v, b1v, w2vt):
    n_slab, t, lanes = raw.shape
    seq = t // batch
    nseg = seq // CMP_STRIDE
    assert n_slab * lanes == 2 * KV_WIDTH and lanes == 2 * HEAD_DIM
    const = lambda arr: pl.BlockSpec(arr.shape, lambda i: (0,) * arr.ndim, pipeline_mode=pl.Buffered(1))
    return pl.pallas_call(
        _compress_kernel,
        grid=(batch,),
        in_specs=[pl.BlockSpec((n_slab, seq, lanes), lambda i: (0, i, 0)),
                  const(posk), const(posv), const(w1k), const(b1k), const(w2k),
                  const(w1v), const(b1v), const(w2vt)],
        out_specs=[pl.BlockSpec((None, nseg, KV_WIDTH), lambda i: (i, 0, 0)),
                   pl.BlockSpec((None, KV_WIDTH, nseg), lambda i: (i, 0, 0))],
        out_shape=[jax.ShapeDtypeStruct((batch, nseg, KV_WIDTH), BF16),
                   jax.ShapeDtypeStruct((batch, KV_WIDTH, nseg), BF16)],
        compiler_params=pltpu.CompilerParams(dimension_semantics=("parallel",),
                                             vmem_limit_bytes=VMEM_LIMIT),
        name="compress",
    )(raw, posk, posv, w1k, b1k, w2k, w1v, b1v, w2vt)


def _nsa_kernel(qt_ref, gt_ref, kc_ref, vct_ref, kslc_ref, vslct_ref, kswa_ref, vswat_ref,
                tsel_ref, tcmp_ref, ovt_ref, far_ref, out_ref, guard_ref,
                qbd_ref, qaug_ref, m_ref, acc_ref, ot_ref, s_ref, p_ref, viol_ref, work_ref, chosen_ref,
                ocmp_ref, *, n_qb, n_cmp_rows, n_sel, top_n, exact):
    qb = pl.program_id(1)
    q0 = qb * Q_BLOCK

    @pl.when(qb == 0)
    def _():
        qbd_ref[...] = jnp.zeros_like(qbd_ref)
        qaug_ref[...] = jnp.zeros_like(qaug_ref)

    for g in range(NSA_GROUPS):
        for r in range(NSA_HPG):
            h = g * NSA_HPG + r
            q_head = qt_ref[h * HEAD_DIM:(h + 1) * HEAD_DIM, :]
            qbd_ref[g * HEAD_DIM:(g + 1) * HEAD_DIM, h * Q_BLOCK:(h + 1) * Q_BLOCK] = q_head
            qaug_ref[g, 0:HEAD_DIM, r * Q_BLOCK:(r + 1) * Q_BLOCK] = q_head

    def gslice(g):
        return slice(g * GCOLS, (g + 1) * GCOLS)

    def values_with_ones(vt_tiles, g):
        vt = jnp.concatenate([t[g * HEAD_DIM:(g + 1) * HEAD_DIM, :] for t in vt_tiles], axis=1)
        return jnp.concatenate([vt, jnp.ones((SUM_ROWS, vt.shape[1]), BF16)], axis=0)

    def normalized(a):
        return a[0:HEAD_DIM, :] * (1.0 / a[HEAD_DIM:HEAD_DIM + 1, :])

    def self_logit(kt):
        cols = []
        for h in range(NSA_HEADS):
            g = h // NSA_HPG
            prod = qt_ref[h * HEAD_DIM:(h + 1) * HEAD_DIM, :].astype(F32) * kt[g * HEAD_DIM:(g + 1) * HEAD_DIM, :]
            cols.append(jnp.sum(prod, axis=0, keepdims=True))
        return jnp.concatenate(cols, axis=1) - far_ref[...]

    diag_rows = pl.ds(pl.multiple_of(q0, Q_BLOCK), Q_BLOCK)

    u0 = pl.multiple_of(8 * (n_qb - 1 - qb), 8)

    def compress_and_select(n_rows, n_blk):
        j_iota = lax.broadcasted_iota(jnp.int32, (n_blk, Q_BLOCK), 0)
        j_f32 = j_iota.astype(F32)
        t_pos = q0 + lax.broadcasted_iota(jnp.int32, (n_blk, Q_BLOCK), 1)
        cur = t_pos // SEL_BLOCK
        valid = j_iota <= cur
        forced = (j_iota == 0) | (j_iota == cur) | (j_iota == cur - 1)

        logits = [_dot(kc_ref[0:n_rows, :], qbd_ref[:, gslice(g)]) for g in range(NSA_GROUPS)]
        for g in range(NSA_GROUPS):
            s = logits[g] + tcmp_ref[pl.ds(u0, n_rows), gslice(g)]
            m = jnp.maximum(jnp.max(s, axis=0, keepdims=True), 0.1 * NEG)
            p_ref[0:n_rows, gslice(g)] = jnp.exp2(s - m).astype(BF16)
        imps = []
        for g in range(NSA_GROUPS):
            e = p_ref[0:n_rows, gslice(g)]
            raw = _dot(values_with_ones([vct_ref[:, 0:n_rows]], g), e)
            total = raw[HEAD_DIM:HEAD_DIM + 1, :]
            inv = 1.0 / jnp.where(total == 0.0, 1.0, total)
            ocmp_ref[g] = raw[0:HEAD_DIM, :] * inv
            imp_heads = _dot(ovt_ref[0:n_blk, 0:n_rows], e) * inv
            imp = imp_heads[:, 0:Q_BLOCK]
            for r in range(1, NSA_HPG):
                imp = imp + imp_heads[:, r * Q_BLOCK:(r + 1) * Q_BLOCK]
            imps.append(imp)

        quota = top_n - (1 + (cur >= 1).astype(jnp.int32) + (cur >= 2).astype(jnp.int32))
        for g in range(NSA_GROUPS):
            work_ref[g, 0:n_blk, :] = jnp.where(forced, -2.0, jnp.where(valid, imps[g], -1.0))
            chosen_ref[g, 0:n_blk, :] = jnp.where(forced, 1.0, 0.0)

        def pick_round(index, limited):
            for g in range(NSA_GROUPS):
                work = work_ref[g, 0:n_blk, :]
                best = jnp.max(work, axis=0, keepdims=True)
                first = jnp.min(jnp.where(work == best, j_f32, float(n_blk)), axis=0, keepdims=True)
                hit = j_f32 == first
                if limited:
                    hit = hit & (quota > index)
                chosen_ref[g, 0:n_blk, :] = jnp.where(hit, 1.0, chosen_ref[g, 0:n_blk, :])
                work_ref[g, 0:n_blk, :] = jnp.where(hit, -2.0, work)

        max_forced = 3
        for index in range(top_n - max_forced):
            pick_round(index, False)

        @pl.when(q0 < (max_forced - 1) * SEL_BLOCK)
        def _():
            for index in range(top_n - max_forced, top_n - 1):
                pick_round(index, True)

        for g in range(NSA_GROUPS):
            addm = jnp.where((chosen_ref[g, 0:n_blk, :] > 0.5) & valid, 0.0, NEG).astype(BF16)
            qaug_ref[g, HEAD_DIM:HEAD_DIM + n_blk, :] = jnp.concatenate([addm] * NSA_HPG, axis=1)
            if n_blk < n_sel:
                qaug_ref[g, HEAD_DIM + n_blk:HEAD_DIM + n_sel, :] = jnp.full((n_sel - n_blk, GCOLS), NEG, BF16)

    first_half = qb < n_qb // 2

    @pl.when(first_half)
    def _():
        compress_and_select(n_cmp_rows // 2, n_sel // 2)

    @pl.when(jnp.logical_not(first_half))
    def _():
        compress_and_select(n_cmp_rows, n_sel)

    def pipelined(items, logits_of, finish):
        pending = {i: logits_of(items[i]) for i in range(min(QK_LOOKAHEAD, len(items)))}
        for i, item in enumerate(items):
            finish(item, pending.pop(i))
            if i + QK_LOOKAHEAD < len(items):
                pending[i + QK_LOOKAHEAD] = logits_of(items[i + QK_LOOKAHEAD])

    def window(exact):
        first_tile = jnp.maximum(qb - (WIN_TILES - 1), 0)
        kw = kswa_ref[pl.ds(pl.multiple_of(first_tile * LANES, LANES), WIN_TILES * LANES), :]
        win_vt_tiles = [vswat_ref[first_tile + t] for t in range(WIN_TILES)]
        win_ids = []
        for t in range(WIN_TILES):
            i = qb - (first_tile + t)
            win_ids.append(jnp.where(i < 0, TILE_MASKED, jnp.where(i == WIN_TILES - 1, TILE_WIN_OLD, i)))
        if not exact:
            m_fix = self_logit(kswa_ref[diag_rows, :].astype(F32).T)

        def finish(g, logits):
            m8 = None
            for t in range(WIN_TILES):
                rows = slice(t * LANES, (t + 1) * LANES)
                v = logits[rows, :] + tsel_ref[win_ids[t], :, gslice(g)]
                if exact:
                    s_ref[rows, gslice(g)] = v
                    vm = jnp.max(v.reshape(LANES // 8, 8, GCOLS), axis=0)
                else:
                    pt = jnp.exp2((v - m_fix[:, gslice(g)]).astype(BF16))
                    p_ref[rows, gslice(g)] = pt
                    vm = jnp.max(pt.reshape(LANES // 16, 16, GCOLS), axis=0)
                m8 = vm if m8 is None else jnp.maximum(m8, vm)
            if exact:
                m = jnp.max(m8, axis=0, keepdims=True)
                p_ref[0:WIN_TILES * LANES, gslice(g)] = jnp.exp2(s_ref[:, gslice(g)] - m).astype(BF16)
            else:
                viol_ref[:, gslice(g)] = jnp.maximum(viol_ref[:, gslice(g)],
                                                     jnp.max(m8.astype(F32), axis=0, keepdims=True))
            acc_ref[g] = _dot(values_with_ones(win_vt_tiles, g), p_ref[0:WIN_TILES * LANES, gslice(g)])

        pipelined(list(range(NSA_GROUPS)), lambda g: _dot(kw, qbd_ref[:, gslice(g)]), finish)

    def sel_steps(steps, near, exact):
        def logits_of(item):
            (tile0, n_tiles, _), g = item
            key_rows = pl.ds(pl.multiple_of(tile0 * LANES, LANES), n_tiles * LANES)
            return _dot(kslc_ref[key_rows, g * AUG_DIM:(g + 1) * AUG_DIM], qaug_ref[g])

        def finish(item, logits):
            (tile0, n_tiles, p_row0), g = item
            vt_tiles = [vslct_ref[tile0 + t] for t in range(n_tiles)]
            m_old = m_ref[:, gslice(g)]
            m8 = None
            for t in range(n_tiles):
                rows = slice(t * LANES, (t + 1) * LANES)
                v = logits[rows, :]
                if near:
                    i = qb - (tile0 + t)
                    tile_id = jnp.where(i < 0, TILE_MASKED, jnp.minimum(i, TILE_FAR))
                    v = v + tsel_ref[tile_id, :, gslice(g)]
                if exact:
                    s_ref[rows, gslice(g)] = v
                    vm = jnp.max(v.reshape(LANES // 8, 8, GCOLS), axis=0)
                else:
                    pt = jnp.exp2((v - m_old).astype(BF16))
                    p_ref[p_row0 + t * LANES:p_row0 + (t + 1) * LANES, gslice(g)] = pt
                    vm = jnp.max(pt.reshape(LANES // 16, 16, GCOLS), axis=0)
                m8 = vm if m8 is None else jnp.maximum(m8, vm)
            if exact:
                m_new = jnp.maximum(m_old, jnp.max(m8, axis=0, keepdims=True))
                alpha = jnp.exp2(m_old - m_new)
                m_ref[:, gslice(g)] = m_new
                for t in range(n_tiles):
                    rows = slice(t * LANES, (t + 1) * LANES)
                    p_ref[p_row0 + t * LANES:p_row0 + (t + 1) * LANES, gslice(g)] = \
                        jnp.exp2(s_ref[rows, gslice(g)] - m_new).astype(BF16)
            else:
                viol_ref[:, gslice(g)] = jnp.maximum(viol_ref[:, gslice(g)],
                                                     jnp.max(m8.astype(F32), axis=0, keepdims=True))
            pv = _dot(values_with_ones(vt_tiles, g), p_ref[p_row0:p_row0 + n_tiles * LANES, gslice(g)])
            if exact:
                acc_ref[g] = acc_ref[g] * alpha + pv
            else:
                acc_ref[g] = acc_ref[g] + pv

        pipelined([(step, g) for step in steps for g in range(NSA_GROUPS)], logits_of, finish)

    def sel_step(tile0, n_tiles, near, exact):
        sel_steps([(tile0, n_tiles, 0)], near, exact)

    def sel_two_steps(tile_a, tile_b, near):
        sel_steps([(tile_a, STEP_TILES, 0), (tile_b, STEP_TILES, STEP)], near, False)

    r_diag = qb // STEP_TILES
    n_far = jnp.maximum((qb - (N_NEAR - 1)) // STEP_TILES, 0)
    half = STEP_TILES // 2

    def selected(exact):
        if exact:
            m_ref[...] = jnp.full(m_ref.shape, NEG, F32)
        else:
            kt = kslc_ref[diag_rows, :].astype(F32).T
            m_ref[...] = self_logit(jnp.concatenate(
                [kt[g * AUG_DIM:g * AUG_DIM + HEAD_DIM, :] for g in range(NSA_GROUPS)], axis=0))
        acc_ref[...] = jnp.zeros_like(acc_ref)

        @pl.when(qb % STEP_TILES < half)
        def _():
            sel_step(STEP_TILES * r_diag, half, True, exact)

        @pl.when(qb % STEP_TILES >= half)
        def _():
            sel_step(STEP_TILES * r_diag, STEP_TILES, True, exact)

        def near_body(i, carry):
            sel_step(STEP_TILES * (r_diag - i), STEP_TILES, True, exact)
            return carry

        def far_body(r, carry):
            sel_step(STEP_TILES * r, STEP_TILES, False, exact)
            return carry

        def far_pair_body(k, carry):
            sel_two_steps(STEP_TILES * 2 * k, STEP_TILES * (2 * k + 1), False)
            return carry

        lax.fori_loop(1, r_diag - n_far + 1, near_body, 0)
        if exact:
            lax.fori_loop(0, n_far, far_body, 0)
        else:
            @pl.when(n_far % 2 == 1)
            def _():
                sel_step(STEP_TILES * (n_far - 1), STEP_TILES, False, False)

            lax.fori_loop(0, n_far // 2, far_pair_body, 0)

    viol_ref[...] = jnp.zeros_like(viol_ref)
    selected(exact)
    o_slc = [normalized(acc_ref[g]) for g in range(NSA_GROUPS)]
    window(exact)
    o_swa = [normalized(acc_ref[g]) for g in range(NSA_GROUPS)]
    guard_ref[...] = jnp.broadcast_to(jnp.max(viol_ref[...], axis=1, keepdims=True), guard_ref.shape)

    gates = _sigmoid(gt_ref[...])
    for g in range(NSA_GROUPS):
        def gate_row(c):
            return jnp.concatenate(
                [gates[c * NSA_HEADS + g * NSA_HPG + r:c * NSA_HEADS + g * NSA_HPG + r + 1, :]
                 for r in range(NSA_HPG)], axis=1)
        og = gate_row(0) * ocmp_ref[g] + gate_row(1) * o_slc[g] + gate_row(2) * o_swa[g]
        for r in range(NSA_HPG):
            h = g * NSA_HPG + r
            ot_ref[h * HEAD_DIM:(h + 1) * HEAD_DIM, :] = og[:, r * Q_BLOCK:(r + 1) * Q_BLOCK]
    out_ref[...] = ot_ref[...].T


def _nsa(qt, gt, kc, vct, kslc, vslct, kswa, vswat, tsel, tcmp, ovt, far, seq, exact):
    b = kc.shape[0]
    n_qb = seq // Q_BLOCK
    assert n_qb % STEP_TILES == 0 and n_qb >= WIN_TILES and WIN_TILES >= STEP_TILES
    n_cmp_rows = kc.shape[1]
    n_sel = seq // SEL_BLOCK
    assert n_sel <= AUG_DIM - HEAD_DIM
    top_n = min(SEL_TOPN, n_sel)
    assert top_n > 3
    n_gate = gt.shape[2]
    const = lambda arr: pl.BlockSpec(arr.shape, lambda i, j: (0,) * arr.ndim,
                                     pipeline_mode=pl.Buffered(1))
    kern = functools.partial(_nsa_kernel, n_qb=n_qb, n_cmp_rows=n_cmp_rows, n_sel=n_sel, top_n=top_n,
                             exact=exact)
    return pl.pallas_call(
        kern,
        grid=(b, n_qb),
        in_specs=[pl.BlockSpec((None, None, NSA_WIDTH, Q_BLOCK), lambda i, j: (i, j, 0, 0)),
                  pl.BlockSpec((None, None, n_gate, Q_BLOCK), lambda i, j: (i, j, 0, 0)),
                  pl.BlockSpec((None, n_cmp_rows, KV_WIDTH), lambda i, j: (i, 0, 0)),
                  pl.BlockSpec((None, KV_WIDTH, n_cmp_rows), lambda i, j: (i, 0, 0)),
                  pl.BlockSpec((None, seq, NSA_GROUPS * AUG_DIM), lambda i, j: (i, 0, 0)),
                  pl.BlockSpec((None, n_qb, KV_WIDTH, Q_BLOCK), lambda i, j: (i, 0, 0, 0)),
                  pl.BlockSpec((None, seq, KV_WIDTH), lambda i, j: (i, 0, 0)),
                  pl.BlockSpec((None, n_qb, KV_WIDTH, Q_BLOCK), lambda i, j: (i, 0, 0, 0)),
                  const(tsel), const(tcmp), const(ovt), const(far)],
        out_specs=[pl.BlockSpec((None, Q_BLOCK, NSA_WIDTH), lambda i, j: (i, j, 0)),
                   pl.BlockSpec((None, None, 8, LANES), lambda i, j: (i, j, 0, 0))],
        out_shape=[jax.ShapeDtypeStruct((b, seq, NSA_WIDTH), F32),
                   jax.ShapeDtypeStruct((b, n_qb, 8, LANES), F32)],
        scratch_shapes=[pltpu.VMEM((KV_WIDTH, QCOLS), BF16),
                        pltpu.VMEM((NSA_GROUPS, AUG_DIM, GCOLS), BF16),
                        pltpu.VMEM((1, QCOLS), F32),
                        pltpu.VMEM((NSA_GROUPS, HEAD_DIM + SUM_ROWS, GCOLS), F32),
                        pltpu.VMEM((NSA_WIDTH, Q_BLOCK), F32),
                        pltpu.VMEM((WIN_TILES * LANES, QCOLS), F32),
                        pltpu.VMEM((max(WIN_TILES * LANES, 2 * STEP), QCOLS), BF16),
                        pltpu.VMEM((1, QCOLS), F32),
                        pltpu.VMEM((NSA_GROUPS, n_sel, Q_BLOCK), F32),
                        pltpu.VMEM((NSA_GROUPS, n_sel, Q_BLOCK), F32),
                        pltpu.VMEM((NSA_GROUPS, HEAD_DIM, GCOLS), F32)],
        compiler_params=pltpu.CompilerParams(dimension_semantics=("parallel", "arbitrary"),
                                             vmem_limit_bytes=VMEM_LIMIT),
        name="nsa_exact" if exact else "nsa",
    )(qt, gt, kc, vct, kslc, vslct, kswa, vswat, tsel, tcmp, ovt, far)


def _ssd_kernel(xbc_ref, dt_ref, convw_ref, convb_ref, dtb_ref, a_ref, dskip_ref, expand_ref, y_ref,
                xp_ref, tail_ref, state_ref, slab_ref):
    @pl.when(pl.program_id(1) == 0)
    def _():
        tail_ref[...] = jnp.zeros_like(tail_ref)
        state_ref[...] = jnp.zeros_like(state_ref)

    n_chunks = xbc_ref.shape[0] // SSM_CHUNK
    fronts = [_ssd_front(ci, xbc_ref, dt_ref, convw_ref, convb_ref, dtb_ref, a_ref, expand_ref,
                         xp_ref.at[ci], tail_ref, slab_ref.at[ci]) for ci in range(n_chunks)]
    for ci in range(n_chunks):
        _ssd_back(ci, fronts[ci], dskip_ref, y_ref, state_ref, slab_ref.at[ci])


def _ssd_front(ci, xbc_ref, dt_ref, convw_ref, convb_ref, dtb_ref, a_ref, expand_ref, xp_ref, tail_ref, slab_ref):
    L = SSM_CHUNK
    NV = L // 8
    halo = CONV_WIDTH - 1
    tok = slice(ci * L, (ci + 1) * L)

    n_slab = CONV_DIM // LANES
    for s in range(n_slab):
        slab_ref[s] = xbc_ref[tok, s * LANES:(s + 1) * LANES]
    slab_ref[n_slab] = jnp.concatenate([dt_ref[tok, :], jnp.zeros((L, LANES - SSM_HEADS), F32)], axis=1)
    for v in range(NV):
        for s in range(n_slab):
            xp_ref[(halo + v) * 8:(halo + v + 1) * 8, s * LANES:(s + 1) * LANES] = \
                slab_ref[s, pl.ds(v, 8, stride=NV), :]
    sub = lax.broadcasted_iota(jnp.int32, (8, CONV_DIM), 0)
    for w in range(halo):
        cur = xp_ref[(NV + w) * 8:(NV + w + 1) * 8, :]
        prev = tail_ref[w * 8:(w + 1) * 8, :]
        xp_ref[w * 8:(w + 1) * 8, :] = pltpu.roll(jnp.where(sub == 7, prev, cur), 1, 0)
    tail_ref[...] = xp_ref[NV * 8:(NV + halo) * 8, :]
    conv = convb_ref[...]
    for k in range(CONV_WIDTH):
        conv = conv + convw_ref[k:k + 1, :] * xp_ref[8 * k:8 * k + L, :]
    xbc = _silu(conv)
    xs = xbc[:, 0:SSM_WIDTH]
    bm = xbc[:, SSM_WIDTH:SSM_WIDTH + SSM_GROUPS * SSM_STATE]
    cm = xbc[:, SSM_WIDTH + SSM_GROUPS * SSM_STATE:]

    dt_raw = jnp.concatenate([slab_ref[n_slab, pl.ds(v, 8, stride=NV), :] for v in range(NV)], axis=0)
    dt = _softplus(dt_raw[:, 0:SSM_HEADS] + dtb_ref[...])
    a_row = -jnp.exp(a_ref[...])
    row_i = lax.broadcasted_iota(jnp.int32, (L, L), 0)
    col_j = lax.broadcasted_iota(jnp.int32, (L, L), 1)
    token = lambda r: (r % 8) * NV + r // 8
    tril = token(row_i) >= token(col_j)
    tril01 = jnp.where(tril, 1.0, 0.0).astype(BF16)
    cs = _dot_exact_rhs(tril01, dt * a_row)
    cst = jnp.concatenate([cs, jnp.zeros((L, L - SSM_HEADS), F32)], axis=1).T

    expand = expand_ref[...]
    cs_x = _dot_exact_lhs(cs, expand)
    dt_x = _dot_exact_lhs(dt, expand)
    total_x = cs_x[L - 1:L, :]
    xdt = (xs * dt_x).astype(BF16)
    xw = (xs * (jnp.exp(total_x - cs_x) * dt_x)).astype(BF16)
    decay_out = jnp.exp(cs_x)
    decay_state = jnp.exp(total_x)
    return xs, bm, cm, cs, cst, tril, xdt, xw, decay_out, decay_state


def _ssd_back(ci, front, dskip_ref, y_ref, state_ref, slab_ref):
    xs, bm, cm, cs, cst, tril, xdt, xw, decay_out, decay_state = front
    L = SSM_CHUNK
    NV = L // 8
    tok = slice(ci * L, (ci + 1) * L)
    lane = lax.broadcasted_iota(jnp.int32, (L, LANES), 1)
    gw = SSM_HPG * SSM_HEAD_DIM
    for g in range(SSM_GROUPS):
        bg = bm[:, g * SSM_STATE:(g + 1) * SSM_STATE]
        cg = cm[:, g * SSM_STATE:(g + 1) * SSM_STATE].astype(BF16)
        cb = _dot_nt(cg, bg.astype(BF16))
        st = state_ref[g]
        y_g = _dot(cg, st.astype(BF16)) * decay_out[:, g * gw:(g + 1) * gw]
        state_ref[g] = st * decay_state[:, g * gw:(g + 1) * gw] + \
            _dot(bg.T.astype(BF16), xw[:, g * gw:(g + 1) * gw])
        pieces = []
        for pair in range(SSM_HPG // 2):
            ws = []
            for hh in range(2):
                h = g * SSM_HPG + 2 * pair + hh
                diff = cs[:, h:h + 1] - cst[h:h + 1, :]
                decay = jnp.where(tril, jnp.exp(jnp.where(tril, diff, 0.0)), 0.0)
                ws.append((cb * decay).astype(BF16))
            w_pair = jnp.concatenate(ws, axis=1)
            lo = g * gw + pair * LANES
            slab = xdt[:, lo:lo + LANES]
            zero = jnp.zeros_like(slab)
            x_bd = jnp.concatenate([jnp.where(lane < SSM_HEAD_DIM, slab, zero),
                                    jnp.where(lane >= SSM_HEAD_DIM, slab, zero)], axis=0)
            pieces.append(_dot(w_pair, x_bd))
        y_g = y_g + jnp.concatenate(pieces, axis=1) + \
            dskip_ref[:, g * gw:(g + 1) * gw] * xs[:, g * gw:(g + 1) * gw]
        for s in range(gw // LANES):
            for v in range(NV):
                slab_ref[s, pl.ds(v, 8, stride=NV), :] = y_g[v * 8:(v + 1) * 8, s * LANES:(s + 1) * LANES]
        for s in range(gw // LANES):
            y_ref[tok, g * gw + s * LANES:g * gw + (s + 1) * LANES] = slab_ref[s]


def _ssd(xbc, dt, convw, convb, dtb, a_log, dskip_x, expand, seq):
    b = xbc.shape[0]
    rows = SSD_CHUNKS_PER_STEP * SSM_CHUNK
    assert seq % rows == 0
    halo_rows = (CONV_WIDTH - 1) * 8
    const = lambda arr: pl.BlockSpec(arr.shape, lambda i, j: (0,) * arr.ndim)
    return pl.pallas_call(
        _ssd_kernel,
        grid=(b, seq // rows),
        in_specs=[pl.BlockSpec((None, rows, CONV_DIM), lambda i, j: (i, j, 0)),
                  pl.BlockSpec((None, rows, SSM_HEADS), lambda i, j: (i, j, 0)),
                  const(convw), const(convb), const(dtb), const(a_log), const(dskip_x), const(expand)],
        out_specs=pl.BlockSpec((None, rows, SSM_WIDTH), lambda i, j: (i, j, 0)),
        out_shape=jax.ShapeDtypeStruct((b, seq, SSM_WIDTH), F32),
        scratch_shapes=[pltpu.VMEM((SSD_CHUNKS_PER_STEP, SSM_CHUNK + halo_rows, CONV_DIM), F32),
                        pltpu.VMEM((halo_rows, CONV_DIM), F32),
                        pltpu.VMEM((SSM_GROUPS, SSM_STATE, SSM_HPG * SSM_HEAD_DIM), F32),
                        pltpu.VMEM((SSD_CHUNKS_PER_STEP, CONV_DIM // LANES + 1, SSM_CHUNK, LANES), F32)],
        compiler_params=pltpu.CompilerParams(dimension_semantics=("parallel", "arbitrary"),
                                             vmem_limit_bytes=VMEM_LIMIT),
        name="ssd",
    )(xbc, dt, convw, convb, dtb, a_log, dskip_x, expand)


def _epilogue_kernel(x_ref, o_ref, y_ref, nw_ref, wz1_ref, wz2_ref, wmg_ref, won_ref, wos_ref, wo_ref,
                     snw_ref, fnw_ref, out_ref):
    x = x_ref[...]
    ms = jnp.mean(x * x, axis=-1, keepdims=True)
    xn = (x * lax.rsqrt(ms + NORM_EPS) * nw_ref[...]).astype(BF16)
    u = (o_ref[...] * _silu(_dot(xn, wz1_ref[...]))).astype(BF16)
    h_nsa = _dot(u, won_ref[...])
    hh = y_ref[...] * _silu(_dot(xn, wz2_ref[...]))
    gw = SSM_WIDTH // SSM_GROUPS
    parts = []
    for g in range(SSM_GROUPS):
        hg = hh[:, g * gw:(g + 1) * gw]
        hg = hg * lax.rsqrt(jnp.mean(hg * hg, axis=-1, keepdims=True) + NORM_EPS)
        parts.append((hg * snw_ref[:, g * gw:(g + 1) * gw]).astype(BF16))
    h_ssm = _dot(jnp.concatenate(parts, axis=1), wos_ref[...])
    gate = _sigmoid(_dot(xn, wmg_ref[...]))
    mix = (gate[:, 0:D_MODEL] * h_nsa + gate[:, D_MODEL:] * h_ssm).astype(BF16)
    r = x + _dot(mix, wo_ref[...])
    ms2 = jnp.mean(r * r, axis=-1, keepdims=True)
    out_ref[...] = r * lax.rsqrt(ms2 + NORM_EPS) * fnw_ref[...]


def _epilogue(x2, o2, y2, nw, wz1, wz2, wmg, won, wos, wo, snw, fnw, tm):
    t, d = x2.shape
    assert t % tm == 0
    const = lambda arr: pl.BlockSpec(arr.shape, lambda i: (0,) * arr.ndim, pipeline_mode=pl.Buffered(1))
    return pl.pallas_call(
        _epilogue_kernel,
        grid=(t // tm,),
        in_specs=[pl.BlockSpec((tm, d), lambda i: (i, 0)),
                  pl.BlockSpec((tm, NSA_WIDTH), lambda i: (i, 0)),
                  pl.BlockSpec((tm, SSM_WIDTH), lambda i: (i, 0)),
                  const(nw), const(wz1), const(wz2), const(wmg), const(won), const(wos), const(wo),
                  const(snw), const(fnw)],
        out_specs=pl.BlockSpec((tm, d), lambda i: (i, 0)),
        out_shape=jax.ShapeDtypeStruct((t, d), F32),
        compiler_params=pltpu.CompilerParams(dimension_semantics=("parallel",),
                                             vmem_limit_bytes=VMEM_LIMIT),
        name="epilogue",
    )(x2, o2, y2, nw, wz1, wz2, wmg, won, wos, wo, snw, fnw)


def _table_kernel(rb_ref, bucket_ref, out_ref):
    bk = bucket_ref[...]
    for h in range(NSA_HEADS):
        acc = jnp.full(bk.shape, NEG, F32)
        for b in range(REL_BUCKETS):
            acc = jnp.where(bk == b, rb_ref[b, h], acc)
        out_ref[:, h * Q_BLOCK:(h + 1) * Q_BLOCK] = acc


def _bias_table(rb, buckets, tr):
    rows = buckets.shape[0]
    assert rows % tr == 0
    return pl.pallas_call(
        _table_kernel,
        grid=(rows // tr,),
        in_specs=[pl.BlockSpec(memory_space=pltpu.SMEM),
                  pl.BlockSpec((tr, Q_BLOCK), lambda i: (i, 0))],
        out_specs=pl.BlockSpec((tr, QCOLS), lambda i: (i, 0)),
        out_shape=jax.ShapeDtypeStruct((rows, QCOLS), F32),
        compiler_params=pltpu.CompilerParams(dimension_semantics=("parallel",)),
        name="bias_table",
    )(rb, buckets)


def _bucket_tables(seq):
    n_qb = seq // Q_BLOCK
    key = np.arange(LANES)[:, None]
    tok = np.arange(Q_BLOCK)[None, :]
    tiles = [_t5_bucket_np(LANES * o + tok - key) for o in range(N_NEAR)]
    tiles.append(np.full((LANES, Q_BLOCK), REL_BUCKETS - 1, np.int32))
    dwin = WINDOW + tok - key
    tiles.append(_t5_bucket_np(np.where(dwin < WINDOW, dwin, -1)))
    tiles.append(np.full((LANES, Q_BLOCK), MASKED_BUCKET, np.int32))
    assert len(tiles) == N_TILES and LANES * N_NEAR - (LANES - 1) >= 790
    n_cmp_rows = seq // CMP_STRIDE
    rows = 8 * (n_qb - 1) + n_cmp_rows
    rows_pad = -(-rows // LANES) * LANES
    u = np.arange(rows_pad)[:, None]
    cmp_tbl = _t5_bucket_np(tok - CMP_STRIDE * u + Q_BLOCK * (n_qb - 1) - (CMP_BLOCK - 1))
    return np.concatenate(tiles, axis=0), cmp_tbl


def _block_onehot(seq, tm):
    pos = np.arange(seq)
    pat = np.zeros((seq, NSA_GROUPS, AUG_DIM), np.float32)
    pat[pos, :, HEAD_DIM + pos // SEL_BLOCK] = 1.0
    return jnp.asarray(pat.reshape(seq // tm, tm, NSA_GROUPS * AUG_DIM), dtype=BF16)


def _overlap_t(seq):
    n_cmp_rows = seq // CMP_STRIDE
    n_sel = seq // SEL_BLOCK
    c_start = np.arange(n_cmp_rows)[None, :] * CMP_STRIDE
    s_start = np.arange(n_sel)[:, None] * SEL_BLOCK
    ov = (c_start < s_start + SEL_BLOCK) & (c_start + CMP_BLOCK > s_start)
    ov[:, n_cmp_rows - 1] = False
    return jnp.asarray(ov, dtype=BF16)


def _layer(x, norm_w, w_in, cmp_pos_k, cmp_pos_v, cmp_k_w1, cmp_k_b1, cmp_k_w2, cmp_v_w1, cmp_v_b1,
           cmp_v_w2, conv_w, conv_b, dt_bias, a_log, d_skip, ssm_norm_w, w_out_nsa, w_out_ssm, w_out,
           rel_bias, out_norm_w):
    b, s, d = x.shape
    assert d == D_MODEL and s % STEP == 0 and s >= 2 * WINDOW
    t = b * s
    n_qb = s // Q_BLOCK
    cols = _column_offsets()
    wcol = lambda name: w_in[:, cols[name][0]:cols[name][1]]
    x2 = x.reshape(t, d)
    nw = norm_w.reshape(1, d).astype(F32)

    gate_w = wcol('nsa_gate').reshape(d, NSA_HEADS, 3).transpose(0, 2, 1).reshape(d, 3 * NSA_HEADS)
    w_kslc = jnp.pad(wcol('k_slc').reshape(d, NSA_GROUPS, HEAD_DIM),
                     ((0, 0), (0, 0), (0, AUG_DIM - HEAD_DIM))).reshape(d, NSA_GROUPS * AUG_DIM)
    wn = jnp.concatenate([w_kslc, wcol('k_swa'), wcol('k_cmp'), wcol('v_cmp'), wcol('xbc'),
                          wcol('dt')], axis=1).astype(BF16)
    wt = jnp.concatenate([wcol('q') * (HEAD_DIM ** -0.5 * LOG2E), wcol('v_slc'), wcol('v_swa'), gate_w],
                         axis=1).T.astype(BF16)
    kslc, kswa, cc, xbc, dt, qt, vslct, vswat, gt = _proj(x2, nw, wn, wt, _block_onehot(s, STEP), STEP)
    kslc = kslc.reshape(b, s, NSA_GROUPS * AUG_DIM)
    kswa = kswa.reshape(b, s, KV_WIDTH)
    xbc = xbc.reshape(b, s, CONV_DIM)
    dt = dt.reshape(b, s, SSM_HEADS)
    qt = qt.reshape(b, n_qb, NSA_WIDTH, Q_BLOCK)
    vslct = vslct.reshape(b, n_qb, KV_WIDTH, Q_BLOCK)
    vswat = vswat.reshape(b, n_qb, KV_WIDTH, Q_BLOCK)
    gt = gt.reshape(b, n_qb, 3 * NSA_HEADS, Q_BLOCK)

    def pair_diag(w1):
        w = w1.reshape(CMP_BLOCK, HEAD_DIM, CMP_HIDDEN)
        z = jnp.zeros_like(w)
        return jnp.concatenate([jnp.concatenate([w, z], axis=2),
                                jnp.concatenate([z, w], axis=2)], axis=1).astype(BF16)

    twice = lambda a: jnp.tile(a.astype(F32), (1, 2))
    kc, vct = _compress(cc, b, twice(cmp_pos_k), twice(cmp_pos_v),
                        pair_diag(cmp_k_w1), twice(cmp_k_b1.reshape(1, -1)), cmp_k_w2.astype(BF16),
                        pair_diag(cmp_v_w1), twice(cmp_v_b1.reshape(1, -1)), cmp_v_w2.T.astype(BF16))

    sel_buckets, cmp_buckets = _bucket_tables(s)
    rb = rel_bias.astype(F32) * LOG2E
    rb_far = rb[REL_BUCKETS - 1:REL_BUCKETS, :]
    tsel = _bias_table(rb - rb_far, jnp.asarray(sel_buckets), LANES).reshape(N_TILES, LANES, QCOLS)
    tcmp = _bias_table(rb, jnp.asarray(cmp_buckets), LANES)
    far = jnp.repeat(rb_far.reshape(-1), Q_BLOCK).reshape(1, QCOLS)
    nsa_args = (qt, gt, kc, vct, kslc, vslct, kswa, vswat, tsel, tcmp, _overlap_t(s), far, s)
    o_nsa, guard = _nsa(*nsa_args, exact=False)
    o_nsa = lax.cond(jnp.logical_not(jnp.max(guard) <= 2.0 ** MAX_EXP2_EXCESS),
                     lambda: _nsa(*nsa_args, exact=True)[0], lambda: o_nsa)

    expand = jnp.asarray(np.kron(np.eye(SSM_HEADS), np.ones((1, SSM_HEAD_DIM))), dtype=BF16)
    y = _ssd(xbc, dt, conv_w.astype(F32), conv_b.reshape(1, -1).astype(F32),
             dt_bias.reshape(1, -1).astype(F32), a_log.reshape(1, -1).astype(F32),
             jnp.repeat(d_skip.astype(F32), SSM_HEAD_DIM).reshape(1, -1), expand, s)

    out = _epilogue(x2, o_nsa.reshape(t, NSA_WIDTH), y.reshape(t, SSM_WIDTH), nw,
                    wcol('z_nsa').astype(BF16), wcol('z_ssm').astype(BF16), wcol('merge_gate').astype(BF16),
                    w_out_nsa.astype(BF16), w_out_ssm.astype(BF16), w_out.astype(BF16),
                    ssm_norm_w.reshape(1, -1).astype(F32), out_norm_w.reshape(1, -1).astype(F32),
                    512 if t % 512 == 0 else Q_BLOCK)
    return out.reshape(b, s, d)


def kernel(x, norm_w, w_in, cmp_pos_k, cmp_pos_v, cmp_k_w1, cmp_k_b1, cmp_k_w2, cmp_v_w1, cmp_v_b1, cmp_v_w2,
           conv_w, conv_b, dt_bias, a_log, d_skip, ssm_norm_w, w_out_nsa, w_out_ssm, w_out, rel_bias,
           final_norm_w):
    depth = norm_w.shape[0]
    assert depth == 1, "the epilogue fuses the final norm into the single layer"
    return _layer(x, norm_w[0], w_in[0], cmp_pos_k[0], cmp_pos_v[0], cmp_k_w1[0], cmp_k_b1[0], cmp_k_w2[0],
                  cmp_v_w1[0], cmp_v_b1[0], cmp_v_w2[0], conv_w[0], conv_b[0], dt_bias[0], a_log[0],
                  d_skip[0], ssm_norm_w[0], w_out_nsa[0], w_out_ssm[0], w_out[0], rel_bias, final_norm_w)
```
